```python
import jax
import jax.numpy as jnp
from jax import lax

D_MODEL = 1024
BATCH = 2
SEQ = 16384
DEPTH = 2

HEAD_DIM = 64
ROPE_THETA = 10000.0
NORM_EPS = 1e-6
Q_BLOCK = 128
MAX_POS_OFFSET = 4096

A_HEADS = 8
A_KV_HEADS = 2
A_WINDOW = 128
B_HEADS = 8
C_HEADS = 8
C_KV_HEADS = 2
C_BLOCK = 256
C_TOPK = 3
D_HEADS = 8
D_KV_HEADS = 2
D_CMP_LEN = 32
D_CMP_STRIDE = 16
D_CMP_HIDDEN = 256
D_SEL_LEN = 64
D_SEL_TOPK = 16
D_WINDOW = 512
N_GROUPS = 4
EXPERTS_PER_GROUP = 16
N_EXPERTS = N_GROUPS * EXPERTS_PER_GROUP
MOE_TOPK = 2
EXPERT_HIDDEN = 512
MOE_BLOCK = 128

AB_SPLITS = (A_HEADS * HEAD_DIM, A_KV_HEADS * HEAD_DIM, A_KV_HEADS * HEAD_DIM, B_HEADS * HEAD_DIM, B_HEADS * HEAD_DIM, B_HEADS * HEAD_DIM)
CD_SPLITS = (C_HEADS * HEAD_DIM, C_KV_HEADS * HEAD_DIM, C_KV_HEADS * HEAD_DIM, D_HEADS * HEAD_DIM) + (D_KV_HEADS * HEAD_DIM,) * 6 + (3 * D_HEADS,)
AB_COLS = sum(AB_SPLITS)
CD_COLS = sum(CD_SPLITS)
AB_OUT = (A_HEADS + B_HEADS) * HEAD_DIM
CD_OUT = (C_HEADS + D_HEADS) * HEAD_DIM

kernel_name = 'hybrid_swa_stickbreak_moba_nsa_hmoe'


def _rmsnorm(x, g):
    xf = x.astype(jnp.float32)
    y = xf * lax.rsqrt(jnp.mean(xf * xf, axis=-1, keepdims=True) + NORM_EPS)
    return (y * g.astype(jnp.float32)).astype(x.dtype)


def _rope_tables(positions):
    inv_freq = ROPE_THETA ** (-jnp.arange(0, HEAD_DIM, 2, dtype=jnp.float32) / HEAD_DIM)
    ang = positions.astype(jnp.float32)[..., None] * inv_freq
    return jnp.cos(ang)[:, :, None, :], jnp.sin(ang)[:, :, None, :]


def _rope(x, cos, sin):
    half = HEAD_DIM // 2
    x1 = x[..., :half].astype(jnp.float32)
    x2 = x[..., half:].astype(jnp.float32)
    return jnp.concatenate([x1 * cos - x2 * sin, x2 * cos + x1 * sin], axis=-1).astype(x.dtype)


def _heads(t):
    return t.reshape(t.shape[0], t.shape[1], -1, HEAD_DIM)


def _split_cols(t, sizes):
    out, start = [], 0
    for size in sizes:
        out.append(t[..., start:start + size])
        start += size
    return out


def _sliding_window_sink_attention(q, k, v, sinks):
    B, S, Hq, d = q.shape
    Hkv = k.shape[2]
    G = Hq // Hkv
    nb = S // Q_BLOCK
    qb = q.reshape(B, nb, Q_BLOCK, Hkv, G, d)
    kb = k.reshape(B, nb, Q_BLOCK, Hkv, d)
    vb = v.reshape(B, nb, Q_BLOCK, Hkv, d)
    pad = ((0, 0), (1, 0), (0, 0), (0, 0), (0, 0))
    kk = jnp.concatenate([jnp.pad(kb, pad)[:, :-1], kb], axis=2)
    vv = jnp.concatenate([jnp.pad(vb, pad)[:, :-1], vb], axis=2)
    s = jnp.einsum('bnqhgd,bnkhd->bnhgqk', qb, kk).astype(jnp.float32) * (d ** -0.5)
    qi = jnp.arange(Q_BLOCK)[:, None] + Q_BLOCK
    ki = jnp.arange(2 * Q_BLOCK)[None, :]
    band = (qi - ki >= 0) & (qi - ki < A_WINDOW)
    exists = (jnp.arange(nb)[:, None] * Q_BLOCK + ki - Q_BLOCK) >= 0
    mask = (band[None] & exists[:, None, :])[None, :, None, None]
    s = jnp.where(mask, s, -jnp.inf)
    sink = sinks.astype(jnp.float32).reshape(1, 1, Hkv, G, 1, 1)
    m = jnp.maximum(s.max(axis=-1, keepdims=True), sink)
    p = jnp.exp(s - m)
    p = p / (p.sum(axis=-1, keepdims=True) + jnp.exp(sink - m))
    o = jnp.einsum('bnhgqk,bnkhd->bnqhgd', p.astype(v.dtype), vv)
    return o.reshape(B, S, Hq * d)


def _stick_breaking_attention(q, k, v):
    B, S, H, d = q.shape
    nb = S // Q_BLOCK
    qs = q.reshape(B, nb, Q_BLOCK, H, d).swapaxes(0, 1)
    spos = jnp.arange(S)

    def block(args):
        qblk, n = args
        z = jnp.einsum('bqhd,bkhd->bhqk', qblk, k).astype(jnp.float32) * (d ** -0.5)
        tpos = n * Q_BLOCK + jnp.arange(Q_BLOCK)
        past = spos[None, :] < tpos[:, None]
        log_stay = jnp.where(past, jax.nn.log_sigmoid(-z), 0.0)
        log_between = lax.cumsum(log_stay, axis=3, reverse=True) - log_stay
        w = jnp.where(past, jnp.exp(jax.nn.log_sigmoid(z) + log_between), 0.0)
        return jnp.einsum('bhqk,bkhd->bqhd', w.astype(v.dtype), v)

    o = lax.map(block, (qs, jnp.arange(nb)))
    return o.swapaxes(0, 1).reshape(B, S, H * d)


def _moba_attention(q, k, v):
    B, S, Hq, d = q.shape
    Hkv = k.shape[2]
    G = Hq // Hkv
    nblk = max(-(-S // C_BLOCK), C_TOPK)
    padw = ((0, 0), (0, nblk * C_BLOCK - S), (0, 0), (0, 0))
    kblk = jnp.pad(k, padw).reshape(B, nblk, C_BLOCK, Hkv, d)
    vblk = jnp.pad(v, padw).reshape(B, nblk, C_BLOCK, Hkv, d)
    kmean = kblk.astype(jnp.float32).mean(axis=2)
    kblk_t = kblk.transpose(0, 3, 1, 2, 4)
    vblk_t = vblk.transpose(0, 3, 1, 2, 4)
    nq = S // Q_BLOCK
    qs = q.reshape(B, nq, Q_BLOCK, Hkv, G, d).swapaxes(0, 1)
    bi = jnp.arange(B)[:, None, None, None, None]
    hi = jnp.arange(Hkv)[None, None, :, None, None]
    scale = d ** -0.5

    def block(args):
        qblk, n = args
        t = n * Q_BLOCK + jnp.arange(Q_BLOCK)
        own = (n * Q_BLOCK) // C_BLOCK
        gate = jnp.einsum('bqhgd,bmhd->bqhgm', qblk.astype(jnp.float32), kmean)
        gate = jnp.where(jnp.arange(nblk) < own, gate, -jnp.inf)
        _, sel = lax.top_k(gate, C_TOPK)
        valid = jnp.arange(C_TOPK) < own
        kg = kblk_t[bi, hi, sel]
        vg = vblk_t[bi, hi, sel]
        s_sel = jnp.einsum('bqhgd,bqhgrcd->bqhgrc', qblk, kg).astype(jnp.float32) * scale
        s_sel = jnp.where(valid[:, None], s_sel, -jnp.inf).reshape(B, Q_BLOCK, Hkv, G, C_TOPK * C_BLOCK)
        k_own = lax.dynamic_index_in_dim(kblk, own, axis=1, keepdims=False)
        v_own = lax.dynamic_index_in_dim(vblk, own, axis=1, keepdims=False)
        s_own = jnp.einsum('bqhgd,bchd->bqhgc', qblk, k_own).astype(jnp.float32) * scale
        causal = (own * C_BLOCK + jnp.arange(C_BLOCK))[None, :] <= t[:, None]
        s_own = jnp.where(causal[None, :, None, None, :], s_own, -jnp.inf)
        p = jax.nn.softmax(jnp.concatenate([s_sel, s_own], axis=-1), axis=-1)
        p_sel = p[..., :C_TOPK * C_BLOCK].reshape(B, Q_BLOCK, Hkv, G, C_TOPK, C_BLOCK)
        p_own = p[..., C_TOPK * C_BLOCK:]
        return (jnp.einsum('bqhgrc,bqhgrcd->bqhgd', p_sel.astype(v.dtype), vg)
                + jnp.einsum('bqhgc,bchd->bqhgd', p_own.astype(v.dtype), v_own))

    o = lax.map(block, (qs, jnp.arange(nq)))
    return o.swapaxes(0, 1).reshape(B, S, Hq * d)


def _nsa_attention(q, k_c, v_c, k_s, v_s, k_w, v_w, gates, k_pe, k_w1, k_w2, v_pe, v_w1, v_w2):
    B, S, Hq, d = q.shape
    Hkv = k_c.shape[2]
    G = Hq // Hkv
    scale = d ** -0.5
    nc = (S - D_CMP_LEN) // D_CMP_STRIDE + 1
    c_start = jnp.arange(nc) * D_CMP_STRIDE
    c_end = c_start + D_CMP_LEN - 1
    win = c_start[:, None] + jnp.arange(D_CMP_LEN)[None, :]

    def compress(t, pe, w1, w2):
        tw = t[:, win] + pe[None, None, :, None, :]
        tw = tw.transpose(0, 1, 3, 2, 4).reshape(B, nc, Hkv, D_CMP_LEN * d)
        return jax.nn.gelu(tw @ w1) @ w2

    k_cmp = compress(k_c, k_pe, k_w1, k_w2)
    v_cmp = compress(v_c, v_pe, v_w1, v_w2)
    nsel = max(-(-S // D_SEL_LEN), D_SEL_TOPK)
    padw = ((0, 0), (0, nsel * D_SEL_LEN - S), (0, 0), (0, 0))
    ks_t = jnp.pad(k_s, padw).reshape(B, nsel, D_SEL_LEN, Hkv, d).transpose(0, 3, 1, 2, 4)
    vs_t = jnp.pad(v_s, padw).reshape(B, nsel, D_SEL_LEN, Hkv, d).transpose(0, 3, 1, 2, 4)
    b_start = jnp.arange(nsel) * D_SEL_LEN
    overlap = ((c_start[:, None] <= b_start[None, :] + D_SEL_LEN - 1)
               & (c_end[:, None] >= b_start[None, :])).astype(jnp.float32)
    wpad = ((0, 0), (D_WINDOW, 0), (0, 0), (0, 0))
    kw_p = jnp.pad(k_w, wpad)
    vw_p = jnp.pad(v_w, wpad)
    nq = S // Q_BLOCK
    qs = q.reshape(B, nq, Q_BLOCK, Hkv, G, d).swapaxes(0, 1)
    gs = gates.reshape(B, nq, Q_BLOCK, Hkv, G, 3).swapaxes(0, 1)
    bi = jnp.arange(B)[:, None, None, None]
    hi = jnp.arange(Hkv)[None, None, :, None]
    jsel = jnp.arange(nsel)

    def block(args):
        qblk, gblk, n = args
        t = n * Q_BLOCK + jnp.arange(Q_BLOCK)
        s_c = jnp.einsum('bqhgd,bchd->bqhgc', qblk, k_cmp).astype(jnp.float32) * scale
        c_ok = (c_end[None, :] <= t[:, None])[None, :, None, None, :]
        s_c = jnp.where(c_ok, s_c, -jnp.inf)
        m = s_c.max(axis=-1, keepdims=True)
        m = jnp.where(jnp.isfinite(m), m, 0.0)
        e = jnp.exp(s_c - m)
        den = e.sum(axis=-1, keepdims=True)
        p_c = e / jnp.where(den > 0, den, 1.0)
        o_c = jnp.einsum('bqhgc,bchd->bqhgd', p_c.astype(v_cmp.dtype), v_cmp)
        imp = jnp.einsum('bqhgc,cj->bqhj', p_c, overlap)
        cur = t // D_SEL_LEN
        allowed = b_start[None, :] <= t[:, None]
        forced = (jsel[None, :] == 0) | (jsel[None, :] == cur[:, None]) | (jsel[None, :] == cur[:, None] - 1)
        imp = jnp.where(allowed[None, :, None, :], imp, -jnp.inf)
        imp = jnp.where((allowed & forced)[None, :, None, :], jnp.inf, imp)
        top_v, sel = lax.top_k(imp, D_SEL_TOPK)
        kg = ks_t[bi, hi, sel]
        vg = vs_t[bi, hi, sel]
        s_s = jnp.einsum('bqhgd,bqhrld->bqhgrl', qblk, kg).astype(jnp.float32) * scale
        kpos = sel[..., None] * D_SEL_LEN + jnp.arange(D_SEL_LEN)
        s_ok = (top_v > -jnp.inf)[..., None] & (kpos <= t[None, :, None, None, None])
        s_s = jnp.where(s_ok[:, :, :, None], s_s, -jnp.inf).reshape(B, Q_BLOCK, Hkv, G, D_SEL_TOPK * D_SEL_LEN)
        p_s = jax.nn.softmax(s_s, axis=-1).reshape(B, Q_BLOCK, Hkv, G, D_SEL_TOPK, D_SEL_LEN)
        o_s = jnp.einsum('bqhgrl,bqhrld->bqhgd', p_s.astype(vg.dtype), vg)
        k_win = lax.dynamic_slice_in_dim(kw_p, n * Q_BLOCK, D_WINDOW + Q_BLOCK, axis=1)
        v_win = lax.dynamic_slice_in_dim(vw_p, n * Q_BLOCK, D_WINDOW + Q_BLOCK, axis=1)
        s_w = jnp.einsum('bqhgd,bkhd->bqhgk', qblk, k_win).astype(jnp.float32) * scale
        wpos = n * Q_BLOCK - D_WINDOW + jnp.arange(D_WINDOW + Q_BLOCK)
        diff = t[:, None] - wpos[None, :]
        w_ok = (diff >= 0) & (diff < D_WINDOW) & (wpos[None, :] >= 0)
        s_w = jnp.where(w_ok[None, :, None, None, :], s_w, -jnp.inf)
        p_w = jax.nn.softmax(s_w, axis=-1)
        o_w = jnp.einsum('bqhgk,bkhd->bqhgd', p_w.astype(v_win.dtype), v_win)
        g = jax.nn.sigmoid(gblk.astype(jnp.float32))
        o = g[..., 0:1] * o_c + g[..., 1:2] * o_s + g[..., 2:3] * o_w
        return o.astype(q.dtype)

    o = lax.map(block, (qs, gs, jnp.arange(nq)))
    return o.swapaxes(0, 1).reshape(B, S, Hq * d)


def _mixer_ab(h, cos, sin, w_in, sinks, w_out):
    qa, ka, va, qb, kb, vb = _split_cols(h @ w_in, AB_SPLITS)
    oa = _sliding_window_sink_attention(_rope(_heads(qa), cos, sin), _rope(_heads(ka), cos, sin), _heads(va), sinks)
    ob = _stick_breaking_attention(_heads(qb), _heads(kb), _heads(vb))
    return jnp.concatenate([oa, ob], axis=-1) @ w_out


def _mixer_cd(h, cos, sin, w_in, k_pe, k_w1, k_w2, v_pe, v_w1, v_w2, w_out):
    qc, kc, vc, qd, kdc, vdc, kds, vds, kdw, vdw, gd = _split_cols(h @ w_in, CD_SPLITS)
    rp = lambda t: _rope(_heads(t), cos, sin)
    oc = _moba_attention(rp(qc), rp(kc), _heads(vc))
    od = _nsa_attention(rp(qd), rp(kdc), _heads(vdc), rp(kds), _heads(vds), rp(kdw), _heads(vdw), gd,
                        k_pe, k_w1, k_w2, v_pe, v_w1, v_w2)
    return jnp.concatenate([oc, od], axis=-1) @ w_out


def _hier_moe(h, router_group, router_expert, w_gate, w_up, w_down):
    B, S, D = h.shape
    n_tok = B * S
    hf = h.reshape(n_tok, D)
    gp = jax.nn.softmax((hf @ router_group).astype(jnp.float32), axis=-1)
    g_prob, g_idx = lax.top_k(gp, 1)
    el = (hf @ router_expert).astype(jnp.float32).reshape(n_tok, N_GROUPS, EXPERTS_PER_GROUP)
    el = jnp.take_along_axis(el, g_idx[:, :, None], axis=1)[:, 0]
    e_prob, e_idx = lax.top_k(jax.nn.softmax(el, axis=-1), MOE_TOPK)
    gate = g_prob * e_prob / e_prob.sum(axis=-1, keepdims=True)
    expert = g_idx * EXPERTS_PER_GROUP + e_idx
    n_asg = n_tok * MOE_TOPK
    e_flat = expert.reshape(n_asg)
    order = jnp.argsort(e_flat)
    e_sorted = e_flat[order]
    tok_sorted = (order // MOE_TOPK).astype(jnp.int32)
    w_sorted = gate.reshape(n_asg)[order]
    counts = jnp.zeros((N_EXPERTS,), jnp.int32).at[e_flat].add(1)
    padded = (counts + MOE_BLOCK - 1) // MOE_BLOCK * MOE_BLOCK
    pad_end = jnp.cumsum(padded)
    pad_start = pad_end - padded
    start = jnp.cumsum(counts) - counts
    dest = pad_start[e_sorted] + jnp.arange(n_asg) - start[e_sorted]
    n_rows = n_asg + N_EXPERTS * MOE_BLOCK
    n_blk = n_rows // MOE_BLOCK
    row_tok = jnp.full((n_rows,), n_tok, jnp.int32).at[dest].set(tok_sorted)
    row_w = jnp.zeros((n_rows,), jnp.float32).at[dest].set(w_sorted)
    blk_expert = jnp.minimum(jnp.searchsorted(pad_end, jnp.arange(n_blk) * MOE_BLOCK, side='right'), N_EXPERTS - 1)
    h_pad = jnp.concatenate([hf, jnp.zeros((1, D), hf.dtype)], axis=0)
    rows = h_pad[row_tok].reshape(n_blk, MOE_BLOCK, D)

    def expert_block(args):
        xb, e = args
        return (jax.nn.silu(xb @ w_gate[e]) * (xb @ w_up[e])) @ w_down[e]

    y = lax.map(expert_block, (rows, blk_expert)).reshape(n_rows, D)
    out = jnp.zeros((n_tok + 1, D), jnp.float32).at[row_tok].add(y.astype(jnp.float32) * row_w[:, None])
    return out[:n_tok].reshape(B, S, D).astype(h.dtype)


def setup_inputs(seed: int = 0) -> dict:
    key = jax.random.key(seed)
    keys = iter(jax.random.split(key, 40))

    def nrm(shape, scale):
        return jax.random.normal(next(keys), shape, jnp.float32) * scale

    def gain():
        return 1.0 + nrm((D_MODEL,), 0.02)

    def moe(suffix):
        return {
            'router_group' + suffix: nrm((D_MODEL, N_GROUPS), D_MODEL ** -0.5),
            'router_expert' + suffix: nrm((D_MODEL, N_EXPERTS), D_MODEL ** -0.5),
            'expert_gate' + suffix: nrm((N_EXPERTS, D_MODEL, EXPERT_HIDDEN), D_MODEL ** -0.5),
            'expert_up' + suffix: nrm((N_EXPERTS, D_MODEL, EXPERT_HIDDEN), D_MODEL ** -0.5),
            'expert_down' + suffix: nrm((N_EXPERTS, EXPERT_HIDDEN, D_MODEL), EXPERT_HIDDEN ** -0.5),
        }

    x = nrm((BATCH, SEQ, D_MODEL), 1.0)
    positions = jnp.arange(SEQ, dtype=jnp.int32)[None, :] + jax.random.randint(next(keys), (BATCH, 1), 0, MAX_POS_OFFSET, dtype=jnp.int32)
    cmp_in = D_CMP_LEN * HEAD_DIM
    inputs = {'x': x, 'positions': positions}
    inputs['ln_mix_0'] = gain()
    inputs['w_in_0'] = nrm((D_MODEL, AB_COLS), D_MODEL ** -0.5)
    inputs['sinks_0'] = nrm((A_HEADS,), 0.5)
    inputs['w_out_0'] = nrm((AB_OUT, D_MODEL), AB_OUT ** -0.5)
    inputs['ln_ffn_0'] = gain()
    inputs.update(moe('_0'))
    inputs['ln_mix_1'] = gain()
    inputs['w_in_1'] = nrm((D_MODEL, CD_COLS), D_MODEL ** -0.5)
    inputs['cmp_k_pe_1'] = nrm((D_CMP_LEN, HEAD_DIM), 0.1)
    inputs['cmp_k_w1_1'] = nrm((cmp_in, D_CMP_HIDDEN), cmp_in ** -0.5)
    inputs['cmp_k_w2_1'] = nrm((D_CMP_HIDDEN, HEAD_DIM), D_CMP_HIDDEN ** -0.5)
    inputs['cmp_v_pe_1'] = nrm((D_CMP_LEN, HEAD_DIM), 0.1)
    inputs['cmp_v_w1_1'] = nrm((cmp_in, D_CMP_HIDDEN), cmp_in ** -0.5)
    inputs['cmp_v_w2_1'] = nrm((D_CMP_HIDDEN, HEAD_DIM), D_CMP_HIDDEN ** -0.5)
    inputs['w_out_1'] = nrm((CD_OUT, D_MODEL), CD_OUT ** -0.5)
    inputs['ln_ffn_1'] = gain()
    inputs.update(moe('_1'))
    inputs['ln_final'] = gain()
    return inputs


def reference(x, positions, ln_mix_0, w_in_0, sinks_0, w_out_0, ln_ffn_0, router_group_0, router_expert_0, expert_gate_0, expert_up_0, expert_down_0, ln_mix_1, w_in_1, cmp_k_pe_1, cmp_k_w1_1, cmp_k_w2_1, cmp_v_pe_1, cmp_v_w1_1, cmp_v_w2_1, w_out_1, ln_ffn_1, router_group_1, router_expert_1, expert_gate_1, expert_up_1, expert_down_1, ln_final):
    cos, sin = _rope_tables(positions)
    mixers = (
        lambda h: _mixer_ab(h, cos, sin, w_in_0, sinks_0, w_out_0),
        lambda h: _mixer_cd(h, cos, sin, w_in_1, cmp_k_pe_1, cmp_k_w1_1, cmp_k_w2_1, cmp_v_pe_1, cmp_v_w1_1, cmp_v_w2_1, w_out_1),
    )
    mix_norms = (ln_mix_0, ln_mix_1)
    ffn_norms = (ln_ffn_0, ln_ffn_1)
    ffns = ((router_group_0, router_expert_0, expert_gate_0, expert_up_0, expert_down_0),
            (router_group_1, router_expert_1, expert_gate_1, expert_up_1, expert_down_1))
    for layer in range(DEPTH):
        x = x + mixers[layer](_rmsnorm(x, mix_norms[layer]))
        x = x + _hier_moe(_rmsnorm(x, ffn_norms[layer]), *ffns[layer])
    return _rmsnorm(x, ln_final)
```

```python
import functools

import jax
import jax.numpy as jnp
from jax import lax
from jax.experimental import pallas as pl
from jax.experimental.pallas import tpu as pltpu

D_MODEL = 1024
HEAD_DIM = 64
HALF = HEAD_DIM // 2
ROPE_THETA = 10000.0
NORM_EPS = 1e-6
Q_BLOCK = 128
SCALE = HEAD_DIM ** -0.5

A_HEADS, A_KV_HEADS, A_WINDOW = 8, 2, 128
B_HEADS = 8
C_HEADS, C_KV_HEADS, C_BLOCK, C_TOPK = 8, 2, 256, 3
D_HEADS, D_KV_HEADS = 8, 2
D_CMP_LEN, D_CMP_STRIDE, D_CMP_HIDDEN = 32, 16, 256
D_SEL_LEN, D_SEL_TOPK, D_WINDOW = 64, 16, 512
N_GROUPS, EXPERTS_PER_GROUP, MOE_TOPK, EXPERT_HIDDEN = 4, 16, 2, 512
N_EXPERTS = N_GROUPS * EXPERTS_PER_GROUP
GROUP = 4

LANES = 128
ROW_TILE = 512
MOE_ROWS = 256
SEL_KEY_TILE = 512
SEL_BIAS_PAD = 16
MASK_BIG = 32768.0
SB_EXIT = -104.0
VMEM_LIMIT = 56 * 1024 * 1024

F32 = jnp.float32
BF16 = jnp.bfloat16
NEG_INF = float("-inf")


def _iota(shape, dim):
    return lax.broadcasted_iota(jnp.int32, shape, dim)


def _dot(a, b):
    return jnp.dot(a, b, preferred_element_type=F32)


def _params(n_grid):
    return pltpu.CompilerParams(dimension_semantics=("arbitrary",) * n_grid, vmem_limit_bytes=VMEM_LIMIT)


def _in_proj_kernel(rope_chunks, kmean_chunk, gate_chunk, combine, *refs):
    refs = list(refs)
    x_ref = refs.pop(0)
    if combine:
        y0_ref, y1_ref = refs.pop(0), refs.pop(0)
    g_ref, w_ref, cos_ref, sin_ref = refs[:4]
    outs = refs[4:]
    x = x_ref[...]
    if combine:
        x = x + y0_ref[...] + y1_ref[...]
        xo_ref = outs.pop(0)
        xo_ref[...] = x
    o_ref = outs.pop(0)
    ms = jnp.mean(x * x, axis=-1, keepdims=True)
    h = (x * lax.rsqrt(ms + NORM_EPS) * g_ref[...]).astype(BF16)
    tm = x.shape[0]
    n_cols = w_ref.shape[1]
    cos = cos_ref[...]
    sin = sin_ref[...]
    first_half = (_iota((tm, LANES), 1) & (HEAD_DIM - 1)) < HALF
    for c in range(n_cols // LANES):
        ch = _dot(h, w_ref[:, c * LANES:(c + 1) * LANES])
        if c in rope_chunks:
            partner = jnp.where(first_half, pltpu.roll(ch, LANES - HALF, 1), pltpu.roll(ch, HALF, 1))
            ch = ch * cos + partner * sin
        if c == kmean_chunk:
            km_ref = outs[0]
            km_ref[...] = jnp.mean(ch.reshape(tm // C_BLOCK, C_BLOCK, LANES), axis=1).reshape(tm // C_BLOCK, 1, LANES)
        if c == gate_chunk:
            outs[-1][...] = ch
        o_ref[:, c * LANES:(c + 1) * LANES] = ch.astype(BF16)


def _in_proj(x, ys, gain, w, cos_t, sin_t, rope_chunks, kmean_chunk=None, gate_chunk=None):
    T = x.shape[0]
    n_cols = w.shape[1]
    tm = ROW_TILE
    combine = ys is not None
    row = lambda i: (i, 0)
    fixed = lambda i: (0, 0)
    in_specs = [pl.BlockSpec((tm, D_MODEL), row)]
    args = [x]
    if combine:
        in_specs += [pl.BlockSpec((tm, D_MODEL), row)] * 2
        args += list(ys)
    in_specs += [pl.BlockSpec((1, D_MODEL), fixed), pl.BlockSpec((D_MODEL, n_cols), fixed),
                 pl.BlockSpec((tm, LANES), row), pl.BlockSpec((tm, LANES), row)]
    args += [gain.reshape(1, D_MODEL), w, cos_t, sin_t]
    out_shape, out_specs = [], []
    if combine:
        out_shape.append(jax.ShapeDtypeStruct((T, D_MODEL), F32))
        out_specs.append(pl.BlockSpec((tm, D_MODEL), row))
    out_shape.append(jax.ShapeDtypeStruct((T, n_cols), BF16))
    out_specs.append(pl.BlockSpec((tm, n_cols), row))
    if kmean_chunk is not None:
        out_shape.append(jax.ShapeDtypeStruct((T // C_BLOCK, 1, LANES), F32))
        out_specs.append(pl.BlockSpec((tm // C_BLOCK, 1, LANES), lambda i: (i, 0, 0)))
    if gate_chunk is not None:
        out_shape.append(jax.ShapeDtypeStruct((T, LANES), F32))
        out_specs.append(pl.BlockSpec((tm, LANES), row))
    kern = functools.partial(_in_proj_kernel, tuple(rope_chunks), kmean_chunk, gate_chunk, combine)
    return pl.pallas_call(kern, grid=(T // tm,), in_specs=in_specs, out_specs=out_specs, out_shape=out_shape,
                          compiler_params=_params(1), name="in_proj")(*args)


def _swa_kernel(q_ref, kp_ref, ko_ref, vp_ref, vo_ref, sink_ref, o_ref):
    n = pl.program_id(2)
    rows = GROUP * Q_BLOCK
    q = q_ref[0].reshape(rows, HEAD_DIM)
    kT = jnp.concatenate([kp_ref[0, 0], ko_ref[0, 0]], axis=1)
    v = jnp.concatenate([vp_ref[0, 0], vo_ref[0, 0]], axis=0)
    s = _dot(q, kT) * SCALE
    qpos = _iota((rows, 2 * Q_BLOCK), 0) & (Q_BLOCK - 1)
    ki = _iota((rows, 2 * Q_BLOCK), 1)
    d = qpos + Q_BLOCK - ki
    ok = (d >= 0) & (d < A_WINDOW) & ((ki >= Q_BLOCK) | (n > 0))
    s = jnp.where(ok, s, NEG_INF)
    sink = sink_ref[0]
    m = jnp.maximum(jnp.max(s, axis=-1, keepdims=True), sink)
    p = jnp.exp(s - m)
    den = jnp.sum(p, axis=-1, keepdims=True) + jnp.exp(sink - m)
    o = _dot(p.astype(BF16), v) / den
    o_ref[0] = o.reshape(GROUP, Q_BLOCK, HEAD_DIM).astype(o_ref.dtype)


def _swa_attention(q, kT, v, sinks):
    B, Hq, S, _ = q.shape
    Hkv = kT.shape[1]
    nb = S // Q_BLOCK
    sink_col = jnp.repeat(sinks.astype(F32).reshape(Hkv, GROUP), Q_BLOCK, axis=1).reshape(Hkv, GROUP * Q_BLOCK, 1)
    prev = lambda n: jnp.maximum(n - 1, 0)
    in_specs = [
        pl.BlockSpec((1, GROUP, Q_BLOCK, HEAD_DIM), lambda b, h, n: (b, h, n, 0)),
        pl.BlockSpec((1, 1, HEAD_DIM, Q_BLOCK), lambda b, h, n: (b, h, 0, prev(n))),
        pl.BlockSpec((1, 1, HEAD_DIM, Q_BLOCK), lambda b, h, n: (b, h, 0, n)),
        pl.BlockSpec((1, 1, Q_BLOCK, HEAD_DIM), lambda b, h, n: (b, h, prev(n), 0)),
        pl.BlockSpec((1, 1, Q_BLOCK, HEAD_DIM), lambda b, h, n: (b, h, n, 0)),
        pl.BlockSpec((1, GROUP * Q_BLOCK, 1), lambda b, h, n: (h, 0, 0)),
    ]
    return pl.pallas_call(
        _swa_kernel, grid=(B, Hkv, nb), in_specs=in_specs,
        out_specs=pl.BlockSpec((1, GROUP, Q_BLOCK, HEAD_DIM), lambda b, h, n: (b, h, n, 0)),
        out_shape=jax.ShapeDtypeStruct((B, Hq, S, HEAD_DIM), BF16),
        compiler_params=_params(3), name="swa_attention")(q, kT, kT, v, v, sink_col)


def _stick_kernel(q_ref, kT_ref, v_ref, o_ref):
    n = pl.program_id(2)
    tq = tk = Q_BLOCK
    q = q_ref[0, 0]
    row = _iota((tq, tk), 0)
    col = _iota((tk, tk), 1)
    upper = jnp.where(_iota((tk, tk), 0) > col, 1.0, 0.0).astype(BF16)
    tpos = n * tq + row

    def body(carry):
        kb, _, c, acc = carry
        start = pl.multiple_of(kb * tk, tk)
        kT = kT_ref[0, 0, :, pl.ds(start, tk)]
        v = v_ref[0, 0, pl.ds(start, tk), :]
        z = _dot(q, kT) * SCALE
        past = (start + col) < tpos
        sp = jnp.maximum(z, 0.0) + jnp.log1p(jnp.exp(-jnp.abs(z)))
        log_stay = jnp.where(past, -sp, 0.0)
        hi = log_stay.astype(BF16)
        lo = (log_stay - hi.astype(F32)).astype(BF16)
        between = _dot(hi, upper) + _dot(lo, upper)
        w = jnp.where(past, jnp.exp(z - sp + between + c), 0.0)
        acc = acc + _dot(w.astype(BF16), v)
        c = c + jnp.sum(log_stay, axis=-1, keepdims=True)
        return kb - 1, jnp.max(c) > SB_EXIT, c, acc

    def cond(carry):
        kb, alive, _, _ = carry
        return (kb >= 0) & alive

    init = (n, jnp.array(True), jnp.zeros((tq, 1), F32), jnp.zeros((tq, HEAD_DIM), F32))
    _, _, _, acc = lax.while_loop(cond, body, init)
    o_ref[0, 0] = acc.astype(o_ref.dtype)


def _stick_attention(q, kT, v):
    B, H, S, _ = q.shape
    nq = S // Q_BLOCK
    in_specs = [
        pl.BlockSpec((1, 1, Q_BLOCK, HEAD_DIM), lambda b, h, n: (b, h, n, 0)),
        pl.BlockSpec((1, 1, HEAD_DIM, S), lambda b, h, n: (b, h, 0, 0)),
        pl.BlockSpec((1, 1, S, HEAD_DIM), lambda b, h, n: (b, h, 0, 0)),
    ]
    return pl.pallas_call(
        _stick_kernel, grid=(B, H, nq), in_specs=in_specs,
        out_specs=pl.BlockSpec((1, 1, Q_BLOCK, HEAD_DIM), lambda b, h, n: (b, h, n, 0)),
        out_shape=jax.ShapeDtypeStruct((B, H, S, HEAD_DIM), BF16),
        compiler_params=_params(3), name="stick_attention")(q, kT, v)


def _online_update(s, v, m, l, acc):
    m_new = jnp.maximum(m, jnp.max(s, axis=-1, keepdims=True))
    alpha = jnp.exp(m - m_new)
    p = jnp.exp(s - m_new)
    l = alpha * l + jnp.sum(p, axis=-1, keepdims=True)
    acc = alpha * acc + _dot(p.astype(BF16), v)
    return m_new, l, acc


def _moba_kernel(q_ref, kT_ref, v_ref, kmT_ref, o_ref):
    n = pl.program_id(2)
    rows = GROUP * Q_BLOCK
    nblk = kmT_ref.shape[-1]
    q = q_ref[0].reshape(rows, HEAD_DIM)
    own = (n * Q_BLOCK) // C_BLOCK
    gate = _dot(q, kmT_ref[0, 0])
    blk = _iota((rows, nblk), 1)
    gate = jnp.where(blk < own, gate, NEG_INF)
    bias = jnp.full((rows, nblk), -1e30, F32)
    for _ in range(C_TOPK):
        mx = jnp.max(gate, axis=-1, keepdims=True)
        idx = jnp.min(jnp.where(gate == mx, blk, nblk), axis=-1, keepdims=True)
        hit = blk == idx
        bias = jnp.where(hit & (mx > NEG_INF), 0.0, bias)
        gate = jnp.where(hit, NEG_INF, gate)
    start = pl.multiple_of(own * C_BLOCK, C_BLOCK)
    s = _dot(q, kT_ref[0, 0, :, pl.ds(start, C_BLOCK)]) * SCALE
    tpos = n * Q_BLOCK + (_iota((rows, C_BLOCK), 0) & (Q_BLOCK - 1))
    s = jnp.where(start + _iota((rows, C_BLOCK), 1) <= tpos, s, NEG_INF)
    m = jnp.max(s, axis=-1, keepdims=True)
    p = jnp.exp(s - m)
    l = jnp.sum(p, axis=-1, keepdims=True)
    acc = _dot(p.astype(BF16), v_ref[0, 0, pl.ds(start, C_BLOCK), :])

    def body(kb, carry):
        m, l, acc = carry
        colb = jnp.sum(jnp.where(blk == kb, bias, 0.0), axis=-1, keepdims=True)

        def attend(_):
            st = pl.multiple_of(kb * C_BLOCK, C_BLOCK)
            sb = _dot(q, kT_ref[0, 0, :, pl.ds(st, C_BLOCK)]) * SCALE + colb
            return _online_update(sb, v_ref[0, 0, pl.ds(st, C_BLOCK), :], m, l, acc)

        return lax.cond(jnp.max(colb) > -1.0, attend, lambda _: (m, l, acc), 0)

    m, l, acc = lax.fori_loop(0, own, body, (m, l, acc))
    o_ref[0] = (acc / l).reshape(GROUP, Q_BLOCK, HEAD_DIM).astype(o_ref.dtype)


def _moba_attention(q, kT, v, kmT):
    B, Hq, S, _ = q.shape
    Hkv = kT.shape[1]
    nq = S // Q_BLOCK
    nblk = kmT.shape[-1]
    in_specs = [
        pl.BlockSpec((1, GROUP, Q_BLOCK, HEAD_DIM), lambda b, h, n: (b, h, n, 0)),
        pl.BlockSpec((1, 1, HEAD_DIM, S), lambda b, h, n: (b, h, 0, 0)),
        pl.BlockSpec((1, 1, S, HEAD_DIM), lambda b, h, n: (b, h, 0, 0)),
        pl.BlockSpec((1, 1, HEAD_DIM, nblk), lambda b, h, n: (b, h, 0, 0)),
    ]
    return pl.pallas_call(
        _moba_kernel, grid=(B, Hkv, nq), in_specs=in_specs,
        out_specs=pl.BlockSpec((1, GROUP, Q_BLOCK, HEAD_DIM), lambda b, h, n: (b, h, n, 0)),
        out_shape=jax.ShapeDtypeStruct((B, Hq, S, HEAD_DIM), BF16),
        compiler_params=_params(3), name="moba_attention")(q, kT, v, kmT)


def _compress_kernel(u_ref, us_ref, pe_ref, w1_ref, w2_ref, o_ref):
    a = (u_ref[0, 0].astype(F32) + pe_ref[0:1, :]).astype(BF16)
    b = (us_ref[0, 0].astype(F32) + pe_ref[1:2, :]).astype(BF16)
    pre = _dot(a, w1_ref[0]) + _dot(b, w1_ref[1])
    hid = jax.nn.gelu(pre)
    o_ref[0, 0] = _dot(hid.astype(BF16), w2_ref[...]).astype(o_ref.dtype)


def _compress(t, pe, w1, w2):
    B, H, S, _ = t.shape
    nrow = S // D_CMP_STRIDE
    width = D_CMP_STRIDE * HEAD_DIM
    u = t.reshape(B, H, nrow, width)
    us = jnp.concatenate([u[:, :, 1:], jnp.zeros((B, H, 1, width), u.dtype)], axis=2)
    blk = lambda b, h: (b, h, 0, 0)
    in_specs = [
        pl.BlockSpec((1, 1, nrow, width), blk), pl.BlockSpec((1, 1, nrow, width), blk),
        pl.BlockSpec((2, width), lambda b, h: (0, 0)),
        pl.BlockSpec((2, width, D_CMP_HIDDEN), lambda b, h: (0, 0, 0)),
        pl.BlockSpec((D_CMP_HIDDEN, HEAD_DIM), lambda b, h: (0, 0)),
    ]
    return pl.pallas_call(
        _compress_kernel, grid=(B, H), in_specs=in_specs,
        out_specs=pl.BlockSpec((1, 1, nrow, HEAD_DIM), blk),
        out_shape=jax.ShapeDtypeStruct((B, H, nrow, HEAD_DIM), BF16),
        compiler_params=_params(2), name="nsa_compress")(
            u, us, pe.astype(F32).reshape(2, width), w1.astype(BF16).reshape(2, width, D_CMP_HIDDEN), w2.astype(BF16))


def _cmp_select_kernel(nc, q_ref, kcT_ref, vc_ref, ov_ref, oc_ref, bias_ref):
    n = pl.program_id(2)
    rows = GROUP * Q_BLOCK
    ncp = kcT_ref.shape[-1]
    nsel = ov_ref.shape[-1]
    q = q_ref[0].reshape(rows, HEAD_DIM)
    s = _dot(q, kcT_ref[0, 0]) * SCALE
    tpos = n * Q_BLOCK + (_iota((rows, ncp), 0) & (Q_BLOCK - 1))
    c = _iota((rows, ncp), 1)
    ok = (c * D_CMP_STRIDE + (D_CMP_LEN - 1) <= tpos) & (c < nc)
    s = jnp.where(ok, s, NEG_INF)
    m = jnp.max(s, axis=-1, keepdims=True)
    m = jnp.where(m > NEG_INF, m, 0.0)
    e = jnp.exp(s - m)
    den = jnp.sum(e, axis=-1, keepdims=True)
    p = (e / jnp.where(den > 0, den, 1.0)).astype(BF16)
    oc_ref[0] = _dot(p, vc_ref[0, 0]).reshape(GROUP, Q_BLOCK, HEAD_DIM)
    imp = _dot(p[0:Q_BLOCK], ov_ref[...])
    for g in range(1, GROUP):
        imp = imp + _dot(p[g * Q_BLOCK:(g + 1) * Q_BLOCK], ov_ref[...])
    t = n * Q_BLOCK + _iota((Q_BLOCK, nsel), 0)
    j = _iota((Q_BLOCK, nsel), 1)
    cur = t >> 6
    allowed = j * D_SEL_LEN <= t
    forced = (j == 0) | (j == cur) | (j == cur - 1)
    imp = jnp.where(allowed, imp, NEG_INF)
    imp = jnp.where(allowed & forced, float("inf"), imp)
    bias = jnp.full((Q_BLOCK, nsel), -MASK_BIG, F32)
    for _ in range(D_SEL_TOPK):
        mx = jnp.max(imp, axis=-1, keepdims=True)
        idx = jnp.min(jnp.where(imp == mx, j, nsel), axis=-1, keepdims=True)
        hit = j == idx
        bias = jnp.where(hit & (mx > NEG_INF), 0.0, bias)
        imp = jnp.where(hit, NEG_INF, imp)
    bias_ref[0, 0] = bias.astype(bias_ref.dtype)


def _cmp_select(q, kcT, vc, overlap, nc):
    B, Hq, S, _ = q.shape
    Hkv = kcT.shape[1]
    nq = S // Q_BLOCK
    ncp = kcT.shape[-1]
    nsel = overlap.shape[-1]
    in_specs = [
        pl.BlockSpec((1, GROUP, Q_BLOCK, HEAD_DIM), lambda b, h, n: (b, h, n, 0)),
        pl.BlockSpec((1, 1, HEAD_DIM, ncp), lambda b, h, n: (b, h, 0, 0)),
        pl.BlockSpec((1, 1, ncp, HEAD_DIM), lambda b, h, n: (b, h, 0, 0)),
        pl.BlockSpec((ncp, nsel), lambda b, h, n: (0, 0)),
    ]
    out_specs = [
        pl.BlockSpec((1, GROUP, Q_BLOCK, HEAD_DIM), lambda b, h, n: (b, h, n, 0)),
        pl.BlockSpec((1, 1, Q_BLOCK, nsel), lambda b, h, n: (b, h, n, 0)),
    ]
    out_shape = [jax.ShapeDtypeStruct((B, Hq, S, HEAD_DIM), F32), jax.ShapeDtypeStruct((B, Hkv, S, nsel), BF16)]
    return pl.pallas_call(
        functools.partial(_cmp_select_kernel, nc), grid=(B, Hkv, nq), in_specs=in_specs, out_specs=out_specs,
        out_shape=out_shape, compiler_params=_params(3), name="nsa_cmp_select")(q, kcT, vc, overlap)


def _sel_kernel(q_ref, kT_ref, v_ref, bias_ref, o_ref):
    n = pl.program_id(2)
    rows = GROUP * Q_BLOCK
    tk = SEL_KEY_TILE
    q = (q_ref[0].reshape(rows, HEAD_DIM).astype(F32) * SCALE).astype(BF16)
    expand = jnp.where((_iota((SEL_BIAS_PAD, tk), 1) >> 6) == _iota((SEL_BIAS_PAD, tk), 0), 1.0, 0.0).astype(BF16)
    tpos = n * Q_BLOCK + (_iota((rows, tk), 0) & (Q_BLOCK - 1))
    kcol = _iota((rows, tk), 1)
    n_tiles = (n * Q_BLOCK + Q_BLOCK - 1) // tk + 1

    def body(kt, carry):
        m, l, acc = carry
        st = pl.multiple_of(kt * tk, tk)
        b = bias_ref[0, 0, kt]
        qa = jnp.concatenate([q, jnp.concatenate([b] * GROUP, axis=0)], axis=1)
        ka = jnp.concatenate([kT_ref[0, 0, :, pl.ds(st, tk)], expand], axis=0)
        s = _dot(qa, ka)
        s = jnp.where(st + kcol <= tpos, s, -1e30)
        return _online_update(s, v_ref[0, 0, pl.ds(st, tk), :], m, l, acc)

    init = (jnp.full((rows, 1), -1e30, F32), jnp.zeros((rows, 1), F32), jnp.zeros((rows, HEAD_DIM), F32))
    m, l, acc = lax.fori_loop(0, n_tiles, body, init)
    o_ref[0] = (acc / l).reshape(GROUP, Q_BLOCK, HEAD_DIM)


def _sel_attention(q, kT, v, bias_tiles):
    B, Hq, S, _ = q.shape
    Hkv = kT.shape[1]
    nq = S // Q_BLOCK
    nkt = bias_tiles.shape[2]
    in_specs = [
        pl.BlockSpec((1, GROUP, Q_BLOCK, HEAD_DIM), lambda b, h, n: (b, h, n, 0)),
        pl.BlockSpec((1, 1, HEAD_DIM, S), lambda b, h, n: (b, h, 0, 0)),
        pl.BlockSpec((1, 1, S, HEAD_DIM), lambda b, h, n: (b, h, 0, 0)),
        pl.BlockSpec((1, 1, nkt, Q_BLOCK, SEL_BIAS_PAD), lambda b, h, n: (b, h, 0, n, 0)),
    ]
    return pl.pallas_call(
        _sel_kernel, grid=(B, Hkv, nq), in_specs=in_specs,
        out_specs=pl.BlockSpec((1, GROUP, Q_BLOCK, HEAD_DIM), lambda b, h, n: (b, h, n, 0)),
        out_shape=jax.ShapeDtypeStruct((B, Hq, S, HEAD_DIM), F32),
        compiler_params=_params(3), name="nsa_selected")(q, kT, v, bias_tiles)


def _win_kernel(span, q_ref, kT_ref, v_ref, o_ref):
    n = pl.program_id(2)
    rows = GROUP * Q_BLOCK
    q = q_ref[0].reshape(rows, HEAD_DIM)
    start = pl.multiple_of(jnp.maximum(n * Q_BLOCK + Q_BLOCK - span, 0), Q_BLOCK)
    s = _dot(q, kT_ref[0, 0, :, pl.ds(start, span)]) * SCALE
    tpos = n * Q_BLOCK + (_iota((rows, span), 0) & (Q_BLOCK - 1))
    d = tpos - (start + _iota((rows, span), 1))
    s = jnp.where((d >= 0) & (d < D_WINDOW), s, NEG_INF)
    m = jnp.max(s, axis=-1, keepdims=True)
    p = jnp.exp(s - m)
    l = jnp.sum(p, axis=-1, keepdims=True)
    o = _dot(p.astype(BF16), v_ref[0, 0, pl.ds(start, span), :]) / l
    o_ref[0] = o.reshape(GROUP, Q_BLOCK, HEAD_DIM)


def _win_attention(q, kT, v):
    B, Hq, S, _ = q.shape
    Hkv = kT.shape[1]
    nq = S // Q_BLOCK
    span = min(D_WINDOW + Q_BLOCK, S)
    in_specs = [
        pl.BlockSpec((1, GROUP, Q_BLOCK, HEAD_DIM), lambda b, h, n: (b, h, n, 0)),
        pl.BlockSpec((1, 1, HEAD_DIM, S), lambda b, h, n: (b, h, 0, 0)),
        pl.BlockSpec((1, 1, S, HEAD_DIM), lambda b, h, n: (b, h, 0, 0)),
    ]
    return pl.pallas_call(
        functools.partial(_win_kernel, span), grid=(B, Hkv, nq), in_specs=in_specs,
        out_specs=pl.BlockSpec((1, GROUP, Q_BLOCK, HEAD_DIM), lambda b, h, n: (b, h, n, 0)),
        out_shape=jax.ShapeDtypeStruct((B, Hq, S, HEAD_DIM), F32),
        compiler_params=_params(3), name="nsa_window")(q, kT, v)


def _route(logits):
    tm = logits.shape[0]
    lane = _iota((tm, LANES), 1)
    gl = jnp.where(lane < N_GROUPS, logits, NEG_INF)
    gmax = jnp.max(gl, axis=-1, keepdims=True)
    gidx = jnp.min(jnp.where(gl == gmax, lane, LANES), axis=-1, keepdims=True)
    g_prob = 1.0 / jnp.sum(jnp.exp(gl - gmax), axis=-1, keepdims=True)
    elane = lane - N_GROUPS
    in_group = (elane >= 0) & (elane < N_EXPERTS) & ((elane >> 4) == gidx)
    el = jnp.where(in_group, logits, NEG_INF)
    ee = jnp.exp(el - jnp.max(el, axis=-1, keepdims=True))
    ep = jnp.where(in_group, ee / jnp.sum(ee, axis=-1, keepdims=True), -1.0)
    p1 = jnp.max(ep, axis=-1, keepdims=True)
    i1 = jnp.min(jnp.where(ep == p1, lane, LANES), axis=-1, keepdims=True)
    ep2 = jnp.where(lane == i1, -1.0, ep)
    p2 = jnp.max(ep2, axis=-1, keepdims=True)
    i2 = jnp.min(jnp.where(ep2 == p2, lane, LANES), axis=-1, keepdims=True)
    den = p1 + p2
    vals = [(i1 - N_GROUPS).astype(F32), (i2 - N_GROUPS).astype(F32), g_prob * p1 / den, g_prob * p2 / den]
    out = jnp.zeros((tm, LANES), F32)
    for k, val in enumerate(vals):
        out = jnp.where(lane == k, val, out)
    return out


def _out_tail(x_new, gain_ref, wr_ref, xo_ref, h_ref, route_ref):
    xo_ref[...] = x_new
    ms = jnp.mean(x_new * x_new, axis=-1, keepdims=True)
    h = x_new * lax.rsqrt(ms + NORM_EPS) * gain_ref[...]
    h_ref[...] = h.astype(BF16)
    logits = jnp.dot(h, wr_ref[...], preferred_element_type=F32, precision=lax.Precision.HIGHEST)
    route_ref[...] = _route(logits)


def _out_proj_kernel(o_ref, x_ref, w_ref, gain_ref, wr_ref, xo_ref, h_ref, route_ref):
    x_new = x_ref[...] + _dot(o_ref[...], w_ref[...])
    _out_tail(x_new, gain_ref, wr_ref, xo_ref, h_ref, route_ref)


def _out_proj_nsa_kernel(oc_ref, b0_ref, b1_ref, b2_ref, gd_ref, x_ref, w_ref, gain_ref, wr_ref, xo_ref, h_ref, route_ref):
    half = D_HEADS * HEAD_DIM
    g = jax.nn.sigmoid(gd_ref[...])
    g_hi = g.astype(BF16)
    g_lo = (g - g_hi.astype(F32)).astype(BF16)
    src = _iota((LANES, half), 0)
    head3 = (_iota((LANES, half), 1) >> 6) * 3
    od = None
    for br, b_ref in enumerate((b0_ref, b1_ref, b2_ref)):
        spread = jnp.where(src == head3 + br, 1.0, 0.0).astype(BF16)
        term = (_dot(g_hi, spread) + _dot(g_lo, spread)) * b_ref[...]
        od = term if od is None else od + term
    x_new = x_ref[...] + _dot(oc_ref[...], w_ref[0:half, :]) + _dot(od.astype(BF16), w_ref[half:, :])
    _out_tail(x_new, gain_ref, wr_ref, xo_ref, h_ref, route_ref)


def _out_proj(o_parts, gd, x, w_out, gain, w_router):
    T = x.shape[0]
    tm = ROW_TILE
    row = lambda i: (i, 0)
    fixed = lambda i: (0, 0)
    if gd is None:
        kern = _out_proj_kernel
        args = [o_parts[0]]
        in_specs = [pl.BlockSpec((tm, D_MODEL), row)]
    else:
        kern = _out_proj_nsa_kernel
        args = list(o_parts) + [gd]
        in_specs = [pl.BlockSpec((tm, D_MODEL // 2), row)] * 4 + [pl.BlockSpec((tm, LANES), row)]
    args += [x, w_out, gain.reshape(1, D_MODEL), w_router]
    in_specs += [pl.BlockSpec((tm, D_MODEL), row), pl.BlockSpec((D_MODEL, D_MODEL), fixed),
                 pl.BlockSpec((1, D_MODEL), fixed), pl.BlockSpec((D_MODEL, LANES), fixed)]
    out_shape = [jax.ShapeDtypeStruct((T, D_MODEL), F32), jax.ShapeDtypeStruct((T, D_MODEL), BF16),
                 jax.ShapeDtypeStruct((T, LANES), F32)]
    out_specs = [pl.BlockSpec((tm, D_MODEL), row), pl.BlockSpec((tm, D_MODEL), row), pl.BlockSpec((tm, LANES), row)]
    return pl.pallas_call(kern, grid=(T // tm,), in_specs=in_specs, out_specs=out_specs, out_shape=out_shape,
                          compiler_params=_params(1), name="out_proj_router")(*args)


def _expert_kernel(be_ref, rows_ref, rw_ref, wg_ref, wu_ref, wd_ref, y_ref):
    del be_ref
    xb = rows_ref[...]
    hid = jax.nn.silu(_dot(xb, wg_ref[0])) * _dot(xb, wu_ref[0])
    y_ref[...] = _dot(hid.astype(BF16), wd_ref[0]) * rw_ref[...]


def _expert_ffn(rows, row_w, blk_expert, wg, wu, wd):
    n_rows = rows.shape[0]
    n_blk = n_rows // MOE_ROWS
    grid_spec = pltpu.PrefetchScalarGridSpec(
        num_scalar_prefetch=1, grid=(n_blk,),
        in_specs=[
            pl.BlockSpec((MOE_ROWS, D_MODEL), lambda i, be: (i, 0)),
            pl.BlockSpec((MOE_ROWS, 1), lambda i, be: (i, 0)),
            pl.BlockSpec((1, D_MODEL, EXPERT_HIDDEN), lambda i, be: (be[i], 0, 0)),
            pl.BlockSpec((1, D_MODEL, EXPERT_HIDDEN), lambda i, be: (be[i], 0, 0)),
            pl.BlockSpec((1, EXPERT_HIDDEN, D_MODEL), lambda i, be: (be[i], 0, 0)),
        ],
        out_specs=pl.BlockSpec((MOE_ROWS, D_MODEL), lambda i, be: (i, 0)))
    return pl.pallas_call(
        _expert_kernel, grid_spec=grid_spec, out_shape=jax.ShapeDtypeStruct((n_rows, D_MODEL), F32),
        compiler_params=_params(1), name="expert_ffn")(blk_expert, rows, row_w, wg, wu, wd)


def _moe_dispatch(route, h):
    n_tok = h.shape[0]
    n_asg = n_tok * MOE_TOPK
    e_flat = route[:, 0:MOE_TOPK].astype(jnp.int32).reshape(n_asg)
    w_flat = route[:, MOE_TOPK:2 * MOE_TOPK].reshape(n_asg)
    order = jnp.argsort(e_flat)
    e_sorted = e_flat[order]
    counts = jnp.zeros((N_EXPERTS,), jnp.int32).at[e_flat].add(1)
    padded = (counts + MOE_ROWS - 1) // MOE_ROWS * MOE_ROWS
    pad_end = jnp.cumsum(padded)
    pad_start = pad_end - padded
    start = jnp.cumsum(counts) - counts
    dest = pad_start[e_sorted] + jnp.arange(n_asg, dtype=jnp.int32) - start[e_sorted]
    n_rows = n_asg + N_EXPERTS * MOE_ROWS
    n_blk = n_rows // MOE_ROWS
    row_tok = jnp.full((n_rows,), n_tok, jnp.int32).at[dest].set((order // MOE_TOPK).astype(jnp.int32))
    row_w = jnp.zeros((n_rows,), F32).at[dest].set(w_flat[order])
    blk_expert = jnp.minimum(jnp.searchsorted(pad_end, jnp.arange(n_blk, dtype=jnp.int32) * MOE_ROWS, side='right'),
                             N_EXPERTS - 1).astype(jnp.int32)
    pos = jnp.zeros((n_asg,), jnp.int32).at[order].set(dest.astype(jnp.int32)).reshape(n_tok, MOE_TOPK)
    h_pad = jnp.concatenate([h, jnp.zeros((1, h.shape[1]), h.dtype)], axis=0)
    return h_pad[row_tok], row_w.reshape(n_rows, 1), blk_expert, pos


def _moe(route, h, wg, wu, wd):
    rows, row_w, blk_expert, pos = _moe_dispatch(route, h)
    y = _expert_ffn(rows, row_w, blk_expert, wg.astype(BF16), wu.astype(BF16), wd.astype(BF16))
    return y[pos[:, 0]], y[pos[:, 1]]


def _final_kernel(x_ref, y0_ref, y1_ref, g_ref, o_ref):
    x = x_ref[...] + y0_ref[...] + y1_ref[...]
    ms = jnp.mean(x * x, axis=-1, keepdims=True)
    o_ref[...] = x * lax.rsqrt(ms + NORM_EPS) * g_ref[...]


def _final_norm(x, y0, y1, gain):
    T = x.shape[0]
    tm = ROW_TILE
    row = lambda i: (i, 0)
    return pl.pallas_call(
        _final_kernel, grid=(T // tm,),
        in_specs=[pl.BlockSpec((tm, D_MODEL), row)] * 3 + [pl.BlockSpec((1, D_MODEL), lambda i: (0, 0))],
        out_specs=pl.BlockSpec((tm, D_MODEL), row), out_shape=jax.ShapeDtypeStruct((T, D_MODEL), F32),
        compiler_params=_params(1), name="final_norm")(x, y0, y1, gain.reshape(1, D_MODEL))


def _rope_tables(positions):
    inv_freq = ROPE_THETA ** (-jnp.arange(0, HEAD_DIM, 2, dtype=F32) / HEAD_DIM)
    ang = positions.astype(F32).reshape(-1, 1) * inv_freq
    cos, sin = jnp.cos(ang), jnp.sin(ang)
    reps = LANES // HEAD_DIM
    return jnp.tile(jnp.concatenate([cos, cos], axis=1), (1, reps)), jnp.tile(jnp.concatenate([-sin, sin], axis=1), (1, reps))


def _heads(t, B, S):
    return t.reshape(B, S, -1, HEAD_DIM).transpose(0, 2, 1, 3)


def _heads_t(t, B, S):
    return t.reshape(B, S, -1, HEAD_DIM).transpose(0, 2, 3, 1)


def _tokens(o):
    B, H, S, d = o.shape
    return o.transpose(0, 2, 1, 3).reshape(B * S, H * d)


def _router_weights(router_group, router_expert):
    pad = jnp.zeros((D_MODEL, LANES - N_GROUPS - N_EXPERTS), F32)
    return jnp.concatenate([router_group.astype(F32), router_expert.astype(F32), pad], axis=1)


def _pad_cols(w, n):
    return jnp.concatenate([w, jnp.zeros((w.shape[0], n - w.shape[1]), w.dtype)], axis=1)


def _mixer_ab(proj, sinks, B, S):
    hd = lambda a, b: _heads(proj[:, a:b], B, S)
    hdt = lambda a, b: _heads_t(proj[:, a:b], B, S)
    oa = _swa_attention(hd(0, 512), hdt(512, 640), hd(640, 768), sinks)
    ob = _stick_attention(hd(768, 1280), hdt(1280, 1792), hd(1792, 2304))
    return jnp.concatenate([_tokens(oa), _tokens(ob)], axis=1)


def _mixer_cd(proj, kmean, B, S, k_pe, k_w1, k_w2, v_pe, v_w1, v_w2):
    hd = lambda a, b: _heads(proj[:, a:b], B, S)
    hdt = lambda a, b: _heads_t(proj[:, a:b], B, S)
    nblk = S // C_BLOCK
    kmT = kmean.reshape(B, nblk, C_KV_HEADS, HEAD_DIM).transpose(0, 2, 3, 1).astype(BF16)
    oc = _moba_attention(hd(0, 512), hdt(512, 640), hd(640, 768), kmT)
    qd = hd(768, 1280)
    k_cmp = _compress(hd(1280, 1408), k_pe, k_w1, k_w2)
    v_cmp = _compress(hd(1408, 1536), v_pe, v_w1, v_w2)
    nc = (S - D_CMP_LEN) // D_CMP_STRIDE + 1
    ncp = S // D_CMP_STRIDE
    nsel = S // D_SEL_LEN
    c_start = jnp.arange(ncp) * D_CMP_STRIDE
    b_start = jnp.arange(nsel) * D_SEL_LEN
    overlap = ((c_start[:, None] <= b_start[None, :] + D_SEL_LEN - 1) & (c_start[:, None] + D_CMP_LEN - 1 >= b_start[None, :])
               & (jnp.arange(ncp)[:, None] < nc)).astype(BF16)
    o_cmp, bias = _cmp_select(qd, k_cmp.transpose(0, 1, 3, 2), v_cmp, overlap, nc)
    per_tile = SEL_KEY_TILE // D_SEL_LEN
    nkt = S // SEL_KEY_TILE
    bias_tiles = bias.reshape(B, D_KV_HEADS, S, nkt, per_tile).transpose(0, 1, 3, 2, 4)
    bias_tiles = jnp.concatenate([bias_tiles, jnp.zeros(bias_tiles.shape[:-1] + (SEL_BIAS_PAD - per_tile,), BF16)], axis=-1)
    o_sel = _sel_attention(qd, hdt(1536, 1664), hd(1664, 1792), bias_tiles)
    o_win = _win_attention(qd, hdt(1792, 1920), hd(1920, 2048))
    return _tokens(oc), _tokens(o_cmp), _tokens(o_sel), _tokens(o_win)


def kernel(x, positions, ln_mix_0, w_in_0, sinks_0, w_out_0, ln_ffn_0, router_group_0, router_expert_0, expert_gate_0, expert_up_0, expert_down_0, ln_mix_1, w_in_1, cmp_k_pe_1, cmp_k_w1_1, cmp_k_w2_1, cmp_v_pe_1, cmp_v_w1_1, cmp_v_w2_1, w_out_1, ln_ffn_1, router_group_1, router_expert_1, expert_gate_1, expert_up_1, expert_down_1, ln_final):
    B, S, _ = x.shape
    T = B * S
    assert S % SEL_KEY_TILE == 0 and T % ROW_TILE == 0
    cos_t, sin_t = _rope_tables(positions)
    xf = x.reshape(T, D_MODEL)

    proj = _in_proj(xf, None, ln_mix_0, w_in_0.astype(BF16), cos_t, sin_t, rope_chunks=range(0, 5))[0]
    o_ab = _mixer_ab(proj, sinks_0, B, S)
    x1, h1, route1 = _out_proj([o_ab], None, xf, w_out_0.astype(BF16), ln_ffn_0, _router_weights(router_group_0, router_expert_0))
    y0, y1 = _moe(route1, h1, expert_gate_0, expert_up_0, expert_down_0)

    n_cols = 17 * LANES
    x2, proj, kmean, gd = _in_proj(x1, (y0, y1), ln_mix_1, _pad_cols(w_in_1.astype(BF16), n_cols), cos_t, sin_t,
                                   rope_chunks=(0, 1, 2, 3, 4, 6, 7, 8, 9, 10, 12, 14), kmean_chunk=4, gate_chunk=16)
    parts = _mixer_cd(proj, kmean, B, S, cmp_k_pe_1, cmp_k_w1_1, cmp_k_w2_1, cmp_v_pe_1, cmp_v_w1_1, cmp_v_w2_1)
    x3, h3, route3 = _out_proj(parts, gd, x2, w_out_1.astype(BF16), ln_ffn_1, _router_weights(router_group_1, router_expert_1))
    y0, y1 = _moe(route3, h3, expert_gate_1, expert_up_1, expert_down_1)
    return _final_norm(x3, y0, y1, ln_final).reshape(B, S, D_MODEL)
```

```python
import functools

import jax
import jax.numpy as jnp
from jax import lax
from jax.experimental import pallas as pl
from jax.experimental.pallas import tpu as pltpu

D_MODEL = 1024
HEAD_DIM = 64
HALF = HEAD_DIM // 2
ROPE_THETA = 10000.0
NORM_EPS = 1e-6
Q_BLOCK = 128
SCALE = HEAD_DIM ** -0.5

A_HEADS, A_KV_HEADS, A_WINDOW = 8, 2, 128
B_HEADS = 8
C_HEADS, C_KV_HEADS, C_BLOCK, C_TOPK = 8, 2, 256, 3
D_HEADS, D_KV_HEADS = 8, 2
D_CMP_LEN, D_CMP_STRIDE, D_CMP_HIDDEN = 32, 16, 256
D_SEL_LEN, D_SEL_TOPK, D_WINDOW = 64, 16, 512
N_GROUPS, EXPERTS_PER_GROUP, MOE_TOPK, EXPERT_HIDDEN = 4, 16, 2, 512
N_EXPERTS = N_GROUPS * EXPERTS_PER_GROUP
GROUP = 4

LANES = 128
ROW_TILE = 512
MOE_ROWS = 256
KEY_TILE = 512
AUG_ROWS = 8
MASK_BIG = 32768.0
SB_EXIT = -104.0
VMEM_LIMIT = 56 * 1024 * 1024

F32 = jnp.float32
BF16 = jnp.bfloat16
NEG_INF = float("-inf")


def _iota(shape, dim):
    return lax.broadcasted_iota(jnp.int32, shape, dim)


def _dot(a, b):
    return jnp.dot(a, b, preferred_element_type=F32)


def _params(n_grid):
    return pltpu.CompilerParams(dimension_semantics=("arbitrary",) * n_grid, vmem_limit_bytes=VMEM_LIMIT)


def _in_proj_kernel(rope_chunks, kmean_chunk, gate_chunk, combine, *refs):
    refs = list(refs)
    x_ref = refs.pop(0)
    if combine:
        y0_ref, y1_ref = refs.pop(0), refs.pop(0)
    g_ref, w_ref, cos_ref, sin_ref = refs[:4]
    outs = refs[4:]
    x = x_ref[...]
    if combine:
        x = x + y0_ref[...] + y1_ref[...]
        xo_ref = outs.pop(0)
        xo_ref[...] = x
    o_ref = outs.pop(0)
    ms = jnp.mean(x * x, axis=-1, keepdims=True)
    h = (x * lax.rsqrt(ms + NORM_EPS) * g_ref[...]).astype(BF16)
    tm = x.shape[0]
    n_cols = w_ref.shape[1]
    cos = cos_ref[...]
    sin = sin_ref[...]
    first_half = (_iota((tm, LANES), 1) & (HEAD_DIM - 1)) < HALF
    for c in range(n_cols // LANES):
        ch = _dot(h, w_ref[:, c * LANES:(c + 1) * LANES])
        if c in rope_chunks:
            partner = jnp.where(first_half, pltpu.roll(ch, LANES - HALF, 1), pltpu.roll(ch, HALF, 1))
            ch = ch * cos + partner * sin
        if c == kmean_chunk:
            km_ref = outs[0]
            km_ref[...] = jnp.mean(ch.reshape(tm // C_BLOCK, C_BLOCK, LANES), axis=1).reshape(tm // C_BLOCK, 1, LANES)
        if c == gate_chunk:
            outs[-1][...] = ch
        o_ref[:, c * LANES:(c + 1) * LANES] = ch.astype(BF16)


def _in_proj(x, ys, gain, w, cos_t, sin_t, rope_chunks, kmean_chunk=None, gate_chunk=None):
    T = x.shape[0]
    n_cols = w.shape[1]
    tm = ROW_TILE
    combine = ys is not None
    row = lambda i: (i, 0)
    fixed = lambda i: (0, 0)
    in_specs = [pl.BlockSpec((tm, D_MODEL), row)]
    args = [x]
    if combine:
        in_specs += [pl.BlockSpec((tm, D_MODEL), row)] * 2
        args += list(ys)
    in_specs += [pl.BlockSpec((1, D_MODEL), fixed), pl.BlockSpec((D_MODEL, n_cols), fixed),
                 pl.BlockSpec((tm, LANES), row), pl.BlockSpec((tm, LANES), row)]
    args += [gain.reshape(1, D_MODEL), w, cos_t, sin_t]
    out_shape, out_specs = [], []
    if combine:
        out_shape.append(jax.ShapeDtypeStruct((T, D_MODEL), F32))
        out_specs.append(pl.BlockSpec((tm, D_MODEL), row))
    out_shape.append(jax.ShapeDtypeStruct((T, n_cols), BF16))
    out_specs.append(pl.BlockSpec((tm, n_cols), row))
    if kmean_chunk is not None:
        out_shape.append(jax.ShapeDtypeStruct((T // C_BLOCK, 1, LANES), F32))
        out_specs.append(pl.BlockSpec((tm // C_BLOCK, 1, LANES), lambda i: (i, 0, 0)))
    if gate_chunk is not None:
        out_shape.append(jax.ShapeDtypeStruct((T, LANES), F32))
        out_specs.append(pl.BlockSpec((tm, LANES), row))
    kern = functools.partial(_in_proj_kernel, tuple(rope_chunks), kmean_chunk, gate_chunk, combine)
    return pl.pallas_call(kern, grid=(T // tm,), in_specs=in_specs, out_specs=out_specs, out_shape=out_shape,
                          compiler_params=_params(1), name="in_proj")(*args)


def _swa_kernel(q_ref, kp_ref, ko_ref, vp_ref, vo_ref, sink_ref, o_ref):
    n = pl.program_id(2)
    rows = GROUP * Q_BLOCK
    q = q_ref[0].reshape(rows, HEAD_DIM)
    kT = jnp.concatenate([kp_ref[0, 0], ko_ref[0, 0]], axis=1)
    v = jnp.concatenate([vp_ref[0, 0], vo_ref[0, 0]], axis=0)
    s = _dot(q, kT) * SCALE
    qpos = _iota((rows, 2 * Q_BLOCK), 0) & (Q_BLOCK - 1)
    ki = _iota((rows, 2 * Q_BLOCK), 1)
    d = qpos + Q_BLOCK - ki
    ok = (d >= 0) & (d < A_WINDOW) & ((ki >= Q_BLOCK) | (n > 0))
    s = jnp.where(ok, s, NEG_INF)
    sink = sink_ref[0]
    m = jnp.maximum(jnp.max(s, axis=-1, keepdims=True), sink)
    p = jnp.exp(s - m)
    den = jnp.sum(p, axis=-1, keepdims=True) + jnp.exp(sink - m)
    o = _dot(p.astype(BF16), v) / den
    o_ref[0] = o.reshape(GROUP, Q_BLOCK, HEAD_DIM).astype(o_ref.dtype)


def _swa_attention(q, kT, v, sinks):
    B, Hq, S, _ = q.shape
    Hkv = kT.shape[1]
    nb = S // Q_BLOCK
    sink_col = jnp.repeat(sinks.astype(F32).reshape(Hkv, GROUP), Q_BLOCK, axis=1).reshape(Hkv, GROUP * Q_BLOCK, 1)
    prev = lambda n: jnp.maximum(n - 1, 0)
    in_specs = [
        pl.BlockSpec((1, GROUP, Q_BLOCK, HEAD_DIM), lambda b, h, n: (b, h, n, 0)),
        pl.BlockSpec((1, 1, HEAD_DIM, Q_BLOCK), lambda b, h, n: (b, h, 0, prev(n))),
        pl.BlockSpec((1, 1, HEAD_DIM, Q_BLOCK), lambda b, h, n: (b, h, 0, n)),
        pl.BlockSpec((1, 1, Q_BLOCK, HEAD_DIM), lambda b, h, n: (b, h, prev(n), 0)),
        pl.BlockSpec((1, 1, Q_BLOCK, HEAD_DIM), lambda b, h, n: (b, h, n, 0)),
        pl.BlockSpec((1, GROUP * Q_BLOCK, 1), lambda b, h, n: (h, 0, 0)),
    ]
    return pl.pallas_call(
        _swa_kernel, grid=(B, Hkv, nb), in_specs=in_specs,
        out_specs=pl.BlockSpec((1, GROUP, Q_BLOCK, HEAD_DIM), lambda b, h, n: (b, h, n, 0)),
        out_shape=jax.ShapeDtypeStruct((B, Hq, S, HEAD_DIM), BF16),
        compiler_params=_params(3), name="swa_attention")(q, kT, kT, v, v, sink_col)


def _stick_kernel(q_ref, kT_ref, v_ref, o_ref):
    n = pl.program_id(2)
    tq = tk = Q_BLOCK
    q = q_ref[0, 0]
    row = _iota((tq, tk), 0)
    col = _iota((tk, tk), 1)
    upper = jnp.where(_iota((tk, tk), 0) > col, 1.0, 0.0).astype(BF16)
    tpos = n * tq + row

    def body(carry):
        kb, _, c, acc = carry
        start = pl.multiple_of(kb * tk, tk)
        kT = kT_ref[0, 0, :, pl.ds(start, tk)]
        v = v_ref[0, 0, pl.ds(start, tk), :]
        z = _dot(q, kT) * SCALE
        past = (start + col) < tpos
        sp = jnp.maximum(z, 0.0) + jnp.log1p(jnp.exp(-jnp.abs(z)))
        log_stay = jnp.where(past, -sp, 0.0)
        hi = log_stay.astype(BF16)
        lo = (log_stay - hi.astype(F32)).astype(BF16)
        between = _dot(hi, upper) + _dot(lo, upper)
        w = jnp.where(past, jnp.exp(z - sp + between + c), 0.0)
        acc = acc + _dot(w.astype(BF16), v)
        c = c + jnp.sum(log_stay, axis=-1, keepdims=True)
        return kb - 1, jnp.max(c) > SB_EXIT, c, acc

    def cond(carry):
        kb, alive, _, _ = carry
        return (kb >= 0) & alive

    init = (n, jnp.array(True), jnp.zeros((tq, 1), F32), jnp.zeros((tq, HEAD_DIM), F32))
    _, _, _, acc = lax.while_loop(cond, body, init)
    o_ref[0, 0] = acc.astype(o_ref.dtype)


def _stick_attention(q, kT, v):
    B, H, S, _ = q.shape
    nq = S // Q_BLOCK
    in_specs = [
        pl.BlockSpec((1, 1, Q_BLOCK, HEAD_DIM), lambda b, h, n: (b, h, n, 0)),
        pl.BlockSpec((1, 1, HEAD_DIM, S), lambda b, h, n: (b, h, 0, 0)),
        pl.BlockSpec((1, 1, S, HEAD_DIM), lambda b, h, n: (b, h, 0, 0)),
    ]
    return pl.pallas_call(
        _stick_kernel, grid=(B, H, nq), in_specs=in_specs,
        out_specs=pl.BlockSpec((1, 1, Q_BLOCK, HEAD_DIM), lambda b, h, n: (b, h, n, 0)),
        out_shape=jax.ShapeDtypeStruct((B, H, S, HEAD_DIM), BF16),
        compiler_params=_params(3), name="stick_attention")(q, kT, v)


def _masked_flash_t(n, qT, bias_rows, ka_ref, vT_ref):
    R = qT.shape[1]
    tk = KEY_TILE
    pad = jnp.zeros((LANES - HEAD_DIM - AUG_ROWS, R), F32)

    def tile(kt, carry, causal):
        m, l, acc = carry
        st = pl.multiple_of(kt * tk, tk)
        low = jnp.concatenate([bias_rows(kt), pad], axis=0).astype(BF16)
        w = jnp.concatenate([qT, low], axis=0)
        s = _dot(ka_ref[0, 0, pl.ds(st, tk), :], w)
        if causal:
            tpos = n * Q_BLOCK + (_iota((tk, R), 1) & (Q_BLOCK - 1))
            s = jnp.where(st + _iota((tk, R), 0) <= tpos, s, -1e30)
        m_new = jnp.maximum(m, jnp.max(s, axis=0, keepdims=True))
        alpha = jnp.exp(m - m_new)
        p = jnp.exp(s - m_new)
        l = alpha * l + jnp.sum(p, axis=0, keepdims=True)
        acc = alpha * acc + _dot(vT_ref[0, 0, :, pl.ds(st, tk)], p.astype(BF16))
        return m_new, l, acc

    diag = (n * Q_BLOCK) // tk
    init = (jnp.full((1, R), -1e30, F32), jnp.zeros((1, R), F32), jnp.zeros((HEAD_DIM, R), F32))
    carry = lax.fori_loop(0, diag, lambda kt, c: tile(kt, c, False), init)
    _, l, acc = tile(diag, carry, True)
    return acc * (1.0 / l)


def _moba_kernel(qT_ref, ka_ref, vT_ref, km_ref, o_ref, bias_scr):
    n = pl.program_id(2)
    nblk = km_ref.shape[2]
    qT = qT_ref[0, 0, 0]
    R = qT.shape[1]
    own = (n * Q_BLOCK) // C_BLOCK
    gate = _dot(km_ref[0, 0], qT)
    blk = _iota((nblk, R), 0)
    gate = jnp.where(blk < own, gate, NEG_INF)
    bias = jnp.where(blk == own, 0.0, -MASK_BIG)
    for _ in range(C_TOPK):
        mx = jnp.max(gate, axis=0, keepdims=True)
        idx = jnp.min(jnp.where(gate == mx, blk, nblk), axis=0, keepdims=True)
        hit = blk == idx
        bias = jnp.where(hit & (mx > NEG_INF), 0.0, bias)
        gate = jnp.where(hit, NEG_INF, gate)
    per = KEY_TILE // C_BLOCK
    rows = bias_scr.shape[0]
    r = _iota((rows, nblk), 0)
    spread = jnp.where(((r & (AUG_ROWS - 1)) < per) & (_iota((rows, nblk), 1) == (r >> 3) * per + (r & (AUG_ROWS - 1))), 1.0, 0.0)
    bias_scr[...] = _dot(spread.astype(BF16), bias.astype(BF16))
    qs = (qT.astype(F32) * SCALE).astype(BF16)
    oT = _masked_flash_t(n, qs, lambda kt: bias_scr[pl.ds(pl.multiple_of(kt * AUG_ROWS, AUG_ROWS), AUG_ROWS), :], ka_ref, vT_ref)
    o_ref[0, 0, 0] = oT.astype(o_ref.dtype)


def _moba_attention(qT, ka, vT, km):
    B, Hkv, nq, _, R = qT.shape
    S = ka.shape[2]
    nblk = km.shape[2]
    nkt = S // KEY_TILE
    in_specs = [
        pl.BlockSpec((1, 1, 1, HEAD_DIM, R), lambda b, h, n: (b, h, n, 0, 0)),
        pl.BlockSpec((1, 1, S, LANES), lambda b, h, n: (b, h, 0, 0)),
        pl.BlockSpec((1, 1, HEAD_DIM, S), lambda b, h, n: (b, h, 0, 0)),
        pl.BlockSpec((1, 1, nblk, HEAD_DIM), lambda b, h, n: (b, h, 0, 0)),
    ]
    return pl.pallas_call(
        _moba_kernel, grid=(B, Hkv, nq), in_specs=in_specs,
        out_specs=pl.BlockSpec((1, 1, 1, HEAD_DIM, R), lambda b, h, n: (b, h, n, 0, 0)),
        out_shape=jax.ShapeDtypeStruct((B, Hkv, nq, HEAD_DIM, R), BF16),
        scratch_shapes=[pltpu.VMEM((nkt * AUG_ROWS, R), F32)],
        compiler_params=_params(3), name="moba_attention")(qT, ka, vT, km)


def _compress_kernel(u_ref, us_ref, pe_ref, w1_ref, w2_ref, o_ref):
    a = (u_ref[0, 0].astype(F32) + pe_ref[0:1, :]).astype(BF16)
    b = (us_ref[0, 0].astype(F32) + pe_ref[1:2, :]).astype(BF16)
    pre = _dot(a, w1_ref[0]) + _dot(b, w1_ref[1])
    hid = jax.nn.gelu(pre)
    o_ref[0, 0] = _dot(hid.astype(BF16), w2_ref[...]).astype(o_ref.dtype)


def _compress(t, pe, w1, w2):
    B, H, S, _ = t.shape
    nrow = S // D_CMP_STRIDE
    width = D_CMP_STRIDE * HEAD_DIM
    u = t.reshape(B, H, nrow, width)
    us = jnp.concatenate([u[:, :, 1:], jnp.zeros((B, H, 1, width), u.dtype)], axis=2)
    blk = lambda b, h: (b, h, 0, 0)
    in_specs = [
        pl.BlockSpec((1, 1, nrow, width), blk), pl.BlockSpec((1, 1, nrow, width), blk),
        pl.BlockSpec((2, width), lambda b, h: (0, 0)),
        pl.BlockSpec((2, width, D_CMP_HIDDEN), lambda b, h: (0, 0, 0)),
        pl.BlockSpec((D_CMP_HIDDEN, HEAD_DIM), lambda b, h: (0, 0)),
    ]
    return pl.pallas_call(
        _compress_kernel, grid=(B, H), in_specs=in_specs,
        out_specs=pl.BlockSpec((1, 1, nrow, HEAD_DIM), blk),
        out_shape=jax.ShapeDtypeStruct((B, H, nrow, HEAD_DIM), BF16),
        compiler_params=_params(2), name="nsa_compress")(
            u, us, pe.astype(F32).reshape(2, width), w1.astype(BF16).reshape(2, width, D_CMP_HIDDEN), w2.astype(BF16))


def _cmp_select_kernel(nc, q_ref, kcT_ref, vc_ref, ov_ref, oc_ref, bias_ref):
    n = pl.program_id(2)
    rows = GROUP * Q_BLOCK
    ncp = kcT_ref.shape[-1]
    nsel = ov_ref.shape[-1]
    q = q_ref[0].reshape(rows, HEAD_DIM)
    s = _dot(q, kcT_ref[0, 0]) * SCALE
    tpos = n * Q_BLOCK + (_iota((rows, ncp), 0) & (Q_BLOCK - 1))
    c = _iota((rows, ncp), 1)
    ok = (c * D_CMP_STRIDE + (D_CMP_LEN - 1) <= tpos) & (c < nc)
    s = jnp.where(ok, s, NEG_INF)
    m = jnp.max(s, axis=-1, keepdims=True)
    m = jnp.where(m > NEG_INF, m, 0.0)
    e = jnp.exp(s - m)
    den = jnp.sum(e, axis=-1, keepdims=True)
    p = (e / jnp.where(den > 0, den, 1.0)).astype(BF16)
    oc_ref[0] = _dot(p, vc_ref[0, 0]).reshape(GROUP, Q_BLOCK, HEAD_DIM)
    imp = _dot(p[0:Q_BLOCK], ov_ref[...])
    for g in range(1, GROUP):
        imp = imp + _dot(p[g * Q_BLOCK:(g + 1) * Q_BLOCK], ov_ref[...])
    t = n * Q_BLOCK + _iota((Q_BLOCK, nsel), 0)
    j = _iota((Q_BLOCK, nsel), 1)
    cur = t >> 6
    allowed = j * D_SEL_LEN <= t
    forced = (j == 0) | (j == cur) | (j == cur - 1)
    imp = jnp.where(allowed, imp, NEG_INF)
    imp = jnp.where(allowed & forced, float("inf"), imp)
    bias = jnp.full((Q_BLOCK, nsel), -MASK_BIG, F32)
    for _ in range(D_SEL_TOPK):
        mx = jnp.max(imp, axis=-1, keepdims=True)
        idx = jnp.min(jnp.where(imp == mx, j, nsel), axis=-1, keepdims=True)
        hit = j == idx
        bias = jnp.where(hit & (mx > NEG_INF), 0.0, bias)
        imp = jnp.where(hit, NEG_INF, imp)
    bias_ref[0, 0, 0] = bias.T


def _cmp_select(q, kcT, vc, overlap, nc):
    B, Hq, S, _ = q.shape
    Hkv = kcT.shape[1]
    nq = S // Q_BLOCK
    ncp = kcT.shape[-1]
    nsel = overlap.shape[-1]
    in_specs = [
        pl.BlockSpec((1, GROUP, Q_BLOCK, HEAD_DIM), lambda b, h, n: (b, h, n, 0)),
        pl.BlockSpec((1, 1, HEAD_DIM, ncp), lambda b, h, n: (b, h, 0, 0)),
        pl.BlockSpec((1, 1, ncp, HEAD_DIM), lambda b, h, n: (b, h, 0, 0)),
        pl.BlockSpec((ncp, nsel), lambda b, h, n: (0, 0)),
    ]
    out_specs = [
        pl.BlockSpec((1, GROUP, Q_BLOCK, HEAD_DIM), lambda b, h, n: (b, h, n, 0)),
        pl.BlockSpec((1, 1, 1, nsel, Q_BLOCK), lambda b, h, n: (b, h, n, 0, 0)),
    ]
    out_shape = [jax.ShapeDtypeStruct((B, Hq, S, HEAD_DIM), F32), jax.ShapeDtypeStruct((B, Hkv, nq, nsel, Q_BLOCK), F32)]
    return pl.pallas_call(
        functools.partial(_cmp_select_kernel, nc), grid=(B, Hkv, nq), in_specs=in_specs, out_specs=out_specs,
        out_shape=out_shape, compiler_params=_params(3), name="nsa_cmp_select")(q, kcT, vc, overlap)


def _sel_kernel(qT_ref, ka_ref, vT_ref, bias_ref, o_ref):
    n = pl.program_id(2)
    qs = (qT_ref[0, 0, 0].astype(F32) * SCALE).astype(BF16)

    def bias_rows(kt):
        b = bias_ref[0, 0, 0, pl.ds(pl.multiple_of(kt * AUG_ROWS, AUG_ROWS), AUG_ROWS), :]
        return jnp.concatenate([b] * GROUP, axis=1)

    o_ref[0, 0, 0] = _masked_flash_t(n, qs, bias_rows, ka_ref, vT_ref)


def _sel_attention(qT, ka, vT, biasT):
    B, Hkv, nq, _, R = qT.shape
    S = ka.shape[2]
    nsel = biasT.shape[3]
    assert KEY_TILE // D_SEL_LEN == AUG_ROWS
    in_specs = [
        pl.BlockSpec((1, 1, 1, HEAD_DIM, R), lambda b, h, n: (b, h, n, 0, 0)),
        pl.BlockSpec((1, 1, S, LANES), lambda b, h, n: (b, h, 0, 0)),
        pl.BlockSpec((1, 1, HEAD_DIM, S), lambda b, h, n: (b, h, 0, 0)),
        pl.BlockSpec((1, 1, 1, nsel, Q_BLOCK), lambda b, h, n: (b, h, n, 0, 0)),
    ]
    return pl.pallas_call(
        _sel_kernel, grid=(B, Hkv, nq), in_specs=in_specs,
        out_specs=pl.BlockSpec((1, 1, 1, HEAD_DIM, R), lambda b, h, n: (b, h, n, 0, 0)),
        out_shape=jax.ShapeDtypeStruct((B, Hkv, nq, HEAD_DIM, R), F32),
        compiler_params=_params(3), name="nsa_selected")(qT, ka, vT, biasT)


def _win_kernel(span, q_ref, kT_ref, v_ref, o_ref):
    n = pl.program_id(2)
    rows = GROUP * Q_BLOCK
    q = q_ref[0].reshape(rows, HEAD_DIM)
    start = pl.multiple_of(jnp.maximum(n * Q_BLOCK + Q_BLOCK - span, 0), Q_BLOCK)
    s = _dot(q, kT_ref[0, 0, :, pl.ds(start, span)]) * SCALE
    tpos = n * Q_BLOCK + (_iota((rows, span), 0) & (Q_BLOCK - 1))
    d = tpos - (start + _iota((rows, span), 1))
    s = jnp.where((d >= 0) & (d < D_WINDOW), s, NEG_INF)
    m = jnp.max(s, axis=-1, keepdims=True)
    p = jnp.exp(s - m)
    l = jnp.sum(p, axis=-1, keepdims=True)
    o = _dot(p.astype(BF16), v_ref[0, 0, pl.ds(start, span), :]) / l
    o_ref[0] = o.reshape(GROUP, Q_BLOCK, HEAD_DIM)


def _win_attention(q, kT, v):
    B, Hq, S, _ = q.shape
    Hkv = kT.shape[1]
    nq = S // Q_BLOCK
    span = min(D_WINDOW + Q_BLOCK, S)
    in_specs = [
        pl.BlockSpec((1, GROUP, Q_BLOCK, HEAD_DIM), lambda b, h, n: (b, h, n, 0)),
        pl.BlockSpec((1, 1, HEAD_DIM, S), lambda b, h, n: (b, h, 0, 0)),
        pl.BlockSpec((1, 1, S, HEAD_DIM), lambda b, h, n: (b, h, 0, 0)),
    ]
    return pl.pallas_call(
        functools.partial(_win_kernel, span), grid=(B, Hkv, nq), in_specs=in_specs,
        out_specs=pl.BlockSpec((1, GROUP, Q_BLOCK, HEAD_DIM), lambda b, h, n: (b, h, n, 0)),
        out_shape=jax.ShapeDtypeStruct((B, Hq, S, HEAD_DIM), F32),
        compiler_params=_params(3), name="nsa_window")(q, kT, v)


def _route(logits):
    tm = logits.shape[0]
    lane = _iota((tm, LANES), 1)
    gl = jnp.where(lane < N_GROUPS, logits, NEG_INF)
    gmax = jnp.max(gl, axis=-1, keepdims=True)
    gidx = jnp.min(jnp.where(gl == gmax, lane, LANES), axis=-1, keepdims=True)
    g_prob = 1.0 / jnp.sum(jnp.exp(gl - gmax), axis=-1, keepdims=True)
    elane = lane - N_GROUPS
    in_group = (elane >= 0) & (elane < N_EXPERTS) & ((elane >> 4) == gidx)
    el = jnp.where(in_group, logits, NEG_INF)
    ee = jnp.exp(el - jnp.max(el, axis=-1, keepdims=True))
    ep = jnp.where(in_group, ee / jnp.sum(ee, axis=-1, keepdims=True), -1.0)
    p1 = jnp.max(ep, axis=-1, keepdims=True)
    i1 = jnp.min(jnp.where(ep == p1, lane, LANES), axis=-1, keepdims=True)
    ep2 = jnp.where(lane == i1, -1.0, ep)
    p2 = jnp.max(ep2, axis=-1, keepdims=True)
    i2 = jnp.min(jnp.where(ep2 == p2, lane, LANES), axis=-1, keepdims=True)
    den = p1 + p2
    vals = [(i1 - N_GROUPS).astype(F32), (i2 - N_GROUPS).astype(F32), g_prob * p1 / den, g_prob * p2 / den]
    out = jnp.zeros((tm, LANES), F32)
    for k, val in enumerate(vals):
        out = jnp.where(lane == k, val, out)
    return out


def _out_tail(x_new, gain_ref, wr_ref, xo_ref, h_ref, route_ref):
    xo_ref[...] = x_new
    ms = jnp.mean(x_new * x_new, axis=-1, keepdims=True)
    h = x_new * lax.rsqrt(ms + NORM_EPS) * gain_ref[...]
    h_ref[...] = h.astype(BF16)
    logits = jnp.dot(h, wr_ref[...], preferred_element_type=F32, precision=lax.Precision.HIGHEST)
    route_ref[...] = _route(logits)


def _out_proj_kernel(o_ref, x_ref, w_ref, gain_ref, wr_ref, xo_ref, h_ref, route_ref):
    x_new = x_ref[...] + _dot(o_ref[...], w_ref[...])
    _out_tail(x_new, gain_ref, wr_ref, xo_ref, h_ref, route_ref)


def _out_proj_nsa_kernel(oc_ref, b0_ref, b1_ref, b2_ref, gd_ref, x_ref, w_ref, gain_ref, wr_ref, xo_ref, h_ref, route_ref):
    half = D_HEADS * HEAD_DIM
    g = jax.nn.sigmoid(gd_ref[...])
    g_hi = g.astype(BF16)
    g_lo = (g - g_hi.astype(F32)).astype(BF16)
    src = _iota((LANES, half), 0)
    head3 = (_iota((LANES, half), 1) >> 6) * 3
    od = None
    for br, b_ref in enumerate((b0_ref, b1_ref, b2_ref)):
        spread = jnp.where(src == head3 + br, 1.0, 0.0).astype(BF16)
        term = (_dot(g_hi, spread) + _dot(g_lo, spread)) * b_ref[...]
        od = term if od is None else od + term
    x_new = x_ref[...] + _dot(oc_ref[...], w_ref[0:half, :]) + _dot(od.astype(BF16), w_ref[half:, :])
    _out_tail(x_new, gain_ref, wr_ref, xo_ref, h_ref, route_ref)


def _out_proj(o_parts, gd, x, w_out, gain, w_router):
    T = x.shape[0]
    tm = ROW_TILE
    row = lambda i: (i, 0)
    fixed = lambda i: (0, 0)
    if gd is None:
        kern = _out_proj_kernel
        args = [o_parts[0]]
        in_specs = [pl.BlockSpec((tm, D_MODEL), row)]
    else:
        kern = _out_proj_nsa_kernel
        args = list(o_parts) + [gd]
        in_specs = [pl.BlockSpec((tm, D_MODEL // 2), row)] * 4 + [pl.BlockSpec((tm, LANES), row)]
    args += [x, w_out, gain.reshape(1, D_MODEL), w_router]
    in_specs += [pl.BlockSpec((tm, D_MODEL), row), pl.BlockSpec((D_MODEL, D_MODEL), fixed),
                 pl.BlockSpec((1, D_MODEL), fixed), pl.BlockSpec((D_MODEL, LANES), fixed)]
    out_shape = [jax.ShapeDtypeStruct((T, D_MODEL), F32), jax.ShapeDtypeStruct((T, D_MODEL), BF16),
                 jax.ShapeDtypeStruct((T, LANES), F32)]
    out_specs = [pl.BlockSpec((tm, D_MODEL), row), pl.BlockSpec((tm, D_MODEL), row), pl.BlockSpec((tm, LANES), row)]
    return pl.pallas_call(kern, grid=(T // tm,), in_specs=in_specs, out_specs=out_specs, out_shape=out_shape,
                          compiler_params=_params(1), name="out_proj_router")(*args)


def _expert_kernel(be_ref, rows_ref, rw_ref, wg_ref, wu_ref, wd_ref, y_ref):
    del be_ref
    xb = rows_ref[...]
    hid = jax.nn.silu(_dot(xb, wg_ref[0])) * _dot(xb, wu_ref[0])
    y_ref[...] = _dot(hid.astype(BF16), wd_ref[0]) * rw_ref[...]


def _expert_ffn(rows, row_w, blk_expert, wg, wu, wd):
    n_rows = rows.shape[0]
    n_blk = n_rows // MOE_ROWS
    grid_spec = pltpu.PrefetchScalarGridSpec(
        num_scalar_prefetch=1, grid=(n_blk,),
        in_specs=[
            pl.BlockSpec((MOE_ROWS, D_MODEL), lambda i, be: (i, 0)),
            pl.BlockSpec((MOE_ROWS, 1), lambda i, be: (i, 0)),
            pl.BlockSpec((1, D_MODEL, EXPERT_HIDDEN), lambda i, be: (be[i], 0, 0)),
            pl.BlockSpec((1, D_MODEL, EXPERT_HIDDEN), lambda i, be: (be[i], 0, 0)),
            pl.BlockSpec((1, EXPERT_HIDDEN, D_MODEL), lambda i, be: (be[i], 0, 0)),
        ],
        out_specs=pl.BlockSpec((MOE_ROWS, D_MODEL), lambda i, be: (i, 0)))
    return pl.pallas_call(
        _expert_kernel, grid_spec=grid_spec, out_shape=jax.ShapeDtypeStruct((n_rows, D_MODEL), F32),
        compiler_params=_params(1), name="expert_ffn")(blk_expert, rows, row_w, wg, wu, wd)


def _moe_dispatch(route, h):
    n_tok = h.shape[0]
    n_asg = n_tok * MOE_TOPK
    e_flat = route[:, 0:MOE_TOPK].astype(jnp.int32).reshape(n_asg)
    w_flat = route[:, MOE_TOPK:2 * MOE_TOPK].reshape(n_asg)
    order = jnp.argsort(e_flat)
    e_sorted = e_flat[order]
    counts = jnp.zeros((N_EXPERTS,), jnp.int32).at[e_flat].add(1)
    padded = (counts + MOE_ROWS - 1) // MOE_ROWS * MOE_ROWS
    pad_end = jnp.cumsum(padded)
    pad_start = pad_end - padded
    start = jnp.cumsum(counts) - counts
    dest = pad_start[e_sorted] + jnp.arange(n_asg, dtype=jnp.int32) - start[e_sorted]
    n_rows = n_asg + N_EXPERTS * MOE_ROWS
    n_blk = n_rows // MOE_ROWS
    row_tok = jnp.full((n_rows,), n_tok, jnp.int32).at[dest].set((order // MOE_TOPK).astype(jnp.int32))
    row_w = jnp.zeros((n_rows,), F32).at[dest].set(w_flat[order])
    blk_expert = jnp.minimum(jnp.searchsorted(pad_end, jnp.arange(n_blk, dtype=jnp.int32) * MOE_ROWS, side='right'),
                             N_EXPERTS - 1).astype(jnp.int32)
    pos = jnp.zeros((n_asg,), jnp.int32).at[order].set(dest.astype(jnp.int32)).reshape(n_tok, MOE_TOPK)
    h_pad = jnp.concatenate([h, jnp.zeros((1, h.shape[1]), h.dtype)], axis=0)
    return h_pad[row_tok], row_w.reshape(n_rows, 1), blk_expert, pos


def _moe(route, h, wg, wu, wd):
    rows, row_w, blk_expert, pos = _moe_dispatch(route, h)
    y = _expert_ffn(rows, row_w, blk_expert, wg.astype(BF16), wu.astype(BF16), wd.astype(BF16))
    return y[pos[:, 0]], y[pos[:, 1]]


def _final_kernel(x_ref, y0_ref, y1_ref, g_ref, o_ref):
    x = x_ref[...] + y0_ref[...] + y1_ref[...]
    ms = jnp.mean(x * x, axis=-1, keepdims=True)
    o_ref[...] = x * lax.rsqrt(ms + NORM_EPS) * g_ref[...]


def _final_norm(x, y0, y1, gain):
    T = x.shape[0]
    tm = ROW_TILE
    row = lambda i: (i, 0)
    return pl.pallas_call(
        _final_kernel, grid=(T // tm,),
        in_specs=[pl.BlockSpec((tm, D_MODEL), row)] * 3 + [pl.BlockSpec((1, D_MODEL), lambda i: (0, 0))],
        out_specs=pl.BlockSpec((tm, D_MODEL), row), out_shape=jax.ShapeDtypeStruct((T, D_MODEL), F32),
        compiler_params=_params(1), name="final_norm")(x, y0, y1, gain.reshape(1, D_MODEL))


def _rope_tables(positions):
    inv_freq = ROPE_THETA ** (-jnp.arange(0, HEAD_DIM, 2, dtype=F32) / HEAD_DIM)
    ang = positions.astype(F32).reshape(-1, 1) * inv_freq
    cos, sin = jnp.cos(ang), jnp.sin(ang)
    reps = LANES // HEAD_DIM
    return jnp.tile(jnp.concatenate([cos, cos], axis=1), (1, reps)), jnp.tile(jnp.concatenate([-sin, sin], axis=1), (1, reps))


def _heads(t, B, S):
    return t.reshape(B, S, -1, HEAD_DIM).transpose(0, 2, 1, 3)


def _heads_t(t, B, S):
    return t.reshape(B, S, -1, HEAD_DIM).transpose(0, 2, 3, 1)


def _tokens(o):
    B, H, S, d = o.shape
    return o.transpose(0, 2, 1, 3).reshape(B * S, H * d)


def _group_t(t, B, S):
    nq = S // Q_BLOCK
    t = t.reshape(B, nq, Q_BLOCK, -1, GROUP, HEAD_DIM).transpose(0, 3, 1, 5, 4, 2)
    return t.reshape(B, -1, nq, HEAD_DIM, GROUP * Q_BLOCK)


def _group_tokens(oT):
    B, Hkv, nq, d, _ = oT.shape
    o = oT.reshape(B, Hkv, nq, d, GROUP, Q_BLOCK).transpose(0, 2, 5, 1, 4, 3)
    return o.reshape(B * nq * Q_BLOCK, Hkv * GROUP * d)


def _keys_aug(k, block):
    S = k.shape[2]
    onehot = ((jnp.arange(S)[:, None] % KEY_TILE) // block == jnp.arange(LANES - HEAD_DIM)[None, :]).astype(k.dtype)
    return jnp.concatenate([k, jnp.broadcast_to(onehot, k.shape[:2] + onehot.shape)], axis=-1)


def _router_weights(router_group, router_expert):
    pad = jnp.zeros((D_MODEL, LANES - N_GROUPS - N_EXPERTS), F32)
    return jnp.concatenate([router_group.astype(F32), router_expert.astype(F32), pad], axis=1)


def _pad_cols(w, n):
    return jnp.concatenate([w, jnp.zeros((w.shape[0], n - w.shape[1]), w.dtype)], axis=1)


def _mixer_ab(proj, sinks, B, S):
    hd = lambda a, b: _heads(proj[:, a:b], B, S)
    hdt = lambda a, b: _heads_t(proj[:, a:b], B, S)
    oa = _swa_attention(hd(0, 512), hdt(512, 640), hd(640, 768), sinks)
    ob = _stick_attention(hd(768, 1280), hdt(1280, 1792), hd(1792, 2304))
    return jnp.concatenate([_tokens(oa), _tokens(ob)], axis=1)


def _mixer_cd(proj, kmean, B, S, k_pe, k_w1, k_w2, v_pe, v_w1, v_w2):
    hd = lambda a, b: _heads(proj[:, a:b], B, S)
    hdt = lambda a, b: _heads_t(proj[:, a:b], B, S)
    nblk = S // C_BLOCK
    km = kmean.reshape(B, nblk, C_KV_HEADS, HEAD_DIM).transpose(0, 2, 1, 3).astype(BF16)
    oc = _moba_attention(_group_t(proj[:, 0:512], B, S), _keys_aug(hd(512, 640), C_BLOCK), hdt(640, 768), km)
    qd = hd(768, 1280)
    k_cmp = _compress(hd(1280, 1408), k_pe, k_w1, k_w2)
    v_cmp = _compress(hd(1408, 1536), v_pe, v_w1, v_w2)
    nc = (S - D_CMP_LEN) // D_CMP_STRIDE + 1
    ncp = S // D_CMP_STRIDE
    nsel = S // D_SEL_LEN
    c_start = jnp.arange(ncp) * D_CMP_STRIDE
    b_start = jnp.arange(nsel) * D_SEL_LEN
    overlap = ((c_start[:, None] <= b_start[None, :] + D_SEL_LEN - 1) & (c_start[:, None] + D_CMP_LEN - 1 >= b_start[None, :])
               & (jnp.arange(ncp)[:, None] < nc)).astype(BF16)
    o_cmp, biasT = _cmp_select(qd, k_cmp.transpose(0, 1, 3, 2), v_cmp, overlap, nc)
    o_sel = _sel_attention(_group_t(proj[:, 768:1280], B, S), _keys_aug(hd(1536, 1664), D_SEL_LEN), hdt(1664, 1792), biasT)
    o_win = _win_attention(qd, hdt(1792, 1920), hd(1920, 2048))
    return _group_tokens(oc), _tokens(o_cmp), _group_tokens(o_sel), _tokens(o_win)


def kernel(x, positions, ln_mix_0, w_in_0, sinks_0, w_out_0, ln_ffn_0, router_group_0, router_expert_0, expert_gate_0, expert_up_0, expert_down_0, ln_mix_1, w_in_1, cmp_k_pe_1, cmp_k_w1_1, cmp_k_w2_1, cmp_v_pe_1, cmp_v_w1_1, cmp_v_w2_1, w_out_1, ln_ffn_1, router_group_1, router_expert_1, expert_gate_1, expert_up_1, expert_down_1, ln_final):
    B, S, _ = x.shape
    T = B * S
    assert S % KEY_TILE == 0 and T % ROW_TILE == 0
    cos_t, sin_t = _rope_tables(positions)
    xf = x.reshape(T, D_MODEL)

    proj = _in_proj(xf, None, ln_mix_0, w_in_0.astype(BF16), cos_t, sin_t, rope_chunks=range(0, 5))[0]
    o_ab = _mixer_ab(proj, sinks_0, B, S)
    x1, h1, route1 = _out_proj([o_ab], None, xf, w_out_0.astype(BF16), ln_ffn_0, _router_weights(router_group_0, router_expert_0))
    y0, y1 = _moe(route1, h1, expert_gate_0, expert_up_0, expert_down_0)

    n_cols = 17 * LANES
    x2, proj, kmean, gd = _in_proj(x1, (y0, y1), ln_mix_1, _pad_cols(w_in_1.astype(BF16), n_cols), cos_t, sin_t,
                                   rope_chunks=(0, 1, 2, 3, 4, 6, 7, 8, 9, 10, 12, 14), kmean_chunk=4, gate_chunk=16)
    parts = _mixer_cd(proj, kmean, B, S, cmp_k_pe_1, cmp_k_w1_1, cmp_k_w2_1, cmp_v_pe_1, cmp_v_w1_1, cmp_v_w2_1)
    x3, h3, route3 = _out_proj(parts, gd, x2, w_out_1.astype(BF16), ln_ffn_1, _router_weights(router_group_1, router_expert_1))
    y0, y1 = _moe(route3, h3, expert_gate_1, expert_up_1, expert_down_1)
    return _final_norm(x3, y0, y1, ln_final).reshape(B, S, D_MODEL)
```

```python
import functools

import jax
import jax.numpy as jnp
from jax import lax
from jax.experimental import pallas as pl
from jax.experimental.pallas import tpu as pltpu

D_MODEL = 1024
HEAD_DIM = 64
HALF = HEAD_DIM // 2
ROPE_THETA = 10000.0
NORM_EPS = 1e-6
Q_BLOCK = 128
SCALE = HEAD_DIM ** -0.5

A_HEADS, A_KV_HEADS, A_WINDOW = 8, 2, 128
B_HEADS = 8
C_HEADS, C_KV_HEADS, C_BLOCK, C_TOPK = 8, 2, 256, 3
D_HEADS, D_KV_HEADS = 8, 2
D_CMP_LEN, D_CMP_STRIDE, D_CMP_HIDDEN = 32, 16, 256
D_SEL_LEN, D_SEL_TOPK, D_WINDOW = 64, 16, 512
N_GROUPS, EXPERTS_PER_GROUP, MOE_TOPK, EXPERT_HIDDEN = 4, 16, 2, 512
N_EXPERTS = N_GROUPS * EXPERTS_PER_GROUP
GROUP = 4

LANES = 128
ROW_TILE = 512
MOE_ROWS = 256
KEY_TILE = 512
FLASH_MIN_SUM = 1e-25
V_AUG_ROWS = 80
AUG_ROWS = 8
MASK_BIG = 32768.0
SB_EXIT = -104.0
VMEM_LIMIT = 56 * 1024 * 1024

F32 = jnp.float32
BF16 = jnp.bfloat16
NEG_INF = float("-inf")


def _iota(shape, dim):
    return lax.broadcasted_iota(jnp.int32, shape, dim)


def _dot(a, b):
    return jnp.dot(a, b, preferred_element_type=F32)


def _params(n_grid):
    return pltpu.CompilerParams(dimension_semantics=("arbitrary",) * n_grid, vmem_limit_bytes=VMEM_LIMIT)


def _in_proj_kernel(rope_chunks, kmean_chunk, gate_chunk, combine, *refs):
    refs = list(refs)
    x_ref = refs.pop(0)
    if combine:
        y0_ref, y1_ref = refs.pop(0), refs.pop(0)
    g_ref, w_ref, cos_ref, sin_ref = refs[:4]
    outs = refs[4:]
    x = x_ref[...]
    if combine:
        x = x + y0_ref[...] + y1_ref[...]
        xo_ref = outs.pop(0)
        xo_ref[...] = x
    o_ref = outs.pop(0)
    ms = jnp.mean(x * x, axis=-1, keepdims=True)
    h = (x * lax.rsqrt(ms + NORM_EPS) * g_ref[...]).astype(BF16)
    tm = x.shape[0]
    n_cols = w_ref.shape[1]
    cos = cos_ref[...]
    sin = sin_ref[...]
    first_half = (_iota((tm, LANES), 1) & (HEAD_DIM - 1)) < HALF
    for c in range(n_cols // LANES):
        ch = _dot(h, w_ref[:, c * LANES:(c + 1) * LANES])
        if c in rope_chunks:
            partner = jnp.where(first_half, pltpu.roll(ch, LANES - HALF, 1), pltpu.roll(ch, HALF, 1))
            ch = ch * cos + partner * sin
        if c == kmean_chunk:
            km_ref = outs[0]
            km_ref[...] = jnp.mean(ch.reshape(tm // C_BLOCK, C_BLOCK, LANES), axis=1).reshape(tm // C_BLOCK, 1, LANES)
        if c == gate_chunk:
            outs[-1][...] = ch
        o_ref[:, c * LANES:(c + 1) * LANES] = ch.astype(BF16)


def _in_proj(x, ys, gain, w, cos_t, sin_t, rope_chunks, kmean_chunk=None, gate_chunk=None):
    T = x.shape[0]
    n_cols = w.shape[1]
    tm = ROW_TILE
    combine = ys is not None
    row = lambda i: (i, 0)
    fixed = lambda i: (0, 0)
    in_specs = [pl.BlockSpec((tm, D_MODEL), row)]
    args = [x]
    if combine:
        in_specs += [pl.BlockSpec((tm, D_MODEL), row)] * 2
        args += list(ys)
    in_specs += [pl.BlockSpec((1, D_MODEL), fixed), pl.BlockSpec((D_MODEL, n_cols), fixed),
                 pl.BlockSpec((tm, LANES), row), pl.BlockSpec((tm, LANES), row)]
    args += [gain.reshape(1, D_MODEL), w, cos_t, sin_t]
    out_shape, out_specs = [], []
    if combine:
        out_shape.append(jax.ShapeDtypeStruct((T, D_MODEL), F32))
        out_specs.append(pl.BlockSpec((tm, D_MODEL), row))
    out_shape.append(jax.ShapeDtypeStruct((T, n_cols), BF16))
    out_specs.append(pl.BlockSpec((tm, n_cols), row))
    if kmean_chunk is not None:
        out_shape.append(jax.ShapeDtypeStruct((T // C_BLOCK, 1, LANES), F32))
        out_specs.append(pl.BlockSpec((tm // C_BLOCK, 1, LANES), lambda i: (i, 0, 0)))
    if gate_chunk is not None:
        out_shape.append(jax.ShapeDtypeStruct((T, LANES), F32))
        out_specs.append(pl.BlockSpec((tm, LANES), row))
    kern = functools.partial(_in_proj_kernel, tuple(rope_chunks), kmean_chunk, gate_chunk, combine)
    return pl.pallas_call(kern, grid=(T // tm,), in_specs=in_specs, out_specs=out_specs, out_shape=out_shape,
                          compiler_params=_params(1), name="in_proj")(*args)


def _swa_kernel(q_ref, kp_ref, ko_ref, vp_ref, vo_ref, sink_ref, o_ref):
    n = pl.program_id(2)
    rows = GROUP * Q_BLOCK
    q = q_ref[0].reshape(rows, HEAD_DIM)
    kT = jnp.concatenate([kp_ref[0, 0], ko_ref[0, 0]], axis=1)
    v = jnp.concatenate([vp_ref[0, 0], vo_ref[0, 0]], axis=0)
    s = _dot(q, kT) * SCALE
    qpos = _iota((rows, 2 * Q_BLOCK), 0) & (Q_BLOCK - 1)
    ki = _iota((rows, 2 * Q_BLOCK), 1)
    d = qpos + Q_BLOCK - ki
    ok = (d >= 0) & (d < A_WINDOW) & ((ki >= Q_BLOCK) | (n > 0))
    s = jnp.where(ok, s, NEG_INF)
    sink = sink_ref[0]
    m = jnp.maximum(jnp.max(s, axis=-1, keepdims=True), sink)
    p = jnp.exp(s - m)
    den = jnp.sum(p, axis=-1, keepdims=True) + jnp.exp(sink - m)
    o = _dot(p.astype(BF16), v) / den
    o_ref[0] = o.reshape(GROUP, Q_BLOCK, HEAD_DIM).astype(o_ref.dtype)


def _swa_attention(q, kT, v, sinks):
    B, Hq, S, _ = q.shape
    Hkv = kT.shape[1]
    nb = S // Q_BLOCK
    sink_col = jnp.repeat(sinks.astype(F32).reshape(Hkv, GROUP), Q_BLOCK, axis=1).reshape(Hkv, GROUP * Q_BLOCK, 1)
    prev = lambda n: jnp.maximum(n - 1, 0)
    in_specs = [
        pl.BlockSpec((1, GROUP, Q_BLOCK, HEAD_DIM), lambda b, h, n: (b, h, n, 0)),
        pl.BlockSpec((1, 1, HEAD_DIM, Q_BLOCK), lambda b, h, n: (b, h, 0, prev(n))),
        pl.BlockSpec((1, 1, HEAD_DIM, Q_BLOCK), lambda b, h, n: (b, h, 0, n)),
        pl.BlockSpec((1, 1, Q_BLOCK, HEAD_DIM), lambda b, h, n: (b, h, prev(n), 0)),
        pl.BlockSpec((1, 1, Q_BLOCK, HEAD_DIM), lambda b, h, n: (b, h, n, 0)),
        pl.BlockSpec((1, GROUP * Q_BLOCK, 1), lambda b, h, n: (h, 0, 0)),
    ]
    return pl.pallas_call(
        _swa_kernel, grid=(B, Hkv, nb), in_specs=in_specs,
        out_specs=pl.BlockSpec((1, GROUP, Q_BLOCK, HEAD_DIM), lambda b, h, n: (b, h, n, 0)),
        out_shape=jax.ShapeDtypeStruct((B, Hq, S, HEAD_DIM), BF16),
        compiler_params=_params(3), name="swa_attention")(q, kT, kT, v, v, sink_col)


def _stick_kernel(q_ref, kT_ref, v_ref, o_ref):
    n = pl.program_id(2)
    tq = tk = Q_BLOCK
    q = q_ref[0, 0]
    row = _iota((tq, tk), 0)
    col = _iota((tk, tk), 1)
    upper = jnp.where(_iota((tk, tk), 0) > col, 1.0, 0.0).astype(BF16)
    tpos = n * tq + row

    def body(carry):
        kb, _, c, acc = carry
        start = pl.multiple_of(kb * tk, tk)
        kT = kT_ref[0, 0, :, pl.ds(start, tk)]
        v = v_ref[0, 0, pl.ds(start, tk), :]
        z = _dot(q, kT) * SCALE
        past = (start + col) < tpos
        sp = jnp.maximum(z, 0.0) + jnp.log1p(jnp.exp(-jnp.abs(z)))
        log_stay = jnp.where(past, -sp, 0.0)
        hi = log_stay.astype(BF16)
        lo = (log_stay - hi.astype(F32)).astype(BF16)
        between = _dot(hi, upper) + _dot(lo, upper)
        w = jnp.where(past, jnp.exp(z - sp + between + c), 0.0)
        acc = acc + _dot(w.astype(BF16), v)
        c = c + jnp.sum(log_stay, axis=-1, keepdims=True)
        return kb - 1, jnp.max(c) > SB_EXIT, c, acc

    def cond(carry):
        kb, alive, _, _ = carry
        return (kb >= 0) & alive

    init = (n, jnp.array(True), jnp.zeros((tq, 1), F32), jnp.zeros((tq, HEAD_DIM), F32))
    _, _, _, acc = lax.while_loop(cond, body, init)
    o_ref[0, 0] = acc.astype(o_ref.dtype)


def _stick_attention(q, kT, v):
    B, H, S, _ = q.shape
    nq = S // Q_BLOCK
    in_specs = [
        pl.BlockSpec((1, 1, Q_BLOCK, HEAD_DIM), lambda b, h, n: (b, h, n, 0)),
        pl.BlockSpec((1, 1, HEAD_DIM, S), lambda b, h, n: (b, h, 0, 0)),
        pl.BlockSpec((1, 1, S, HEAD_DIM), lambda b, h, n: (b, h, 0, 0)),
    ]
    return pl.pallas_call(
        _stick_kernel, grid=(B, H, nq), in_specs=in_specs,
        out_specs=pl.BlockSpec((1, 1, Q_BLOCK, HEAD_DIM), lambda b, h, n: (b, h, n, 0)),
        out_shape=jax.ShapeDtypeStruct((B, H, S, HEAD_DIM), BF16),
        compiler_params=_params(3), name="stick_attention")(q, kT, v)


def _max_key_sqnorm(ka_ref, out_ref):
    S = ka_ref.shape[2]
    tk = KEY_TILE
    is_key_lane = _iota((tk, LANES), 1) < HEAD_DIM
    ones = jnp.ones((LANES, LANES), BF16)

    def body(i, mx):
        k = jnp.where(is_key_lane, ka_ref[0, 0, pl.ds(pl.multiple_of(i * tk, tk), tk), :].astype(F32), 0.0)
        sq = _dot((k * k).astype(BF16), ones)
        return jnp.maximum(mx, jnp.max(sq, axis=0, keepdims=True))

    mx = lax.fori_loop(0, S // tk, body, jnp.zeros((1, LANES), F32))
    out_ref[...] = jnp.broadcast_to(mx, out_ref.shape)


def _flash_scratch(R):
    return [pltpu.VMEM((AUG_ROWS, LANES), F32), pltpu.VMEM((2, KEY_TILE, R), BF16), pltpu.VMEM((V_AUG_ROWS, R), F32)]


def _masked_flash_t(n, qT, bias_rows, ka_ref, vTa_ref, ksq_ref, p_scr, acc_scr):
    R = qT.shape[1]
    tk = KEY_TILE
    diag = (n * Q_BLOCK) // tk
    zpad = jnp.zeros((LANES - HEAD_DIM - 2 * AUG_ROWS, R), F32)
    causal = diag * tk + _iota((tk, R), 0) <= n * Q_BLOCK + (_iota((tk, R), 1) & (Q_BLOCK - 1))

    def scores(kt, ref_rows):
        st = pl.multiple_of(kt * tk, tk)
        low = jnp.concatenate([bias_rows(kt), ref_rows, zpad], axis=0).astype(BF16)
        return _dot(ka_ref[0, 0, pl.ds(st, tk), :], jnp.concatenate([qT, low], axis=0))

    def values(kt):
        return vTa_ref[0, 0, :, pl.ds(pl.multiple_of(kt * tk, tk), tk)]

    qsq = jnp.sum(jnp.square(qT.astype(F32)), axis=0, keepdims=True)
    ksq = jnp.concatenate([ksq_ref[0:1, :]] * (R // LANES), axis=1)
    ref = jnp.sqrt(qsq * ksq) * 1.05
    ref_rows = jnp.where(_iota((AUG_ROWS, R), 0) == 0, -ref, 0.0)

    def fast_body(kt, carry):
        slot = kt & 1
        s = scores(kt, ref_rows)
        acc_scr[...] += _dot(values(jnp.maximum(kt - 1, 0)), p_scr[1 - slot])
        p_scr[slot] = jnp.exp(s).astype(BF16)
        return carry

    p_scr[1] = jnp.zeros((tk, R), BF16)
    acc_scr[...] = jnp.zeros(acc_scr.shape, F32)
    lax.fori_loop(0, diag, fast_body, 0)
    acc = acc_scr[...] + _dot(values(jnp.maximum(diag - 1, 0)), p_scr[1 - (diag & 1)])
    s = jnp.where(causal, scores(diag, ref_rows), -1e30)
    acc = acc + _dot(values(diag), jnp.exp(s).astype(BF16))

    def running_max_path(_):
        no_ref = jnp.zeros((AUG_ROWS, R), F32)

        def update(kt, s, m, acc):
            m_new = jnp.maximum(m, jnp.max(s, axis=0, keepdims=True))
            p = jnp.exp(s - m_new).astype(BF16)
            return m_new, jnp.exp(m - m_new) * acc + _dot(values(kt), p)

        init = (jnp.full((1, R), -1e30, F32), jnp.zeros((vTa_ref.shape[2], R), F32))
        m, acc = lax.fori_loop(0, diag, lambda kt, c: update(kt, scores(kt, no_ref), *c), init)
        return update(diag, jnp.where(causal, scores(diag, no_ref), -1e30), m, acc)[1]

    healthy = jnp.min(acc[HEAD_DIM:HEAD_DIM + 1, :]) >= FLASH_MIN_SUM
    acc = lax.cond(healthy, lambda _: acc, running_max_path, 0)
    return acc[0:HEAD_DIM] * (1.0 / acc[HEAD_DIM:HEAD_DIM + 1])


def _moba_kernel(qT_ref, ka_ref, vTa_ref, km_ref, o_ref, bias_scr, ksq_scr, p_scr, acc_scr):
    n = pl.program_id(2)
    pl.when(n == 0)(lambda: _max_key_sqnorm(ka_ref, ksq_scr))
    nblk = km_ref.shape[2]
    qT = qT_ref[0, 0, 0]
    R = qT.shape[1]
    own = (n * Q_BLOCK) // C_BLOCK
    gate = _dot(km_ref[0, 0], qT)
    blk = _iota((nblk, R), 0)
    gate = jnp.where(blk < own, gate, NEG_INF)
    bias = jnp.where(blk == own, 0.0, -MASK_BIG)
    for _ in range(C_TOPK):
        mx = jnp.max(gate, axis=0, keepdims=True)
        idx = jnp.min(jnp.where(gate == mx, blk, nblk), axis=0, keepdims=True)
        hit = blk == idx
        bias = jnp.where(hit & (mx > NEG_INF), 0.0, bias)
        gate = jnp.where(hit, NEG_INF, gate)
    per = KEY_TILE // C_BLOCK
    rows = bias_scr.shape[0]
    r = _iota((rows, nblk), 0)
    spread = jnp.where(((r & (AUG_ROWS - 1)) < per) & (_iota((rows, nblk), 1) == (r >> 3) * per + (r & (AUG_ROWS - 1))), 1.0, 0.0)
    bias_scr[...] = _dot(spread.astype(BF16), bias.astype(BF16))
    qs = (qT.astype(F32) * SCALE).astype(BF16)
    oT = _masked_flash_t(n, qs, lambda kt: bias_scr[pl.ds(pl.multiple_of(kt * AUG_ROWS, AUG_ROWS), AUG_ROWS), :],
                         ka_ref, vTa_ref, ksq_scr, p_scr, acc_scr)
    o_ref[0, 0, 0] = oT.astype(o_ref.dtype)


def _moba_attention(qT, ka, vTa, km):
    B, Hkv, nq, _, R = qT.shape
    S = ka.shape[2]
    nblk = km.shape[2]
    nkt = S // KEY_TILE
    in_specs = [
        pl.BlockSpec((1, 1, 1, HEAD_DIM, R), lambda b, h, n: (b, h, n, 0, 0)),
        pl.BlockSpec((1, 1, S, LANES), lambda b, h, n: (b, h, 0, 0)),
        pl.BlockSpec((1, 1, V_AUG_ROWS, S), lambda b, h, n: (b, h, 0, 0)),
        pl.BlockSpec((1, 1, nblk, HEAD_DIM), lambda b, h, n: (b, h, 0, 0)),
    ]
    return pl.pallas_call(
        _moba_kernel, grid=(B, Hkv, nq), in_specs=in_specs,
        out_specs=pl.BlockSpec((1, 1, 1, HEAD_DIM, R), lambda b, h, n: (b, h, n, 0, 0)),
        out_shape=jax.ShapeDtypeStruct((B, Hkv, nq, HEAD_DIM, R), BF16),
        scratch_shapes=[pltpu.VMEM((nkt * AUG_ROWS, R), F32)] + _flash_scratch(R),
        compiler_params=_params(3), name="moba_attention")(qT, ka, vTa, km)


def _compress_kernel(u_ref, us_ref, pe_ref, w1_ref, w2_ref, o_ref):
    a = (u_ref[0, 0].astype(F32) + pe_ref[0:1, :]).astype(BF16)
    b = (us_ref[0, 0].astype(F32) + pe_ref[1:2, :]).astype(BF16)
    pre = _dot(a, w1_ref[0]) + _dot(b, w1_ref[1])
    hid = jax.nn.gelu(pre)
    o_ref[0, 0] = _dot(hid.astype(BF16), w2_ref[...]).astype(o_ref.dtype)


def _compress(t, pe, w1, w2):
    B, H, S, _ = t.shape
    nrow = S // D_CMP_STRIDE
    width = D_CMP_STRIDE * HEAD_DIM
    u = t.reshape(B, H, nrow, width)
    us = jnp.concatenate([u[:, :, 1:], jnp.zeros((B, H, 1, width), u.dtype)], axis=2)
    blk = lambda b, h: (b, h, 0, 0)
    in_specs = [
        pl.BlockSpec((1, 1, nrow, width), blk), pl.BlockSpec((1, 1, nrow, width), blk),
        pl.BlockSpec((2, width), lambda b, h: (0, 0)),
        pl.BlockSpec((2, width, D_CMP_HIDDEN), lambda b, h: (0, 0, 0)),
        pl.BlockSpec((D_CMP_HIDDEN, HEAD_DIM), lambda b, h: (0, 0)),
    ]
    return pl.pallas_call(
        _compress_kernel, grid=(B, H), in_specs=in_specs,
        out_specs=pl.BlockSpec((1, 1, nrow, HEAD_DIM), blk),
        out_shape=jax.ShapeDtypeStruct((B, H, nrow, HEAD_DIM), BF16),
        compiler_params=_params(2), name="nsa_compress")(
            u, us, pe.astype(F32).reshape(2, width), w1.astype(BF16).reshape(2, width, D_CMP_HIDDEN), w2.astype(BF16))


def _cmp_select_kernel(nc, q_ref, kcT_ref, vc_ref, ov_ref, oc_ref, bias_ref):
    n = pl.program_id(2)
    rows = GROUP * Q_BLOCK
    ncp = kcT_ref.shape[-1]
    nsel = ov_ref.shape[-1]
    q = q_ref[0].reshape(rows, HEAD_DIM)
    s = _dot(q, kcT_ref[0, 0]) * SCALE
    tpos = n * Q_BLOCK + (_iota((rows, ncp), 0) & (Q_BLOCK - 1))
    c = _iota((rows, ncp), 1)
    ok = (c * D_CMP_STRIDE + (D_CMP_LEN - 1) <= tpos) & (c < nc)
    s = jnp.where(ok, s, NEG_INF)
    m = jnp.max(s, axis=-1, keepdims=True)
    m = jnp.where(m > NEG_INF, m, 0.0)
    e = jnp.exp(s - m)
    den = jnp.sum(e, axis=-1, keepdims=True)
    p = (e / jnp.where(den > 0, den, 1.0)).astype(BF16)
    oc_ref[0] = _dot(p, vc_ref[0, 0]).reshape(GROUP, Q_BLOCK, HEAD_DIM)
    imp = _dot(p[0:Q_BLOCK], ov_ref[...])
    for g in range(1, GROUP):
        imp = imp + _dot(p[g * Q_BLOCK:(g + 1) * Q_BLOCK], ov_ref[...])
    t = n * Q_BLOCK + _iota((Q_BLOCK, nsel), 0)
    j = _iota((Q_BLOCK, nsel), 1)
    cur = t >> 6
    allowed = j * D_SEL_LEN <= t
    forced = (j == 0) | (j == cur) | (j == cur - 1)
    imp = jnp.where(allowed, imp, NEG_INF)
    imp = jnp.where(allowed & forced, float("inf"), imp)
    bias = jnp.full((Q_BLOCK, nsel), -MASK_BIG, F32)
    for _ in range(D_SEL_TOPK):
        mx = jnp.max(imp, axis=-1, keepdims=True)
        idx = jnp.min(jnp.where(imp == mx, j, nsel), axis=-1, keepdims=True)
        hit = j == idx
        bias = jnp.where(hit & (mx > NEG_INF), 0.0, bias)
        imp = jnp.where(hit, NEG_INF, imp)
    bias_ref[0, 0, 0] = bias.T


def _cmp_select(q, kcT, vc, overlap, nc):
    B, Hq, S, _ = q.shape
    Hkv = kcT.shape[1]
    nq = S // Q_BLOCK
    ncp = kcT.shape[-1]
    nsel = overlap.shape[-1]
    in_specs = [
        pl.BlockSpec((1, GROUP, Q_BLOCK, HEAD_DIM), lambda b, h, n: (b, h, n, 0)),
        pl.BlockSpec((1, 1, HEAD_DIM, ncp), lambda b, h, n: (b, h, 0, 0)),
        pl.BlockSpec((1, 1, ncp, HEAD_DIM), lambda b, h, n: (b, h, 0, 0)),
        pl.BlockSpec((ncp, nsel), lambda b, h, n: (0, 0)),
    ]
    out_specs = [
        pl.BlockSpec((1, GROUP, Q_BLOCK, HEAD_DIM), lambda b, h, n: (b, h, n, 0)),
        pl.BlockSpec((1, 1, 1, nsel, Q_BLOCK), lambda b, h, n: (b, h, n, 0, 0)),
    ]
    out_shape = [jax.ShapeDtypeStruct((B, Hq, S, HEAD_DIM), F32), jax.ShapeDtypeStruct((B, Hkv, nq, nsel, Q_BLOCK), F32)]
    return pl.pallas_call(
        functools.partial(_cmp_select_kernel, nc), grid=(B, Hkv, nq), in_specs=in_specs, out_specs=out_specs,
        out_shape=out_shape, compiler_params=_params(3), name="nsa_cmp_select")(q, kcT, vc, overlap)


def _sel_kernel(qT_ref, ka_ref, vTa_ref, bias_ref, o_ref, ksq_scr, p_scr, acc_scr):
    n = pl.program_id(2)
    pl.when(n == 0)(lambda: _max_key_sqnorm(ka_ref, ksq_scr))
    qs = (qT_ref[0, 0, 0].astype(F32) * SCALE).astype(BF16)

    def bias_rows(kt):
        b = bias_ref[0, 0, 0, pl.ds(pl.multiple_of(kt * AUG_ROWS, AUG_ROWS), AUG_ROWS), :]
        return jnp.concatenate([b] * GROUP, axis=1)

    o_ref[0, 0, 0] = _masked_flash_t(n, qs, bias_rows, ka_ref, vTa_ref, ksq_scr, p_scr, acc_scr)


def _sel_attention(qT, ka, vTa, biasT):
    B, Hkv, nq, _, R = qT.shape
    S = ka.shape[2]
    nsel = biasT.shape[3]
    assert KEY_TILE // D_SEL_LEN == AUG_ROWS
    in_specs = [
        pl.BlockSpec((1, 1, 1, HEAD_DIM, R), lambda b, h, n: (b, h, n, 0, 0)),
        pl.BlockSpec((1, 1, S, LANES), lambda b, h, n: (b, h, 0, 0)),
        pl.BlockSpec((1, 1, V_AUG_ROWS, S), lambda b, h, n: (b, h, 0, 0)),
        pl.BlockSpec((1, 1, 1, nsel, Q_BLOCK), lambda b, h, n: (b, h, n, 0, 0)),
    ]
    return pl.pallas_call(
        _sel_kernel, grid=(B, Hkv, nq), in_specs=in_specs,
        out_specs=pl.BlockSpec((1, 1, 1, HEAD_DIM, R), lambda b, h, n: (b, h, n, 0, 0)),
        out_shape=jax.ShapeDtypeStruct((B, Hkv, nq, HEAD_DIM, R), F32),
        scratch_shapes=_flash_scratch(R),
        compiler_params=_params(3), name="nsa_selected")(qT, ka, vTa, biasT)


def _win_kernel(span, q_ref, kT_ref, v_ref, o_ref):
    n = pl.program_id(2)
    rows = GROUP * Q_BLOCK
    q = q_ref[0].reshape(rows, HEAD_DIM)
    start = pl.multiple_of(jnp.maximum(n * Q_BLOCK + Q_BLOCK - span, 0), Q_BLOCK)
    s = _dot(q, kT_ref[0, 0, :, pl.ds(start, span)]) * SCALE
    tpos = n * Q_BLOCK + (_iota((rows, span), 0) & (Q_BLOCK - 1))
    d = tpos - (start + _iota((rows, span), 1))
    s = jnp.where((d >= 0) & (d < D_WINDOW), s, NEG_INF)
    m = jnp.max(s, axis=-1, keepdims=True)
    p = jnp.exp(s - m)
    l = jnp.sum(p, axis=-1, keepdims=True)
    o = _dot(p.astype(BF16), v_ref[0, 0, pl.ds(start, span), :]) / l
    o_ref[0] = o.reshape(GROUP, Q_BLOCK, HEAD_DIM)


def _win_attention(q, kT, v):
    B, Hq, S, _ = q.shape
    Hkv = kT.shape[1]
    nq = S // Q_BLOCK
    span = min(D_WINDOW + Q_BLOCK, S)
    in_specs = [
        pl.BlockSpec((1, GROUP, Q_BLOCK, HEAD_DIM), lambda b, h, n: (b, h, n, 0)),
        pl.BlockSpec((1, 1, HEAD_DIM, S), lambda b, h, n: (b, h, 0, 0)),
        pl.BlockSpec((1, 1, S, HEAD_DIM), lambda b, h, n: (b, h, 0, 0)),
    ]
    return pl.pallas_call(
        functools.partial(_win_kernel, span), grid=(B, Hkv, nq), in_specs=in_specs,
        out_specs=pl.BlockSpec((1, GROUP, Q_BLOCK, HEAD_DIM), lambda b, h, n: (b, h, n, 0)),
        out_shape=jax.ShapeDtypeStruct((B, Hq, S, HEAD_DIM), F32),
        compiler_params=_params(3), name="nsa_window")(q, kT, v)


def _route(logits):
    tm = logits.shape[0]
    lane = _iota((tm, LANES), 1)
    gl = jnp.where(lane < N_GROUPS, logits, NEG_INF)
    gmax = jnp.max(gl, axis=-1, keepdims=True)
    gidx = jnp.min(jnp.where(gl == gmax, lane, LANES), axis=-1, keepdims=True)
    g_prob = 1.0 / jnp.sum(jnp.exp(gl - gmax), axis=-1, keepdims=True)
    elane = lane - N_GROUPS
    in_group = (elane >= 0) & (elane < N_EXPERTS) & ((elane >> 4) == gidx)
    el = jnp.where(in_group, logits, NEG_INF)
    ee = jnp.exp(el - jnp.max(el, axis=-1, keepdims=True))
    ep = jnp.where(in_group, ee / jnp.sum(ee, axis=-1, keepdims=True), -1.0)
    p1 = jnp.max(ep, axis=-1, keepdims=True)
    i1 = jnp.min(jnp.where(ep == p1, lane, LANES), axis=-1, keepdims=True)
    ep2 = jnp.where(lane == i1, -1.0, ep)
    p2 = jnp.max(ep2, axis=-1, keepdims=True)
    i2 = jnp.min(jnp.where(ep2 == p2, lane, LANES), axis=-1, keepdims=True)
    den = p1 + p2
    vals = [(i1 - N_GROUPS).astype(F32), (i2 - N_GROUPS).astype(F32), g_prob * p1 / den, g_prob * p2 / den]
    out = jnp.zeros((tm, LANES), F32)
    for k, val in enumerate(vals):
        out = jnp.where(lane == k, val, out)
    return out


def _out_tail(x_new, gain_ref, wr_ref, xo_ref, h_ref, route_ref):
    xo_ref[...] = x_new
    ms = jnp.mean(x_new * x_new, axis=-1, keepdims=True)
    h = x_new * lax.rsqrt(ms + NORM_EPS) * gain_ref[...]
    h_ref[...] = h.astype(BF16)
    logits = jnp.dot(h, wr_ref[...], preferred_element_type=F32, precision=lax.Precision.HIGHEST)
    route_ref[...] = _route(logits)


def _out_proj_kernel(o_ref, x_ref, w_ref, gain_ref, wr_ref, xo_ref, h_ref, route_ref):
    x_new = x_ref[...] + _dot(o_ref[...], w_ref[...])
    _out_tail(x_new, gain_ref, wr_ref, xo_ref, h_ref, route_ref)


def _out_proj_nsa_kernel(oc_ref, b0_ref, b1_ref, b2_ref, gd_ref, x_ref, w_ref, gain_ref, wr_ref, xo_ref, h_ref, route_ref):
    half = D_HEADS * HEAD_DIM
    g = jax.nn.sigmoid(gd_ref[...])
    g_hi = g.astype(BF16)
    g_lo = (g - g_hi.astype(F32)).astype(BF16)
    src = _iota((LANES, half), 0)
    head3 = (_iota((LANES, half), 1) >> 6) * 3
    od = None
    for br, b_ref in enumerate((b0_ref, b1_ref, b2_ref)):
        spread = jnp.where(src == head3 + br, 1.0, 0.0).astype(BF16)
        term = (_dot(g_hi, spread) + _dot(g_lo, spread)) * b_ref[...]
        od = term if od is None else od + term
    x_new = x_ref[...] + _dot(oc_ref[...], w_ref[0:half, :]) + _dot(od.astype(BF16), w_ref[half:, :])
    _out_tail(x_new, gain_ref, wr_ref, xo_ref, h_ref, route_ref)


def _out_proj(o_parts, gd, x, w_out, gain, w_router):
    T = x.shape[0]
    tm = ROW_TILE
    row = lambda i: (i, 0)
    fixed = lambda i: (0, 0)
    if gd is None:
        kern = _out_proj_kernel
        args = [o_parts[0]]
        in_specs = [pl.BlockSpec((tm, D_MODEL), row)]
    else:
        kern = _out_proj_nsa_kernel
        args = list(o_parts) + [gd]
        in_specs = [pl.BlockSpec((tm, D_MODEL // 2), row)] * 4 + [pl.BlockSpec((tm, LANES), row)]
    args += [x, w_out, gain.reshape(1, D_MODEL), w_router]
    in_specs += [pl.BlockSpec((tm, D_MODEL), row), pl.BlockSpec((D_MODEL, D_MODEL), fixed),
                 pl.BlockSpec((1, D_MODEL), fixed), pl.BlockSpec((D_MODEL, LANES), fixed)]
    out_shape = [jax.ShapeDtypeStruct((T, D_MODEL), F32), jax.ShapeDtypeStruct((T, D_MODEL), BF16),
                 jax.ShapeDtypeStruct((T, LANES), F32)]
    out_specs = [pl.BlockSpec((tm, D_MODEL), row), pl.BlockSpec((tm, D_MODEL), row), pl.BlockSpec((tm, LANES), row)]
    return pl.pallas_call(kern, grid=(T // tm,), in_specs=in_specs, out_specs=out_specs, out_shape=out_shape,
                          compiler_params=_params(1), name="out_proj_router")(*args)


def _expert_kernel(be_ref, rows_ref, rw_ref, wg_ref, wu_ref, wd_ref, y_ref):
    del be_ref
    xb = rows_ref[...]
    hid = jax.nn.silu(_dot(xb, wg_ref[0])) * _dot(xb, wu_ref[0])
    y_ref[...] = _dot(hid.astype(BF16), wd_ref[0]) * rw_ref[...]


def _expert_ffn(rows, row_w, blk_expert, wg, wu, wd):
    n_rows = rows.shape[0]
    n_blk = n_rows // MOE_ROWS
    grid_spec = pltpu.PrefetchScalarGridSpec(
        num_scalar_prefetch=1, grid=(n_blk,),
        in_specs=[
            pl.BlockSpec((MOE_ROWS, D_MODEL), lambda i, be: (i, 0)),
            pl.BlockSpec((MOE_ROWS, 1), lambda i, be: (i, 0)),
            pl.BlockSpec((1, D_MODEL, EXPERT_HIDDEN), lambda i, be: (be[i], 0, 0)),
            pl.BlockSpec((1, D_MODEL, EXPERT_HIDDEN), lambda i, be: (be[i], 0, 0)),
            pl.BlockSpec((1, EXPERT_HIDDEN, D_MODEL), lambda i, be: (be[i], 0, 0)),
        ],
        out_specs=pl.BlockSpec((MOE_ROWS, D_MODEL), lambda i, be: (i, 0)))
    return pl.pallas_call(
        _expert_kernel, grid_spec=grid_spec, out_shape=jax.ShapeDtypeStruct((n_rows, D_MODEL), F32),
        compiler_params=_params(1), name="expert_ffn")(blk_expert, rows, row_w, wg, wu, wd)


def _moe_dispatch(route, h):
    n_tok = h.shape[0]
    n_asg = n_tok * MOE_TOPK
    e_flat = route[:, 0:MOE_TOPK].astype(jnp.int32).reshape(n_asg)
    w_flat = route[:, MOE_TOPK:2 * MOE_TOPK].reshape(n_asg)
    order = jnp.argsort(e_flat)
    e_sorted = e_flat[order]
    counts = jnp.zeros((N_EXPERTS,), jnp.int32).at[e_flat].add(1)
    padded = (counts + MOE_ROWS - 1) // MOE_ROWS * MOE_ROWS
    pad_end = jnp.cumsum(padded)
    pad_start = pad_end - padded
    start = jnp.cumsum(counts) - counts
    dest = pad_start[e_sorted] + jnp.arange(n_asg, dtype=jnp.int32) - start[e_sorted]
    n_rows = n_asg + N_EXPERTS * MOE_ROWS
    n_blk = n_rows // MOE_ROWS
    row_tok = jnp.full((n_rows,), n_tok, jnp.int32).at[dest].set((order // MOE_TOPK).astype(jnp.int32))
    row_w = jnp.zeros((n_rows,), F32).at[dest].set(w_flat[order])
    blk_expert = jnp.minimum(jnp.searchsorted(pad_end, jnp.arange(n_blk, dtype=jnp.int32) * MOE_ROWS, side='right'),
                             N_EXPERTS - 1).astype(jnp.int32)
    pos = jnp.zeros((n_asg,), jnp.int32).at[order].set(dest.astype(jnp.int32)).reshape(n_tok, MOE_TOPK)
    h_pad = jnp.concatenate([h, jnp.zeros((1, h.shape[1]), h.dtype)], axis=0)
    return h_pad[row_tok], row_w.reshape(n_rows, 1), blk_expert, pos


def _moe(route, h, wg, wu, wd):
    rows, row_w, blk_expert, pos = _moe_dispatch(route, h)
    y = _expert_ffn(rows, row_w, blk_expert, wg.astype(BF16), wu.astype(BF16), wd.astype(BF16))
    return y[pos[:, 0]], y[pos[:, 1]]


def _final_kernel(x_ref, y0_ref, y1_ref, g_ref, o_ref):
    x = x_ref[...] + y0_ref[...] + y1_ref[...]
    ms = jnp.mean(x * x, axis=-1, keepdims=True)
    o_ref[...] = x * lax.rsqrt(ms + NORM_EPS) * g_ref[...]


def _final_norm(x, y0, y1, gain):
    T = x.shape[0]
    tm = ROW_TILE
    row = lambda i: (i, 0)
    return pl.pallas_call(
        _final_kernel, grid=(T // tm,),
        in_specs=[pl.BlockSpec((tm, D_MODEL), row)] * 3 + [pl.BlockSpec((1, D_MODEL), lambda i: (0, 0))],
        out_specs=pl.BlockSpec((tm, D_MODEL), row), out_shape=jax.ShapeDtypeStruct((T, D_MODEL), F32),
        compiler_params=_params(1), name="final_norm")(x, y0, y1, gain.reshape(1, D_MODEL))


def _rope_tables(positions):
    inv_freq = ROPE_THETA ** (-jnp.arange(0, HEAD_DIM, 2, dtype=F32) / HEAD_DIM)
    ang = positions.astype(F32).reshape(-1, 1) * inv_freq
    cos, sin = jnp.cos(ang), jnp.sin(ang)
    reps = LANES // HEAD_DIM
    return jnp.tile(jnp.concatenate([cos, cos], axis=1), (1, reps)), jnp.tile(jnp.concatenate([-sin, sin], axis=1), (1, reps))


def _heads(t, B, S):
    return t.reshape(B, S, -1, HEAD_DIM).transpose(0, 2, 1, 3)


def _heads_t(t, B, S):
    return t.reshape(B, S, -1, HEAD_DIM).transpose(0, 2, 3, 1)


def _tokens(o):
    B, H, S, d = o.shape
    return o.transpose(0, 2, 1, 3).reshape(B * S, H * d)


def _group_t(t, B, S):
    nq = S // Q_BLOCK
    t = t.reshape(B, nq, Q_BLOCK, -1, GROUP, HEAD_DIM).transpose(0, 3, 1, 5, 4, 2)
    return t.reshape(B, -1, nq, HEAD_DIM, GROUP * Q_BLOCK)


def _group_tokens(oT):
    B, Hkv, nq, d, _ = oT.shape
    o = oT.reshape(B, Hkv, nq, d, GROUP, Q_BLOCK).transpose(0, 2, 5, 1, 4, 3)
    return o.reshape(B * nq * Q_BLOCK, Hkv * GROUP * d)


def _keys_aug(k, block):
    S = k.shape[2]
    lane = jnp.arange(LANES - HEAD_DIM)[None, :]
    extra = ((jnp.arange(S)[:, None] % KEY_TILE) // block == lane) | (lane == AUG_ROWS)
    return jnp.concatenate([k, jnp.broadcast_to(extra.astype(k.dtype), k.shape[:2] + extra.shape)], axis=-1)


def _values_aug(vT):
    B, H, _, S = vT.shape
    ones = jnp.ones((B, H, 1, S), vT.dtype)
    return jnp.concatenate([vT, ones, jnp.zeros((B, H, V_AUG_ROWS - HEAD_DIM - 1, S), vT.dtype)], axis=2)


def _router_weights(router_group, router_expert):
    pad = jnp.zeros((D_MODEL, LANES - N_GROUPS - N_EXPERTS), F32)
    return jnp.concatenate([router_group.astype(F32), router_expert.astype(F32), pad], axis=1)


def _pad_cols(w, n):
    return jnp.concatenate([w, jnp.zeros((w.shape[0], n - w.shape[1]), w.dtype)], axis=1)


def _mixer_ab(proj, sinks, B, S):
    hd = lambda a, b: _heads(proj[:, a:b], B, S)
    hdt = lambda a, b: _heads_t(proj[:, a:b], B, S)
    oa = _swa_attention(hd(0, 512), hdt(512, 640), hd(640, 768), sinks)
    ob = _stick_attention(hd(768, 1280), hdt(1280, 1792), hd(1792, 2304))
    return jnp.concatenate([_tokens(oa), _tokens(ob)], axis=1)


def _mixer_cd(proj, kmean, B, S, k_pe, k_w1, k_w2, v_pe, v_w1, v_w2):
    hd = lambda a, b: _heads(proj[:, a:b], B, S)
    hdt = lambda a, b: _heads_t(proj[:, a:b], B, S)
    nblk = S // C_BLOCK
    km = kmean.reshape(B, nblk, C_KV_HEADS, HEAD_DIM).transpose(0, 2, 1, 3).astype(BF16)
    oc = _moba_attention(_group_t(proj[:, 0:512], B, S), _keys_aug(hd(512, 640), C_BLOCK), _values_aug(hdt(640, 768)), km)
    qd = hd(768, 1280)
    k_cmp = _compress(hd(1280, 1408), k_pe, k_w1, k_w2)
    v_cmp = _compress(hd(1408, 1536), v_pe, v_w1, v_w2)
    nc = (S - D_CMP_LEN) // D_CMP_STRIDE + 1
    ncp = S // D_CMP_STRIDE
    nsel = S // D_SEL_LEN
    c_start = jnp.arange(ncp) * D_CMP_STRIDE
    b_start = jnp.arange(nsel) * D_SEL_LEN
    overlap = ((c_start[:, None] <= b_start[None, :] + D_SEL_LEN - 1) & (c_start[:, None] + D_CMP_LEN - 1 >= b_start[None, :])
               & (jnp.arange(ncp)[:, None] < nc)).astype(BF16)
    o_cmp, biasT = _cmp_select(qd, k_cmp.transpose(0, 1, 3, 2), v_cmp, overlap, nc)
    o_sel = _sel_attention(_group_t(proj[:, 768:1280], B, S), _keys_aug(hd(1536, 1664), D_SEL_LEN),
                           _values_aug(hdt(1664, 1792)), biasT)
    o_win = _win_attention(qd, hdt(1792, 1920), hd(1920, 2048))
    return _group_tokens(oc), _tokens(o_cmp), _group_tokens(o_sel), _tokens(o_win)


def kernel(x, positions, ln_mix_0, w_in_0, sinks_0, w_out_0, ln_ffn_0, router_group_0, router_expert_0, expert_gate_0, expert_up_0, expert_down_0, ln_mix_1, w_in_1, cmp_k_pe_1, cmp_k_w1_1, cmp_k_w2_1, cmp_v_pe_1, cmp_v_w1_1, cmp_v_w2_1, w_out_1, ln_ffn_1, router_group_1, router_expert_1, expert_gate_1, expert_up_1, expert_down_1, ln_final):
    B, S, _ = x.shape
    T = B * S
    assert S % KEY_TILE == 0 and T % ROW_TILE == 0
    cos_t, sin_t = _rope_tables(positions)
    xf = x.reshape(T, D_MODEL)

    proj = _in_proj(xf, None, ln_mix_0, w_in_0.astype(BF16), cos_t, sin_t, rope_chunks=range(0, 5))[0]
    o_ab = _mixer_ab(proj, sinks_0, B, S)
    x1, h1, route1 = _out_proj([o_ab], None, xf, w_out_0.astype(BF16), ln_ffn_0, _router_weights(router_group_0, router_expert_0))
    y0, y1 = _moe(route1, h1, expert_gate_0, expert_up_0, expert_down_0)

    n_cols = 17 * LANES
    x2, proj, kmean, gd = _in_proj(x1, (y0, y1), ln_mix_1, _pad_cols(w_in_1.astype(BF16), n_cols), cos_t, sin_t,
                                   rope_chunks=(0, 1, 2, 3, 4, 6, 7, 8, 9, 10, 12, 14), kmean_chunk=4, gate_chunk=16)
    parts = _mixer_cd(proj, kmean, B, S, cmp_k_pe_1, cmp_k_w1_1, cmp_k_w2_1, cmp_v_pe_1, cmp_v_w1_1, cmp_v_w2_1)
    x3, h3, route3 = _out_proj(parts, gd, x2, w_out_1.astype(BF16), ln_ffn_1, _router_weights(router_group_1, router_expert_1))
    y0, y1 = _moe(route3, h3, expert_gate_1, expert_up_1, expert_down_1)
    return _final_norm(x3, y0, y1, ln_final).reshape(B, S, D_MODEL)
```

```python
import functools

import jax
import jax.numpy as jnp
from jax import lax
from jax.experimental import pallas as pl
from jax.experimental.pallas import tpu as pltpu

D_MODEL = 1024
HEAD_DIM = 64
HALF = HEAD_DIM // 2
ROPE_THETA = 10000.0
NORM_EPS = 1e-6
Q_BLOCK = 128
SCALE = HEAD_DIM ** -0.5

A_HEADS, A_KV_HEADS, A_WINDOW = 8, 2, 128
B_HEADS = 8
C_HEADS, C_KV_HEADS, C_BLOCK, C_TOPK = 8, 2, 256, 3
D_HEADS, D_KV_HEADS = 8, 2
D_CMP_LEN, D_CMP_STRIDE, D_CMP_HIDDEN = 32, 16, 256
D_SEL_LEN, D_SEL_TOPK, D_WINDOW = 64, 16, 512
N_GROUPS, EXPERTS_PER_GROUP, MOE_TOPK, EXPERT_HIDDEN = 4, 16, 2, 512
N_EXPERTS = N_GROUPS * EXPERTS_PER_GROUP
GROUP = 4

LANES = 128
ROW_TILE = 512
MOE_ROWS = 256
KEY_TILE = 512
FLASH_MIN_SUM = 1e-25
V_AUG_ROWS = 80
AUG_ROWS = 8
MASK_BIG = 32768.0
SB_EXIT = -104.0
STICK_HEADS = 4
VMEM_LIMIT = 56 * 1024 * 1024

F32 = jnp.float32
BF16 = jnp.bfloat16
NEG_INF = float("-inf")


def _iota(shape, dim):
    return lax.broadcasted_iota(jnp.int32, shape, dim)


def _dot(a, b):
    return jnp.dot(a, b, preferred_element_type=F32)


def _params(n_grid):
    return pltpu.CompilerParams(dimension_semantics=("arbitrary",) * n_grid, vmem_limit_bytes=VMEM_LIMIT)


def _in_proj_kernel(rope_chunks, kmean_chunk, gate_chunk, combine, *refs):
    refs = list(refs)
    x_ref = refs.pop(0)
    if combine:
        y0_ref, y1_ref = refs.pop(0), refs.pop(0)
    g_ref, w_ref, cos_ref, sin_ref = refs[:4]
    outs = refs[4:]
    x = x_ref[...]
    if combine:
        x = x + y0_ref[...] + y1_ref[...]
        xo_ref = outs.pop(0)
        xo_ref[...] = x
    o_ref = outs.pop(0)
    ms = jnp.mean(x * x, axis=-1, keepdims=True)
    h = (x * lax.rsqrt(ms + NORM_EPS) * g_ref[...]).astype(BF16)
    tm = x.shape[0]
    n_cols = w_ref.shape[1]
    cos = cos_ref[...]
    sin = sin_ref[...]
    first_half = (_iota((tm, LANES), 1) & (HEAD_DIM - 1)) < HALF
    for c in range(n_cols // LANES):
        ch = _dot(h, w_ref[:, c * LANES:(c + 1) * LANES])
        if c in rope_chunks:
            partner = jnp.where(first_half, pltpu.roll(ch, LANES - HALF, 1), pltpu.roll(ch, HALF, 1))
            ch = ch * cos + partner * sin
        if c == kmean_chunk:
            km_ref = outs[0]
            km_ref[...] = jnp.mean(ch.reshape(tm // C_BLOCK, C_BLOCK, LANES), axis=1).reshape(tm // C_BLOCK, 1, LANES)
        if c == gate_chunk:
            outs[-1][...] = ch
        o_ref[:, c * LANES:(c + 1) * LANES] = ch.astype(BF16)


def _in_proj(x, ys, gain, w, cos_t, sin_t, rope_chunks, kmean_chunk=None, gate_chunk=None):
    T = x.shape[0]
    n_cols = w.shape[1]
    tm = ROW_TILE
    combine = ys is not None
    row = lambda i: (i, 0)
    fixed = lambda i: (0, 0)
    in_specs = [pl.BlockSpec((tm, D_MODEL), row)]
    args = [x]
    if combine:
        in_specs += [pl.BlockSpec((tm, D_MODEL), row)] * 2
        args += list(ys)
    in_specs += [pl.BlockSpec((1, D_MODEL), fixed), pl.BlockSpec((D_MODEL, n_cols), fixed),
                 pl.BlockSpec((tm, LANES), row), pl.BlockSpec((tm, LANES), row)]
    args += [gain.reshape(1, D_MODEL), w, cos_t, sin_t]
    out_shape, out_specs = [], []
    if combine:
        out_shape.append(jax.ShapeDtypeStruct((T, D_MODEL), F32))
        out_specs.append(pl.BlockSpec((tm, D_MODEL), row))
    out_shape.append(jax.ShapeDtypeStruct((T, n_cols), BF16))
    out_specs.append(pl.BlockSpec((tm, n_cols), row))
    if kmean_chunk is not None:
        out_shape.append(jax.ShapeDtypeStruct((T // C_BLOCK, 1, LANES), F32))
        out_specs.append(pl.BlockSpec((tm // C_BLOCK, 1, LANES), lambda i: (i, 0, 0)))
    if gate_chunk is not None:
        out_shape.append(jax.ShapeDtypeStruct((T, LANES), F32))
        out_specs.append(pl.BlockSpec((tm, LANES), row))
    kern = functools.partial(_in_proj_kernel, tuple(rope_chunks), kmean_chunk, gate_chunk, combine)
    return pl.pallas_call(kern, grid=(T // tm,), in_specs=in_specs, out_specs=out_specs, out_shape=out_shape,
                          compiler_params=_params(1), name="in_proj")(*args)


def _swa_kernel(q_ref, kp_ref, ko_ref, vp_ref, vo_ref, sink_ref, o_ref):
    n = pl.program_id(2)
    rows = GROUP * Q_BLOCK
    q = q_ref[0].reshape(rows, HEAD_DIM)
    kT = jnp.concatenate([kp_ref[0, 0], ko_ref[0, 0]], axis=1)
    v = jnp.concatenate([vp_ref[0, 0], vo_ref[0, 0]], axis=0)
    s = _dot(q, kT) * SCALE
    qpos = _iota((rows, 2 * Q_BLOCK), 0) & (Q_BLOCK - 1)
    ki = _iota((rows, 2 * Q_BLOCK), 1)
    d = qpos + Q_BLOCK - ki
    ok = (d >= 0) & (d < A_WINDOW) & ((ki >= Q_BLOCK) | (n > 0))
    s = jnp.where(ok, s, NEG_INF)
    sink = sink_ref[0]
    m = jnp.maximum(jnp.max(s, axis=-1, keepdims=True), sink)
    p = jnp.exp(s - m)
    den = jnp.sum(p, axis=-1, keepdims=True) + jnp.exp(sink - m)
    o = _dot(p.astype(BF16), v) / den
    o_ref[0] = o.reshape(GROUP, Q_BLOCK, HEAD_DIM).astype(o_ref.dtype)


def _swa_attention(q, kT, v, sinks):
    B, Hq, S, _ = q.shape
    Hkv = kT.shape[1]
    nb = S // Q_BLOCK
    sink_col = jnp.repeat(sinks.astype(F32).reshape(Hkv, GROUP), Q_BLOCK, axis=1).reshape(Hkv, GROUP * Q_BLOCK, 1)
    prev = lambda n: jnp.maximum(n - 1, 0)
    in_specs = [
        pl.BlockSpec((1, GROUP, Q_BLOCK, HEAD_DIM), lambda b, h, n: (b, h, n, 0)),
        pl.BlockSpec((1, 1, HEAD_DIM, Q_BLOCK), lambda b, h, n: (b, h, 0, prev(n))),
        pl.BlockSpec((1, 1, HEAD_DIM, Q_BLOCK), lambda b, h, n: (b, h, 0, n)),
        pl.BlockSpec((1, 1, Q_BLOCK, HEAD_DIM), lambda b, h, n: (b, h, prev(n), 0)),
        pl.BlockSpec((1, 1, Q_BLOCK, HEAD_DIM), lambda b, h, n: (b, h, n, 0)),
        pl.BlockSpec((1, GROUP * Q_BLOCK, 1), lambda b, h, n: (h, 0, 0)),
    ]
    return pl.pallas_call(
        _swa_kernel, grid=(B, Hkv, nb), in_specs=in_specs,
        out_specs=pl.BlockSpec((1, GROUP, Q_BLOCK, HEAD_DIM), lambda b, h, n: (b, h, n, 0)),
        out_shape=jax.ShapeDtypeStruct((B, Hq, S, HEAD_DIM), BF16),
        compiler_params=_params(3), name="swa_attention")(q, kT, kT, v, v, sink_col)


def _stick_kernel(q_ref, kT_ref, v_ref, o_ref):
    n = pl.program_id(2)
    tq = tk = Q_BLOCK
    heads = q_ref.shape[1]
    row = _iota((tq, tk), 0)
    col = _iota((tk, tk), 1)
    upper = jnp.where(_iota((tk, tk), 0) > col, 1.0, 0.0).astype(BF16)
    tpos = n * tq + row

    def body(carry):
        kb, _, cs, accs = carry
        start = pl.multiple_of(kb * tk, tk)
        past = (start + col) < tpos
        hs = range(heads)
        zs = [_dot(q_ref[0, h], kT_ref[0, h, :, pl.ds(start, tk)]) * SCALE for h in hs]
        sps = [jnp.maximum(z, 0.0) + jnp.log1p(jnp.exp(-jnp.abs(z))) for z in zs]
        stays = [jnp.where(past, -sp, 0.0) for sp in sps]
        his = [st.astype(BF16) for st in stays]
        los = [(st - hi.astype(F32)).astype(BF16) for st, hi in zip(stays, his)]
        betweens = [_dot(hi, upper) + _dot(lo, upper) for hi, lo in zip(his, los)]
        ws = [jnp.where(past, jnp.exp(zs[h] - sps[h] + betweens[h] + cs[h]), 0.0).astype(BF16) for h in hs]
        new_accs = [accs[h] + _dot(ws[h], v_ref[0, h, pl.ds(start, tk), :]) for h in hs]
        new_cs = [cs[h] + jnp.sum(stays[h], axis=-1, keepdims=True) for h in hs]
        worst = functools.reduce(jnp.maximum, new_cs)
        return kb - 1, jnp.max(worst) > SB_EXIT, tuple(new_cs), tuple(new_accs)

    def cond(carry):
        kb, alive, _, _ = carry
        return (kb >= 0) & alive

    init = (n, jnp.array(True), (jnp.zeros((tq, 1), F32),) * heads, (jnp.zeros((tq, HEAD_DIM), F32),) * heads)
    _, _, _, accs = lax.while_loop(cond, body, init)
    for h in range(heads):
        o_ref[0, h] = accs[h].astype(o_ref.dtype)


def _stick_attention(q, kT, v):
    B, H, S, _ = q.shape
    nq = S // Q_BLOCK
    hb = STICK_HEADS
    resident = dict(pipeline_mode=pl.Buffered(1))
    in_specs = [
        pl.BlockSpec((1, hb, Q_BLOCK, HEAD_DIM), lambda b, h, n: (b, h, n, 0)),
        pl.BlockSpec((1, hb, HEAD_DIM, S), lambda b, h, n: (b, h, 0, 0), **resident),
        pl.BlockSpec((1, hb, S, HEAD_DIM), lambda b, h, n: (b, h, 0, 0), **resident),
    ]
    return pl.pallas_call(
        _stick_kernel, grid=(B, H // hb, nq), in_specs=in_specs,
        out_specs=pl.BlockSpec((1, hb, Q_BLOCK, HEAD_DIM), lambda b, h, n: (b, h, n, 0)),
        out_shape=jax.ShapeDtypeStruct((B, H, S, HEAD_DIM), BF16),
        compiler_params=_params(3), name="stick_attention")(q, kT, v)


def _max_key_sqnorm(ka_ref, out_ref):
    S = ka_ref.shape[2]
    tk = KEY_TILE
    is_key_lane = _iota((tk, LANES), 1) < HEAD_DIM
    ones = jnp.ones((LANES, LANES), BF16)

    def body(i, mx):
        k = jnp.where(is_key_lane, ka_ref[0, 0, pl.ds(pl.multiple_of(i * tk, tk), tk), :].astype(F32), 0.0)
        sq = _dot((k * k).astype(BF16), ones)
        return jnp.maximum(mx, jnp.max(sq, axis=0, keepdims=True))

    mx = lax.fori_loop(0, S // tk, body, jnp.zeros((1, LANES), F32))
    out_ref[...] = jnp.broadcast_to(mx, out_ref.shape)


def _flash_scratch(R):
    return [pltpu.VMEM((AUG_ROWS, LANES), F32), pltpu.VMEM((2, KEY_TILE, R), BF16), pltpu.VMEM((V_AUG_ROWS, R), F32)]


def _masked_flash_t(n, qT, bias_rows, ka_ref, vTa_ref, ksq_ref, p_scr, acc_scr):
    R = qT.shape[1]
    tk = KEY_TILE
    diag = (n * Q_BLOCK) // tk
    zpad = jnp.zeros((LANES - HEAD_DIM - 2 * AUG_ROWS, R), F32)
    causal = diag * tk + _iota((tk, R), 0) <= n * Q_BLOCK + (_iota((tk, R), 1) & (Q_BLOCK - 1))

    def scores(kt, ref_rows):
        st = pl.multiple_of(kt * tk, tk)
        low = jnp.concatenate([bias_rows(kt), ref_rows, zpad], axis=0).astype(BF16)
        return _dot(ka_ref[0, 0, pl.ds(st, tk), :], jnp.concatenate([qT, low], axis=0))

    def values(kt):
        return vTa_ref[0, 0, :, pl.ds(pl.multiple_of(kt * tk, tk), tk)]

    qsq = jnp.sum(jnp.square(qT.astype(F32)), axis=0, keepdims=True)
    ksq = jnp.concatenate([ksq_ref[0:1, :]] * (R // LANES), axis=1)
    ref = jnp.sqrt(qsq * ksq) * 1.05
    ref_rows = jnp.where(_iota((AUG_ROWS, R), 0) == 0, -ref, 0.0)

    def fast_body(kt, carry):
        slot = kt & 1
        s = scores(kt, ref_rows)
        acc_scr[...] += _dot(values(jnp.maximum(kt - 1, 0)), p_scr[1 - slot])
        p_scr[slot] = jnp.exp(s).astype(BF16)
        return carry

    p_scr[1] = jnp.zeros((tk, R), BF16)
    acc_scr[...] = jnp.zeros(acc_scr.shape, F32)
    lax.fori_loop(0, diag, fast_body, 0)
    acc = acc_scr[...] + _dot(values(jnp.maximum(diag - 1, 0)), p_scr[1 - (diag & 1)])
    s = jnp.where(causal, scores(diag, ref_rows), -1e30)
    acc = acc + _dot(values(diag), jnp.exp(s).astype(BF16))

    def running_max_path(_):
        no_ref = jnp.zeros((AUG_ROWS, R), F32)

        def update(kt, s, m, acc):
            m_new = jnp.maximum(m, jnp.max(s, axis=0, keepdims=True))
            p = jnp.exp(s - m_new).astype(BF16)
            return m_new, jnp.exp(m - m_new) * acc + _dot(values(kt), p)

        init = (jnp.full((1, R), -1e30, F32), jnp.zeros((vTa_ref.shape[2], R), F32))
        m, acc = lax.fori_loop(0, diag, lambda kt, c: update(kt, scores(kt, no_ref), *c), init)
        return update(diag, jnp.where(causal, scores(diag, no_ref), -1e30), m, acc)[1]

    healthy = jnp.min(acc[HEAD_DIM:HEAD_DIM + 1, :]) >= FLASH_MIN_SUM
    acc = lax.cond(healthy, lambda _: acc, running_max_path, 0)
    return acc[0:HEAD_DIM] * (1.0 / acc[HEAD_DIM:HEAD_DIM + 1])


def _moba_kernel(qT_ref, ka_ref, vTa_ref, km_ref, o_ref, bias_scr, ksq_scr, p_scr, acc_scr):
    n = pl.program_id(2)
    pl.when(n == 0)(lambda: _max_key_sqnorm(ka_ref, ksq_scr))
    nblk = km_ref.shape[2]
    qT = qT_ref[0, 0, 0]
    R = qT.shape[1]
    own = (n * Q_BLOCK) // C_BLOCK
    gate = _dot(km_ref[0, 0], qT)
    blk = _iota((nblk, R), 0)
    gate = jnp.where(blk < own, gate, NEG_INF)
    bias = jnp.where(blk == own, 0.0, -MASK_BIG)
    for _ in range(C_TOPK):
        mx = jnp.max(gate, axis=0, keepdims=True)
        idx = jnp.min(jnp.where(gate == mx, blk, nblk), axis=0, keepdims=True)
        hit = blk == idx
        bias = jnp.where(hit & (mx > NEG_INF), 0.0, bias)
        gate = jnp.where(hit, NEG_INF, gate)
    per = KEY_TILE // C_BLOCK
    rows = bias_scr.shape[0]
    r = _iota((rows, nblk), 0)
    spread = jnp.where(((r & (AUG_ROWS - 1)) < per) & (_iota((rows, nblk), 1) == (r >> 3) * per + (r & (AUG_ROWS - 1))), 1.0, 0.0)
    bias_scr[...] = _dot(spread.astype(BF16), bias.astype(BF16))
    qs = (qT.astype(F32) * SCALE).astype(BF16)
    oT = _masked_flash_t(n, qs, lambda kt: bias_scr[pl.ds(pl.multiple_of(kt * AUG_ROWS, AUG_ROWS), AUG_ROWS), :],
                         ka_ref, vTa_ref, ksq_scr, p_scr, acc_scr)
    o_ref[0, 0, 0] = oT.astype(o_ref.dtype)


def _moba_attention(qT, ka, vTa, km):
    B, Hkv, nq, _, R = qT.shape
    S = ka.shape[2]
    nblk = km.shape[2]
    nkt = S // KEY_TILE
    in_specs = [
        pl.BlockSpec((1, 1, 1, HEAD_DIM, R), lambda b, h, n: (b, h, n, 0, 0)),
        pl.BlockSpec((1, 1, S, LANES), lambda b, h, n: (b, h, 0, 0)),
        pl.BlockSpec((1, 1, V_AUG_ROWS, S), lambda b, h, n: (b, h, 0, 0)),
        pl.BlockSpec((1, 1, nblk, HEAD_DIM), lambda b, h, n: (b, h, 0, 0)),
    ]
    return pl.pallas_call(
        _moba_kernel, grid=(B, Hkv, nq), in_specs=in_specs,
        out_specs=pl.BlockSpec((1, 1, 1, HEAD_DIM, R), lambda b, h, n: (b, h, n, 0, 0)),
        out_shape=jax.ShapeDtypeStruct((B, Hkv, nq, HEAD_DIM, R), BF16),
        scratch_shapes=[pltpu.VMEM((nkt * AUG_ROWS, R), F32)] + _flash_scratch(R),
        compiler_params=_params(3), name="moba_attention")(qT, ka, vTa, km)


def _compress_kernel(u_ref, us_ref, pe_ref, w1_ref, w2_ref, o_ref):
    a = (u_ref[0, 0].astype(F32) + pe_ref[0:1, :]).astype(BF16)
    b = (us_ref[0, 0].astype(F32) + pe_ref[1:2, :]).astype(BF16)
    pre = _dot(a, w1_ref[0]) + _dot(b, w1_ref[1])
    hid = jax.nn.gelu(pre)
    o_ref[0, 0] = _dot(hid.astype(BF16), w2_ref[...]).astype(o_ref.dtype)


def _compress(t, pe, w1, w2):
    B, H, S, _ = t.shape
    nrow = S // D_CMP_STRIDE
    width = D_CMP_STRIDE * HEAD_DIM
    u = t.reshape(B, H, nrow, width)
    us = jnp.concatenate([u[:, :, 1:], jnp.zeros((B, H, 1, width), u.dtype)], axis=2)
    blk = lambda b, h: (b, h, 0, 0)
    in_specs = [
        pl.BlockSpec((1, 1, nrow, width), blk), pl.BlockSpec((1, 1, nrow, width), blk),
        pl.BlockSpec((2, width), lambda b, h: (0, 0)),
        pl.BlockSpec((2, width, D_CMP_HIDDEN), lambda b, h: (0, 0, 0)),
        pl.BlockSpec((D_CMP_HIDDEN, HEAD_DIM), lambda b, h: (0, 0)),
    ]
    return pl.pallas_call(
        _compress_kernel, grid=(B, H), in_specs=in_specs,
        out_specs=pl.BlockSpec((1, 1, nrow, HEAD_DIM), blk),
        out_shape=jax.ShapeDtypeStruct((B, H, nrow, HEAD_DIM), BF16),
        compiler_params=_params(2), name="nsa_compress")(
            u, us, pe.astype(F32).reshape(2, width), w1.astype(BF16).reshape(2, width, D_CMP_HIDDEN), w2.astype(BF16))


def _cmp_select_kernel(nc, qT_ref, kc_ref, vcT_ref, ovT_ref, oc_ref, bias_ref):
    n = pl.program_id(2)
    ncp = kc_ref.shape[2]
    nsel = ovT_ref.shape[0]
    qs = (qT_ref[0, 0, 0].astype(F32) * SCALE).astype(BF16)
    R = qs.shape[1]
    s = _dot(kc_ref[0, 0], qs)
    tpos = n * Q_BLOCK + (_iota((ncp, R), 1) & (Q_BLOCK - 1))
    c = _iota((ncp, R), 0)
    ok = (c * D_CMP_STRIDE + (D_CMP_LEN - 1) <= tpos) & (c < nc)
    s = jnp.where(ok, s, NEG_INF)
    m = jnp.max(s, axis=0, keepdims=True)
    m = jnp.where(m > NEG_INF, m, 0.0)
    e = jnp.exp(s - m)
    den = jnp.sum(e, axis=0, keepdims=True)
    p = (e * (1.0 / jnp.where(den > 0, den, 1.0))).astype(BF16)
    oc_ref[0, 0, 0] = _dot(vcT_ref[0, 0], p)
    imp_heads = _dot(ovT_ref[...], p)
    imp = imp_heads[:, 0:Q_BLOCK]
    for g in range(1, GROUP):
        imp = imp + imp_heads[:, g * Q_BLOCK:(g + 1) * Q_BLOCK]
    t = n * Q_BLOCK + _iota((nsel, Q_BLOCK), 1)
    j = _iota((nsel, Q_BLOCK), 0)
    cur = t >> 6
    allowed = j * D_SEL_LEN <= t
    forced = (j == 0) | (j == cur) | (j == cur - 1)
    imp = jnp.where(allowed, imp, NEG_INF)
    imp = jnp.where(allowed & forced, float("inf"), imp)
    bias = jnp.full((nsel, Q_BLOCK), -MASK_BIG, F32)
    for _ in range(D_SEL_TOPK):
        mx = jnp.max(imp, axis=0, keepdims=True)
        idx = jnp.min(jnp.where(imp == mx, j, nsel), axis=0, keepdims=True)
        hit = j == idx
        bias = jnp.where(hit & (mx > NEG_INF), 0.0, bias)
        imp = jnp.where(hit, NEG_INF, imp)
    bias_ref[0, 0, 0] = bias


def _cmp_select(qT, kc, vcT, overlapT, nc):
    B, Hkv, nq, _, R = qT.shape
    ncp = kc.shape[2]
    nsel = overlapT.shape[0]
    in_specs = [
        pl.BlockSpec((1, 1, 1, HEAD_DIM, R), lambda b, h, n: (b, h, n, 0, 0)),
        pl.BlockSpec((1, 1, ncp, HEAD_DIM), lambda b, h, n: (b, h, 0, 0)),
        pl.BlockSpec((1, 1, HEAD_DIM, ncp), lambda b, h, n: (b, h, 0, 0)),
        pl.BlockSpec((nsel, ncp), lambda b, h, n: (0, 0)),
    ]
    out_specs = [
        pl.BlockSpec((1, 1, 1, HEAD_DIM, R), lambda b, h, n: (b, h, n, 0, 0)),
        pl.BlockSpec((1, 1, 1, nsel, Q_BLOCK), lambda b, h, n: (b, h, n, 0, 0)),
    ]
    out_shape = [jax.ShapeDtypeStruct((B, Hkv, nq, HEAD_DIM, R), F32), jax.ShapeDtypeStruct((B, Hkv, nq, nsel, Q_BLOCK), F32)]
    return pl.pallas_call(
        functools.partial(_cmp_select_kernel, nc), grid=(B, Hkv, nq), in_specs=in_specs, out_specs=out_specs,
        out_shape=out_shape, compiler_params=_params(3), name="nsa_cmp_select")(qT, kc, vcT, overlapT)


def _sel_kernel(qT_ref, ka_ref, vTa_ref, bias_ref, o_ref, ksq_scr, p_scr, acc_scr):
    n = pl.program_id(2)
    pl.when(n == 0)(lambda: _max_key_sqnorm(ka_ref, ksq_scr))
    qs = (qT_ref[0, 0, 0].astype(F32) * SCALE).astype(BF16)

    def bias_rows(kt):
        b = bias_ref[0, 0, 0, pl.ds(pl.multiple_of(kt * AUG_ROWS, AUG_ROWS), AUG_ROWS), :]
        return jnp.concatenate([b] * GROUP, axis=1)

    o_ref[0, 0, 0] = _masked_flash_t(n, qs, bias_rows, ka_ref, vTa_ref, ksq_scr, p_scr, acc_scr)


def _sel_attention(qT, ka, vTa, biasT):
    B, Hkv, nq, _, R = qT.shape
    S = ka.shape[2]
    nsel = biasT.shape[3]
    assert KEY_TILE // D_SEL_LEN == AUG_ROWS
    in_specs = [
        pl.BlockSpec((1, 1, 1, HEAD_DIM, R), lambda b, h, n: (b, h, n, 0, 0)),
        pl.BlockSpec((1, 1, S, LANES), lambda b, h, n: (b, h, 0, 0)),
        pl.BlockSpec((1, 1, V_AUG_ROWS, S), lambda b, h, n: (b, h, 0, 0)),
        pl.BlockSpec((1, 1, 1, nsel, Q_BLOCK), lambda b, h, n: (b, h, n, 0, 0)),
    ]
    return pl.pallas_call(
        _sel_kernel, grid=(B, Hkv, nq), in_specs=in_specs,
        out_specs=pl.BlockSpec((1, 1, 1, HEAD_DIM, R), lambda b, h, n: (b, h, n, 0, 0)),
        out_shape=jax.ShapeDtypeStruct((B, Hkv, nq, HEAD_DIM, R), F32),
        scratch_shapes=_flash_scratch(R),
        compiler_params=_params(3), name="nsa_selected")(qT, ka, vTa, biasT)


def _win_kernel(span, q_ref, kT_ref, v_ref, o_ref):
    n = pl.program_id(2)
    rows = GROUP * Q_BLOCK
    q = q_ref[0].reshape(rows, HEAD_DIM)
    start = pl.multiple_of(jnp.maximum(n * Q_BLOCK + Q_BLOCK - span, 0), Q_BLOCK)
    s = _dot(q, kT_ref[0, 0, :, pl.ds(start, span)]) * SCALE
    tpos = n * Q_BLOCK + (_iota((rows, span), 0) & (Q_BLOCK - 1))
    d = tpos - (start + _iota((rows, span), 1))
    s = jnp.where((d >= 0) & (d < D_WINDOW), s, NEG_INF)
    m = jnp.max(s, axis=-1, keepdims=True)
    p = jnp.exp(s - m)
    l = jnp.sum(p, axis=-1, keepdims=True)
    o = _dot(p.astype(BF16), v_ref[0, 0, pl.ds(start, span), :]) / l
    o_ref[0] = o.reshape(GROUP, Q_BLOCK, HEAD_DIM)


def _win_attention(q, kT, v):
    B, Hq, S, _ = q.shape
    Hkv = kT.shape[1]
    nq = S // Q_BLOCK
    span = min(D_WINDOW + Q_BLOCK, S)
    in_specs = [
        pl.BlockSpec((1, GROUP, Q_BLOCK, HEAD_DIM), lambda b, h, n: (b, h, n, 0)),
        pl.BlockSpec((1, 1, HEAD_DIM, S), lambda b, h, n: (b, h, 0, 0)),
        pl.BlockSpec((1, 1, S, HEAD_DIM), lambda b, h, n: (b, h, 0, 0)),
    ]
    return pl.pallas_call(
        functools.partial(_win_kernel, span), grid=(B, Hkv, nq), in_specs=in_specs,
        out_specs=pl.BlockSpec((1, GROUP, Q_BLOCK, HEAD_DIM), lambda b, h, n: (b, h, n, 0)),
        out_shape=jax.ShapeDtypeStruct((B, Hq, S, HEAD_DIM), F32),
        compiler_params=_params(3), name="nsa_window")(q, kT, v)


def _route(logits):
    tm = logits.shape[0]
    lane = _iota((tm, LANES), 1)
    gl = jnp.where(lane < N_GROUPS, logits, NEG_INF)
    gmax = jnp.max(gl, axis=-1, keepdims=True)
    gidx = jnp.min(jnp.where(gl == gmax, lane, LANES), axis=-1, keepdims=True)
    g_prob = 1.0 / jnp.sum(jnp.exp(gl - gmax), axis=-1, keepdims=True)
    elane = lane - N_GROUPS
    in_group = (elane >= 0) & (elane < N_EXPERTS) & ((elane >> 4) == gidx)
    el = jnp.where(in_group, logits, NEG_INF)
    ee = jnp.exp(el - jnp.max(el, axis=-1, keepdims=True))
    ep = jnp.where(in_group, ee / jnp.sum(ee, axis=-1, keepdims=True), -1.0)
    p1 = jnp.max(ep, axis=-1, keepdims=True)
    i1 = jnp.min(jnp.where(ep == p1, lane, LANES), axis=-1, keepdims=True)
    ep2 = jnp.where(lane == i1, -1.0, ep)
    p2 = jnp.max(ep2, axis=-1, keepdims=True)
    i2 = jnp.min(jnp.where(ep2 == p2, lane, LANES), axis=-1, keepdims=True)
    den = p1 + p2
    vals = [(i1 - N_GROUPS).astype(F32), (i2 - N_GROUPS).astype(F32), g_prob * p1 / den, g_prob * p2 / den]
    out = jnp.zeros((tm, LANES), F32)
    for k, val in enumerate(vals):
        out = jnp.where(lane == k, val, out)
    return out


def _out_tail(x_new, gain_ref, wr_ref, xo_ref, h_ref, route_ref):
    xo_ref[...] = x_new
    ms = jnp.mean(x_new * x_new, axis=-1, keepdims=True)
    h = x_new * lax.rsqrt(ms + NORM_EPS) * gain_ref[...]
    h_ref[...] = h.astype(BF16)
    logits = jnp.dot(h, wr_ref[...], preferred_element_type=F32, precision=lax.Precision.HIGHEST)
    route_ref[...] = _route(logits)


def _out_proj_kernel(o_ref, x_ref, w_ref, gain_ref, wr_ref, xo_ref, h_ref, route_ref):
    x_new = x_ref[...] + _dot(o_ref[...], w_ref[...])
    _out_tail(x_new, gain_ref, wr_ref, xo_ref, h_ref, route_ref)


def _out_proj_nsa_kernel(oc_ref, b0_ref, b1_ref, b2_ref, gd_ref, x_ref, w_ref, gain_ref, wr_ref, xo_ref, h_ref, route_ref):
    half = D_HEADS * HEAD_DIM
    g = jax.nn.sigmoid(gd_ref[...])
    g_hi = g.astype(BF16)
    g_lo = (g - g_hi.astype(F32)).astype(BF16)
    src = _iota((LANES, half), 0)
    head3 = (_iota((LANES, half), 1) >> 6) * 3
    od = None
    for br, b_ref in enumerate((b0_ref, b1_ref, b2_ref)):
        spread = jnp.where(src == head3 + br, 1.0, 0.0).astype(BF16)
        term = (_dot(g_hi, spread) + _dot(g_lo, spread)) * b_ref[...]
        od = term if od is None else od + term
    x_new = x_ref[...] + _dot(oc_ref[...], w_ref[0:half, :]) + _dot(od.astype(BF16), w_ref[half:, :])
    _out_tail(x_new, gain_ref, wr_ref, xo_ref, h_ref, route_ref)


def _out_proj(o_parts, gd, x, w_out, gain, w_router):
    T = x.shape[0]
    tm = ROW_TILE
    row = lambda i: (i, 0)
    fixed = lambda i: (0, 0)
    if gd is None:
        kern = _out_proj_kernel
        args = [o_parts[0]]
        in_specs = [pl.BlockSpec((tm, D_MODEL), row)]
    else:
        kern = _out_proj_nsa_kernel
        args = list(o_parts) + [gd]
        in_specs = [pl.BlockSpec((tm, D_MODEL // 2), row)] * 4 + [pl.BlockSpec((tm, LANES), row)]
    args += [x, w_out, gain.reshape(1, D_MODEL), w_router]
    in_specs += [pl.BlockSpec((tm, D_MODEL), row), pl.BlockSpec((D_MODEL, D_MODEL), fixed),
                 pl.BlockSpec((1, D_MODEL), fixed), pl.BlockSpec((D_MODEL, LANES), fixed)]
    out_shape = [jax.ShapeDtypeStruct((T, D_MODEL), F32), jax.ShapeDtypeStruct((T, D_MODEL), BF16),
                 jax.ShapeDtypeStruct((T, LANES), F32)]
    out_specs = [pl.BlockSpec((tm, D_MODEL), row), pl.BlockSpec((tm, D_MODEL), row), pl.BlockSpec((tm, LANES), row)]
    return pl.pallas_call(kern, grid=(T // tm,), in_specs=in_specs, out_specs=out_specs, out_shape=out_shape,
                          compiler_params=_params(1), name="out_proj_router")(*args)


def _expert_kernel(be_ref, rows_ref, rw_ref, wg_ref, wu_ref, wd_ref, y_ref, wg_s, wu_s, wd_s):
    i = pl.program_id(0)
    n_blk = pl.num_programs(0)

    @pl.when((i == 0) | (be_ref[i] != be_ref[jnp.maximum(i - 1, 0)]))
    def _():
        wg_s[...] = wg_ref[0].astype(BF16)
        wu_s[...] = wu_ref[0].astype(BF16)
        wd_s[...] = wd_ref[0].astype(BF16)

    @pl.when(i < be_ref[n_blk])
    def _():
        xb = rows_ref[...]
        hid = jax.nn.silu(_dot(xb, wg_s[...])) * _dot(xb, wu_s[...])
        y_ref[...] = _dot(hid.astype(BF16), wd_s[...]) * rw_ref[...]

    @pl.when(i >= be_ref[n_blk])
    def _():
        y_ref[...] = jnp.zeros(y_ref.shape, F32)


def _expert_ffn(rows, row_w, blk_info, wg, wu, wd):
    n_rows = rows.shape[0]
    n_blk = n_rows // MOE_ROWS
    grid_spec = pltpu.PrefetchScalarGridSpec(
        num_scalar_prefetch=1, grid=(n_blk,),
        in_specs=[
            pl.BlockSpec((MOE_ROWS, D_MODEL), lambda i, be: (i, 0)),
            pl.BlockSpec((MOE_ROWS, 1), lambda i, be: (i, 0)),
            pl.BlockSpec((1, D_MODEL, EXPERT_HIDDEN), lambda i, be: (be[i], 0, 0)),
            pl.BlockSpec((1, D_MODEL, EXPERT_HIDDEN), lambda i, be: (be[i], 0, 0)),
            pl.BlockSpec((1, EXPERT_HIDDEN, D_MODEL), lambda i, be: (be[i], 0, 0)),
        ],
        out_specs=pl.BlockSpec((MOE_ROWS, D_MODEL), lambda i, be: (i, 0)),
        scratch_shapes=[pltpu.VMEM((D_MODEL, EXPERT_HIDDEN), BF16), pltpu.VMEM((D_MODEL, EXPERT_HIDDEN), BF16),
                        pltpu.VMEM((EXPERT_HIDDEN, D_MODEL), BF16)])
    return pl.pallas_call(
        _expert_kernel, grid_spec=grid_spec, out_shape=jax.ShapeDtypeStruct((n_rows, D_MODEL), F32),
        compiler_params=_params(1), name="expert_ffn")(blk_info, rows, row_w, wg, wu, wd)


def _moe_dispatch(route, h):
    n_tok = h.shape[0]
    n_asg = n_tok * MOE_TOPK
    i32 = jnp.int32
    e_flat = route[:, 0:MOE_TOPK].astype(i32).reshape(n_asg)
    w_flat = route[:, MOE_TOPK:2 * MOE_TOPK].reshape(n_asg)
    is_e = e_flat[:, None] == jnp.arange(N_EXPERTS, dtype=i32)[None, :]
    counts = jnp.sum(is_e, axis=0, dtype=i32)
    order = jnp.argsort(e_flat).astype(i32)
    rank = jnp.argsort(order).astype(i32)
    padded = (counts + MOE_ROWS - 1) // MOE_ROWS * MOE_ROWS
    pad_end = jnp.cumsum(padded)
    pad_start = pad_end - padded
    start = jnp.cumsum(counts) - counts
    n_rows = n_asg + N_EXPERTS * MOE_ROWS
    n_blk = n_rows // MOE_ROWS
    blk_start = jnp.arange(n_blk, dtype=i32) * MOE_ROWS
    blk_expert = jnp.minimum(jnp.sum(pad_end[None, :] <= blk_start[:, None], axis=1, dtype=i32), N_EXPERTS - 1)
    within = (blk_start - pad_start[blk_expert])[:, None] + jnp.arange(MOE_ROWS, dtype=i32)[None, :]
    valid = (within < counts[blk_expert][:, None]).reshape(n_rows)
    asg = order[jnp.clip(start[blk_expert][:, None] + within, 0, n_asg - 1).reshape(n_rows)]
    row_tok = jnp.where(valid, asg // MOE_TOPK, n_tok)
    row_w = jnp.where(valid, w_flat[asg], 0.0)
    shift = jnp.sum(jnp.where(is_e, (pad_start - start)[None, :], 0), axis=1, dtype=i32)
    pos = (rank + shift).reshape(n_tok, MOE_TOPK)
    blk_info = jnp.concatenate([blk_expert, (pad_end[-1:] // MOE_ROWS).astype(i32)])
    h_pad = jnp.concatenate([h, jnp.zeros((1, h.shape[1]), h.dtype)], axis=0)
    return h_pad[row_tok], row_w.reshape(n_rows, 1), blk_info, pos


def _moe(route, h, wg, wu, wd):
    rows, row_w, blk_info, pos = _moe_dispatch(route, h)
    y = _expert_ffn(rows, row_w, blk_info, wg, wu, wd)
    return y[pos[:, 0]], y[pos[:, 1]]


def _final_kernel(x_ref, y0_ref, y1_ref, g_ref, o_ref):
    x = x_ref[...] + y0_ref[...] + y1_ref[...]
    ms = jnp.mean(x * x, axis=-1, keepdims=True)
    o_ref[...] = x * lax.rsqrt(ms + NORM_EPS) * g_ref[...]


def _final_norm(x, y0, y1, gain):
    T = x.shape[0]
    tm = ROW_TILE
    row = lambda i: (i, 0)
    return pl.pallas_call(
        _final_kernel, grid=(T // tm,),
        in_specs=[pl.BlockSpec((tm, D_MODEL), row)] * 3 + [pl.BlockSpec((1, D_MODEL), lambda i: (0, 0))],
        out_specs=pl.BlockSpec((tm, D_MODEL), row), out_shape=jax.ShapeDtypeStruct((T, D_MODEL), F32),
        compiler_params=_params(1), name="final_norm")(x, y0, y1, gain.reshape(1, D_MODEL))


def _rope_tables(positions):
    inv_freq = ROPE_THETA ** (-jnp.arange(0, HEAD_DIM, 2, dtype=F32) / HEAD_DIM)
    ang = positions.astype(F32).reshape(-1, 1) * inv_freq
    cos, sin = jnp.cos(ang), jnp.sin(ang)
    reps = LANES // HEAD_DIM
    return jnp.tile(jnp.concatenate([cos, cos], axis=1), (1, reps)), jnp.tile(jnp.concatenate([-sin, sin], axis=1), (1, reps))


def _heads(t, B, S):
    return t.reshape(B, S, -1, HEAD_DIM).transpose(0, 2, 1, 3)


def _heads_t(t, B, S):
    return t.reshape(B, S, -1, HEAD_DIM).transpose(0, 2, 3, 1)


def _tokens(o):
    B, H, S, d = o.shape
    return o.transpose(0, 2, 1, 3).reshape(B * S, H * d)


def _group_t(t, B, S):
    nq = S // Q_BLOCK
    t = t.reshape(B, nq, Q_BLOCK, -1, GROUP, HEAD_DIM).transpose(0, 3, 1, 5, 4, 2)
    return t.reshape(B, -1, nq, HEAD_DIM, GROUP * Q_BLOCK)


def _group_tokens(oT):
    B, Hkv, nq, d, _ = oT.shape
    o = oT.reshape(B, Hkv, nq, d, GROUP, Q_BLOCK).transpose(0, 2, 5, 1, 4, 3)
    return o.reshape(B * nq * Q_BLOCK, Hkv * GROUP * d)


def _keys_aug(k, block):
    S = k.shape[2]
    lane = jnp.arange(LANES - HEAD_DIM)[None, :]
    extra = ((jnp.arange(S)[:, None] % KEY_TILE) // block == lane) | (lane == AUG_ROWS)
    return jnp.concatenate([k, jnp.broadcast_to(extra.astype(k.dtype), k.shape[:2] + extra.shape)], axis=-1)


def _values_aug(vT):
    B, H, _, S = vT.shape
    ones = jnp.ones((B, H, 1, S), vT.dtype)
    return jnp.concatenate([vT, ones, jnp.zeros((B, H, V_AUG_ROWS - HEAD_DIM - 1, S), vT.dtype)], axis=2)


def _router_weights(router_group, router_expert):
    pad = jnp.zeros((D_MODEL, LANES - N_GROUPS - N_EXPERTS), F32)
    return jnp.concatenate([router_group.astype(F32), router_expert.astype(F32), pad], axis=1)


def _pad_cols(w, n):
    return jnp.concatenate([w, jnp.zeros((w.shape[0], n - w.shape[1]), w.dtype)], axis=1)


def _mixer_ab(proj, sinks, B, S):
    hd = lambda a, b: _heads(proj[:, a:b], B, S)
    hdt = lambda a, b: _heads_t(proj[:, a:b], B, S)
    oa = _swa_attention(hd(0, 512), hdt(512, 640), hd(640, 768), sinks)
    ob = _stick_attention(hd(768, 1280), hdt(1280, 1792), hd(1792, 2304))
    return jnp.concatenate([_tokens(oa), _tokens(ob)], axis=1)


def _mixer_cd(proj, kmean, B, S, k_pe, k_w1, k_w2, v_pe, v_w1, v_w2):
    hd = lambda a, b: _heads(proj[:, a:b], B, S)
    hdt = lambda a, b: _heads_t(proj[:, a:b], B, S)
    nblk = S // C_BLOCK
    km = kmean.reshape(B, nblk, C_KV_HEADS, HEAD_DIM).transpose(0, 2, 1, 3).astype(BF16)
    oc = _moba_attention(_group_t(proj[:, 0:512], B, S), _keys_aug(hd(512, 640), C_BLOCK), _values_aug(hdt(640, 768)), km)
    qd = hd(768, 1280)
    k_cmp = _compress(hd(1280, 1408), k_pe, k_w1, k_w2)
    v_cmp = _compress(hd(1408, 1536), v_pe, v_w1, v_w2)
    nc = (S - D_CMP_LEN) // D_CMP_STRIDE + 1
    ncp = S // D_CMP_STRIDE
    nsel = S // D_SEL_LEN
    c_start = jnp.arange(ncp) * D_CMP_STRIDE
    b_start = jnp.arange(nsel) * D_SEL_LEN
    overlap = ((c_start[:, None] <= b_start[None, :] + D_SEL_LEN - 1) & (c_start[:, None] + D_CMP_LEN - 1 >= b_start[None, :])
               & (jnp.arange(ncp)[:, None] < nc)).astype(BF16)
    qdT = _group_t(proj[:, 768:1280], B, S)
    o_cmp, biasT = _cmp_select(qdT, k_cmp, v_cmp.transpose(0, 1, 3, 2), overlap.T, nc)
    o_sel = _sel_attention(qdT, _keys_aug(hd(1536, 1664), D_SEL_LEN), _values_aug(hdt(1664, 1792)), biasT)
    o_win = _win_attention(qd, hdt(1792, 1920), hd(1920, 2048))
    return _group_tokens(oc), _group_tokens(o_cmp), _group_tokens(o_sel), _tokens(o_win)


def kernel(x, positions, ln_mix_0, w_in_0, sinks_0, w_out_0, ln_ffn_0, router_group_0, router_expert_0, expert_gate_0, expert_up_0, expert_down_0, ln_mix_1, w_in_1, cmp_k_pe_1, cmp_k_w1_1, cmp_k_w2_1, cmp_v_pe_1, cmp_v_w1_1, cmp_v_w2_1, w_out_1, ln_ffn_1, router_group_1, router_expert_1, expert_gate_1, expert_up_1, expert_down_1, ln_final):
    B, S, _ = x.shape
    T = B * S
    assert S % KEY_TILE == 0 and T % ROW_TILE == 0
    cos_t, sin_t = _rope_tables(positions)
    xf = x.reshape(T, D_MODEL)

    proj = _in_proj(xf, None, ln_mix_0, w_in_0.astype(BF16), cos_t, sin_t, rope_chunks=range(0, 5))[0]
    o_ab = _mixer_ab(proj, sinks_0, B, S)
    x1, h1, route1 = _out_proj([o_ab], None, xf, w_out_0.astype(BF16), ln_ffn_0, _router_weights(router_group_0, router_expert_0))
    y0, y1 = _moe(route1, h1, expert_gate_0, expert_up_0, expert_down_0)

    n_cols = 17 * LANES
    x2, proj, kmean, gd = _in_proj(x1, (y0, y1), ln_mix_1, _pad_cols(w_in_1.astype(BF16), n_cols), cos_t, sin_t,
                                   rope_chunks=(0, 1, 2, 3, 4, 6, 7, 8, 9, 10, 12, 14), kmean_chunk=4, gate_chunk=16)
    parts = _mixer_cd(proj, kmean, B, S, cmp_k_pe_1, cmp_k_w1_1, cmp_k_w2_1, cmp_v_pe_1, cmp_v_w1_1, cmp_v_w2_1)
    x3, h3, route3 = _out_proj(parts, gd, x2, w_out_1.astype(BF16), ln_ffn_1, _router_weights(router_group_1, router_expert_1))
    y0, y1 = _moe(route3, h3, expert_gate_1, expert_up_1, expert_down_1)
    return _final_norm(x3, y0, y1, ln_final).reshape(B, S, D_MODEL)
```

```python
import functools

import jax
import jax.numpy as jnp
from jax import lax
from jax.experimental import pallas as pl
from jax.experimental.pallas import tpu as pltpu

D_MODEL = 1024
HEAD_DIM = 64
HALF = HEAD_DIM // 2
ROPE_THETA = 10000.0
NORM_EPS = 1e-6
Q_BLOCK = 128
SCALE = HEAD_DIM ** -0.5

A_HEADS, A_KV_HEADS, A_WINDOW = 8, 2, 128
B_HEADS = 8
C_HEADS, C_KV_HEADS, C_BLOCK, C_TOPK = 8, 2, 256, 3
D_HEADS, D_KV_HEADS = 8, 2
D_CMP_LEN, D_CMP_STRIDE, D_CMP_HIDDEN = 32, 16, 256
D_SEL_LEN, D_SEL_TOPK, D_WINDOW = 64, 16, 512
N_GROUPS, EXPERTS_PER_GROUP, MOE_TOPK, EXPERT_HIDDEN = 4, 16, 2, 512
N_EXPERTS = N_GROUPS * EXPERTS_PER_GROUP
GROUP = 4

LANES = 128
ROW_TILE = 512
MOE_ROWS = 256
KEY_TILE = 512
FLASH_MIN_SUM = 1e-25
V_AUG_ROWS = 80
AUG_ROWS = 8
MASK_BIG = 32768.0
SB_EXIT = -104.0
STICK_HEADS = 4
VMEM_LIMIT = 56 * 1024 * 1024

F32 = jnp.float32
BF16 = jnp.bfloat16
NEG_INF = float("-inf")


def _iota(shape, dim):
    return lax.broadcasted_iota(jnp.int32, shape, dim)


def _dot(a, b):
    return jnp.dot(a, b, preferred_element_type=F32)


def _params(n_grid):
    return pltpu.CompilerParams(dimension_semantics=("arbitrary",) * n_grid, vmem_limit_bytes=VMEM_LIMIT)


def _moe_combine(x, y0_ref, y1_ref, route_ref):
    return x + (y0_ref[...] * route_ref[:, MOE_TOPK:MOE_TOPK + 1] + y1_ref[...] * route_ref[:, MOE_TOPK + 1:MOE_TOPK + 2])


def _in_proj_kernel(rope_chunks, kmean_chunk, gate_chunk, combine, *refs):
    refs = list(refs)
    x_ref = refs.pop(0)
    if combine:
        y0_ref, y1_ref, route_ref = refs.pop(0), refs.pop(0), refs.pop(0)
    g_ref, w_ref, cos_ref, sin_ref = refs[:4]
    outs = refs[4:]
    x = x_ref[...]
    if combine:
        x = _moe_combine(x, y0_ref, y1_ref, route_ref)
        xo_ref = outs.pop(0)
        xo_ref[...] = x
    o_ref = outs.pop(0)
    ms = jnp.mean(x * x, axis=-1, keepdims=True)
    h = (x * lax.rsqrt(ms + NORM_EPS) * g_ref[...]).astype(BF16)
    tm = x.shape[0]
    n_cols = w_ref.shape[1]
    cos = cos_ref[...]
    sin = sin_ref[...]
    first_half = (_iota((tm, LANES), 1) & (HEAD_DIM - 1)) < HALF
    for c in range(n_cols // LANES):
        ch = _dot(h, w_ref[:, c * LANES:(c + 1) * LANES])
        if c in rope_chunks:
            partner = jnp.where(first_half, pltpu.roll(ch, LANES - HALF, 1), pltpu.roll(ch, HALF, 1))
            ch = ch * cos + partner * sin
        if c == kmean_chunk:
            km_ref = outs[0]
            km_ref[...] = jnp.mean(ch.reshape(tm // C_BLOCK, C_BLOCK, LANES), axis=1).reshape(tm // C_BLOCK, 1, LANES)
        if c == gate_chunk:
            outs[-1][...] = ch
        o_ref[:, c * LANES:(c + 1) * LANES] = ch.astype(BF16)


def _in_proj(x, ys, gain, w, cos_t, sin_t, rope_chunks, kmean_chunk=None, gate_chunk=None):
    T = x.shape[0]
    n_cols = w.shape[1]
    tm = ROW_TILE
    combine = ys is not None
    row = lambda i: (i, 0)
    fixed = lambda i: (0, 0)
    in_specs = [pl.BlockSpec((tm, D_MODEL), row)]
    args = [x]
    if combine:
        in_specs += [pl.BlockSpec((tm, D_MODEL), row)] * 2 + [pl.BlockSpec((tm, LANES), row)]
        args += list(ys)
    in_specs += [pl.BlockSpec((1, D_MODEL), fixed), pl.BlockSpec((D_MODEL, n_cols), fixed),
                 pl.BlockSpec((tm, LANES), row), pl.BlockSpec((tm, LANES), row)]
    args += [gain.reshape(1, D_MODEL), w, cos_t, sin_t]
    out_shape, out_specs = [], []
    if combine:
        out_shape.append(jax.ShapeDtypeStruct((T, D_MODEL), F32))
        out_specs.append(pl.BlockSpec((tm, D_MODEL), row))
    out_shape.append(jax.ShapeDtypeStruct((T, n_cols), BF16))
    out_specs.append(pl.BlockSpec((tm, n_cols), row))
    if kmean_chunk is not None:
        out_shape.append(jax.ShapeDtypeStruct((T // C_BLOCK, 1, LANES), F32))
        out_specs.append(pl.BlockSpec((tm // C_BLOCK, 1, LANES), lambda i: (i, 0, 0)))
    if gate_chunk is not None:
        out_shape.append(jax.ShapeDtypeStruct((T, LANES), F32))
        out_specs.append(pl.BlockSpec((tm, LANES), row))
    kern = functools.partial(_in_proj_kernel, tuple(rope_chunks), kmean_chunk, gate_chunk, combine)
    return pl.pallas_call(kern, grid=(T // tm,), in_specs=in_specs, out_specs=out_specs, out_shape=out_shape,
                          compiler_params=_params(1), name="in_proj")(*args)


def _swa_kernel(qT_ref, kp_ref, ko_ref, vp_ref, vo_ref, sink_ref, o_ref):
    n = pl.program_id(2)
    qs = (qT_ref[0, 0, 0].astype(F32) * SCALE).astype(BF16)
    R = qs.shape[1]
    k = jnp.concatenate([kp_ref[0, 0], ko_ref[0, 0]], axis=0)
    vT = jnp.concatenate([vp_ref[0, 0], vo_ref[0, 0]], axis=1)
    s = _dot(k, qs)
    qpos = _iota((2 * Q_BLOCK, R), 1) & (Q_BLOCK - 1)
    ki = _iota((2 * Q_BLOCK, R), 0)
    d = qpos + Q_BLOCK - ki
    ok = (d >= 0) & (d < A_WINDOW) & ((ki >= Q_BLOCK) | (n > 0))
    s = jnp.where(ok, s, NEG_INF)
    sink = sink_ref[0]
    m = jnp.maximum(jnp.max(s, axis=0, keepdims=True), sink)
    p = jnp.exp(s - m)
    den = jnp.sum(p, axis=0, keepdims=True) + jnp.exp(sink - m)
    o_ref[0, 0, 0] = (_dot(vT, p.astype(BF16)) * (1.0 / den)).astype(o_ref.dtype)


def _swa_attention(qT, k, vT, sinks):
    B, Hkv, nb, _, R = qT.shape
    sink_row = jnp.repeat(sinks.astype(F32).reshape(Hkv, GROUP), Q_BLOCK, axis=1).reshape(Hkv, 1, R)
    prev = lambda n: jnp.maximum(n - 1, 0)
    in_specs = [
        pl.BlockSpec((1, 1, 1, HEAD_DIM, R), lambda b, h, n: (b, h, n, 0, 0)),
        pl.BlockSpec((1, 1, Q_BLOCK, HEAD_DIM), lambda b, h, n: (b, h, prev(n), 0)),
        pl.BlockSpec((1, 1, Q_BLOCK, HEAD_DIM), lambda b, h, n: (b, h, n, 0)),
        pl.BlockSpec((1, 1, HEAD_DIM, Q_BLOCK), lambda b, h, n: (b, h, 0, prev(n))),
        pl.BlockSpec((1, 1, HEAD_DIM, Q_BLOCK), lambda b, h, n: (b, h, 0, n)),
        pl.BlockSpec((1, 1, R), lambda b, h, n: (h, 0, 0)),
    ]
    return pl.pallas_call(
        _swa_kernel, grid=(B, Hkv, nb), in_specs=in_specs,
        out_specs=pl.BlockSpec((1, 1, 1, HEAD_DIM, R), lambda b, h, n: (b, h, n, 0, 0)),
        out_shape=jax.ShapeDtypeStruct((B, Hkv, nb, HEAD_DIM, R), BF16),
        compiler_params=_params(3), name="swa_attention")(qT, k, k, vT, vT, sink_row)


def _stick_kernel(q_ref, kT_ref, v_ref, o_ref):
    n = pl.program_id(2)
    tq = tk = Q_BLOCK
    heads = q_ref.shape[1]
    row = _iota((tq, tk), 0)
    col = _iota((tk, tk), 1)
    upper = jnp.where(_iota((tk, tk), 0) > col, 1.0, 0.0).astype(BF16)
    tpos = n * tq + row

    def body(carry):
        kb, _, cs, accs = carry
        start = pl.multiple_of(kb * tk, tk)
        past = (start + col) < tpos
        hs = range(heads)
        zs = [_dot(q_ref[0, h], kT_ref[0, h, :, pl.ds(start, tk)]) * SCALE for h in hs]
        sps = [jnp.maximum(z, 0.0) + jnp.log1p(jnp.exp(-jnp.abs(z))) for z in zs]
        stays = [jnp.where(past, -sp, 0.0) for sp in sps]
        his = [st.astype(BF16) for st in stays]
        los = [(st - hi.astype(F32)).astype(BF16) for st, hi in zip(stays, his)]
        betweens = [_dot(hi, upper) + _dot(lo, upper) for hi, lo in zip(his, los)]
        ws = [jnp.where(past, jnp.exp(zs[h] - sps[h] + betweens[h] + cs[h]), 0.0).astype(BF16) for h in hs]
        new_accs = [accs[h] + _dot(ws[h], v_ref[0, h, pl.ds(start, tk), :]) for h in hs]
        new_cs = [cs[h] + jnp.sum(stays[h], axis=-1, keepdims=True) for h in hs]
        worst = functools.reduce(jnp.maximum, new_cs)
        return kb - 1, jnp.max(worst) > SB_EXIT, tuple(new_cs), tuple(new_accs)

    def cond(carry):
        kb, alive, _, _ = carry
        return (kb >= 0) & alive

    init = (n, jnp.array(True), (jnp.zeros((tq, 1), F32),) * heads, (jnp.zeros((tq, HEAD_DIM), F32),) * heads)
    _, _, _, accs = lax.while_loop(cond, body, init)
    for h in range(heads):
        o_ref[0, h] = accs[h].astype(o_ref.dtype)


def _stick_attention(q, kT, v):
    B, H, S, _ = q.shape
    nq = S // Q_BLOCK
    hb = STICK_HEADS
    resident = dict(pipeline_mode=pl.Buffered(1))
    in_specs = [
        pl.BlockSpec((1, hb, Q_BLOCK, HEAD_DIM), lambda b, h, n: (b, h, n, 0)),
        pl.BlockSpec((1, hb, HEAD_DIM, S), lambda b, h, n: (b, h, 0, 0), **resident),
        pl.BlockSpec((1, hb, S, HEAD_DIM), lambda b, h, n: (b, h, 0, 0), **resident),
    ]
    return pl.pallas_call(
        _stick_kernel, grid=(B, H // hb, nq), in_specs=in_specs,
        out_specs=pl.BlockSpec((1, hb, Q_BLOCK, HEAD_DIM), lambda b, h, n: (b, h, n, 0)),
        out_shape=jax.ShapeDtypeStruct((B, H, S, HEAD_DIM), BF16),
        compiler_params=_params(3), name="stick_attention")(q, kT, v)


def _max_key_sqnorm(ka_ref, out_ref):
    S = ka_ref.shape[2]
    tk = KEY_TILE
    is_key_lane = _iota((tk, LANES), 1) < HEAD_DIM
    ones = jnp.ones((LANES, LANES), BF16)

    def body(i, mx):
        k = jnp.where(is_key_lane, ka_ref[0, 0, pl.ds(pl.multiple_of(i * tk, tk), tk), :].astype(F32), 0.0)
        sq = _dot((k * k).astype(BF16), ones)
        return jnp.maximum(mx, jnp.max(sq, axis=0, keepdims=True))

    mx = lax.fori_loop(0, S // tk, body, jnp.zeros((1, LANES), F32))
    out_ref[...] = jnp.broadcast_to(mx, out_ref.shape)


def _flash_scratch(R):
    return [pltpu.VMEM((AUG_ROWS, LANES), F32), pltpu.VMEM((2, KEY_TILE, R), BF16), pltpu.VMEM((V_AUG_ROWS, R), F32)]


def _masked_flash_t(n, qT, bias_rows, ka_ref, vTa_ref, ksq_ref, p_scr, acc_scr):
    R = qT.shape[1]
    tk = KEY_TILE
    diag = (n * Q_BLOCK) // tk
    zpad = jnp.zeros((LANES - HEAD_DIM - 2 * AUG_ROWS, R), F32)
    causal = diag * tk + _iota((tk, R), 0) <= n * Q_BLOCK + (_iota((tk, R), 1) & (Q_BLOCK - 1))

    def scores(kt, ref_rows):
        st = pl.multiple_of(kt * tk, tk)
        low = jnp.concatenate([bias_rows(kt), ref_rows, zpad], axis=0).astype(BF16)
        return _dot(ka_ref[0, 0, pl.ds(st, tk), :], jnp.concatenate([qT, low], axis=0))

    def values(kt):
        return vTa_ref[0, 0, :, pl.ds(pl.multiple_of(kt * tk, tk), tk)]

    qsq = jnp.sum(jnp.square(qT.astype(F32)), axis=0, keepdims=True)
    ksq = jnp.concatenate([ksq_ref[0:1, :]] * (R // LANES), axis=1)
    ref = jnp.sqrt(qsq * ksq) * 1.05
    ref_rows = jnp.where(_iota((AUG_ROWS, R), 0) == 0, -ref, 0.0)

    def fast_body(kt, carry):
        slot = kt & 1
        s = scores(kt, ref_rows)
        acc_scr[...] += _dot(values(jnp.maximum(kt - 1, 0)), p_scr[1 - slot])
        p_scr[slot] = jnp.exp(s).astype(BF16)
        return carry

    p_scr[1] = jnp.zeros((tk, R), BF16)
    acc_scr[...] = jnp.zeros(acc_scr.shape, F32)
    lax.fori_loop(0, diag, fast_body, 0)
    acc = acc_scr[...] + _dot(values(jnp.maximum(diag - 1, 0)), p_scr[1 - (diag & 1)])
    s = jnp.where(causal, scores(diag, ref_rows), -1e30)
    acc = acc + _dot(values(diag), jnp.exp(s).astype(BF16))

    def running_max_path(_):
        no_ref = jnp.zeros((AUG_ROWS, R), F32)

        def update(kt, s, m, acc):
            m_new = jnp.maximum(m, jnp.max(s, axis=0, keepdims=True))
            p = jnp.exp(s - m_new).astype(BF16)
            return m_new, jnp.exp(m - m_new) * acc + _dot(values(kt), p)

        init = (jnp.full((1, R), -1e30, F32), jnp.zeros((vTa_ref.shape[2], R), F32))
        m, acc = lax.fori_loop(0, diag, lambda kt, c: update(kt, scores(kt, no_ref), *c), init)
        return update(diag, jnp.where(causal, scores(diag, no_ref), -1e30), m, acc)[1]

    healthy = jnp.min(acc[HEAD_DIM:HEAD_DIM + 1, :]) >= FLASH_MIN_SUM
    acc = lax.cond(healthy, lambda _: acc, running_max_path, 0)
    return acc[0:HEAD_DIM] * (1.0 / acc[HEAD_DIM:HEAD_DIM + 1])


def _moba_kernel(qT_ref, ka_ref, vTa_ref, km_ref, o_ref, bias_scr, ksq_scr, p_scr, acc_scr):
    n = pl.program_id(2)
    pl.when(n == 0)(lambda: _max_key_sqnorm(ka_ref, ksq_scr))
    nblk = km_ref.shape[2]
    qT = qT_ref[0, 0, 0]
    R = qT.shape[1]
    own = (n * Q_BLOCK) // C_BLOCK
    gate = _dot(km_ref[0, 0], qT)
    blk = _iota((nblk, R), 0)
    gate = jnp.where(blk < own, gate, NEG_INF)
    bias = jnp.where(blk == own, 0.0, -MASK_BIG)
    for _ in range(C_TOPK):
        mx = jnp.max(gate, axis=0, keepdims=True)
        idx = jnp.min(jnp.where(gate == mx, blk, nblk), axis=0, keepdims=True)
        hit = blk == idx
        bias = jnp.where(hit & (mx > NEG_INF), 0.0, bias)
        gate = jnp.where(hit, NEG_INF, gate)
    per = KEY_TILE // C_BLOCK
    rows = bias_scr.shape[0]
    r = _iota((rows, nblk), 0)
    spread = jnp.where(((r & (AUG_ROWS - 1)) < per) & (_iota((rows, nblk), 1) == (r >> 3) * per + (r & (AUG_ROWS - 1))), 1.0, 0.0)
    bias_scr[...] = _dot(spread.astype(BF16), bias.astype(BF16))
    qs = (qT.astype(F32) * SCALE).astype(BF16)
    oT = _masked_flash_t(n, qs, lambda kt: bias_scr[pl.ds(pl.multiple_of(kt * AUG_ROWS, AUG_ROWS), AUG_ROWS), :],
                         ka_ref, vTa_ref, ksq_scr, p_scr, acc_scr)
    o_ref[0, 0, 0] = oT.astype(o_ref.dtype)


def _moba_attention(qT, ka, vTa, km):
    B, Hkv, nq, _, R = qT.shape
    S = ka.shape[2]
    nblk = km.shape[2]
    nkt = S // KEY_TILE
    in_specs = [
        pl.BlockSpec((1, 1, 1, HEAD_DIM, R), lambda b, h, n: (b, h, n, 0, 0)),
        pl.BlockSpec((1, 1, S, LANES), lambda b, h, n: (b, h, 0, 0)),
        pl.BlockSpec((1, 1, V_AUG_ROWS, S), lambda b, h, n: (b, h, 0, 0)),
        pl.BlockSpec((1, 1, nblk, HEAD_DIM), lambda b, h, n: (b, h, 0, 0)),
    ]
    return pl.pallas_call(
        _moba_kernel, grid=(B, Hkv, nq), in_specs=in_specs,
        out_specs=pl.BlockSpec((1, 1, 1, HEAD_DIM, R), lambda b, h, n: (b, h, n, 0, 0)),
        out_shape=jax.ShapeDtypeStruct((B, Hkv, nq, HEAD_DIM, R), BF16),
        scratch_shapes=[pltpu.VMEM((nkt * AUG_ROWS, R), F32)] + _flash_scratch(R),
        compiler_params=_params(3), name="moba_attention")(qT, ka, vTa, km)


def _compress_kernel(u_ref, us_ref, pe_ref, w1_ref, w2_ref, o_ref):
    a = (u_ref[0, 0].astype(F32) + pe_ref[0:1, :]).astype(BF16)
    b = (us_ref[0, 0].astype(F32) + pe_ref[1:2, :]).astype(BF16)
    pre = _dot(a, w1_ref[0]) + _dot(b, w1_ref[1])
    hid = jax.nn.gelu(pre)
    o_ref[0, 0] = _dot(hid.astype(BF16), w2_ref[...]).astype(o_ref.dtype)


def _compress(t, pe, w1, w2):
    B, H, S, _ = t.shape
    nrow = S // D_CMP_STRIDE
    width = D_CMP_STRIDE * HEAD_DIM
    u = t.reshape(B, H, nrow, width)
    us = jnp.concatenate([u[:, :, 1:], jnp.zeros((B, H, 1, width), u.dtype)], axis=2)
    blk = lambda b, h: (b, h, 0, 0)
    in_specs = [
        pl.BlockSpec((1, 1, nrow, width), blk), pl.BlockSpec((1, 1, nrow, width), blk),
        pl.BlockSpec((2, width), lambda b, h: (0, 0)),
        pl.BlockSpec((2, width, D_CMP_HIDDEN), lambda b, h: (0, 0, 0)),
        pl.BlockSpec((D_CMP_HIDDEN, HEAD_DIM), lambda b, h: (0, 0)),
    ]
    return pl.pallas_call(
        _compress_kernel, grid=(B, H), in_specs=in_specs,
        out_specs=pl.BlockSpec((1, 1, nrow, HEAD_DIM), blk),
        out_shape=jax.ShapeDtypeStruct((B, H, nrow, HEAD_DIM), BF16),
        compiler_params=_params(2), name="nsa_compress")(
            u, us, pe.astype(F32).reshape(2, width), w1.astype(BF16).reshape(2, width, D_CMP_HIDDEN), w2.astype(BF16))


def _cmp_select_kernel(nc, qT_ref, kc_ref, vcT_ref, ovT_ref, oc_ref, bias_ref):
    n = pl.program_id(2)
    ncp = kc_ref.shape[2]
    nsel = ovT_ref.shape[0]
    qs = (qT_ref[0, 0, 0].astype(F32) * SCALE).astype(BF16)
    R = qs.shape[1]
    s = _dot(kc_ref[0, 0], qs)
    tpos = n * Q_BLOCK + (_iota((ncp, R), 1) & (Q_BLOCK - 1))
    c = _iota((ncp, R), 0)
    ok = (c * D_CMP_STRIDE + (D_CMP_LEN - 1) <= tpos) & (c < nc)
    s = jnp.where(ok, s, NEG_INF)
    m = jnp.max(s, axis=0, keepdims=True)
    m = jnp.where(m > NEG_INF, m, 0.0)
    e = jnp.exp(s - m)
    den = jnp.sum(e, axis=0, keepdims=True)
    p = (e * (1.0 / jnp.where(den > 0, den, 1.0))).astype(BF16)
    oc_ref[0, 0, 0] = _dot(vcT_ref[0, 0], p)
    imp_heads = _dot(ovT_ref[...], p)
    imp = imp_heads[:, 0:Q_BLOCK]
    for g in range(1, GROUP):
        imp = imp + imp_heads[:, g * Q_BLOCK:(g + 1) * Q_BLOCK]
    t = n * Q_BLOCK + _iota((nsel, Q_BLOCK), 1)
    j = _iota((nsel, Q_BLOCK), 0)
    cur = t >> 6
    allowed = j * D_SEL_LEN <= t
    forced = (j == 0) | (j == cur) | (j == cur - 1)
    imp = jnp.where(allowed, imp, NEG_INF)
    imp = jnp.where(allowed & forced, float("inf"), imp)
    bias = jnp.full((nsel, Q_BLOCK), -MASK_BIG, F32)
    for _ in range(D_SEL_TOPK):
        mx = jnp.max(imp, axis=0, keepdims=True)
        idx = jnp.min(jnp.where(imp == mx, j, nsel), axis=0, keepdims=True)
        hit = j == idx
        bias = jnp.where(hit & (mx > NEG_INF), 0.0, bias)
        imp = jnp.where(hit, NEG_INF, imp)
    bias_ref[0, 0, 0] = bias


def _cmp_select(qT, kc, vcT, overlapT, nc):
    B, Hkv, nq, _, R = qT.shape
    ncp = kc.shape[2]
    nsel = overlapT.shape[0]
    in_specs = [
        pl.BlockSpec((1, 1, 1, HEAD_DIM, R), lambda b, h, n: (b, h, n, 0, 0)),
        pl.BlockSpec((1, 1, ncp, HEAD_DIM), lambda b, h, n: (b, h, 0, 0)),
        pl.BlockSpec((1, 1, HEAD_DIM, ncp), lambda b, h, n: (b, h, 0, 0)),
        pl.BlockSpec((nsel, ncp), lambda b, h, n: (0, 0)),
    ]
    out_specs = [
        pl.BlockSpec((1, 1, 1, HEAD_DIM, R), lambda b, h, n: (b, h, n, 0, 0)),
        pl.BlockSpec((1, 1, 1, nsel, Q_BLOCK), lambda b, h, n: (b, h, n, 0, 0)),
    ]
    out_shape = [jax.ShapeDtypeStruct((B, Hkv, nq, HEAD_DIM, R), F32), jax.ShapeDtypeStruct((B, Hkv, nq, nsel, Q_BLOCK), F32)]
    return pl.pallas_call(
        functools.partial(_cmp_select_kernel, nc), grid=(B, Hkv, nq), in_specs=in_specs, out_specs=out_specs,
        out_shape=out_shape, compiler_params=_params(3), name="nsa_cmp_select")(qT, kc, vcT, overlapT)


def _sel_kernel(qT_ref, ka_ref, vTa_ref, bias_ref, o_ref, ksq_scr, p_scr, acc_scr):
    n = pl.program_id(2)
    pl.when(n == 0)(lambda: _max_key_sqnorm(ka_ref, ksq_scr))
    qs = (qT_ref[0, 0, 0].astype(F32) * SCALE).astype(BF16)

    def bias_rows(kt):
        b = bias_ref[0, 0, 0, pl.ds(pl.multiple_of(kt * AUG_ROWS, AUG_ROWS), AUG_ROWS), :]
        return jnp.concatenate([b] * GROUP, axis=1)

    o_ref[0, 0, 0] = _masked_flash_t(n, qs, bias_rows, ka_ref, vTa_ref, ksq_scr, p_scr, acc_scr)


def _sel_attention(qT, ka, vTa, biasT):
    B, Hkv, nq, _, R = qT.shape
    S = ka.shape[2]
    nsel = biasT.shape[3]
    assert KEY_TILE // D_SEL_LEN == AUG_ROWS
    in_specs = [
        pl.BlockSpec((1, 1, 1, HEAD_DIM, R), lambda b, h, n: (b, h, n, 0, 0)),
        pl.BlockSpec((1, 1, S, LANES), lambda b, h, n: (b, h, 0, 0)),
        pl.BlockSpec((1, 1, V_AUG_ROWS, S), lambda b, h, n: (b, h, 0, 0)),
        pl.BlockSpec((1, 1, 1, nsel, Q_BLOCK), lambda b, h, n: (b, h, n, 0, 0)),
    ]
    return pl.pallas_call(
        _sel_kernel, grid=(B, Hkv, nq), in_specs=in_specs,
        out_specs=pl.BlockSpec((1, 1, 1, HEAD_DIM, R), lambda b, h, n: (b, h, n, 0, 0)),
        out_shape=jax.ShapeDtypeStruct((B, Hkv, nq, HEAD_DIM, R), F32),
        scratch_shapes=_flash_scratch(R),
        compiler_params=_params(3), name="nsa_selected")(qT, ka, vTa, biasT)


def _win_kernel(span, qT_ref, k_ref, vT_ref, o_ref):
    n = pl.program_id(2)
    qs = (qT_ref[0, 0, 0].astype(F32) * SCALE).astype(BF16)
    R = qs.shape[1]
    start = pl.multiple_of(jnp.maximum(n * Q_BLOCK + Q_BLOCK - span, 0), Q_BLOCK)
    s = _dot(k_ref[0, 0, pl.ds(start, span), :], qs)
    tpos = n * Q_BLOCK + (_iota((span, R), 1) & (Q_BLOCK - 1))
    d = tpos - (start + _iota((span, R), 0))
    s = jnp.where((d >= 0) & (d < D_WINDOW), s, NEG_INF)
    m = jnp.max(s, axis=0, keepdims=True)
    p = jnp.exp(s - m)
    l = jnp.sum(p, axis=0, keepdims=True)
    o_ref[0, 0, 0] = _dot(vT_ref[0, 0, :, pl.ds(start, span)], p.astype(BF16)) * (1.0 / l)


def _win_attention(qT, k, vT):
    B, Hkv, nq, _, R = qT.shape
    S = k.shape[2]
    span = min(D_WINDOW + Q_BLOCK, S)
    in_specs = [
        pl.BlockSpec((1, 1, 1, HEAD_DIM, R), lambda b, h, n: (b, h, n, 0, 0)),
        pl.BlockSpec((1, 1, S, HEAD_DIM), lambda b, h, n: (b, h, 0, 0)),
        pl.BlockSpec((1, 1, HEAD_DIM, S), lambda b, h, n: (b, h, 0, 0)),
    ]
    return pl.pallas_call(
        functools.partial(_win_kernel, span), grid=(B, Hkv, nq), in_specs=in_specs,
        out_specs=pl.BlockSpec((1, 1, 1, HEAD_DIM, R), lambda b, h, n: (b, h, n, 0, 0)),
        out_shape=jax.ShapeDtypeStruct((B, Hkv, nq, HEAD_DIM, R), F32),
        compiler_params=_params(3), name="nsa_window")(qT, k, vT)


def _route(logits):
    tm = logits.shape[0]
    lane = _iota((tm, LANES), 1)
    gl = jnp.where(lane < N_GROUPS, logits, NEG_INF)
    gmax = jnp.max(gl, axis=-1, keepdims=True)
    gidx = jnp.min(jnp.where(gl == gmax, lane, LANES), axis=-1, keepdims=True)
    g_prob = 1.0 / jnp.sum(jnp.exp(gl - gmax), axis=-1, keepdims=True)
    elane = lane - N_GROUPS
    in_group = (elane >= 0) & (elane < N_EXPERTS) & ((elane >> 4) == gidx)
    el = jnp.where(in_group, logits, NEG_INF)
    ee = jnp.exp(el - jnp.max(el, axis=-1, keepdims=True))
    ep = jnp.where(in_group, ee / jnp.sum(ee, axis=-1, keepdims=True), -1.0)
    p1 = jnp.max(ep, axis=-1, keepdims=True)
    i1 = jnp.min(jnp.where(ep == p1, lane, LANES), axis=-1, keepdims=True)
    ep2 = jnp.where(lane == i1, -1.0, ep)
    p2 = jnp.max(ep2, axis=-1, keepdims=True)
    i2 = jnp.min(jnp.where(ep2 == p2, lane, LANES), axis=-1, keepdims=True)
    den = p1 + p2
    vals = [(i1 - N_GROUPS).astype(F32), (i2 - N_GROUPS).astype(F32), g_prob * p1 / den, g_prob * p2 / den]
    out = jnp.zeros((tm, LANES), F32)
    for k, val in enumerate(vals):
        out = jnp.where(lane == k, val, out)
    return out


def _out_tail(x_new, gain_ref, wr_ref, xo_ref, h_ref, route_ref):
    xo_ref[...] = x_new
    ms = jnp.mean(x_new * x_new, axis=-1, keepdims=True)
    h = x_new * lax.rsqrt(ms + NORM_EPS) * gain_ref[...]
    h_ref[...] = h.astype(BF16)
    logits = jnp.dot(h, wr_ref[...], preferred_element_type=F32, precision=lax.Precision.HIGHEST)
    route_ref[...] = _route(logits)


def _out_proj_kernel(o_ref, x_ref, w_ref, gain_ref, wr_ref, xo_ref, h_ref, route_ref):
    x_new = x_ref[...] + _dot(o_ref[...], w_ref[...])
    _out_tail(x_new, gain_ref, wr_ref, xo_ref, h_ref, route_ref)


def _out_proj_nsa_kernel(oc_ref, b0_ref, b1_ref, b2_ref, gd_ref, x_ref, w_ref, gain_ref, wr_ref, xo_ref, h_ref, route_ref):
    half = D_HEADS * HEAD_DIM
    g = jax.nn.sigmoid(gd_ref[...])
    g_hi = g.astype(BF16)
    g_lo = (g - g_hi.astype(F32)).astype(BF16)
    src = _iota((LANES, half), 0)
    head3 = (_iota((LANES, half), 1) >> 6) * 3
    od = None
    for br, b_ref in enumerate((b0_ref, b1_ref, b2_ref)):
        spread = jnp.where(src == head3 + br, 1.0, 0.0).astype(BF16)
        term = (_dot(g_hi, spread) + _dot(g_lo, spread)) * b_ref[...]
        od = term if od is None else od + term
    x_new = x_ref[...] + _dot(oc_ref[...], w_ref[0:half, :]) + _dot(od.astype(BF16), w_ref[half:, :])
    _out_tail(x_new, gain_ref, wr_ref, xo_ref, h_ref, route_ref)


def _out_proj(o_parts, gd, x, w_out, gain, w_router):
    T = x.shape[0]
    tm = ROW_TILE
    row = lambda i: (i, 0)
    fixed = lambda i: (0, 0)
    if gd is None:
        kern = _out_proj_kernel
        args = [o_parts[0]]
        in_specs = [pl.BlockSpec((tm, D_MODEL), row)]
    else:
        kern = _out_proj_nsa_kernel
        args = list(o_parts) + [gd]
        in_specs = [pl.BlockSpec((tm, D_MODEL // 2), row)] * 4 + [pl.BlockSpec((tm, LANES), row)]
    args += [x, w_out, gain.reshape(1, D_MODEL), w_router]
    in_specs += [pl.BlockSpec((tm, D_MODEL), row), pl.BlockSpec((D_MODEL, D_MODEL), fixed),
                 pl.BlockSpec((1, D_MODEL), fixed), pl.BlockSpec((D_MODEL, LANES), fixed)]
    out_shape = [jax.ShapeDtypeStruct((T, D_MODEL), F32), jax.ShapeDtypeStruct((T, D_MODEL), BF16),
                 jax.ShapeDtypeStruct((T, LANES), F32)]
    out_specs = [pl.BlockSpec((tm, D_MODEL), row), pl.BlockSpec((tm, D_MODEL), row), pl.BlockSpec((tm, LANES), row)]
    return pl.pallas_call(kern, grid=(T // tm,), in_specs=in_specs, out_specs=out_specs, out_shape=out_shape,
                          compiler_params=_params(1), name="out_proj_router")(*args)


def _expert_kernel(be_ref, rows_ref, wg_ref, wu_ref, wd_ref, y_ref, wg_s, wu_s, wd_s):
    i = pl.program_id(0)
    n_blk = pl.num_programs(0)

    @pl.when((i == 0) | (be_ref[i] != be_ref[jnp.maximum(i - 1, 0)]))
    def _():
        wg_s[...] = wg_ref[0].astype(BF16)
        wu_s[...] = wu_ref[0].astype(BF16)
        wd_s[...] = wd_ref[0].astype(BF16)

    @pl.when(i < be_ref[n_blk])
    def _():
        xb = rows_ref[...]
        hid = jax.nn.silu(_dot(xb, wg_s[...])) * _dot(xb, wu_s[...])
        y_ref[...] = _dot(hid.astype(BF16), wd_s[...])

    @pl.when(i >= be_ref[n_blk])
    def _():
        y_ref[...] = jnp.zeros(y_ref.shape, F32)


def _expert_ffn(rows, blk_info, wg, wu, wd):
    n_rows = rows.shape[0]
    n_blk = n_rows // MOE_ROWS
    grid_spec = pltpu.PrefetchScalarGridSpec(
        num_scalar_prefetch=1, grid=(n_blk,),
        in_specs=[
            pl.BlockSpec((MOE_ROWS, D_MODEL), lambda i, be: (i, 0)),
            pl.BlockSpec((1, D_MODEL, EXPERT_HIDDEN), lambda i, be: (be[i], 0, 0)),
            pl.BlockSpec((1, D_MODEL, EXPERT_HIDDEN), lambda i, be: (be[i], 0, 0)),
            pl.BlockSpec((1, EXPERT_HIDDEN, D_MODEL), lambda i, be: (be[i], 0, 0)),
        ],
        out_specs=pl.BlockSpec((MOE_ROWS, D_MODEL), lambda i, be: (i, 0)),
        scratch_shapes=[pltpu.VMEM((D_MODEL, EXPERT_HIDDEN), BF16), pltpu.VMEM((D_MODEL, EXPERT_HIDDEN), BF16),
                        pltpu.VMEM((EXPERT_HIDDEN, D_MODEL), BF16)])
    return pl.pallas_call(
        _expert_kernel, grid_spec=grid_spec, out_shape=jax.ShapeDtypeStruct((n_rows, D_MODEL), F32),
        compiler_params=_params(1), name="expert_ffn")(blk_info, rows, wg, wu, wd)


def _moe_dispatch(route, h):
    n_tok = h.shape[0]
    n_asg = n_tok * MOE_TOPK
    i32 = jnp.int32
    e_flat = route[:, 0:MOE_TOPK].astype(i32).reshape(n_asg)
    is_e = e_flat[:, None] == jnp.arange(N_EXPERTS, dtype=i32)[None, :]
    counts = jnp.sum(is_e, axis=0, dtype=i32)
    order = jnp.argsort(e_flat).astype(i32)
    rank = jnp.argsort(order).astype(i32)
    padded = (counts + MOE_ROWS - 1) // MOE_ROWS * MOE_ROWS
    pad_end = jnp.cumsum(padded)
    pad_start = pad_end - padded
    start = jnp.cumsum(counts) - counts
    n_rows = n_asg + N_EXPERTS * MOE_ROWS
    n_blk = n_rows // MOE_ROWS
    blk_start = jnp.arange(n_blk, dtype=i32) * MOE_ROWS
    blk_expert = jnp.minimum(jnp.sum(pad_end[None, :] <= blk_start[:, None], axis=1, dtype=i32), N_EXPERTS - 1)
    tok_sorted = jnp.concatenate([order // MOE_TOPK, jnp.full((MOE_ROWS,), n_tok, i32)])
    blk_off = blk_start - pad_start[blk_expert]
    run = jnp.clip(start[blk_expert] + blk_off, 0, n_asg)
    row_tok = jax.vmap(lambda s: lax.dynamic_slice(tok_sorted, (s,), (MOE_ROWS,)))(run)
    valid = blk_off[:, None] + jnp.arange(MOE_ROWS, dtype=i32)[None, :] < counts[blk_expert][:, None]
    row_tok = jnp.where(valid, row_tok, n_tok).reshape(n_rows)
    shift = jnp.sum(jnp.where(is_e, (pad_start - start)[None, :], 0), axis=1, dtype=i32)
    pos = (rank + shift).reshape(n_tok, MOE_TOPK)
    blk_info = jnp.concatenate([blk_expert, (pad_end[-1:] // MOE_ROWS).astype(i32)])
    h_pad = jnp.concatenate([h, jnp.zeros((1, h.shape[1]), h.dtype)], axis=0)
    return h_pad[row_tok], blk_info, pos


def _moe(route, h, wg, wu, wd):
    rows, blk_info, pos = _moe_dispatch(route, h)
    y = _expert_ffn(rows, blk_info, wg, wu, wd)
    return y[pos[:, 0]], y[pos[:, 1]], route


def _final_kernel(x_ref, y0_ref, y1_ref, route_ref, g_ref, o_ref):
    x = _moe_combine(x_ref[...], y0_ref, y1_ref, route_ref)
    ms = jnp.mean(x * x, axis=-1, keepdims=True)
    o_ref[...] = x * lax.rsqrt(ms + NORM_EPS) * g_ref[...]


def _final_norm(x, ys, gain):
    T = x.shape[0]
    tm = ROW_TILE
    row = lambda i: (i, 0)
    return pl.pallas_call(
        _final_kernel, grid=(T // tm,),
        in_specs=[pl.BlockSpec((tm, D_MODEL), row)] * 3 + [pl.BlockSpec((tm, LANES), row), pl.BlockSpec((1, D_MODEL), lambda i: (0, 0))],
        out_specs=pl.BlockSpec((tm, D_MODEL), row), out_shape=jax.ShapeDtypeStruct((T, D_MODEL), F32),
        compiler_params=_params(1), name="final_norm")(x, *ys, gain.reshape(1, D_MODEL))


def _rope_tables(positions):
    inv_freq = ROPE_THETA ** (-jnp.arange(0, HEAD_DIM, 2, dtype=F32) / HEAD_DIM)
    ang = positions.astype(F32).reshape(-1, 1) * inv_freq
    cos, sin = jnp.cos(ang), jnp.sin(ang)
    reps = LANES // HEAD_DIM
    return jnp.tile(jnp.concatenate([cos, cos], axis=1), (1, reps)), jnp.tile(jnp.concatenate([-sin, sin], axis=1), (1, reps))


def _heads(t, B, S):
    return t.reshape(B, S, -1, HEAD_DIM).transpose(0, 2, 1, 3)


def _heads_t(t, B, S):
    return t.reshape(B, S, -1, HEAD_DIM).transpose(0, 2, 3, 1)


def _tokens(o):
    B, H, S, d = o.shape
    return o.transpose(0, 2, 1, 3).reshape(B * S, H * d)


def _group_t(t, B, S):
    nq = S // Q_BLOCK
    t = t.reshape(B, nq, Q_BLOCK, -1, GROUP, HEAD_DIM).transpose(0, 3, 1, 5, 4, 2)
    return t.reshape(B, -1, nq, HEAD_DIM, GROUP * Q_BLOCK)


def _group_tokens(oT):
    B, Hkv, nq, d, _ = oT.shape
    o = oT.reshape(B, Hkv, nq, d, GROUP, Q_BLOCK).transpose(0, 2, 5, 1, 4, 3)
    return o.reshape(B * nq * Q_BLOCK, Hkv * GROUP * d)


def _keys_aug(k, block):
    S = k.shape[2]
    lane = jnp.arange(LANES - HEAD_DIM)[None, :]
    extra = ((jnp.arange(S)[:, None] % KEY_TILE) // block == lane) | (lane == AUG_ROWS)
    return jnp.concatenate([k, jnp.broadcast_to(extra.astype(k.dtype), k.shape[:2] + extra.shape)], axis=-1)


def _values_aug(vT):
    B, H, _, S = vT.shape
    ones = jnp.ones((B, H, 1, S), vT.dtype)
    return jnp.concatenate([vT, ones, jnp.zeros((B, H, V_AUG_ROWS - HEAD_DIM - 1, S), vT.dtype)], axis=2)


def _router_weights(router_group, router_expert):
    pad = jnp.zeros((D_MODEL, LANES - N_GROUPS - N_EXPERTS), F32)
    return jnp.concatenate([router_group.astype(F32), router_expert.astype(F32), pad], axis=1)


def _pad_cols(w, n):
    return jnp.concatenate([w, jnp.zeros((w.shape[0], n - w.shape[1]), w.dtype)], axis=1)


def _mixer_ab(proj, sinks, B, S):
    hd = lambda a, b: _heads(proj[:, a:b], B, S)
    hdt = lambda a, b: _heads_t(proj[:, a:b], B, S)
    oa = _swa_attention(_group_t(proj[:, 0:512], B, S), hd(512, 640), hdt(640, 768), sinks)
    ob = _stick_attention(hd(768, 1280), hdt(1280, 1792), hd(1792, 2304))
    return jnp.concatenate([_group_tokens(oa), _tokens(ob)], axis=1)


def _mixer_cd(proj, kmean, B, S, k_pe, k_w1, k_w2, v_pe, v_w1, v_w2):
    hd = lambda a, b: _heads(proj[:, a:b], B, S)
    hdt = lambda a, b: _heads_t(proj[:, a:b], B, S)
    nblk = S // C_BLOCK
    km = kmean.reshape(B, nblk, C_KV_HEADS, HEAD_DIM).transpose(0, 2, 1, 3).astype(BF16)
    oc = _moba_attention(_group_t(proj[:, 0:512], B, S), _keys_aug(hd(512, 640), C_BLOCK), _values_aug(hdt(640, 768)), km)
    k_cmp = _compress(hd(1280, 1408), k_pe, k_w1, k_w2)
    v_cmp = _compress(hd(1408, 1536), v_pe, v_w1, v_w2)
    nc = (S - D_CMP_LEN) // D_CMP_STRIDE + 1
    ncp = S // D_CMP_STRIDE
    nsel = S // D_SEL_LEN
    c_start = jnp.arange(ncp) * D_CMP_STRIDE
    b_start = jnp.arange(nsel) * D_SEL_LEN
    overlap = ((c_start[:, None] <= b_start[None, :] + D_SEL_LEN - 1) & (c_start[:, None] + D_CMP_LEN - 1 >= b_start[None, :])
               & (jnp.arange(ncp)[:, None] < nc)).astype(BF16)
    qdT = _group_t(proj[:, 768:1280], B, S)
    o_cmp, biasT = _cmp_select(qdT, k_cmp, v_cmp.transpose(0, 1, 3, 2), overlap.T, nc)
    o_sel = _sel_attention(qdT, _keys_aug(hd(1536, 1664), D_SEL_LEN), _values_aug(hdt(1664, 1792)), biasT)
    o_win = _win_attention(qdT, hd(1792, 1920), hdt(1920, 2048))
    return _group_tokens(oc), _group_tokens(o_cmp), _group_tokens(o_sel), _group_tokens(o_win)


def kernel(x, positions, ln_mix_0, w_in_0, sinks_0, w_out_0, ln_ffn_0, router_group_0, router_expert_0, expert_gate_0, expert_up_0, expert_down_0, ln_mix_1, w_in_1, cmp_k_pe_1, cmp_k_w1_1, cmp_k_w2_1, cmp_v_pe_1, cmp_v_w1_1, cmp_v_w2_1, w_out_1, ln_ffn_1, router_group_1, router_expert_1, expert_gate_1, expert_up_1, expert_down_1, ln_final):
    B, S, _ = x.shape
    T = B * S
    assert S % KEY_TILE == 0 and T % ROW_TILE == 0
    cos_t, sin_t = _rope_tables(positions)
    xf = x.reshape(T, D_MODEL)

    proj = _in_proj(xf, None, ln_mix_0, w_in_0.astype(BF16), cos_t, sin_t, rope_chunks=range(0, 5))[0]
    o_ab = _mixer_ab(proj, sinks_0, B, S)
    x1, h1, route1 = _out_proj([o_ab], None, xf, w_out_0.astype(BF16), ln_ffn_0, _router_weights(router_group_0, router_expert_0))
    ys = _moe(route1, h1, expert_gate_0, expert_up_0, expert_down_0)

    n_cols = 17 * LANES
    x2, proj, kmean, gd = _in_proj(x1, ys, ln_mix_1, _pad_cols(w_in_1.astype(BF16), n_cols), cos_t, sin_t,
                                   rope_chunks=(0, 1, 2, 3, 4, 6, 7, 8, 9, 10, 12, 14), kmean_chunk=4, gate_chunk=16)
    parts = _mixer_cd(proj, kmean, B, S, cmp_k_pe_1, cmp_k_w1_1, cmp_k_w2_1, cmp_v_pe_1, cmp_v_w1_1, cmp_v_w2_1)
    x3, h3, route3 = _out_proj(parts, gd, x2, w_out_1.astype(BF16), ln_ffn_1, _router_weights(router_group_1, router_expert_1))
    ys = _moe(route3, h3, expert_gate_1, expert_up_1, expert_down_1)
    return _final_norm(x3, ys, ln_final).reshape(B, S, D_MODEL)
```

```python
import functools

import jax
import jax.numpy as jnp
from jax import lax
from jax.experimental import pallas as pl
from jax.experimental.pallas import tpu as pltpu

D_MODEL = 1024
HEAD_DIM = 64
HALF = HEAD_DIM // 2
ROPE_THETA = 10000.0
NORM_EPS = 1e-6
Q_BLOCK = 128
SCALE = HEAD_DIM ** -0.5

A_HEADS, A_KV_HEADS, A_WINDOW = 8, 2, 128
B_HEADS = 8
C_HEADS, C_KV_HEADS, C_BLOCK, C_TOPK = 8, 2, 256, 3
D_HEADS, D_KV_HEADS = 8, 2
D_CMP_LEN, D_CMP_STRIDE, D_CMP_HIDDEN = 32, 16, 256
D_SEL_LEN, D_SEL_TOPK, D_WINDOW = 64, 16, 512
N_GROUPS, EXPERTS_PER_GROUP, MOE_TOPK, EXPERT_HIDDEN = 4, 16, 2, 512
N_EXPERTS = N_GROUPS * EXPERTS_PER_GROUP
GROUP = 4

LANES = 128
ROW_TILE = 512
MOE_ROWS = 256
KEY_TILE = 512
FLASH_MIN_SUM = 1e-25
V_AUG_ROWS = 80
AUG_ROWS = 8
MASK_BIG = 32768.0
SB_EXIT = -104.0
STICK_HEADS = 4
VMEM_LIMIT = 56 * 1024 * 1024

F32 = jnp.float32
BF16 = jnp.bfloat16
NEG_INF = float("-inf")


def _iota(shape, dim):
    return lax.broadcasted_iota(jnp.int32, shape, dim)


def _dot(a, b):
    return jnp.dot(a, b, preferred_element_type=F32)


def _params(n_grid):
    return pltpu.CompilerParams(dimension_semantics=("arbitrary",) * n_grid, vmem_limit_bytes=VMEM_LIMIT)


def _moe_combine(x, y0_ref, y1_ref, route_ref):
    return x + (y0_ref[...] * route_ref[:, MOE_TOPK:MOE_TOPK + 1] + y1_ref[...] * route_ref[:, MOE_TOPK + 1:MOE_TOPK + 2])


def _in_proj_kernel(rope_chunks, kmean_chunk, gate_chunk, combine, *refs):
    refs = list(refs)
    x_ref = refs.pop(0)
    if combine:
        y0_ref, y1_ref, route_ref = refs.pop(0), refs.pop(0), refs.pop(0)
    g_ref, w_ref, cos_ref, sin_ref = refs[:4]
    outs = refs[4:]
    x = x_ref[...]
    if combine:
        x = _moe_combine(x, y0_ref, y1_ref, route_ref)
        xo_ref = outs.pop(0)
        xo_ref[...] = x
    o_ref = outs.pop(0)
    ms = jnp.mean(x * x, axis=-1, keepdims=True)
    h = (x * lax.rsqrt(ms + NORM_EPS) * g_ref[...]).astype(BF16)
    tm = x.shape[0]
    n_cols = w_ref.shape[1]
    cos = cos_ref[...]
    sin = sin_ref[...]
    first_half = (_iota((tm, LANES), 1) & (HEAD_DIM - 1)) < HALF
    for c in range(n_cols // LANES):
        ch = _dot(h, w_ref[:, c * LANES:(c + 1) * LANES])
        if c in rope_chunks:
            partner = jnp.where(first_half, pltpu.roll(ch, LANES - HALF, 1), pltpu.roll(ch, HALF, 1))
            ch = ch * cos + partner * sin
        if c == kmean_chunk:
            km_ref = outs[0]
            km_ref[...] = jnp.mean(ch.reshape(tm // C_BLOCK, C_BLOCK, LANES), axis=1).reshape(tm // C_BLOCK, 1, LANES)
        if c == gate_chunk:
            outs[-1][...] = ch
        o_ref[:, c * LANES:(c + 1) * LANES] = ch.astype(BF16)


def _in_proj(x, ys, gain, w, cos_t, sin_t, rope_chunks, kmean_chunk=None, gate_chunk=None):
    T = x.shape[0]
    n_cols = w.shape[1]
    tm = ROW_TILE
    combine = ys is not None
    row = lambda i: (i, 0)
    fixed = lambda i: (0, 0)
    in_specs = [pl.BlockSpec((tm, D_MODEL), row)]
    args = [x]
    if combine:
        in_specs += [pl.BlockSpec((tm, D_MODEL), row)] * 2 + [pl.BlockSpec((tm, LANES), row)]
        args += list(ys)
    in_specs += [pl.BlockSpec((1, D_MODEL), fixed), pl.BlockSpec((D_MODEL, n_cols), fixed),
                 pl.BlockSpec((tm, LANES), row), pl.BlockSpec((tm, LANES), row)]
    args += [gain.reshape(1, D_MODEL), w, cos_t, sin_t]
    out_shape, out_specs = [], []
    if combine:
        out_shape.append(jax.ShapeDtypeStruct((T, D_MODEL), F32))
        out_specs.append(pl.BlockSpec((tm, D_MODEL), row))
    out_shape.append(jax.ShapeDtypeStruct((T, n_cols), BF16))
    out_specs.append(pl.BlockSpec((tm, n_cols), row))
    if kmean_chunk is not None:
        out_shape.append(jax.ShapeDtypeStruct((T // C_BLOCK, 1, LANES), F32))
        out_specs.append(pl.BlockSpec((tm // C_BLOCK, 1, LANES), lambda i: (i, 0, 0)))
    if gate_chunk is not None:
        out_shape.append(jax.ShapeDtypeStruct((T, LANES), F32))
        out_specs.append(pl.BlockSpec((tm, LANES), row))
    kern = functools.partial(_in_proj_kernel, tuple(rope_chunks), kmean_chunk, gate_chunk, combine)
    return pl.pallas_call(kern, grid=(T // tm,), in_specs=in_specs, out_specs=out_specs, out_shape=out_shape,
                          compiler_params=_params(1), name="in_proj")(*args)


def _swa_kernel(qT_ref, kp_ref, ko_ref, vp_ref, vo_ref, sink_ref, o_ref):
    n = pl.program_id(2)
    qs = (qT_ref[0, 0, 0].astype(F32) * SCALE).astype(BF16)
    R = qs.shape[1]
    k = jnp.concatenate([kp_ref[0, 0], ko_ref[0, 0]], axis=0)
    vT = jnp.concatenate([vp_ref[0, 0], vo_ref[0, 0]], axis=1)
    s = _dot(k, qs)
    qpos = _iota((2 * Q_BLOCK, R), 1) & (Q_BLOCK - 1)
    ki = _iota((2 * Q_BLOCK, R), 0)
    d = qpos + Q_BLOCK - ki
    ok = (d >= 0) & (d < A_WINDOW) & ((ki >= Q_BLOCK) | (n > 0))
    s = jnp.where(ok, s, NEG_INF)
    sink = sink_ref[0]
    m = jnp.maximum(jnp.max(s, axis=0, keepdims=True), sink)
    p = jnp.exp(s - m)
    den = jnp.sum(p, axis=0, keepdims=True) + jnp.exp(sink - m)
    o_ref[0, 0, 0] = (_dot(vT, p.astype(BF16)) * (1.0 / den)).astype(o_ref.dtype)


def _swa_attention(qT, k, vT, sinks):
    B, Hkv, nb, _, R = qT.shape
    sink_row = jnp.repeat(sinks.astype(F32).reshape(Hkv, GROUP), Q_BLOCK, axis=1).reshape(Hkv, 1, R)
    prev = lambda n: jnp.maximum(n - 1, 0)
    in_specs = [
        pl.BlockSpec((1, 1, 1, HEAD_DIM, R), lambda b, h, n: (b, h, n, 0, 0)),
        pl.BlockSpec((1, 1, Q_BLOCK, HEAD_DIM), lambda b, h, n: (b, h, prev(n), 0)),
        pl.BlockSpec((1, 1, Q_BLOCK, HEAD_DIM), lambda b, h, n: (b, h, n, 0)),
        pl.BlockSpec((1, 1, HEAD_DIM, Q_BLOCK), lambda b, h, n: (b, h, 0, prev(n))),
        pl.BlockSpec((1, 1, HEAD_DIM, Q_BLOCK), lambda b, h, n: (b, h, 0, n)),
        pl.BlockSpec((1, 1, R), lambda b, h, n: (h, 0, 0)),
    ]
    return pl.pallas_call(
        _swa_kernel, grid=(B, Hkv, nb), in_specs=in_specs,
        out_specs=pl.BlockSpec((1, 1, 1, HEAD_DIM, R), lambda b, h, n: (b, h, n, 0, 0)),
        out_shape=jax.ShapeDtypeStruct((B, Hkv, nb, HEAD_DIM, R), BF16),
        compiler_params=_params(3), name="swa_attention")(qT, k, k, vT, vT, sink_row)


def _stick_kernel(q_ref, kT_ref, v_ref, o_ref):
    n = pl.program_id(2)
    tq = tk = Q_BLOCK
    heads = q_ref.shape[1]
    row = _iota((tq, tk), 0)
    col = _iota((tk, tk), 1)
    upper = jnp.where(_iota((tk, tk), 0) > col, 1.0, 0.0).astype(BF16)
    tpos = n * tq + row

    def body(carry):
        kb, _, cs, accs = carry
        start = pl.multiple_of(kb * tk, tk)
        past = (start + col) < tpos
        hs = range(heads)
        zs = [_dot(q_ref[0, h], kT_ref[0, h, :, pl.ds(start, tk)]) * SCALE for h in hs]
        sps = [jnp.maximum(z, 0.0) + jnp.log1p(jnp.exp(-jnp.abs(z))) for z in zs]
        stays = [jnp.where(past, -sp, 0.0) for sp in sps]
        his = [st.astype(BF16) for st in stays]
        los = [(st - hi.astype(F32)).astype(BF16) for st, hi in zip(stays, his)]
        betweens = [_dot(hi, upper) + _dot(lo, upper) for hi, lo in zip(his, los)]
        ws = [jnp.where(past, jnp.exp(zs[h] - sps[h] + betweens[h] + cs[h]), 0.0).astype(BF16) for h in hs]
        new_accs = [accs[h] + _dot(ws[h], v_ref[0, h, pl.ds(start, tk), :]) for h in hs]
        new_cs = [cs[h] + jnp.sum(stays[h], axis=-1, keepdims=True) for h in hs]
        worst = functools.reduce(jnp.maximum, new_cs)
        return kb - 1, jnp.max(worst) > SB_EXIT, tuple(new_cs), tuple(new_accs)

    def cond(carry):
        kb, alive, _, _ = carry
        return (kb >= 0) & alive

    init = (n, jnp.array(True), (jnp.zeros((tq, 1), F32),) * heads, (jnp.zeros((tq, HEAD_DIM), F32),) * heads)
    _, _, _, accs = lax.while_loop(cond, body, init)
    for h in range(heads):
        o_ref[0, h] = accs[h].astype(o_ref.dtype)


def _stick_attention(q, kT, v):
    B, H, S, _ = q.shape
    nq = S // Q_BLOCK
    hb = STICK_HEADS
    resident = dict(pipeline_mode=pl.Buffered(1))
    in_specs = [
        pl.BlockSpec((1, hb, Q_BLOCK, HEAD_DIM), lambda b, h, n: (b, h, n, 0)),
        pl.BlockSpec((1, hb, HEAD_DIM, S), lambda b, h, n: (b, h, 0, 0), **resident),
        pl.BlockSpec((1, hb, S, HEAD_DIM), lambda b, h, n: (b, h, 0, 0), **resident),
    ]
    return pl.pallas_call(
        _stick_kernel, grid=(B, H // hb, nq), in_specs=in_specs,
        out_specs=pl.BlockSpec((1, hb, Q_BLOCK, HEAD_DIM), lambda b, h, n: (b, h, n, 0)),
        out_shape=jax.ShapeDtypeStruct((B, H, S, HEAD_DIM), BF16),
        compiler_params=_params(3), name="stick_attention")(q, kT, v)


def _max_key_sqnorm(ka_ref, out_ref):
    S = ka_ref.shape[2]
    tk = KEY_TILE
    is_key_lane = _iota((tk, LANES), 1) < HEAD_DIM
    ones = jnp.ones((LANES, LANES), BF16)

    def body(i, mx):
        k = jnp.where(is_key_lane, ka_ref[0, 0, pl.ds(pl.multiple_of(i * tk, tk), tk), :].astype(F32), 0.0)
        sq = _dot((k * k).astype(BF16), ones)
        return jnp.maximum(mx, jnp.max(sq, axis=0, keepdims=True))

    mx = lax.fori_loop(0, S // tk, body, jnp.zeros((1, LANES), F32))
    out_ref[...] = jnp.broadcast_to(mx, out_ref.shape)


def _flash_scratch(R):
    return [pltpu.VMEM((AUG_ROWS, LANES), F32), pltpu.VMEM((2, KEY_TILE, R), BF16), pltpu.VMEM((V_AUG_ROWS, R), F32)]


def _masked_flash_t(n, qT, bias_rows, ka_ref, vTa_ref, ksq_ref, p_scr, acc_scr):
    R = qT.shape[1]
    tk = KEY_TILE
    diag = (n * Q_BLOCK) // tk
    zpad = jnp.zeros((LANES - HEAD_DIM - 2 * AUG_ROWS, R), F32)
    causal = diag * tk + _iota((tk, R), 0) <= n * Q_BLOCK + (_iota((tk, R), 1) & (Q_BLOCK - 1))

    def scores(kt, ref_rows):
        st = pl.multiple_of(kt * tk, tk)
        low = jnp.concatenate([bias_rows(kt), ref_rows, zpad], axis=0).astype(BF16)
        return _dot(ka_ref[0, 0, pl.ds(st, tk), :], jnp.concatenate([qT, low], axis=0))

    def values(kt):
        return vTa_ref[0, 0, :, pl.ds(pl.multiple_of(kt * tk, tk), tk)]

    qsq = jnp.sum(jnp.square(qT.astype(F32)), axis=0, keepdims=True)
    ksq = jnp.concatenate([ksq_ref[0:1, :]] * (R // LANES), axis=1)
    ref = jnp.sqrt(qsq * ksq) * 1.05
    ref_rows = jnp.where(_iota((AUG_ROWS, R), 0) == 0, -ref, 0.0)

    def fast_body(kt, carry):
        slot = kt & 1
        s = scores(kt, ref_rows)
        acc_scr[...] += _dot(values(jnp.maximum(kt - 1, 0)), p_scr[1 - slot])
        p_scr[slot] = jnp.exp(s).astype(BF16)
        return carry

    p_scr[1] = jnp.zeros((tk, R), BF16)
    acc_scr[...] = jnp.zeros(acc_scr.shape, F32)
    lax.fori_loop(0, diag, fast_body, 0)
    acc = acc_scr[...] + _dot(values(jnp.maximum(diag - 1, 0)), p_scr[1 - (diag & 1)])
    s = jnp.where(causal, scores(diag, ref_rows), -1e30)
    acc = acc + _dot(values(diag), jnp.exp(s).astype(BF16))

    def running_max_path(_):
        no_ref = jnp.zeros((AUG_ROWS, R), F32)

        def update(kt, s, m, acc):
            m_new = jnp.maximum(m, jnp.max(s, axis=0, keepdims=True))
            p = jnp.exp(s - m_new).astype(BF16)
            return m_new, jnp.exp(m - m_new) * acc + _dot(values(kt), p)

        init = (jnp.full((1, R), -1e30, F32), jnp.zeros((vTa_ref.shape[2], R), F32))
        m, acc = lax.fori_loop(0, diag, lambda kt, c: update(kt, scores(kt, no_ref), *c), init)
        return update(diag, jnp.where(causal, scores(diag, no_ref), -1e30), m, acc)[1]

    healthy = jnp.min(acc[HEAD_DIM:HEAD_DIM + 1, :]) >= FLASH_MIN_SUM
    acc = lax.cond(healthy, lambda _: acc, running_max_path, 0)
    return acc[0:HEAD_DIM] * (1.0 / acc[HEAD_DIM:HEAD_DIM + 1])


def _moba_kernel(qT_ref, ka_ref, vTa_ref, km_ref, o_ref, bias_scr, ksq_scr, p_scr, acc_scr):
    n = pl.program_id(2)
    pl.when(n == 0)(lambda: _max_key_sqnorm(ka_ref, ksq_scr))
    nblk = km_ref.shape[2]
    qT = qT_ref[0, 0, 0]
    R = qT.shape[1]
    own = (n * Q_BLOCK) // C_BLOCK
    gate = _dot(km_ref[0, 0], qT)
    blk = _iota((nblk, R), 0)
    gate = jnp.where(blk < own, gate, NEG_INF)
    bias = jnp.where(blk == own, 0.0, -MASK_BIG)
    for _ in range(C_TOPK):
        mx = jnp.max(gate, axis=0, keepdims=True)
        idx = jnp.min(jnp.where(gate == mx, blk, nblk), axis=0, keepdims=True)
        hit = blk == idx
        bias = jnp.where(hit & (mx > NEG_INF), 0.0, bias)
        gate = jnp.where(hit, NEG_INF, gate)
    per = KEY_TILE // C_BLOCK
    rows = bias_scr.shape[0]
    r = _iota((rows, nblk), 0)
    spread = jnp.where(((r & (AUG_ROWS - 1)) < per) & (_iota((rows, nblk), 1) == (r >> 3) * per + (r & (AUG_ROWS - 1))), 1.0, 0.0)
    bias_scr[...] = _dot(spread.astype(BF16), bias.astype(BF16))
    qs = (qT.astype(F32) * SCALE).astype(BF16)
    oT = _masked_flash_t(n, qs, lambda kt: bias_scr[pl.ds(pl.multiple_of(kt * AUG_ROWS, AUG_ROWS), AUG_ROWS), :],
                         ka_ref, vTa_ref, ksq_scr, p_scr, acc_scr)
    o_ref[0, 0, 0] = oT.astype(o_ref.dtype)


def _moba_attention(qT, ka, vTa, km):
    B, Hkv, nq, _, R = qT.shape
    S = ka.shape[2]
    nblk = km.shape[2]
    nkt = S // KEY_TILE
    in_specs = [
        pl.BlockSpec((1, 1, 1, HEAD_DIM, R), lambda b, h, n: (b, h, n, 0, 0)),
        pl.BlockSpec((1, 1, S, LANES), lambda b, h, n: (b, h, 0, 0)),
        pl.BlockSpec((1, 1, V_AUG_ROWS, S), lambda b, h, n: (b, h, 0, 0)),
        pl.BlockSpec((1, 1, nblk, HEAD_DIM), lambda b, h, n: (b, h, 0, 0)),
    ]
    return pl.pallas_call(
        _moba_kernel, grid=(B, Hkv, nq), in_specs=in_specs,
        out_specs=pl.BlockSpec((1, 1, 1, HEAD_DIM, R), lambda b, h, n: (b, h, n, 0, 0)),
        out_shape=jax.ShapeDtypeStruct((B, Hkv, nq, HEAD_DIM, R), BF16),
        scratch_shapes=[pltpu.VMEM((nkt * AUG_ROWS, R), F32)] + _flash_scratch(R),
        compiler_params=_params(3), name="moba_attention")(qT, ka, vTa, km)


def _compress_kernel(u_ref, us_ref, pe_ref, w1_ref, w2_ref, o_ref):
    a = (u_ref[0, 0].astype(F32) + pe_ref[0:1, :]).astype(BF16)
    b = (us_ref[0, 0].astype(F32) + pe_ref[1:2, :]).astype(BF16)
    pre = _dot(a, w1_ref[0]) + _dot(b, w1_ref[1])
    hid = jax.nn.gelu(pre)
    o_ref[0, 0] = _dot(hid.astype(BF16), w2_ref[...]).astype(o_ref.dtype)


def _compress(t, pe, w1, w2):
    B, H, S, _ = t.shape
    nrow = S // D_CMP_STRIDE
    width = D_CMP_STRIDE * HEAD_DIM
    u = t.reshape(B, H, nrow, width)
    us = jnp.concatenate([u[:, :, 1:], jnp.zeros((B, H, 1, width), u.dtype)], axis=2)
    blk = lambda b, h: (b, h, 0, 0)
    in_specs = [
        pl.BlockSpec((1, 1, nrow, width), blk), pl.BlockSpec((1, 1, nrow, width), blk),
        pl.BlockSpec((2, width), lambda b, h: (0, 0)),
        pl.BlockSpec((2, width, D_CMP_HIDDEN), lambda b, h: (0, 0, 0)),
        pl.BlockSpec((D_CMP_HIDDEN, HEAD_DIM), lambda b, h: (0, 0)),
    ]
    return pl.pallas_call(
        _compress_kernel, grid=(B, H), in_specs=in_specs,
        out_specs=pl.BlockSpec((1, 1, nrow, HEAD_DIM), blk),
        out_shape=jax.ShapeDtypeStruct((B, H, nrow, HEAD_DIM), BF16),
        compiler_params=_params(2), name="nsa_compress")(
            u, us, pe.astype(F32).reshape(2, width), w1.astype(BF16).reshape(2, width, D_CMP_HIDDEN), w2.astype(BF16))


def _cmp_select_kernel(nc, qT_ref, kc_ref, vcT_ref, ovT_ref, oc_ref, bias_ref):
    n = pl.program_id(2)
    ncp = kc_ref.shape[2]
    nsel = ovT_ref.shape[0]
    qs = (qT_ref[0, 0, 0].astype(F32) * SCALE).astype(BF16)
    R = qs.shape[1]
    s = _dot(kc_ref[0, 0], qs)
    tpos = n * Q_BLOCK + (_iota((ncp, R), 1) & (Q_BLOCK - 1))
    c = _iota((ncp, R), 0)
    ok = (c * D_CMP_STRIDE + (D_CMP_LEN - 1) <= tpos) & (c < nc)
    s = jnp.where(ok, s, NEG_INF)
    m = jnp.max(s, axis=0, keepdims=True)
    m = jnp.where(m > NEG_INF, m, 0.0)
    e = jnp.exp(s - m)
    den = jnp.sum(e, axis=0, keepdims=True)
    p = (e * (1.0 / jnp.where(den > 0, den, 1.0))).astype(BF16)
    oc_ref[0, 0, 0] = _dot(vcT_ref[0, 0], p)
    imp_heads = _dot(ovT_ref[...], p)
    imp = imp_heads[:, 0:Q_BLOCK]
    for g in range(1, GROUP):
        imp = imp + imp_heads[:, g * Q_BLOCK:(g + 1) * Q_BLOCK]
    t = n * Q_BLOCK + _iota((nsel, Q_BLOCK), 1)
    j = _iota((nsel, Q_BLOCK), 0)
    cur = t >> 6
    allowed = j * D_SEL_LEN <= t
    forced = (j == 0) | (j == cur) | (j == cur - 1)
    imp = jnp.where(allowed, imp, NEG_INF)
    imp = jnp.where(allowed & forced, float("inf"), imp)
    bias = jnp.full((nsel, Q_BLOCK), -MASK_BIG, F32)
    for _ in range(D_SEL_TOPK):
        mx = jnp.max(imp, axis=0, keepdims=True)
        idx = jnp.min(jnp.where(imp == mx, j, nsel), axis=0, keepdims=True)
        hit = j == idx
        bias = jnp.where(hit & (mx > NEG_INF), 0.0, bias)
        imp = jnp.where(hit, NEG_INF, imp)
    bias_ref[0, 0, 0] = bias


def _cmp_select(qT, kc, vcT, overlapT, nc):
    B, Hkv, nq, _, R = qT.shape
    ncp = kc.shape[2]
    nsel = overlapT.shape[0]
    in_specs = [
        pl.BlockSpec((1, 1, 1, HEAD_DIM, R), lambda b, h, n: (b, h, n, 0, 0)),
        pl.BlockSpec((1, 1, ncp, HEAD_DIM), lambda b, h, n: (b, h, 0, 0)),
        pl.BlockSpec((1, 1, HEAD_DIM, ncp), lambda b, h, n: (b, h, 0, 0)),
        pl.BlockSpec((nsel, ncp), lambda b, h, n: (0, 0)),
    ]
    out_specs = [
        pl.BlockSpec((1, 1, 1, HEAD_DIM, R), lambda b, h, n: (b, h, n, 0, 0)),
        pl.BlockSpec((1, 1, 1, nsel, Q_BLOCK), lambda b, h, n: (b, h, n, 0, 0)),
    ]
    out_shape = [jax.ShapeDtypeStruct((B, Hkv, nq, HEAD_DIM, R), F32), jax.ShapeDtypeStruct((B, Hkv, nq, nsel, Q_BLOCK), F32)]
    return pl.pallas_call(
        functools.partial(_cmp_select_kernel, nc), grid=(B, Hkv, nq), in_specs=in_specs, out_specs=out_specs,
        out_shape=out_shape, compiler_params=_params(3), name="nsa_cmp_select")(qT, kc, vcT, overlapT)


def _sel_kernel(qT_ref, ka_ref, vTa_ref, bias_ref, o_ref, ksq_scr, p_scr, acc_scr):
    n = pl.program_id(2)
    pl.when(n == 0)(lambda: _max_key_sqnorm(ka_ref, ksq_scr))
    qs = (qT_ref[0, 0, 0].astype(F32) * SCALE).astype(BF16)

    def bias_rows(kt):
        b = bias_ref[0, 0, 0, pl.ds(pl.multiple_of(kt * AUG_ROWS, AUG_ROWS), AUG_ROWS), :]
        return jnp.concatenate([b] * GROUP, axis=1)

    o_ref[0, 0, 0] = _masked_flash_t(n, qs, bias_rows, ka_ref, vTa_ref, ksq_scr, p_scr, acc_scr)


def _sel_attention(qT, ka, vTa, biasT):
    B, Hkv, nq, _, R = qT.shape
    S = ka.shape[2]
    nsel = biasT.shape[3]
    assert KEY_TILE // D_SEL_LEN == AUG_ROWS
    in_specs = [
        pl.BlockSpec((1, 1, 1, HEAD_DIM, R), lambda b, h, n: (b, h, n, 0, 0)),
        pl.BlockSpec((1, 1, S, LANES), lambda b, h, n: (b, h, 0, 0)),
        pl.BlockSpec((1, 1, V_AUG_ROWS, S), lambda b, h, n: (b, h, 0, 0)),
        pl.BlockSpec((1, 1, 1, nsel, Q_BLOCK), lambda b, h, n: (b, h, n, 0, 0)),
    ]
    return pl.pallas_call(
        _sel_kernel, grid=(B, Hkv, nq), in_specs=in_specs,
        out_specs=pl.BlockSpec((1, 1, 1, HEAD_DIM, R), lambda b, h, n: (b, h, n, 0, 0)),
        out_shape=jax.ShapeDtypeStruct((B, Hkv, nq, HEAD_DIM, R), F32),
        scratch_shapes=_flash_scratch(R),
        compiler_params=_params(3), name="nsa_selected")(qT, ka, vTa, biasT)


def _win_kernel(span, qT_ref, k_ref, vT_ref, o_ref):
    n = pl.program_id(2)
    qs = (qT_ref[0, 0, 0].astype(F32) * SCALE).astype(BF16)
    R = qs.shape[1]
    start = pl.multiple_of(jnp.maximum(n * Q_BLOCK + Q_BLOCK - span, 0), Q_BLOCK)
    s = _dot(k_ref[0, 0, pl.ds(start, span), :], qs)
    tpos = n * Q_BLOCK + (_iota((span, R), 1) & (Q_BLOCK - 1))
    d = tpos - (start + _iota((span, R), 0))
    s = jnp.where((d >= 0) & (d < D_WINDOW), s, NEG_INF)
    m = jnp.max(s, axis=0, keepdims=True)
    p = jnp.exp(s - m)
    l = jnp.sum(p, axis=0, keepdims=True)
    o_ref[0, 0, 0] = _dot(vT_ref[0, 0, :, pl.ds(start, span)], p.astype(BF16)) * (1.0 / l)


def _win_attention(qT, k, vT):
    B, Hkv, nq, _, R = qT.shape
    S = k.shape[2]
    span = min(D_WINDOW + Q_BLOCK, S)
    in_specs = [
        pl.BlockSpec((1, 1, 1, HEAD_DIM, R), lambda b, h, n: (b, h, n, 0, 0)),
        pl.BlockSpec((1, 1, S, HEAD_DIM), lambda b, h, n: (b, h, 0, 0)),
        pl.BlockSpec((1, 1, HEAD_DIM, S), lambda b, h, n: (b, h, 0, 0)),
    ]
    return pl.pallas_call(
        functools.partial(_win_kernel, span), grid=(B, Hkv, nq), in_specs=in_specs,
        out_specs=pl.BlockSpec((1, 1, 1, HEAD_DIM, R), lambda b, h, n: (b, h, n, 0, 0)),
        out_shape=jax.ShapeDtypeStruct((B, Hkv, nq, HEAD_DIM, R), F32),
        compiler_params=_params(3), name="nsa_window")(qT, k, vT)


def _route(logits):
    tm = logits.shape[0]
    lane = _iota((tm, LANES), 1)
    gl = jnp.where(lane < N_GROUPS, logits, NEG_INF)
    gmax = jnp.max(gl, axis=-1, keepdims=True)
    gidx = jnp.min(jnp.where(gl == gmax, lane, LANES), axis=-1, keepdims=True)
    g_prob = 1.0 / jnp.sum(jnp.exp(gl - gmax), axis=-1, keepdims=True)
    elane = lane - N_GROUPS
    in_group = (elane >= 0) & (elane < N_EXPERTS) & ((elane >> 4) == gidx)
    el = jnp.where(in_group, logits, NEG_INF)
    ee = jnp.exp(el - jnp.max(el, axis=-1, keepdims=True))
    ep = jnp.where(in_group, ee / jnp.sum(ee, axis=-1, keepdims=True), -1.0)
    p1 = jnp.max(ep, axis=-1, keepdims=True)
    i1 = jnp.min(jnp.where(ep == p1, lane, LANES), axis=-1, keepdims=True)
    ep2 = jnp.where(lane == i1, -1.0, ep)
    p2 = jnp.max(ep2, axis=-1, keepdims=True)
    i2 = jnp.min(jnp.where(ep2 == p2, lane, LANES), axis=-1, keepdims=True)
    den = p1 + p2
    vals = [(i1 - N_GROUPS).astype(F32), (i2 - N_GROUPS).astype(F32), g_prob * p1 / den, g_prob * p2 / den]
    out = jnp.zeros((tm, LANES), F32)
    for k, val in enumerate(vals):
        out = jnp.where(lane == k, val, out)
    return out


def _out_tail(x_new, gain_ref, wr_ref, xo_ref, h_ref, route_ref):
    xo_ref[...] = x_new
    ms = jnp.mean(x_new * x_new, axis=-1, keepdims=True)
    h = x_new * lax.rsqrt(ms + NORM_EPS) * gain_ref[...]
    h_ref[...] = h
    logits = jnp.dot(h, wr_ref[...], preferred_element_type=F32, precision=lax.Precision.HIGHEST)
    route_ref[...] = _route(logits)


def _out_proj_kernel(o_ref, x_ref, w_ref, gain_ref, wr_ref, xo_ref, h_ref, route_ref):
    x_new = x_ref[...] + _dot(o_ref[...], w_ref[...])
    _out_tail(x_new, gain_ref, wr_ref, xo_ref, h_ref, route_ref)


def _out_proj_nsa_kernel(oc_ref, b0_ref, b1_ref, b2_ref, gd_ref, x_ref, w_ref, gain_ref, wr_ref, xo_ref, h_ref, route_ref):
    half = D_HEADS * HEAD_DIM
    g = jax.nn.sigmoid(gd_ref[...])
    g_hi = g.astype(BF16)
    g_lo = (g - g_hi.astype(F32)).astype(BF16)
    src = _iota((LANES, half), 0)
    head3 = (_iota((LANES, half), 1) >> 6) * 3
    od = None
    for br, b_ref in enumerate((b0_ref, b1_ref, b2_ref)):
        spread = jnp.where(src == head3 + br, 1.0, 0.0).astype(BF16)
        term = (_dot(g_hi, spread) + _dot(g_lo, spread)) * b_ref[...]
        od = term if od is None else od + term
    x_new = x_ref[...] + _dot(oc_ref[...], w_ref[0:half, :]) + _dot(od.astype(BF16), w_ref[half:, :])
    _out_tail(x_new, gain_ref, wr_ref, xo_ref, h_ref, route_ref)


def _out_proj(o_parts, gd, x, w_out, gain, w_router):
    T = x.shape[0]
    tm = ROW_TILE
    row = lambda i: (i, 0)
    fixed = lambda i: (0, 0)
    if gd is None:
        kern = _out_proj_kernel
        args = [o_parts[0]]
        in_specs = [pl.BlockSpec((tm, D_MODEL), row)]
    else:
        kern = _out_proj_nsa_kernel
        args = list(o_parts) + [gd]
        in_specs = [pl.BlockSpec((tm, D_MODEL // 2), row)] * 4 + [pl.BlockSpec((tm, LANES), row)]
    args += [x, w_out, gain.reshape(1, D_MODEL), w_router]
    in_specs += [pl.BlockSpec((tm, D_MODEL), row), pl.BlockSpec((D_MODEL, D_MODEL), fixed),
                 pl.BlockSpec((1, D_MODEL), fixed), pl.BlockSpec((D_MODEL, LANES), fixed)]
    out_shape = [jax.ShapeDtypeStruct((T, D_MODEL), F32), jax.ShapeDtypeStruct((T, D_MODEL), F32),
                 jax.ShapeDtypeStruct((T, LANES), F32)]
    out_specs = [pl.BlockSpec((tm, D_MODEL), row), pl.BlockSpec((tm, D_MODEL), row), pl.BlockSpec((tm, LANES), row)]
    return pl.pallas_call(kern, grid=(T // tm,), in_specs=in_specs, out_specs=out_specs, out_shape=out_shape,
                          compiler_params=_params(1), name="out_proj_router")(*args)


def _expert_kernel(be_ref, rows_ref, wg_ref, wu_ref, wd_ref, y_ref, wg_s, wu_s, wd_s):
    i = pl.program_id(0)
    n_blk = pl.num_programs(0)

    @pl.when((i == 0) | (be_ref[i] != be_ref[jnp.maximum(i - 1, 0)]))
    def _():
        wg_s[...] = wg_ref[0].astype(BF16)
        wu_s[...] = wu_ref[0].astype(BF16)
        wd_s[...] = wd_ref[0].astype(BF16)

    @pl.when(i < be_ref[n_blk])
    def _():
        xb = rows_ref[...].astype(BF16)
        hid = jax.nn.silu(_dot(xb, wg_s[...])) * _dot(xb, wu_s[...])
        y_ref[...] = _dot(hid.astype(BF16), wd_s[...])

    @pl.when(i >= be_ref[n_blk])
    def _():
        y_ref[...] = jnp.zeros(y_ref.shape, F32)


def _expert_ffn(rows, blk_info, wg, wu, wd):
    n_rows = rows.shape[0]
    n_blk = n_rows // MOE_ROWS
    grid_spec = pltpu.PrefetchScalarGridSpec(
        num_scalar_prefetch=1, grid=(n_blk,),
        in_specs=[
            pl.BlockSpec((MOE_ROWS, D_MODEL), lambda i, be: (i, 0)),
            pl.BlockSpec((1, D_MODEL, EXPERT_HIDDEN), lambda i, be: (be[i], 0, 0)),
            pl.BlockSpec((1, D_MODEL, EXPERT_HIDDEN), lambda i, be: (be[i], 0, 0)),
            pl.BlockSpec((1, EXPERT_HIDDEN, D_MODEL), lambda i, be: (be[i], 0, 0)),
        ],
        out_specs=pl.BlockSpec((MOE_ROWS, D_MODEL), lambda i, be: (i, 0)),
        scratch_shapes=[pltpu.VMEM((D_MODEL, EXPERT_HIDDEN), BF16), pltpu.VMEM((D_MODEL, EXPERT_HIDDEN), BF16),
                        pltpu.VMEM((EXPERT_HIDDEN, D_MODEL), BF16)])
    return pl.pallas_call(
        _expert_kernel, grid_spec=grid_spec, out_shape=jax.ShapeDtypeStruct((n_rows, D_MODEL), F32),
        compiler_params=_params(1), name="expert_ffn")(blk_info, rows, wg, wu, wd)


def _moe_dispatch(route, h):
    n_tok = h.shape[0]
    n_asg = n_tok * MOE_TOPK
    i32 = jnp.int32
    e_flat = route[:, 0:MOE_TOPK].astype(i32).reshape(n_asg)
    is_e = e_flat[:, None] == jnp.arange(N_EXPERTS, dtype=i32)[None, :]
    counts = jnp.sum(is_e, axis=0, dtype=i32)
    order = jnp.argsort(e_flat).astype(i32)
    rank = jnp.argsort(order).astype(i32)
    padded = (counts + MOE_ROWS - 1) // MOE_ROWS * MOE_ROWS
    pad_end = jnp.cumsum(padded)
    pad_start = pad_end - padded
    start = jnp.cumsum(counts) - counts
    n_rows = n_asg + N_EXPERTS * MOE_ROWS
    n_blk = n_rows // MOE_ROWS
    blk_start = jnp.arange(n_blk, dtype=i32) * MOE_ROWS
    blk_expert = jnp.minimum(jnp.sum(pad_end[None, :] <= blk_start[:, None], axis=1, dtype=i32), N_EXPERTS - 1)
    within = (blk_start - pad_start[blk_expert])[:, None] + jnp.arange(MOE_ROWS, dtype=i32)[None, :]
    valid = within < counts[blk_expert][:, None]
    src = jnp.clip(start[blk_expert][:, None] + within, 0, n_asg - 1)
    row_tok = jnp.where(valid, (order // MOE_TOPK)[src], src // MOE_TOPK).reshape(n_rows)
    shift = jnp.sum(jnp.where(is_e, (pad_start - start)[None, :], 0), axis=1, dtype=i32)
    pos = (rank + shift).reshape(n_tok, MOE_TOPK)
    blk_info = jnp.concatenate([blk_expert, (pad_end[-1:] // MOE_ROWS).astype(i32)])
    return h[row_tok], blk_info, pos


def _moe(route, h, wg, wu, wd):
    rows, blk_info, pos = _moe_dispatch(route, h)
    y = _expert_ffn(rows, blk_info, wg, wu, wd)
    return y[pos[:, 0]], y[pos[:, 1]], route


def _final_kernel(x_ref, y0_ref, y1_ref, route_ref, g_ref, o_ref):
    x = _moe_combine(x_ref[...], y0_ref, y1_ref, route_ref)
    ms = jnp.mean(x * x, axis=-1, keepdims=True)
    o_ref[...] = x * lax.rsqrt(ms + NORM_EPS) * g_ref[...]


def _final_norm(x, ys, gain):
    T = x.shape[0]
    tm = ROW_TILE
    row = lambda i: (i, 0)
    return pl.pallas_call(
        _final_kernel, grid=(T // tm,),
        in_specs=[pl.BlockSpec((tm, D_MODEL), row)] * 3 + [pl.BlockSpec((tm, LANES), row), pl.BlockSpec((1, D_MODEL), lambda i: (0, 0))],
        out_specs=pl.BlockSpec((tm, D_MODEL), row), out_shape=jax.ShapeDtypeStruct((T, D_MODEL), F32),
        compiler_params=_params(1), name="final_norm")(x, *ys, gain.reshape(1, D_MODEL))


def _rope_tables(positions):
    inv_freq = ROPE_THETA ** (-jnp.arange(0, HEAD_DIM, 2, dtype=F32) / HEAD_DIM)
    ang = positions.astype(F32).reshape(-1, 1) * inv_freq
    cos, sin = jnp.cos(ang), jnp.sin(ang)
    reps = LANES // HEAD_DIM
    return jnp.tile(jnp.concatenate([cos, cos], axis=1), (1, reps)), jnp.tile(jnp.concatenate([-sin, sin], axis=1), (1, reps))


def _heads(t, B, S):
    return t.reshape(B, S, -1, HEAD_DIM).transpose(0, 2, 1, 3)


def _heads_t(t, B, S):
    return t.reshape(B, S, -1, HEAD_DIM).transpose(0, 2, 3, 1)


def _tokens(o):
    B, H, S, d = o.shape
    return o.transpose(0, 2, 1, 3).reshape(B * S, H * d)


def _group_t(t, B, S):
    nq = S // Q_BLOCK
    t = t.reshape(B, nq, Q_BLOCK, -1, GROUP, HEAD_DIM).transpose(0, 3, 1, 5, 4, 2)
    return t.reshape(B, -1, nq, HEAD_DIM, GROUP * Q_BLOCK)


def _group_tokens(oT):
    B, Hkv, nq, d, _ = oT.shape
    o = oT.reshape(B, Hkv, nq, d, GROUP, Q_BLOCK).transpose(0, 2, 5, 1, 4, 3)
    return o.reshape(B * nq * Q_BLOCK, Hkv * GROUP * d)


def _keys_aug(k, block):
    S = k.shape[2]
    lane = jnp.arange(LANES - HEAD_DIM)[None, :]
    extra = ((jnp.arange(S)[:, None] % KEY_TILE) // block == lane) | (lane == AUG_ROWS)
    return jnp.concatenate([k, jnp.broadcast_to(extra.astype(k.dtype), k.shape[:2] + extra.shape)], axis=-1)


def _values_aug(vT):
    B, H, _, S = vT.shape
    ones = jnp.ones((B, H, 1, S), vT.dtype)
    return jnp.concatenate([vT, ones, jnp.zeros((B, H, V_AUG_ROWS - HEAD_DIM - 1, S), vT.dtype)], axis=2)


def _router_weights(router_group, router_expert):
    pad = jnp.zeros((D_MODEL, LANES - N_GROUPS - N_EXPERTS), F32)
    return jnp.concatenate([router_group.astype(F32), router_expert.astype(F32), pad], axis=1)


def _pad_cols(w, n):
    return jnp.concatenate([w, jnp.zeros((w.shape[0], n - w.shape[1]), w.dtype)], axis=1)


def _mixer_ab(proj, sinks, B, S):
    hd = lambda a, b: _heads(proj[:, a:b], B, S)
    hdt = lambda a, b: _heads_t(proj[:, a:b], B, S)
    oa = _swa_attention(_group_t(proj[:, 0:512], B, S), hd(512, 640), hdt(640, 768), sinks)
    ob = _stick_attention(hd(768, 1280), hdt(1280, 1792), hd(1792, 2304))
    return jnp.concatenate([_group_tokens(oa), _tokens(ob)], axis=1)


def _mixer_cd(proj, kmean, B, S, k_pe, k_w1, k_w2, v_pe, v_w1, v_w2):
    hd = lambda a, b: _heads(proj[:, a:b], B, S)
    hdt = lambda a, b: _heads_t(proj[:, a:b], B, S)
    nblk = S // C_BLOCK
    km = kmean.reshape(B, nblk, C_KV_HEADS, HEAD_DIM).transpose(0, 2, 1, 3).astype(BF16)
    oc = _moba_attention(_group_t(proj[:, 0:512], B, S), _keys_aug(hd(512, 640), C_BLOCK), _values_aug(hdt(640, 768)), km)
    k_cmp = _compress(hd(1280, 1408), k_pe, k_w1, k_w2)
    v_cmp = _compress(hd(1408, 1536), v_pe, v_w1, v_w2)
    nc = (S - D_CMP_LEN) // D_CMP_STRIDE + 1
    ncp = S // D_CMP_STRIDE
    nsel = S // D_SEL_LEN
    c_start = jnp.arange(ncp) * D_CMP_STRIDE
    b_start = jnp.arange(nsel) * D_SEL_LEN
    overlap = ((c_start[:, None] <= b_start[None, :] + D_SEL_LEN - 1) & (c_start[:, None] + D_CMP_LEN - 1 >= b_start[None, :])
               & (jnp.arange(ncp)[:, None] < nc)).astype(BF16)
    qdT = _group_t(proj[:, 768:1280], B, S)
    o_cmp, biasT = _cmp_select(qdT, k_cmp, v_cmp.transpose(0, 1, 3, 2), overlap.T, nc)
    o_sel = _sel_attention(qdT, _keys_aug(hd(1536, 1664), D_SEL_LEN), _values_aug(hdt(1664, 1792)), biasT)
    o_win = _win_attention(qdT, hd(1792, 1920), hdt(1920, 2048))
    return _group_tokens(oc), _group_tokens(o_cmp), _group_tokens(o_sel), _group_tokens(o_win)


def kernel(x, positions, ln_mix_0, w_in_0, sinks_0, w_out_0, ln_ffn_0, router_group_0, router_expert_0, expert_gate_0, expert_up_0, expert_down_0, ln_mix_1, w_in_1, cmp_k_pe_1, cmp_k_w1_1, cmp_k_w2_1, cmp_v_pe_1, cmp_v_w1_1, cmp_v_w2_1, w_out_1, ln_ffn_1, router_group_1, router_expert_1, expert_gate_1, expert_up_1, expert_down_1, ln_final):
    B, S, _ = x.shape
    T = B * S
    assert S % KEY_TILE == 0 and T % ROW_TILE == 0
    cos_t, sin_t = _rope_tables(positions)
    xf = x.reshape(T, D_MODEL)

    proj = _in_proj(xf, None, ln_mix_0, w_in_0.astype(BF16), cos_t, sin_t, rope_chunks=range(0, 5))[0]
    o_ab = _mixer_ab(proj, sinks_0, B, S)
    x1, h1, route1 = _out_proj([o_ab], None, xf, w_out_0.astype(BF16), ln_ffn_0, _router_weights(router_group_0, router_expert_0))
    ys = _moe(route1, h1, expert_gate_0, expert_up_0, expert_down_0)

    n_cols = 17 * LANES
    x2, proj, kmean, gd = _in_proj(x1, ys, ln_mix_1, _pad_cols(w_in_1.astype(BF16), n_cols), cos_t, sin_t,
                                   rope_chunks=(0, 1, 2, 3, 4, 6, 7, 8, 9, 10, 12, 14), kmean_chunk=4, gate_chunk=16)
    parts = _mixer_cd(proj, kmean, B, S, cmp_k_pe_1, cmp_k_w1_1, cmp_k_w2_1, cmp_v_pe_1, cmp_v_w1_1, cmp_v_w2_1)
    x3, h3, route3 = _out_proj(parts, gd, x2, w_out_1.astype(BF16), ln_ffn_1, _router_weights(router_group_1, router_expert_1))
    ys = _moe(route3, h3, expert_gate_1, expert_up_1, expert_down_1)
    return _final_norm(x3, ys, ln_final).reshape(B, S, D_MODEL)
```

```python
import functools

import jax
import jax.numpy as jnp
from jax import lax
from jax.experimental import pallas as pl
from jax.experimental.pallas import tpu as pltpu

D_MODEL = 1024
HEAD_DIM = 64
HALF = HEAD_DIM // 2
ROPE_THETA = 10000.0
NORM_EPS = 1e-6
Q_BLOCK = 128
SCALE = HEAD_DIM ** -0.5

A_HEADS, A_KV_HEADS, A_WINDOW = 8, 2, 128
B_HEADS = 8
C_HEADS, C_KV_HEADS, C_BLOCK, C_TOPK = 8, 2, 256, 3
D_HEADS, D_KV_HEADS = 8, 2
D_CMP_LEN, D_CMP_STRIDE, D_CMP_HIDDEN = 32, 16, 256
D_SEL_LEN, D_SEL_TOPK, D_WINDOW = 64, 16, 512
N_GROUPS, EXPERTS_PER_GROUP, MOE_TOPK, EXPERT_HIDDEN = 4, 16, 2, 512
N_EXPERTS = N_GROUPS * EXPERTS_PER_GROUP
GROUP = 4

LANES = 128
ROW_TILE = 512
MOE_ROWS = 256
KEY_TILE = 512
FLASH_MIN_SUM = 1e-25
V_AUG_ROWS = 80
AUG_ROWS = 8
MASK_BIG = 32768.0
SB_EXIT = -104.0
STICK_HEADS = 8
VMEM_LIMIT = 56 * 1024 * 1024

F32 = jnp.float32
BF16 = jnp.bfloat16
NEG_INF = float("-inf")


def _iota(shape, dim):
    return lax.broadcasted_iota(jnp.int32, shape, dim)


def _dot(a, b):
    return jnp.dot(a, b, preferred_element_type=F32)


def _params(n_grid):
    return pltpu.CompilerParams(dimension_semantics=("arbitrary",) * n_grid, vmem_limit_bytes=VMEM_LIMIT)


GROUP_COLS = GROUP * HEAD_DIM


def _group_qT(q_ref):
    t = (q_ref[...].astype(F32) * SCALE).T
    return jnp.concatenate([t[g * HEAD_DIM:(g + 1) * HEAD_DIM] for g in range(GROUP)], axis=1).astype(BF16)


def _group_rows(oT):
    return jnp.concatenate([oT[:, g * Q_BLOCK:(g + 1) * Q_BLOCK] for g in range(GROUP)], axis=0).T


def _q_spec(S, col0):
    nq = S // Q_BLOCK
    return pl.BlockSpec((Q_BLOCK, GROUP_COLS), lambda b, h, n: (b * nq + n, col0 // GROUP_COLS + h))


def _o_spec(S):
    nq = S // Q_BLOCK
    return pl.BlockSpec((Q_BLOCK, GROUP_COLS), lambda b, h, n: (b * nq + n, h))


def _moe_combine(x, y0_ref, y1_ref, route_ref):
    return x + (y0_ref[...] * route_ref[:, MOE_TOPK:MOE_TOPK + 1] + y1_ref[...] * route_ref[:, MOE_TOPK + 1:MOE_TOPK + 2])


def _in_proj_kernel(rope_chunks, kmean_chunk, gate_chunk, combine, *refs):
    refs = list(refs)
    x_ref = refs.pop(0)
    if combine:
        y0_ref, y1_ref, route_ref = refs.pop(0), refs.pop(0), refs.pop(0)
    g_ref, w_ref, cos_ref, sin_ref = refs[:4]
    outs = refs[4:]
    x = x_ref[...]
    if combine:
        x = _moe_combine(x, y0_ref, y1_ref, route_ref)
        xo_ref = outs.pop(0)
        xo_ref[...] = x
    o_ref = outs.pop(0)
    ms = jnp.mean(x * x, axis=-1, keepdims=True)
    h = (x * lax.rsqrt(ms + NORM_EPS) * g_ref[...]).astype(BF16)
    tm = x.shape[0]
    n_cols = w_ref.shape[1]
    cos = cos_ref[...]
    sin = sin_ref[...]
    first_half = (_iota((tm, LANES), 1) & (HEAD_DIM - 1)) < HALF
    for c in range(n_cols // LANES):
        ch = _dot(h, w_ref[:, c * LANES:(c + 1) * LANES])
        if c in rope_chunks:
            partner = jnp.where(first_half, pltpu.roll(ch, LANES - HALF, 1), pltpu.roll(ch, HALF, 1))
            ch = ch * cos + partner * sin
        if c == kmean_chunk:
            km_ref = outs[0]
            km_ref[...] = jnp.mean(ch.reshape(tm // C_BLOCK, C_BLOCK, LANES), axis=1).reshape(tm // C_BLOCK, 1, LANES)
        if c == gate_chunk:
            outs[-1][...] = ch
        o_ref[:, c * LANES:(c + 1) * LANES] = ch.astype(BF16)


def _in_proj(x, ys, gain, w, cos_t, sin_t, rope_chunks, kmean_chunk=None, gate_chunk=None):
    T = x.shape[0]
    n_cols = w.shape[1]
    tm = ROW_TILE
    combine = ys is not None
    row = lambda i: (i, 0)
    fixed = lambda i: (0, 0)
    in_specs = [pl.BlockSpec((tm, D_MODEL), row)]
    args = [x]
    if combine:
        in_specs += [pl.BlockSpec((tm, D_MODEL), row)] * 2 + [pl.BlockSpec((tm, LANES), row)]
        args += list(ys)
    in_specs += [pl.BlockSpec((1, D_MODEL), fixed), pl.BlockSpec((D_MODEL, n_cols), fixed),
                 pl.BlockSpec((tm, LANES), row), pl.BlockSpec((tm, LANES), row)]
    args += [gain.reshape(1, D_MODEL), w, cos_t, sin_t]
    out_shape, out_specs = [], []
    if combine:
        out_shape.append(jax.ShapeDtypeStruct((T, D_MODEL), F32))
        out_specs.append(pl.BlockSpec((tm, D_MODEL), row))
    out_shape.append(jax.ShapeDtypeStruct((T, n_cols), BF16))
    out_specs.append(pl.BlockSpec((tm, n_cols), row))
    if kmean_chunk is not None:
        out_shape.append(jax.ShapeDtypeStruct((T // C_BLOCK, 1, LANES), F32))
        out_specs.append(pl.BlockSpec((tm // C_BLOCK, 1, LANES), lambda i: (i, 0, 0)))
    if gate_chunk is not None:
        out_shape.append(jax.ShapeDtypeStruct((T, LANES), F32))
        out_specs.append(pl.BlockSpec((tm, LANES), row))
    kern = functools.partial(_in_proj_kernel, tuple(rope_chunks), kmean_chunk, gate_chunk, combine)
    return pl.pallas_call(kern, grid=(T // tm,), in_specs=in_specs, out_specs=out_specs, out_shape=out_shape,
                          compiler_params=_params(1), name="in_proj")(*args)


def _swa_kernel(q_ref, kp_ref, ko_ref, vp_ref, vo_ref, sink_ref, o_ref):
    n = pl.program_id(2)
    qs = _group_qT(q_ref)
    R = qs.shape[1]
    k = jnp.concatenate([kp_ref[0, 0], ko_ref[0, 0]], axis=0)
    vT = jnp.concatenate([vp_ref[0, 0], vo_ref[0, 0]], axis=1)
    s = _dot(k, qs)
    qpos = _iota((2 * Q_BLOCK, R), 1) & (Q_BLOCK - 1)
    ki = _iota((2 * Q_BLOCK, R), 0)
    d = qpos + Q_BLOCK - ki
    ok = (d >= 0) & (d < A_WINDOW) & ((ki >= Q_BLOCK) | (n > 0))
    s = jnp.where(ok, s, NEG_INF)
    sink = sink_ref[0]
    m = jnp.maximum(jnp.max(s, axis=0, keepdims=True), sink)
    p = jnp.exp(s - m)
    den = jnp.sum(p, axis=0, keepdims=True) + jnp.exp(sink - m)
    o_ref[...] = _group_rows(_dot(vT, p.astype(BF16)) * (1.0 / den)).astype(o_ref.dtype)


def _swa_attention(proj, q_col, k, vT, sinks):
    B, Hkv, S, _ = k.shape
    nb = S // Q_BLOCK
    R = GROUP * Q_BLOCK
    sink_row = jnp.repeat(sinks.astype(F32).reshape(Hkv, GROUP), Q_BLOCK, axis=1).reshape(Hkv, 1, R)
    prev = lambda n: jnp.maximum(n - 1, 0)
    in_specs = [
        _q_spec(S, q_col),
        pl.BlockSpec((1, 1, Q_BLOCK, HEAD_DIM), lambda b, h, n: (b, h, prev(n), 0)),
        pl.BlockSpec((1, 1, Q_BLOCK, HEAD_DIM), lambda b, h, n: (b, h, n, 0)),
        pl.BlockSpec((1, 1, HEAD_DIM, Q_BLOCK), lambda b, h, n: (b, h, 0, prev(n))),
        pl.BlockSpec((1, 1, HEAD_DIM, Q_BLOCK), lambda b, h, n: (b, h, 0, n)),
        pl.BlockSpec((1, 1, R), lambda b, h, n: (h, 0, 0)),
    ]
    return pl.pallas_call(
        _swa_kernel, grid=(B, Hkv, nb), in_specs=in_specs,
        out_specs=_o_spec(S), out_shape=jax.ShapeDtypeStruct((B * S, Hkv * GROUP_COLS), BF16),
        compiler_params=_params(3), name="swa_attention")(proj, k, k, vT, vT, sink_row)


def _stick_kernel(q_ref, kT_ref, vT_ref, o_ref):
    n = pl.program_id(2)
    tq = tk = Q_BLOCK
    heads = q_ref.shape[1]
    row = _iota((tq, tk), 0)
    col = _iota((tk, tk), 1)
    upper = jnp.where(_iota((tk, tk), 0) > col, 1.0, 0.0).astype(BF16)
    tpos = n * tq + row

    def body(carry):
        kb, _, cs, accs = carry
        start = pl.multiple_of(kb * tk, tk)
        past = (start + col) < tpos
        hs = range(heads)
        zs = [_dot(q_ref[0, h], kT_ref[0, h, :, pl.ds(start, tk)]) * SCALE for h in hs]
        sps = [jnp.maximum(z, 0.0) + jnp.log1p(jnp.exp(-jnp.abs(z))) for z in zs]
        stays = [jnp.where(past, -sp, 0.0) for sp in sps]
        his = [st.astype(BF16) for st in stays]
        los = [(st - hi.astype(F32)).astype(BF16) for st, hi in zip(stays, his)]
        betweens = [_dot(hi, upper) + _dot(lo, upper) for hi, lo in zip(his, los)]
        ws = [jnp.where(past, jnp.exp(zs[h] - sps[h] + betweens[h] + cs[h]), 0.0).astype(BF16) for h in hs]
        new_accs = [accs[h] + lax.dot_general(ws[h], vT_ref[0, h, :, pl.ds(start, tk)], (((1,), (1,)), ((), ())),
                                              preferred_element_type=F32) for h in hs]
        new_cs = [cs[h] + jnp.sum(stays[h], axis=-1, keepdims=True) for h in hs]
        worst = functools.reduce(jnp.maximum, new_cs)
        return kb - 1, jnp.max(worst) > SB_EXIT, tuple(new_cs), tuple(new_accs)

    def cond(carry):
        kb, alive, _, _ = carry
        return (kb >= 0) & alive

    init = (n, jnp.array(True), (jnp.zeros((tq, 1), F32),) * heads, (jnp.zeros((tq, HEAD_DIM), F32),) * heads)
    _, _, _, accs = lax.while_loop(cond, body, init)
    for h in range(heads):
        o_ref[0, h] = accs[h].astype(o_ref.dtype)


def _stick_attention(q, kT, v):
    B, H, S, _ = q.shape
    nq = S // Q_BLOCK
    hb = STICK_HEADS
    resident = dict(pipeline_mode=pl.Buffered(1))
    in_specs = [
        pl.BlockSpec((1, hb, Q_BLOCK, HEAD_DIM), lambda b, h, n: (b, h, n, 0)),
        pl.BlockSpec((1, hb, HEAD_DIM, S), lambda b, h, n: (b, h, 0, 0), **resident),
        pl.BlockSpec((1, hb, HEAD_DIM, S), lambda b, h, n: (b, h, 0, 0), **resident),
    ]
    return pl.pallas_call(
        _stick_kernel, grid=(B, H // hb, nq), in_specs=in_specs,
        out_specs=pl.BlockSpec((1, hb, Q_BLOCK, HEAD_DIM), lambda b, h, n: (b, h, n, 0)),
        out_shape=jax.ShapeDtypeStruct((B, H, S, HEAD_DIM), BF16),
        compiler_params=_params(3), name="stick_attention")(q, kT, v)


def _max_key_sqnorm(ka_ref, out_ref):
    S = ka_ref.shape[2]
    tk = KEY_TILE
    is_key_lane = _iota((tk, LANES), 1) < HEAD_DIM
    ones = jnp.ones((LANES, LANES), BF16)

    def body(i, mx):
        k = jnp.where(is_key_lane, ka_ref[0, 0, pl.ds(pl.multiple_of(i * tk, tk), tk), :].astype(F32), 0.0)
        sq = _dot((k * k).astype(BF16), ones)
        return jnp.maximum(mx, jnp.max(sq, axis=0, keepdims=True))

    mx = lax.fori_loop(0, S // tk, body, jnp.zeros((1, LANES), F32))
    out_ref[...] = jnp.broadcast_to(mx, out_ref.shape)


def _flash_scratch(R):
    return [pltpu.VMEM((AUG_ROWS, LANES), F32), pltpu.VMEM((2, KEY_TILE, R), BF16), pltpu.VMEM((V_AUG_ROWS, R), F32)]


def _masked_flash_t(n, qT, bias_rows, ka_ref, vTa_ref, ksq_ref, p_scr, acc_scr):
    R = qT.shape[1]
    tk = KEY_TILE
    diag = (n * Q_BLOCK) // tk
    zpad = jnp.zeros((LANES - HEAD_DIM - 2 * AUG_ROWS, R), F32)
    causal = diag * tk + _iota((tk, R), 0) <= n * Q_BLOCK + (_iota((tk, R), 1) & (Q_BLOCK - 1))

    def scores(kt, ref_rows):
        st = pl.multiple_of(kt * tk, tk)
        low = jnp.concatenate([bias_rows(kt), ref_rows, zpad], axis=0).astype(BF16)
        return _dot(ka_ref[0, 0, pl.ds(st, tk), :], jnp.concatenate([qT, low], axis=0))

    def values(kt):
        return vTa_ref[0, 0, :, pl.ds(pl.multiple_of(kt * tk, tk), tk)]

    qsq = jnp.sum(jnp.square(qT.astype(F32)), axis=0, keepdims=True)
    ksq = jnp.concatenate([ksq_ref[0:1, :]] * (R // LANES), axis=1)
    ref = jnp.sqrt(qsq * ksq) * 1.05
    ref_rows = jnp.where(_iota((AUG_ROWS, R), 0) == 0, -ref, 0.0)

    def fast_body(kt, carry):
        slot = kt & 1
        s = scores(kt, ref_rows)
        acc_scr[...] += _dot(values(jnp.maximum(kt - 1, 0)), p_scr[1 - slot])
        p_scr[slot] = jnp.exp(s).astype(BF16)
        return carry

    p_scr[1] = jnp.zeros((tk, R), BF16)
    acc_scr[...] = jnp.zeros(acc_scr.shape, F32)
    lax.fori_loop(0, diag, fast_body, 0)
    acc = acc_scr[...] + _dot(values(jnp.maximum(diag - 1, 0)), p_scr[1 - (diag & 1)])
    s = jnp.where(causal, scores(diag, ref_rows), -1e30)
    acc = acc + _dot(values(diag), jnp.exp(s).astype(BF16))

    def running_max_path(_):
        no_ref = jnp.zeros((AUG_ROWS, R), F32)

        def update(kt, s, m, acc):
            m_new = jnp.maximum(m, jnp.max(s, axis=0, keepdims=True))
            p = jnp.exp(s - m_new).astype(BF16)
            return m_new, jnp.exp(m - m_new) * acc + _dot(values(kt), p)

        init = (jnp.full((1, R), -1e30, F32), jnp.zeros((vTa_ref.shape[2], R), F32))
        m, acc = lax.fori_loop(0, diag, lambda kt, c: update(kt, scores(kt, no_ref), *c), init)
        return update(diag, jnp.where(causal, scores(diag, no_ref), -1e30), m, acc)[1]

    healthy = jnp.min(acc[HEAD_DIM:HEAD_DIM + 1, :]) >= FLASH_MIN_SUM
    acc = lax.cond(healthy, lambda _: acc, running_max_path, 0)
    return acc[0:HEAD_DIM] * (1.0 / acc[HEAD_DIM:HEAD_DIM + 1])


def _moba_kernel(q_ref, ka_ref, vTa_ref, km_ref, o_ref, bias_scr, ksq_scr, p_scr, acc_scr):
    n = pl.program_id(2)
    pl.when(n == 0)(lambda: _max_key_sqnorm(ka_ref, ksq_scr))
    nblk = km_ref.shape[2]
    qs = _group_qT(q_ref)
    R = qs.shape[1]
    own = (n * Q_BLOCK) // C_BLOCK
    gate = _dot(km_ref[0, 0], qs)
    blk = _iota((nblk, R), 0)
    gate = jnp.where(blk < own, gate, NEG_INF)
    bias = jnp.where(blk == own, 0.0, -MASK_BIG)
    for _ in range(C_TOPK):
        mx = jnp.max(gate, axis=0, keepdims=True)
        idx = jnp.min(jnp.where(gate == mx, blk, nblk), axis=0, keepdims=True)
        hit = blk == idx
        bias = jnp.where(hit & (mx > NEG_INF), 0.0, bias)
        gate = jnp.where(hit, NEG_INF, gate)
    per = KEY_TILE // C_BLOCK
    rows = bias_scr.shape[0]
    r = _iota((rows, nblk), 0)
    spread = jnp.where(((r & (AUG_ROWS - 1)) < per) & (_iota((rows, nblk), 1) == (r >> 3) * per + (r & (AUG_ROWS - 1))), 1.0, 0.0)
    bias_scr[...] = _dot(spread.astype(BF16), bias.astype(BF16))
    oT = _masked_flash_t(n, qs, lambda kt: bias_scr[pl.ds(pl.multiple_of(kt * AUG_ROWS, AUG_ROWS), AUG_ROWS), :],
                         ka_ref, vTa_ref, ksq_scr, p_scr, acc_scr)
    o_ref[...] = _group_rows(oT).astype(o_ref.dtype)


def _moba_attention(proj, q_col, ka, vTa, km):
    B, Hkv, S, _ = ka.shape
    nq = S // Q_BLOCK
    R = GROUP * Q_BLOCK
    nblk = km.shape[2]
    nkt = S // KEY_TILE
    in_specs = [
        _q_spec(S, q_col),
        pl.BlockSpec((1, 1, S, LANES), lambda b, h, n: (b, h, 0, 0)),
        pl.BlockSpec((1, 1, V_AUG_ROWS, S), lambda b, h, n: (b, h, 0, 0)),
        pl.BlockSpec((1, 1, nblk, HEAD_DIM), lambda b, h, n: (b, h, 0, 0)),
    ]
    return pl.pallas_call(
        _moba_kernel, grid=(B, Hkv, nq), in_specs=in_specs,
        out_specs=_o_spec(S), out_shape=jax.ShapeDtypeStruct((B * S, Hkv * GROUP_COLS), BF16),
        scratch_shapes=[pltpu.VMEM((nkt * AUG_ROWS, R), F32)] + _flash_scratch(R),
        compiler_params=_params(3), name="moba_attention")(proj, ka, vTa, km)


def _compress_kernel(u_ref, us_ref, pe_ref, w1_ref, w2_ref, o_ref):
    a = (u_ref[0, 0].astype(F32) + pe_ref[0:1, :]).astype(BF16)
    b = (us_ref[0, 0].astype(F32) + pe_ref[1:2, :]).astype(BF16)
    pre = _dot(a, w1_ref[0]) + _dot(b, w1_ref[1])
    hid = jax.nn.gelu(pre)
    o_ref[0, 0] = _dot(hid.astype(BF16), w2_ref[...]).astype(o_ref.dtype)


def _compress(t, pe, w1, w2):
    B, H, S, _ = t.shape
    nrow = S // D_CMP_STRIDE
    width = D_CMP_STRIDE * HEAD_DIM
    u = t.reshape(B, H, nrow, width)
    us = jnp.concatenate([u[:, :, 1:], jnp.zeros((B, H, 1, width), u.dtype)], axis=2)
    blk = lambda b, h: (b, h, 0, 0)
    in_specs = [
        pl.BlockSpec((1, 1, nrow, width), blk), pl.BlockSpec((1, 1, nrow, width), blk),
        pl.BlockSpec((2, width), lambda b, h: (0, 0)),
        pl.BlockSpec((2, width, D_CMP_HIDDEN), lambda b, h: (0, 0, 0)),
        pl.BlockSpec((D_CMP_HIDDEN, HEAD_DIM), lambda b, h: (0, 0)),
    ]
    return pl.pallas_call(
        _compress_kernel, grid=(B, H), in_specs=in_specs,
        out_specs=pl.BlockSpec((1, 1, nrow, HEAD_DIM), blk),
        out_shape=jax.ShapeDtypeStruct((B, H, nrow, HEAD_DIM), BF16),
        compiler_params=_params(2), name="nsa_compress")(
            u, us, pe.astype(F32).reshape(2, width), w1.astype(BF16).reshape(2, width, D_CMP_HIDDEN), w2.astype(BF16))


def _cmp_select_kernel(nc, q_ref, kc_ref, vcT_ref, ovT_ref, oc_ref, bias_ref):
    n = pl.program_id(2)
    ncp = kc_ref.shape[2]
    nsel = ovT_ref.shape[0]
    qs = _group_qT(q_ref)
    R = qs.shape[1]
    s = _dot(kc_ref[0, 0], qs)
    tpos = n * Q_BLOCK + (_iota((ncp, R), 1) & (Q_BLOCK - 1))
    c = _iota((ncp, R), 0)
    ok = (c * D_CMP_STRIDE + (D_CMP_LEN - 1) <= tpos) & (c < nc)
    s = jnp.where(ok, s, NEG_INF)
    m = jnp.max(s, axis=0, keepdims=True)
    m = jnp.where(m > NEG_INF, m, 0.0)
    e = jnp.exp(s - m)
    den = jnp.sum(e, axis=0, keepdims=True)
    p = (e * (1.0 / jnp.where(den > 0, den, 1.0))).astype(BF16)
    oc_ref[...] = _group_rows(_dot(vcT_ref[0, 0], p))
    imp_heads = _dot(ovT_ref[...], p)
    imp = imp_heads[:, 0:Q_BLOCK]
    for g in range(1, GROUP):
        imp = imp + imp_heads[:, g * Q_BLOCK:(g + 1) * Q_BLOCK]
    t = n * Q_BLOCK + _iota((nsel, Q_BLOCK), 1)
    j = _iota((nsel, Q_BLOCK), 0)
    cur = t >> 6
    allowed = j * D_SEL_LEN <= t
    forced = (j == 0) | (j == cur) | (j == cur - 1)
    imp = jnp.where(allowed, imp, NEG_INF)
    imp = jnp.where(allowed & forced, float("inf"), imp)
    bias = jnp.full((nsel, Q_BLOCK), -MASK_BIG, F32)
    for _ in range(D_SEL_TOPK):
        mx = jnp.max(imp, axis=0, keepdims=True)
        idx = jnp.min(jnp.where(imp == mx, j, nsel), axis=0, keepdims=True)
        hit = j == idx
        bias = jnp.where(hit & (mx > NEG_INF), 0.0, bias)
        imp = jnp.where(hit, NEG_INF, imp)
    bias_ref[0, 0, 0] = bias


def _cmp_select(proj, q_col, kc, vcT, overlapT, nc, S):
    B, Hkv, ncp, _ = kc.shape
    nq = S // Q_BLOCK
    nsel = overlapT.shape[0]
    in_specs = [
        _q_spec(S, q_col),
        pl.BlockSpec((1, 1, ncp, HEAD_DIM), lambda b, h, n: (b, h, 0, 0)),
        pl.BlockSpec((1, 1, HEAD_DIM, ncp), lambda b, h, n: (b, h, 0, 0)),
        pl.BlockSpec((nsel, ncp), lambda b, h, n: (0, 0)),
    ]
    out_specs = [_o_spec(S), pl.BlockSpec((1, 1, 1, nsel, Q_BLOCK), lambda b, h, n: (b, h, n, 0, 0))]
    out_shape = [jax.ShapeDtypeStruct((B * S, Hkv * GROUP_COLS), F32), jax.ShapeDtypeStruct((B, Hkv, nq, nsel, Q_BLOCK), F32)]
    return pl.pallas_call(
        functools.partial(_cmp_select_kernel, nc), grid=(B, Hkv, nq), in_specs=in_specs, out_specs=out_specs,
        out_shape=out_shape, compiler_params=_params(3), name="nsa_cmp_select")(proj, kc, vcT, overlapT)


def _sel_kernel(q_ref, ka_ref, vTa_ref, bias_ref, o_ref, ksq_scr, p_scr, acc_scr):
    n = pl.program_id(2)
    pl.when(n == 0)(lambda: _max_key_sqnorm(ka_ref, ksq_scr))
    qs = _group_qT(q_ref)

    def bias_rows(kt):
        b = bias_ref[0, 0, 0, pl.ds(pl.multiple_of(kt * AUG_ROWS, AUG_ROWS), AUG_ROWS), :]
        return jnp.concatenate([b] * GROUP, axis=1)

    o_ref[...] = _group_rows(_masked_flash_t(n, qs, bias_rows, ka_ref, vTa_ref, ksq_scr, p_scr, acc_scr))


def _sel_attention(proj, q_col, ka, vTa, biasT):
    B, Hkv, S, _ = ka.shape
    nq = S // Q_BLOCK
    nsel = biasT.shape[3]
    assert KEY_TILE // D_SEL_LEN == AUG_ROWS
    in_specs = [
        _q_spec(S, q_col),
        pl.BlockSpec((1, 1, S, LANES), lambda b, h, n: (b, h, 0, 0)),
        pl.BlockSpec((1, 1, V_AUG_ROWS, S), lambda b, h, n: (b, h, 0, 0)),
        pl.BlockSpec((1, 1, 1, nsel, Q_BLOCK), lambda b, h, n: (b, h, n, 0, 0)),
    ]
    return pl.pallas_call(
        _sel_kernel, grid=(B, Hkv, nq), in_specs=in_specs,
        out_specs=_o_spec(S), out_shape=jax.ShapeDtypeStruct((B * S, Hkv * GROUP_COLS), F32),
        scratch_shapes=_flash_scratch(GROUP * Q_BLOCK),
        compiler_params=_params(3), name="nsa_selected")(proj, ka, vTa, biasT)


def _win_kernel(span, q_ref, k_ref, vT_ref, o_ref):
    n = pl.program_id(2)
    qs = _group_qT(q_ref)
    R = qs.shape[1]
    start = pl.multiple_of(jnp.maximum(n * Q_BLOCK + Q_BLOCK - span, 0), Q_BLOCK)
    s = _dot(k_ref[0, 0, pl.ds(start, span), :], qs)
    tpos = n * Q_BLOCK + (_iota((span, R), 1) & (Q_BLOCK - 1))
    d = tpos - (start + _iota((span, R), 0))
    s = jnp.where((d >= 0) & (d < D_WINDOW), s, NEG_INF)
    m = jnp.max(s, axis=0, keepdims=True)
    p = jnp.exp(s - m)
    l = jnp.sum(p, axis=0, keepdims=True)
    o_ref[...] = _group_rows(_dot(vT_ref[0, 0, :, pl.ds(start, span)], p.astype(BF16)) * (1.0 / l))


def _win_attention(proj, q_col, k, vT):
    B, Hkv, S, _ = k.shape
    nq = S // Q_BLOCK
    span = min(D_WINDOW + Q_BLOCK, S)
    in_specs = [
        _q_spec(S, q_col),
        pl.BlockSpec((1, 1, S, HEAD_DIM), lambda b, h, n: (b, h, 0, 0)),
        pl.BlockSpec((1, 1, HEAD_DIM, S), lambda b, h, n: (b, h, 0, 0)),
    ]
    return pl.pallas_call(
        functools.partial(_win_kernel, span), grid=(B, Hkv, nq), in_specs=in_specs,
        out_specs=_o_spec(S), out_shape=jax.ShapeDtypeStruct((B * S, Hkv * GROUP_COLS), F32),
        compiler_params=_params(3), name="nsa_window")(proj, k, vT)


def _route(logits):
    tm = logits.shape[0]
    lane = _iota((tm, LANES), 1)
    gl = jnp.where(lane < N_GROUPS, logits, NEG_INF)
    gmax = jnp.max(gl, axis=-1, keepdims=True)
    gidx = jnp.min(jnp.where(gl == gmax, lane, LANES), axis=-1, keepdims=True)
    g_prob = 1.0 / jnp.sum(jnp.exp(gl - gmax), axis=-1, keepdims=True)
    elane = lane - N_GROUPS
    in_group = (elane >= 0) & (elane < N_EXPERTS) & ((elane >> 4) == gidx)
    el = jnp.where(in_group, logits, NEG_INF)
    ee = jnp.exp(el - jnp.max(el, axis=-1, keepdims=True))
    ep = jnp.where(in_group, ee / jnp.sum(ee, axis=-1, keepdims=True), -1.0)
    p1 = jnp.max(ep, axis=-1, keepdims=True)
    i1 = jnp.min(jnp.where(ep == p1, lane, LANES), axis=-1, keepdims=True)
    ep2 = jnp.where(lane == i1, -1.0, ep)
    p2 = jnp.max(ep2, axis=-1, keepdims=True)
    i2 = jnp.min(jnp.where(ep2 == p2, lane, LANES), axis=-1, keepdims=True)
    den = p1 + p2
    vals = [(i1 - N_GROUPS).astype(F32), (i2 - N_GROUPS).astype(F32), g_prob * p1 / den, g_prob * p2 / den]
    out = jnp.zeros((tm, LANES), F32)
    for k, val in enumerate(vals):
        out = jnp.where(lane == k, val, out)
    return out


def _out_tail(x_new, gain_ref, wr_ref, xo_ref, h_ref, route_ref):
    xo_ref[...] = x_new
    ms = jnp.mean(x_new * x_new, axis=-1, keepdims=True)
    h = x_new * lax.rsqrt(ms + NORM_EPS) * gain_ref[...]
    h_ref[...] = h
    logits = jnp.dot(h, wr_ref[...], preferred_element_type=F32, precision=lax.Precision.HIGHEST)
    route_ref[...] = _route(logits)


def _out_proj_kernel(o_ref, x_ref, w_ref, gain_ref, wr_ref, xo_ref, h_ref, route_ref):
    x_new = x_ref[...] + _dot(o_ref[...], w_ref[...])
    _out_tail(x_new, gain_ref, wr_ref, xo_ref, h_ref, route_ref)


def _out_proj_nsa_kernel(oc_ref, b0_ref, b1_ref, b2_ref, gd_ref, x_ref, w_ref, gain_ref, wr_ref, xo_ref, h_ref, route_ref):
    half = D_HEADS * HEAD_DIM
    g = jax.nn.sigmoid(gd_ref[...])
    g_hi = g.astype(BF16)
    g_lo = (g - g_hi.astype(F32)).astype(BF16)
    src = _iota((LANES, half), 0)
    head3 = (_iota((LANES, half), 1) >> 6) * 3
    od = None
    for br, b_ref in enumerate((b0_ref, b1_ref, b2_ref)):
        spread = jnp.where(src == head3 + br, 1.0, 0.0).astype(BF16)
        term = (_dot(g_hi, spread) + _dot(g_lo, spread)) * b_ref[...]
        od = term if od is None else od + term
    x_new = x_ref[...] + _dot(oc_ref[...], w_ref[0:half, :]) + _dot(od.astype(BF16), w_ref[half:, :])
    _out_tail(x_new, gain_ref, wr_ref, xo_ref, h_ref, route_ref)


def _out_proj(o_parts, gd, x, w_out, gain, w_router):
    T = x.shape[0]
    tm = ROW_TILE
    row = lambda i: (i, 0)
    fixed = lambda i: (0, 0)
    if gd is None:
        kern = _out_proj_kernel
        args = [o_parts[0]]
        in_specs = [pl.BlockSpec((tm, D_MODEL), row)]
    else:
        kern = _out_proj_nsa_kernel
        args = list(o_parts) + [gd]
        in_specs = [pl.BlockSpec((tm, D_MODEL // 2), row)] * 4 + [pl.BlockSpec((tm, LANES), row)]
    args += [x, w_out, gain.reshape(1, D_MODEL), w_router]
    in_specs += [pl.BlockSpec((tm, D_MODEL), row), pl.BlockSpec((D_MODEL, D_MODEL), fixed),
                 pl.BlockSpec((1, D_MODEL), fixed), pl.BlockSpec((D_MODEL, LANES), fixed)]
    out_shape = [jax.ShapeDtypeStruct((T, D_MODEL), F32), jax.ShapeDtypeStruct((T, D_MODEL), F32),
                 jax.ShapeDtypeStruct((T, LANES), F32)]
    out_specs = [pl.BlockSpec((tm, D_MODEL), row), pl.BlockSpec((tm, D_MODEL), row), pl.BlockSpec((tm, LANES), row)]
    return pl.pallas_call(kern, grid=(T // tm,), in_specs=in_specs, out_specs=out_specs, out_shape=out_shape,
                          compiler_params=_params(1), name="out_proj_router")(*args)


def _expert_kernel(be_ref, rows_ref, wg_ref, wu_ref, wd_ref, y_ref, wg_s, wu_s, wd_s):
    i = pl.program_id(0)
    n_blk = pl.num_programs(0)

    @pl.when((i == 0) | (be_ref[i] != be_ref[jnp.maximum(i - 1, 0)]))
    def _():
        wg_s[...] = wg_ref[0].astype(BF16)
        wu_s[...] = wu_ref[0].astype(BF16)
        wd_s[...] = wd_ref[0].astype(BF16)

    @pl.when(i < be_ref[n_blk])
    def _():
        xb = rows_ref[...].astype(BF16)
        hid = jax.nn.silu(_dot(xb, wg_s[...])) * _dot(xb, wu_s[...])
        y_ref[...] = _dot(hid.astype(BF16), wd_s[...])

    @pl.when(i >= be_ref[n_blk])
    def _():
        y_ref[...] = jnp.zeros(y_ref.shape, F32)


def _expert_ffn(rows, blk_info, wg, wu, wd):
    n_rows = rows.shape[0]
    n_blk = n_rows // MOE_ROWS
    grid_spec = pltpu.PrefetchScalarGridSpec(
        num_scalar_prefetch=1, grid=(n_blk,),
        in_specs=[
            pl.BlockSpec((MOE_ROWS, D_MODEL), lambda i, be: (i, 0)),
            pl.BlockSpec((1, D_MODEL, EXPERT_HIDDEN), lambda i, be: (be[i], 0, 0)),
            pl.BlockSpec((1, D_MODEL, EXPERT_HIDDEN), lambda i, be: (be[i], 0, 0)),
            pl.BlockSpec((1, EXPERT_HIDDEN, D_MODEL), lambda i, be: (be[i], 0, 0)),
        ],
        out_specs=pl.BlockSpec((MOE_ROWS, D_MODEL), lambda i, be: (i, 0)),
        scratch_shapes=[pltpu.VMEM((D_MODEL, EXPERT_HIDDEN), BF16), pltpu.VMEM((D_MODEL, EXPERT_HIDDEN), BF16),
                        pltpu.VMEM((EXPERT_HIDDEN, D_MODEL), BF16)])
    return pl.pallas_call(
        _expert_kernel, grid_spec=grid_spec, out_shape=jax.ShapeDtypeStruct((n_rows, D_MODEL), F32),
        compiler_params=_params(1), name="expert_ffn")(blk_info, rows, wg, wu, wd)


def _moe_dispatch(route, h):
    n_tok = h.shape[0]
    n_asg = n_tok * MOE_TOPK
    i32 = jnp.int32
    e_flat = route[:, 0:MOE_TOPK].astype(i32).reshape(n_asg)
    is_e = e_flat[:, None] == jnp.arange(N_EXPERTS, dtype=i32)[None, :]
    counts = jnp.sum(is_e, axis=0, dtype=i32)
    order = jnp.argsort(e_flat).astype(i32)
    rank = jnp.argsort(order).astype(i32)
    padded = (counts + MOE_ROWS - 1) // MOE_ROWS * MOE_ROWS
    pad_end = jnp.cumsum(padded)
    pad_start = pad_end - padded
    start = jnp.cumsum(counts) - counts
    n_rows = n_asg + N_EXPERTS * MOE_ROWS
    n_blk = n_rows // MOE_ROWS
    blk_start = jnp.arange(n_blk, dtype=i32) * MOE_ROWS
    blk_expert = jnp.minimum(jnp.sum(pad_end[None, :] <= blk_start[:, None], axis=1, dtype=i32), N_EXPERTS - 1)
    within = (blk_start - pad_start[blk_expert])[:, None] + jnp.arange(MOE_ROWS, dtype=i32)[None, :]
    valid = within < counts[blk_expert][:, None]
    src = jnp.clip(start[blk_expert][:, None] + within, 0, n_asg - 1)
    row_tok = jnp.where(valid, (order // MOE_TOPK)[src], src // MOE_TOPK).reshape(n_rows)
    shift = jnp.sum(jnp.where(is_e, (pad_start - start)[None, :], 0), axis=1, dtype=i32)
    pos = (rank + shift).reshape(n_tok, MOE_TOPK)
    blk_info = jnp.concatenate([blk_expert, (pad_end[-1:] // MOE_ROWS).astype(i32)])
    return h[row_tok], blk_info, pos


def _moe(route, h, wg, wu, wd):
    rows, blk_info, pos = _moe_dispatch(route, h)
    y = _expert_ffn(rows, blk_info, wg, wu, wd)
    return y[pos[:, 0]], y[pos[:, 1]], route


def _final_kernel(x_ref, y0_ref, y1_ref, route_ref, g_ref, o_ref):
    x = _moe_combine(x_ref[...], y0_ref, y1_ref, route_ref)
    ms = jnp.mean(x * x, axis=-1, keepdims=True)
    o_ref[...] = x * lax.rsqrt(ms + NORM_EPS) * g_ref[...]


def _final_norm(x, ys, gain):
    T = x.shape[0]
    tm = ROW_TILE
    row = lambda i: (i, 0)
    return pl.pallas_call(
        _final_kernel, grid=(T // tm,),
        in_specs=[pl.BlockSpec((tm, D_MODEL), row)] * 3 + [pl.BlockSpec((tm, LANES), row), pl.BlockSpec((1, D_MODEL), lambda i: (0, 0))],
        out_specs=pl.BlockSpec((tm, D_MODEL), row), out_shape=jax.ShapeDtypeStruct((T, D_MODEL), F32),
        compiler_params=_params(1), name="final_norm")(x, *ys, gain.reshape(1, D_MODEL))


def _rope_tables(positions):
    inv_freq = ROPE_THETA ** (-jnp.arange(0, HEAD_DIM, 2, dtype=F32) / HEAD_DIM)
    ang = positions.astype(F32).reshape(-1, 1) * inv_freq
    cos, sin = jnp.cos(ang), jnp.sin(ang)
    reps = LANES // HEAD_DIM
    return jnp.tile(jnp.concatenate([cos, cos], axis=1), (1, reps)), jnp.tile(jnp.concatenate([-sin, sin], axis=1), (1, reps))


def _heads(t, B, S):
    return t.reshape(B, S, -1, HEAD_DIM).transpose(0, 2, 1, 3)


def _heads_t(t, B, S):
    return t.reshape(B, S, -1, HEAD_DIM).transpose(0, 2, 3, 1)


def _tokens(o):
    B, H, S, d = o.shape
    return o.transpose(0, 2, 1, 3).reshape(B * S, H * d)


def _keys_aug(k, block):
    S = k.shape[2]
    lane = jnp.arange(LANES - HEAD_DIM)[None, :]
    extra = ((jnp.arange(S)[:, None] % KEY_TILE) // block == lane) | (lane == AUG_ROWS)
    return jnp.concatenate([k, jnp.broadcast_to(extra.astype(k.dtype), k.shape[:2] + extra.shape)], axis=-1)


def _values_aug(vT):
    B, H, _, S = vT.shape
    ones = jnp.ones((B, H, 1, S), vT.dtype)
    return jnp.concatenate([vT, ones, jnp.zeros((B, H, V_AUG_ROWS - HEAD_DIM - 1, S), vT.dtype)], axis=2)


def _router_weights(router_group, router_expert):
    pad = jnp.zeros((D_MODEL, LANES - N_GROUPS - N_EXPERTS), F32)
    return jnp.concatenate([router_group.astype(F32), router_expert.astype(F32), pad], axis=1)


def _pad_cols(w, n):
    return jnp.concatenate([w, jnp.zeros((w.shape[0], n - w.shape[1]), w.dtype)], axis=1)


def _mixer_ab(proj, sinks, B, S):
    hd = lambda a, b: _heads(proj[:, a:b], B, S)
    hdt = lambda a, b: _heads_t(proj[:, a:b], B, S)
    oa = _swa_attention(proj, 0, hd(512, 640), hdt(640, 768), sinks)
    ob = _stick_attention(hd(768, 1280), hdt(1280, 1792), hdt(1792, 2304))
    return jnp.concatenate([oa, _tokens(ob)], axis=1)


def _mixer_cd(proj, kmean, B, S, k_pe, k_w1, k_w2, v_pe, v_w1, v_w2):
    hd = lambda a, b: _heads(proj[:, a:b], B, S)
    hdt = lambda a, b: _heads_t(proj[:, a:b], B, S)
    nblk = S // C_BLOCK
    km = kmean.reshape(B, nblk, C_KV_HEADS, HEAD_DIM).transpose(0, 2, 1, 3).astype(BF16)
    oc = _moba_attention(proj, 0, _keys_aug(hd(512, 640), C_BLOCK), _values_aug(hdt(640, 768)), km)
    k_cmp = _compress(hd(1280, 1408), k_pe, k_w1, k_w2)
    v_cmp = _compress(hd(1408, 1536), v_pe, v_w1, v_w2)
    nc = (S - D_CMP_LEN) // D_CMP_STRIDE + 1
    ncp = S // D_CMP_STRIDE
    nsel = S // D_SEL_LEN
    c_start = jnp.arange(ncp) * D_CMP_STRIDE
    b_start = jnp.arange(nsel) * D_SEL_LEN
    overlap = ((c_start[:, None] <= b_start[None, :] + D_SEL_LEN - 1) & (c_start[:, None] + D_CMP_LEN - 1 >= b_start[None, :])
               & (jnp.arange(ncp)[:, None] < nc)).astype(BF16)
    qd_col = 768
    o_cmp, biasT = _cmp_select(proj, qd_col, k_cmp, v_cmp.transpose(0, 1, 3, 2), overlap.T, nc, S)
    o_sel = _sel_attention(proj, qd_col, _keys_aug(hd(1536, 1664), D_SEL_LEN), _values_aug(hdt(1664, 1792)), biasT)
    o_win = _win_attention(proj, qd_col, hd(1792, 1920), hdt(1920, 2048))
    return oc, o_cmp, o_sel, o_win


def kernel(x, positions, ln_mix_0, w_in_0, sinks_0, w_out_0, ln_ffn_0, router_group_0, router_expert_0, expert_gate_0, expert_up_0, expert_down_0, ln_mix_1, w_in_1, cmp_k_pe_1, cmp_k_w1_1, cmp_k_w2_1, cmp_v_pe_1, cmp_v_w1_1, cmp_v_w2_1, w_out_1, ln_ffn_1, router_group_1, router_expert_1, expert_gate_1, expert_up_1, expert_down_1, ln_final):
    B, S, _ = x.shape
    T = B * S
    assert S % KEY_TILE == 0 and T % ROW_TILE == 0
    cos_t, sin_t = _rope_tables(positions)
    xf = x.reshape(T, D_MODEL)

    proj = _in_proj(xf, None, ln_mix_0, w_in_0.astype(BF16), cos_t, sin_t, rope_chunks=range(0, 5))[0]
    o_ab = _mixer_ab(proj, sinks_0, B, S)
    x1, h1, route1 = _out_proj([o_ab], None, xf, w_out_0.astype(BF16), ln_ffn_0, _router_weights(router_group_0, router_expert_0))
    ys = _moe(route1, h1, expert_gate_0, expert_up_0, expert_down_0)

    n_cols = 17 * LANES
    x2, proj, kmean, gd = _in_proj(x1, ys, ln_mix_1, _pad_cols(w_in_1.astype(BF16), n_cols), cos_t, sin_t,
                                   rope_chunks=(0, 1, 2, 3, 4, 6, 7, 8, 9, 10, 12, 14), kmean_chunk=4, gate_chunk=16)
    parts = _mixer_cd(proj, kmean, B, S, cmp_k_pe_1, cmp_k_w1_1, cmp_k_w2_1, cmp_v_pe_1, cmp_v_w1_1, cmp_v_w2_1)
    x3, h3, route3 = _out_proj(parts, gd, x2, w_out_1.astype(BF16), ln_ffn_1, _router_weights(router_group_1, router_expert_1))
    ys = _moe(route3, h3, expert_gate_1, expert_up_1, expert_down_1)
    return _final_norm(x3, ys, ln_final).reshape(B, S, D_MODEL)
```

```python
import functools

import jax
import jax.numpy as jnp
from jax import lax
from jax.experimental import pallas as pl
from jax.experimental.pallas import tpu as pltpu

D_MODEL = 1024
HEAD_DIM = 64
HALF = HEAD_DIM // 2
ROPE_THETA = 10000.0
NORM_EPS = 1e-6
Q_BLOCK = 128
SCALE = HEAD_DIM ** -0.5

A_HEADS, A_KV_HEADS, A_WINDOW = 8, 2, 128
B_HEADS = 8
C_HEADS, C_KV_HEADS, C_BLOCK, C_TOPK = 8, 2, 256, 3
D_HEADS, D_KV_HEADS = 8, 2
D_CMP_LEN, D_CMP_STRIDE, D_CMP_HIDDEN = 32, 16, 256
D_SEL_LEN, D_SEL_TOPK, D_WINDOW = 64, 16, 512
N_GROUPS, EXPERTS_PER_GROUP, MOE_TOPK, EXPERT_HIDDEN = 4, 16, 2, 512
N_EXPERTS = N_GROUPS * EXPERTS_PER_GROUP
GROUP = 4

LANES = 128
ROW_TILE = 512
MOE_ROWS = 256
KEY_TILE = 512
FLASH_MIN_SUM = 1e-25
V_AUG_ROWS = 80
AUG_ROWS = 8
MASK_BIG = 32768.0
SB_EXIT = -104.0
STICK_HEADS = 8
VMEM_LIMIT = 56 * 1024 * 1024

F32 = jnp.float32
BF16 = jnp.bfloat16
NEG_INF = float("-inf")


def _iota(shape, dim):
    return lax.broadcasted_iota(jnp.int32, shape, dim)


def _dot(a, b):
    return jnp.dot(a, b, preferred_element_type=F32)


def _params(n_grid):
    return pltpu.CompilerParams(dimension_semantics=("arbitrary",) * n_grid, vmem_limit_bytes=VMEM_LIMIT)


GROUP_COLS = GROUP * HEAD_DIM


def _group_qT(q_ref):
    t = (q_ref[...].astype(F32) * SCALE).T
    return jnp.concatenate([t[g * HEAD_DIM:(g + 1) * HEAD_DIM] for g in range(GROUP)], axis=1).astype(BF16)


def _group_rows(oT):
    return jnp.concatenate([oT[:, g * Q_BLOCK:(g + 1) * Q_BLOCK] for g in range(GROUP)], axis=0).T


def _q_spec(S, col0):
    nq = S // Q_BLOCK
    return pl.BlockSpec((Q_BLOCK, GROUP_COLS), lambda b, h, n: (b * nq + n, col0 // GROUP_COLS + h))


def _o_spec(S):
    nq = S // Q_BLOCK
    return pl.BlockSpec((Q_BLOCK, GROUP_COLS), lambda b, h, n: (b * nq + n, h))


def _moe_combine(x, y0_ref, y1_ref, route_ref):
    return x + (y0_ref[...] * route_ref[:, MOE_TOPK:MOE_TOPK + 1] + y1_ref[...] * route_ref[:, MOE_TOPK + 1:MOE_TOPK + 2])


def _emit_chunk(ch, actions, outs):
    tm = ch.shape[0]
    chT = None
    for kind, oi, arg in actions:
        o = outs[oi]
        if kind == "tok":
            o[:, arg * LANES:(arg + 1) * LANES] = ch.astype(BF16)
        elif kind == "f32":
            o[...] = ch
        elif kind == "kmean":
            o[...] = jnp.mean(ch.reshape(tm // C_BLOCK, C_BLOCK, LANES), axis=1).reshape(tm // C_BLOCK, 1, LANES)
        elif kind == "heads":
            for hh in range(2):
                o[0, arg + hh] = ch[:, hh * HEAD_DIM:(hh + 1) * HEAD_DIM].astype(BF16)
        elif kind == "keys_aug":
            lane = _iota((tm, LANES), 1)
            block_in_tile = _iota((tm, LANES), 0) >> (arg.bit_length() - 1)
            aug = jnp.where((lane - HEAD_DIM == block_in_tile) | (lane == HEAD_DIM + AUG_ROWS), 1.0, 0.0)
            for hh in range(2):
                keys = ch if hh == 0 else pltpu.roll(ch, HEAD_DIM, 1)
                o[0, hh] = jnp.where(lane < HEAD_DIM, keys, aug).astype(BF16)
        else:
            chT = ch.astype(BF16).astype(F32).T if chT is None else chT
            for hh in range(2):
                vT = chT[hh * HEAD_DIM:(hh + 1) * HEAD_DIM]
                if kind == "headsT":
                    o[0, arg + hh] = vT.astype(BF16)
                else:
                    assert kind == "valsT_aug"
                    tail = jnp.where(_iota((V_AUG_ROWS - HEAD_DIM, tm), 0) == 0, 1.0, 0.0)
                    o[0, hh] = jnp.concatenate([vT, tail], axis=0).astype(BF16)


def _in_proj_kernel(plan, combine, *refs):
    refs = list(refs)
    x_ref = refs.pop(0)
    if combine:
        y0_ref, y1_ref, route_ref = refs.pop(0), refs.pop(0), refs.pop(0)
    g_ref, w_ref, cos_ref, sin_ref = refs[:4]
    outs = refs[4:]
    x = x_ref[...]
    if combine:
        x = _moe_combine(x, y0_ref, y1_ref, route_ref)
        xo_ref = outs.pop(0)
        xo_ref[...] = x
    ms = jnp.mean(x * x, axis=-1, keepdims=True)
    h = (x * lax.rsqrt(ms + NORM_EPS) * g_ref[...]).astype(BF16)
    tm = x.shape[0]
    cos = cos_ref[...]
    sin = sin_ref[...]
    first_half = (_iota((tm, LANES), 1) & (HEAD_DIM - 1)) < HALF
    for c, (rope, actions) in enumerate(plan):
        ch = _dot(h, w_ref[:, c * LANES:(c + 1) * LANES])
        if rope:
            partner = jnp.where(first_half, pltpu.roll(ch, LANES - HALF, 1), pltpu.roll(ch, HALF, 1))
            ch = ch * cos + partner * sin
        _emit_chunk(ch, actions, outs)


def _in_proj(x, ys, gain, w, cos_t, sin_t, plan, out_kinds, B, S):
    T = x.shape[0]
    n_cols = w.shape[1]
    tm = ROW_TILE
    assert tm == KEY_TILE and S % tm == 0 and len(plan) * LANES == n_cols
    tpb = S // tm
    combine = ys is not None
    row = lambda i: (i, 0)
    fixed = lambda i: (0, 0)
    by_seq = lambda i: (i // tpb, 0, i % tpb, 0)
    by_seq_t = lambda i: (i // tpb, 0, 0, i % tpb)
    in_specs = [pl.BlockSpec((tm, D_MODEL), row)]
    args = [x]
    if combine:
        in_specs += [pl.BlockSpec((tm, D_MODEL), row)] * 2 + [pl.BlockSpec((tm, LANES), row)]
        args += list(ys)
    in_specs += [pl.BlockSpec((1, D_MODEL), fixed), pl.BlockSpec((D_MODEL, n_cols), fixed),
                 pl.BlockSpec((tm, LANES), row), pl.BlockSpec((tm, LANES), row)]
    args += [gain.reshape(1, D_MODEL), w, cos_t, sin_t]
    out_shape, out_specs = [], []
    if combine:
        out_shape.append(jax.ShapeDtypeStruct((T, D_MODEL), F32))
        out_specs.append(pl.BlockSpec((tm, D_MODEL), row))
    for kind, size in out_kinds:
        if kind == "tok":
            shape, dtype, spec = (T, size * LANES), BF16, pl.BlockSpec((tm, size * LANES), row)
        elif kind == "f32":
            shape, dtype, spec = (T, LANES), F32, pl.BlockSpec((tm, LANES), row)
        elif kind == "kmean":
            shape, dtype, spec = (T // C_BLOCK, 1, LANES), F32, pl.BlockSpec((tm // C_BLOCK, 1, LANES), lambda i: (i, 0, 0))
        elif kind == "heads":
            shape, dtype, spec = (B, size, S, HEAD_DIM), BF16, pl.BlockSpec((1, size, tm, HEAD_DIM), by_seq)
        elif kind == "keys_aug":
            shape, dtype, spec = (B, size, S, LANES), BF16, pl.BlockSpec((1, size, tm, LANES), by_seq)
        elif kind == "headsT":
            shape, dtype, spec = (B, size, HEAD_DIM, S), BF16, pl.BlockSpec((1, size, HEAD_DIM, tm), by_seq_t)
        else:
            assert kind == "valsT_aug"
            shape, dtype, spec = (B, size, V_AUG_ROWS, S), BF16, pl.BlockSpec((1, size, V_AUG_ROWS, tm), by_seq_t)
        out_shape.append(jax.ShapeDtypeStruct(shape, dtype))
        out_specs.append(spec)
    kern = functools.partial(_in_proj_kernel, plan, combine)
    return pl.pallas_call(kern, grid=(T // tm,), in_specs=in_specs, out_specs=out_specs, out_shape=out_shape,
                          compiler_params=_params(1), name="in_proj")(*args)


def _swa_kernel(q_ref, kp_ref, ko_ref, vp_ref, vo_ref, sink_ref, o_ref):
    n = pl.program_id(2)
    qs = _group_qT(q_ref)
    R = qs.shape[1]
    k = jnp.concatenate([kp_ref[0, 0], ko_ref[0, 0]], axis=0)
    vT = jnp.concatenate([vp_ref[0, 0], vo_ref[0, 0]], axis=1)
    s = _dot(k, qs)
    qpos = _iota((2 * Q_BLOCK, R), 1) & (Q_BLOCK - 1)
    ki = _iota((2 * Q_BLOCK, R), 0)
    d = qpos + Q_BLOCK - ki
    ok = (d >= 0) & (d < A_WINDOW) & ((ki >= Q_BLOCK) | (n > 0))
    s = jnp.where(ok, s, NEG_INF)
    sink = sink_ref[0]
    m = jnp.maximum(jnp.max(s, axis=0, keepdims=True), sink)
    p = jnp.exp(s - m)
    den = jnp.sum(p, axis=0, keepdims=True) + jnp.exp(sink - m)
    o_ref[...] = _group_rows(_dot(vT, p.astype(BF16)) * (1.0 / den)).astype(o_ref.dtype)


def _swa_attention(proj, q_col, k, vT, sinks):
    B, Hkv, S, _ = k.shape
    nb = S // Q_BLOCK
    R = GROUP * Q_BLOCK
    sink_row = jnp.repeat(sinks.astype(F32).reshape(Hkv, GROUP), Q_BLOCK, axis=1).reshape(Hkv, 1, R)
    prev = lambda n: jnp.maximum(n - 1, 0)
    in_specs = [
        _q_spec(S, q_col),
        pl.BlockSpec((1, 1, Q_BLOCK, HEAD_DIM), lambda b, h, n: (b, h, prev(n), 0)),
        pl.BlockSpec((1, 1, Q_BLOCK, HEAD_DIM), lambda b, h, n: (b, h, n, 0)),
        pl.BlockSpec((1, 1, HEAD_DIM, Q_BLOCK), lambda b, h, n: (b, h, 0, prev(n))),
        pl.BlockSpec((1, 1, HEAD_DIM, Q_BLOCK), lambda b, h, n: (b, h, 0, n)),
        pl.BlockSpec((1, 1, R), lambda b, h, n: (h, 0, 0)),
    ]
    return pl.pallas_call(
        _swa_kernel, grid=(B, Hkv, nb), in_specs=in_specs,
        out_specs=_o_spec(S), out_shape=jax.ShapeDtypeStruct((B * S, Hkv * GROUP_COLS), BF16),
        compiler_params=_params(3), name="swa_attention")(proj, k, k, vT, vT, sink_row)


def _stick_kernel(q_ref, kT_ref, vT_ref, o_ref):
    n = pl.program_id(2)
    tq = tk = Q_BLOCK
    heads = q_ref.shape[1]
    row = _iota((tq, tk), 0)
    col = _iota((tk, tk), 1)
    upper = jnp.where(_iota((tk, tk), 0) > col, 1.0, 0.0).astype(BF16)
    tpos = n * tq + row

    def body(carry):
        kb, _, cs, accs = carry
        start = pl.multiple_of(kb * tk, tk)
        past = (start + col) < tpos
        hs = range(heads)
        zs = [_dot(q_ref[0, h], kT_ref[0, h, :, pl.ds(start, tk)]) * SCALE for h in hs]
        sps = [jnp.maximum(z, 0.0) + jnp.log1p(jnp.exp(-jnp.abs(z))) for z in zs]
        stays = [jnp.where(past, -sp, 0.0) for sp in sps]
        his = [st.astype(BF16) for st in stays]
        los = [(st - hi.astype(F32)).astype(BF16) for st, hi in zip(stays, his)]
        betweens = [_dot(hi, upper) + _dot(lo, upper) for hi, lo in zip(his, los)]
        ws = [jnp.where(past, jnp.exp(zs[h] - sps[h] + betweens[h] + cs[h]), 0.0).astype(BF16) for h in hs]
        new_accs = [accs[h] + lax.dot_general(ws[h], vT_ref[0, h, :, pl.ds(start, tk)], (((1,), (1,)), ((), ())),
                                              preferred_element_type=F32) for h in hs]
        new_cs = [cs[h] + jnp.sum(stays[h], axis=-1, keepdims=True) for h in hs]
        worst = functools.reduce(jnp.maximum, new_cs)
        return kb - 1, jnp.max(worst) > SB_EXIT, tuple(new_cs), tuple(new_accs)

    def cond(carry):
        kb, alive, _, _ = carry
        return (kb >= 0) & alive

    init = (n, jnp.array(True), (jnp.zeros((tq, 1), F32),) * heads, (jnp.zeros((tq, HEAD_DIM), F32),) * heads)
    _, _, _, accs = lax.while_loop(cond, body, init)
    for h in range(heads):
        o_ref[0, h] = accs[h].astype(o_ref.dtype)


def _stick_attention(q, kT, v):
    B, H, S, _ = q.shape
    nq = S // Q_BLOCK
    hb = STICK_HEADS
    resident = dict(pipeline_mode=pl.Buffered(1))
    in_specs = [
        pl.BlockSpec((1, hb, Q_BLOCK, HEAD_DIM), lambda b, h, n: (b, h, n, 0)),
        pl.BlockSpec((1, hb, HEAD_DIM, S), lambda b, h, n: (b, h, 0, 0), **resident),
        pl.BlockSpec((1, hb, HEAD_DIM, S), lambda b, h, n: (b, h, 0, 0), **resident),
    ]
    return pl.pallas_call(
        _stick_kernel, grid=(B, H // hb, nq), in_specs=in_specs,
        out_specs=pl.BlockSpec((1, hb, Q_BLOCK, HEAD_DIM), lambda b, h, n: (b, h, n, 0)),
        out_shape=jax.ShapeDtypeStruct((B, H, S, HEAD_DIM), BF16),
        compiler_params=_params(3), name="stick_attention")(q, kT, v)


def _max_key_sqnorm(ka_ref, out_ref):
    S = ka_ref.shape[2]
    tk = KEY_TILE
    is_key_lane = _iota((tk, LANES), 1) < HEAD_DIM
    ones = jnp.ones((LANES, LANES), BF16)

    def body(i, mx):
        k = jnp.where(is_key_lane, ka_ref[0, 0, pl.ds(pl.multiple_of(i * tk, tk), tk), :].astype(F32), 0.0)
        sq = _dot((k * k).astype(BF16), ones)
        return jnp.maximum(mx, jnp.max(sq, axis=0, keepdims=True))

    mx = lax.fori_loop(0, S // tk, body, jnp.zeros((1, LANES), F32))
    out_ref[...] = jnp.broadcast_to(mx, out_ref.shape)


def _flash_scratch(R):
    return [pltpu.VMEM((AUG_ROWS, LANES), F32), pltpu.VMEM((2, KEY_TILE, R), BF16), pltpu.VMEM((V_AUG_ROWS, R), F32)]


def _masked_flash_t(n, qT, bias_rows, ka_ref, vTa_ref, ksq_ref, p_scr, acc_scr):
    R = qT.shape[1]
    tk = KEY_TILE
    diag = (n * Q_BLOCK) // tk
    zpad = jnp.zeros((LANES - HEAD_DIM - 2 * AUG_ROWS, R), F32)
    causal = diag * tk + _iota((tk, R), 0) <= n * Q_BLOCK + (_iota((tk, R), 1) & (Q_BLOCK - 1))

    def scores(kt, ref_rows):
        st = pl.multiple_of(kt * tk, tk)
        low = jnp.concatenate([bias_rows(kt), ref_rows, zpad], axis=0).astype(BF16)
        return _dot(ka_ref[0, 0, pl.ds(st, tk), :], jnp.concatenate([qT, low], axis=0))

    def values(kt):
        return vTa_ref[0, 0, :, pl.ds(pl.multiple_of(kt * tk, tk), tk)]

    qsq = jnp.sum(jnp.square(qT.astype(F32)), axis=0, keepdims=True)
    ksq = jnp.concatenate([ksq_ref[0:1, :]] * (R // LANES), axis=1)
    ref = jnp.sqrt(qsq * ksq) * 1.05
    ref_rows = jnp.where(_iota((AUG_ROWS, R), 0) == 0, -ref, 0.0)

    def fast_body(kt, carry):
        slot = kt & 1
        s = scores(kt, ref_rows)
        acc_scr[...] += _dot(values(jnp.maximum(kt - 1, 0)), p_scr[1 - slot])
        p_scr[slot] = jnp.exp(s).astype(BF16)
        return carry

    p_scr[1] = jnp.zeros((tk, R), BF16)
    acc_scr[...] = jnp.zeros(acc_scr.shape, F32)
    lax.fori_loop(0, diag, fast_body, 0)
    acc = acc_scr[...] + _dot(values(jnp.maximum(diag - 1, 0)), p_scr[1 - (diag & 1)])
    s = jnp.where(causal, scores(diag, ref_rows), -1e30)
    acc = acc + _dot(values(diag), jnp.exp(s).astype(BF16))

    def running_max_path(_):
        no_ref = jnp.zeros((AUG_ROWS, R), F32)

        def update(kt, s, m, acc):
            m_new = jnp.maximum(m, jnp.max(s, axis=0, keepdims=True))
            p = jnp.exp(s - m_new).astype(BF16)
            return m_new, jnp.exp(m - m_new) * acc + _dot(values(kt), p)

        init = (jnp.full((1, R), -1e30, F32), jnp.zeros((vTa_ref.shape[2], R), F32))
        m, acc = lax.fori_loop(0, diag, lambda kt, c: update(kt, scores(kt, no_ref), *c), init)
        return update(diag, jnp.where(causal, scores(diag, no_ref), -1e30), m, acc)[1]

    healthy = jnp.min(acc[HEAD_DIM:HEAD_DIM + 1, :]) >= FLASH_MIN_SUM
    acc = lax.cond(healthy, lambda _: acc, running_max_path, 0)
    return acc[0:HEAD_DIM] * (1.0 / acc[HEAD_DIM:HEAD_DIM + 1])


def _moba_kernel(q_ref, ka_ref, vTa_ref, km_ref, o_ref, bias_scr, ksq_scr, p_scr, acc_scr):
    n = pl.program_id(2)
    pl.when(n == 0)(lambda: _max_key_sqnorm(ka_ref, ksq_scr))
    nblk = km_ref.shape[2]
    qs = _group_qT(q_ref)
    R = qs.shape[1]
    own = (n * Q_BLOCK) // C_BLOCK
    gate = _dot(km_ref[0, 0], qs)
    blk = _iota((nblk, R), 0)
    gate = jnp.where(blk < own, gate, NEG_INF)
    bias = jnp.where(blk == own, 0.0, -MASK_BIG)
    for _ in range(C_TOPK):
        mx = jnp.max(gate, axis=0, keepdims=True)
        idx = jnp.min(jnp.where(gate == mx, blk, nblk), axis=0, keepdims=True)
        hit = blk == idx
        bias = jnp.where(hit & (mx > NEG_INF), 0.0, bias)
        gate = jnp.where(hit, NEG_INF, gate)
    per = KEY_TILE // C_BLOCK
    rows = bias_scr.shape[0]
    r = _iota((rows, nblk), 0)
    spread = jnp.where(((r & (AUG_ROWS - 1)) < per) & (_iota((rows, nblk), 1) == (r >> 3) * per + (r & (AUG_ROWS - 1))), 1.0, 0.0)
    bias_scr[...] = _dot(spread.astype(BF16), bias.astype(BF16))
    oT = _masked_flash_t(n, qs, lambda kt: bias_scr[pl.ds(pl.multiple_of(kt * AUG_ROWS, AUG_ROWS), AUG_ROWS), :],
                         ka_ref, vTa_ref, ksq_scr, p_scr, acc_scr)
    o_ref[...] = _group_rows(oT).astype(o_ref.dtype)


def _moba_attention(proj, q_col, ka, vTa, km):
    B, Hkv, S, _ = ka.shape
    nq = S // Q_BLOCK
    R = GROUP * Q_BLOCK
    nblk = km.shape[2]
    nkt = S // KEY_TILE
    in_specs = [
        _q_spec(S, q_col),
        pl.BlockSpec((1, 1, S, LANES), lambda b, h, n: (b, h, 0, 0)),
        pl.BlockSpec((1, 1, V_AUG_ROWS, S), lambda b, h, n: (b, h, 0, 0)),
        pl.BlockSpec((1, 1, nblk, HEAD_DIM), lambda b, h, n: (b, h, 0, 0)),
    ]
    return pl.pallas_call(
        _moba_kernel, grid=(B, Hkv, nq), in_specs=in_specs,
        out_specs=_o_spec(S), out_shape=jax.ShapeDtypeStruct((B * S, Hkv * GROUP_COLS), BF16),
        scratch_shapes=[pltpu.VMEM((nkt * AUG_ROWS, R), F32)] + _flash_scratch(R),
        compiler_params=_params(3), name="moba_attention")(proj, ka, vTa, km)


def _compress_kernel(u_ref, us_ref, pe_ref, w1_ref, w2_ref, o_ref):
    a = (u_ref[0, 0].astype(F32) + pe_ref[0:1, :]).astype(BF16)
    b = (us_ref[0, 0].astype(F32) + pe_ref[1:2, :]).astype(BF16)
    pre = _dot(a, w1_ref[0]) + _dot(b, w1_ref[1])
    hid = jax.nn.gelu(pre)
    o_ref[0, 0] = _dot(hid.astype(BF16), w2_ref[...]).astype(o_ref.dtype)


def _compress(t, pe, w1, w2):
    B, H, S, _ = t.shape
    nrow = S // D_CMP_STRIDE
    width = D_CMP_STRIDE * HEAD_DIM
    u = t.reshape(B, H, nrow, width)
    us = jnp.concatenate([u[:, :, 1:], jnp.zeros((B, H, 1, width), u.dtype)], axis=2)
    blk = lambda b, h: (b, h, 0, 0)
    in_specs = [
        pl.BlockSpec((1, 1, nrow, width), blk), pl.BlockSpec((1, 1, nrow, width), blk),
        pl.BlockSpec((2, width), lambda b, h: (0, 0)),
        pl.BlockSpec((2, width, D_CMP_HIDDEN), lambda b, h: (0, 0, 0)),
        pl.BlockSpec((D_CMP_HIDDEN, HEAD_DIM), lambda b, h: (0, 0)),
    ]
    return pl.pallas_call(
        _compress_kernel, grid=(B, H), in_specs=in_specs,
        out_specs=pl.BlockSpec((1, 1, nrow, HEAD_DIM), blk),
        out_shape=jax.ShapeDtypeStruct((B, H, nrow, HEAD_DIM), BF16),
        compiler_params=_params(2), name="nsa_compress")(
            u, us, pe.astype(F32).reshape(2, width), w1.astype(BF16).reshape(2, width, D_CMP_HIDDEN), w2.astype(BF16))


def _cmp_select_kernel(nc, q_ref, kc_ref, vcT_ref, ovT_ref, oc_ref, bias_ref):
    n = pl.program_id(2)
    ncp = kc_ref.shape[2]
    nsel = ovT_ref.shape[0]
    qs = _group_qT(q_ref)
    R = qs.shape[1]
    s = _dot(kc_ref[0, 0], qs)
    tpos = n * Q_BLOCK + (_iota((ncp, R), 1) & (Q_BLOCK - 1))
    c = _iota((ncp, R), 0)
    ok = (c * D_CMP_STRIDE + (D_CMP_LEN - 1) <= tpos) & (c < nc)
    s = jnp.where(ok, s, NEG_INF)
    m = jnp.max(s, axis=0, keepdims=True)
    m = jnp.where(m > NEG_INF, m, 0.0)
    e = jnp.exp(s - m)
    den = jnp.sum(e, axis=0, keepdims=True)
    p = (e * (1.0 / jnp.where(den > 0, den, 1.0))).astype(BF16)
    oc_ref[...] = _group_rows(_dot(vcT_ref[0, 0], p))
    imp_heads = _dot(ovT_ref[...], p)
    imp = imp_heads[:, 0:Q_BLOCK]
    for g in range(1, GROUP):
        imp = imp + imp_heads[:, g * Q_BLOCK:(g + 1) * Q_BLOCK]
    t = n * Q_BLOCK + _iota((nsel, Q_BLOCK), 1)
    j = _iota((nsel, Q_BLOCK), 0)
    cur = t >> 6
    allowed = j * D_SEL_LEN <= t
    forced = (j == 0) | (j == cur) | (j == cur - 1)
    imp = jnp.where(allowed, imp, NEG_INF)
    imp = jnp.where(allowed & forced, float("inf"), imp)
    bias = jnp.full((nsel, Q_BLOCK), -MASK_BIG, F32)
    for _ in range(D_SEL_TOPK):
        mx = jnp.max(imp, axis=0, keepdims=True)
        idx = jnp.min(jnp.where(imp == mx, j, nsel), axis=0, keepdims=True)
        hit = j == idx
        bias = jnp.where(hit & (mx > NEG_INF), 0.0, bias)
        imp = jnp.where(hit, NEG_INF, imp)
    bias_ref[0, 0, 0] = bias


def _cmp_select(proj, q_col, kc, vcT, overlapT, nc, S):
    B, Hkv, ncp, _ = kc.shape
    nq = S // Q_BLOCK
    nsel = overlapT.shape[0]
    in_specs = [
        _q_spec(S, q_col),
        pl.BlockSpec((1, 1, ncp, HEAD_DIM), lambda b, h, n: (b, h, 0, 0)),
        pl.BlockSpec((1, 1, HEAD_DIM, ncp), lambda b, h, n: (b, h, 0, 0)),
        pl.BlockSpec((nsel, ncp), lambda b, h, n: (0, 0)),
    ]
    out_specs = [_o_spec(S), pl.BlockSpec((1, 1, 1, nsel, Q_BLOCK), lambda b, h, n: (b, h, n, 0, 0))]
    out_shape = [jax.ShapeDtypeStruct((B * S, Hkv * GROUP_COLS), F32), jax.ShapeDtypeStruct((B, Hkv, nq, nsel, Q_BLOCK), F32)]
    return pl.pallas_call(
        functools.partial(_cmp_select_kernel, nc), grid=(B, Hkv, nq), in_specs=in_specs, out_specs=out_specs,
        out_shape=out_shape, compiler_params=_params(3), name="nsa_cmp_select")(proj, kc, vcT, overlapT)


def _sel_kernel(q_ref, ka_ref, vTa_ref, bias_ref, o_ref, ksq_scr, p_scr, acc_scr):
    n = pl.program_id(2)
    pl.when(n == 0)(lambda: _max_key_sqnorm(ka_ref, ksq_scr))
    qs = _group_qT(q_ref)

    def bias_rows(kt):
        b = bias_ref[0, 0, 0, pl.ds(pl.multiple_of(kt * AUG_ROWS, AUG_ROWS), AUG_ROWS), :]
        return jnp.concatenate([b] * GROUP, axis=1)

    o_ref[...] = _group_rows(_masked_flash_t(n, qs, bias_rows, ka_ref, vTa_ref, ksq_scr, p_scr, acc_scr))


def _sel_attention(proj, q_col, ka, vTa, biasT):
    B, Hkv, S, _ = ka.shape
    nq = S // Q_BLOCK
    nsel = biasT.shape[3]
    assert KEY_TILE // D_SEL_LEN == AUG_ROWS
    in_specs = [
        _q_spec(S, q_col),
        pl.BlockSpec((1, 1, S, LANES), lambda b, h, n: (b, h, 0, 0)),
        pl.BlockSpec((1, 1, V_AUG_ROWS, S), lambda b, h, n: (b, h, 0, 0)),
        pl.BlockSpec((1, 1, 1, nsel, Q_BLOCK), lambda b, h, n: (b, h, n, 0, 0)),
    ]
    return pl.pallas_call(
        _sel_kernel, grid=(B, Hkv, nq), in_specs=in_specs,
        out_specs=_o_spec(S), out_shape=jax.ShapeDtypeStruct((B * S, Hkv * GROUP_COLS), F32),
        scratch_shapes=_flash_scratch(GROUP * Q_BLOCK),
        compiler_params=_params(3), name="nsa_selected")(proj, ka, vTa, biasT)


def _win_kernel(span, q_ref, k_ref, vT_ref, o_ref):
    n = pl.program_id(2)
    qs = _group_qT(q_ref)
    R = qs.shape[1]
    start = pl.multiple_of(jnp.maximum(n * Q_BLOCK + Q_BLOCK - span, 0), Q_BLOCK)
    s = _dot(k_ref[0, 0, pl.ds(start, span), :], qs)
    tpos = n * Q_BLOCK + (_iota((span, R), 1) & (Q_BLOCK - 1))
    d = tpos - (start + _iota((span, R), 0))
    s = jnp.where((d >= 0) & (d < D_WINDOW), s, NEG_INF)
    m = jnp.max(s, axis=0, keepdims=True)
    p = jnp.exp(s - m)
    l = jnp.sum(p, axis=0, keepdims=True)
    o_ref[...] = _group_rows(_dot(vT_ref[0, 0, :, pl.ds(start, span)], p.astype(BF16)) * (1.0 / l))


def _win_attention(proj, q_col, k, vT):
    B, Hkv, S, _ = k.shape
    nq = S // Q_BLOCK
    span = min(D_WINDOW + Q_BLOCK, S)
    in_specs = [
        _q_spec(S, q_col),
        pl.BlockSpec((1, 1, S, HEAD_DIM), lambda b, h, n: (b, h, 0, 0)),
        pl.BlockSpec((1, 1, HEAD_DIM, S), lambda b, h, n: (b, h, 0, 0)),
    ]
    return pl.pallas_call(
        functools.partial(_win_kernel, span), grid=(B, Hkv, nq), in_specs=in_specs,
        out_specs=_o_spec(S), out_shape=jax.ShapeDtypeStruct((B * S, Hkv * GROUP_COLS), F32),
        compiler_params=_params(3), name="nsa_window")(proj, k, vT)


def _route(logits):
    tm = logits.shape[0]
    lane = _iota((tm, LANES), 1)
    gl = jnp.where(lane < N_GROUPS, logits, NEG_INF)
    gmax = jnp.max(gl, axis=-1, keepdims=True)
    gidx = jnp.min(jnp.where(gl == gmax, lane, LANES), axis=-1, keepdims=True)
    g_prob = 1.0 / jnp.sum(jnp.exp(gl - gmax), axis=-1, keepdims=True)
    elane = lane - N_GROUPS
    in_group = (elane >= 0) & (elane < N_EXPERTS) & ((elane >> 4) == gidx)
    el = jnp.where(in_group, logits, NEG_INF)
    ee = jnp.exp(el - jnp.max(el, axis=-1, keepdims=True))
    ep = jnp.where(in_group, ee / jnp.sum(ee, axis=-1, keepdims=True), -1.0)
    p1 = jnp.max(ep, axis=-1, keepdims=True)
    i1 = jnp.min(jnp.where(ep == p1, lane, LANES), axis=-1, keepdims=True)
    ep2 = jnp.where(lane == i1, -1.0, ep)
    p2 = jnp.max(ep2, axis=-1, keepdims=True)
    i2 = jnp.min(jnp.where(ep2 == p2, lane, LANES), axis=-1, keepdims=True)
    den = p1 + p2
    vals = [(i1 - N_GROUPS).astype(F32), (i2 - N_GROUPS).astype(F32), g_prob * p1 / den, g_prob * p2 / den]
    out = jnp.zeros((tm, LANES), F32)
    for k, val in enumerate(vals):
        out = jnp.where(lane == k, val, out)
    return out


def _out_tail(x_new, gain_ref, wr_ref, xo_ref, h_ref, route_ref):
    xo_ref[...] = x_new
    ms = jnp.mean(x_new * x_new, axis=-1, keepdims=True)
    h = x_new * lax.rsqrt(ms + NORM_EPS) * gain_ref[...]
    h_ref[...] = h
    logits = jnp.dot(h, wr_ref[...], preferred_element_type=F32, precision=lax.Precision.HIGHEST)
    route_ref[...] = _route(logits)


def _out_proj_kernel(o_ref, x_ref, w_ref, gain_ref, wr_ref, xo_ref, h_ref, route_ref):
    x_new = x_ref[...] + _dot(o_ref[...], w_ref[...])
    _out_tail(x_new, gain_ref, wr_ref, xo_ref, h_ref, route_ref)


def _out_proj_nsa_kernel(oc_ref, b0_ref, b1_ref, b2_ref, gd_ref, x_ref, w_ref, gain_ref, wr_ref, xo_ref, h_ref, route_ref):
    half = D_HEADS * HEAD_DIM
    g = jax.nn.sigmoid(gd_ref[...])
    g_hi = g.astype(BF16)
    g_lo = (g - g_hi.astype(F32)).astype(BF16)
    src = _iota((LANES, half), 0)
    head3 = (_iota((LANES, half), 1) >> 6) * 3
    od = None
    for br, b_ref in enumerate((b0_ref, b1_ref, b2_ref)):
        spread = jnp.where(src == head3 + br, 1.0, 0.0).astype(BF16)
        term = (_dot(g_hi, spread) + _dot(g_lo, spread)) * b_ref[...]
        od = term if od is None else od + term
    x_new = x_ref[...] + _dot(oc_ref[...], w_ref[0:half, :]) + _dot(od.astype(BF16), w_ref[half:, :])
    _out_tail(x_new, gain_ref, wr_ref, xo_ref, h_ref, route_ref)


def _out_proj(o_parts, gd, x, w_out, gain, w_router):
    T = x.shape[0]
    tm = ROW_TILE
    row = lambda i: (i, 0)
    fixed = lambda i: (0, 0)
    if gd is None:
        kern = _out_proj_kernel
        args = [o_parts[0]]
        in_specs = [pl.BlockSpec((tm, D_MODEL), row)]
    else:
        kern = _out_proj_nsa_kernel
        args = list(o_parts) + [gd]
        in_specs = [pl.BlockSpec((tm, D_MODEL // 2), row)] * 4 + [pl.BlockSpec((tm, LANES), row)]
    args += [x, w_out, gain.reshape(1, D_MODEL), w_router]
    in_specs += [pl.BlockSpec((tm, D_MODEL), row), pl.BlockSpec((D_MODEL, D_MODEL), fixed),
                 pl.BlockSpec((1, D_MODEL), fixed), pl.BlockSpec((D_MODEL, LANES), fixed)]
    out_shape = [jax.ShapeDtypeStruct((T, D_MODEL), F32), jax.ShapeDtypeStruct((T, D_MODEL), F32),
                 jax.ShapeDtypeStruct((T, LANES), F32)]
    out_specs = [pl.BlockSpec((tm, D_MODEL), row), pl.BlockSpec((tm, D_MODEL), row), pl.BlockSpec((tm, LANES), row)]
    return pl.pallas_call(kern, grid=(T // tm,), in_specs=in_specs, out_specs=out_specs, out_shape=out_shape,
                          compiler_params=_params(1), name="out_proj_router")(*args)


def _expert_kernel(be_ref, rows_ref, wg_ref, wu_ref, wd_ref, y_ref, wg_s, wu_s, wd_s):
    i = pl.program_id(0)
    n_blk = pl.num_programs(0)

    @pl.when((i == 0) | (be_ref[i] != be_ref[jnp.maximum(i - 1, 0)]))
    def _():
        wg_s[...] = wg_ref[0].astype(BF16)
        wu_s[...] = wu_ref[0].astype(BF16)
        wd_s[...] = wd_ref[0].astype(BF16)

    @pl.when(i < be_ref[n_blk])
    def _():
        xb = rows_ref[...].astype(BF16)
        hid = jax.nn.silu(_dot(xb, wg_s[...])) * _dot(xb, wu_s[...])
        y_ref[...] = _dot(hid.astype(BF16), wd_s[...])

    @pl.when(i >= be_ref[n_blk])
    def _():
        y_ref[...] = jnp.zeros(y_ref.shape, F32)


def _expert_ffn(rows, blk_info, wg, wu, wd):
    n_rows = rows.shape[0]
    n_blk = n_rows // MOE_ROWS
    grid_spec = pltpu.PrefetchScalarGridSpec(
        num_scalar_prefetch=1, grid=(n_blk,),
        in_specs=[
            pl.BlockSpec((MOE_ROWS, D_MODEL), lambda i, be: (i, 0)),
            pl.BlockSpec((1, D_MODEL, EXPERT_HIDDEN), lambda i, be: (be[i], 0, 0)),
            pl.BlockSpec((1, D_MODEL, EXPERT_HIDDEN), lambda i, be: (be[i], 0, 0)),
            pl.BlockSpec((1, EXPERT_HIDDEN, D_MODEL), lambda i, be: (be[i], 0, 0)),
        ],
        out_specs=pl.BlockSpec((MOE_ROWS, D_MODEL), lambda i, be: (i, 0)),
        scratch_shapes=[pltpu.VMEM((D_MODEL, EXPERT_HIDDEN), BF16), pltpu.VMEM((D_MODEL, EXPERT_HIDDEN), BF16),
                        pltpu.VMEM((EXPERT_HIDDEN, D_MODEL), BF16)])
    return pl.pallas_call(
        _expert_kernel, grid_spec=grid_spec, out_shape=jax.ShapeDtypeStruct((n_rows, D_MODEL), F32),
        compiler_params=_params(1), name="expert_ffn")(blk_info, rows, wg, wu, wd)


def _moe_dispatch(route, h):
    n_tok = h.shape[0]
    n_asg = n_tok * MOE_TOPK
    i32 = jnp.int32
    e_flat = route[:, 0:MOE_TOPK].astype(i32).reshape(n_asg)
    is_e = e_flat[:, None] == jnp.arange(N_EXPERTS, dtype=i32)[None, :]
    counts = jnp.sum(is_e, axis=0, dtype=i32)
    order = jnp.argsort(e_flat).astype(i32)
    rank = jnp.argsort(order).astype(i32)
    padded = (counts + MOE_ROWS - 1) // MOE_ROWS * MOE_ROWS
    pad_end = jnp.cumsum(padded)
    pad_start = pad_end - padded
    start = jnp.cumsum(counts) - counts
    n_rows = n_asg + N_EXPERTS * MOE_ROWS
    n_blk = n_rows // MOE_ROWS
    blk_start = jnp.arange(n_blk, dtype=i32) * MOE_ROWS
    blk_expert = jnp.minimum(jnp.sum(pad_end[None, :] <= blk_start[:, None], axis=1, dtype=i32), N_EXPERTS - 1)
    within = (blk_start - pad_start[blk_expert])[:, None] + jnp.arange(MOE_ROWS, dtype=i32)[None, :]
    valid = within < counts[blk_expert][:, None]
    src = jnp.clip(start[blk_expert][:, None] + within, 0, n_asg - 1)
    row_tok = jnp.where(valid, (order // MOE_TOPK)[src], src // MOE_TOPK).reshape(n_rows)
    shift = jnp.sum(jnp.where(is_e, (pad_start - start)[None, :], 0), axis=1, dtype=i32)
    pos = (rank + shift).reshape(n_tok, MOE_TOPK)
    blk_info = jnp.concatenate([blk_expert, (pad_end[-1:] // MOE_ROWS).astype(i32)])
    return h[row_tok], blk_info, pos


def _moe(route, h, wg, wu, wd):
    rows, blk_info, pos = _moe_dispatch(route, h)
    y = _expert_ffn(rows, blk_info, wg, wu, wd)
    return y[pos[:, 0]], y[pos[:, 1]], route


def _final_kernel(x_ref, y0_ref, y1_ref, route_ref, g_ref, o_ref):
    x = _moe_combine(x_ref[...], y0_ref, y1_ref, route_ref)
    ms = jnp.mean(x * x, axis=-1, keepdims=True)
    o_ref[...] = x * lax.rsqrt(ms + NORM_EPS) * g_ref[...]


def _final_norm(x, ys, gain):
    T = x.shape[0]
    tm = ROW_TILE
    row = lambda i: (i, 0)
    return pl.pallas_call(
        _final_kernel, grid=(T // tm,),
        in_specs=[pl.BlockSpec((tm, D_MODEL), row)] * 3 + [pl.BlockSpec((tm, LANES), row), pl.BlockSpec((1, D_MODEL), lambda i: (0, 0))],
        out_specs=pl.BlockSpec((tm, D_MODEL), row), out_shape=jax.ShapeDtypeStruct((T, D_MODEL), F32),
        compiler_params=_params(1), name="final_norm")(x, *ys, gain.reshape(1, D_MODEL))


def _rope_tables(positions):
    inv_freq = ROPE_THETA ** (-jnp.arange(0, HEAD_DIM, 2, dtype=F32) / HEAD_DIM)
    ang = positions.astype(F32).reshape(-1, 1) * inv_freq
    cos, sin = jnp.cos(ang), jnp.sin(ang)
    reps = LANES // HEAD_DIM
    return jnp.tile(jnp.concatenate([cos, cos], axis=1), (1, reps)), jnp.tile(jnp.concatenate([-sin, sin], axis=1), (1, reps))


def _tokens(o):
    B, H, S, d = o.shape
    return o.transpose(0, 2, 1, 3).reshape(B * S, H * d)


def _chunks(first, count, rope, action):
    del first
    return tuple((rope, action(j)) for j in range(count))


AB_OUTS = (("tok", 4), ("heads", A_KV_HEADS), ("headsT", A_KV_HEADS), ("heads", B_HEADS), ("headsT", B_HEADS), ("headsT", B_HEADS))
AB_PLAN = (_chunks(0, 4, True, lambda j: (("tok", 0, j),))
           + _chunks(4, 1, True, lambda j: (("heads", 1, 0),))
           + _chunks(5, 1, False, lambda j: (("headsT", 2, 0),))
           + _chunks(6, 4, False, lambda j: (("heads", 3, 2 * j),))
           + _chunks(10, 4, False, lambda j: (("headsT", 4, 2 * j),))
           + _chunks(14, 4, False, lambda j: (("headsT", 5, 2 * j),)))
CD_OUTS = (("tok", 8), ("keys_aug", C_KV_HEADS), ("kmean", 0), ("valsT_aug", C_KV_HEADS), ("heads", D_KV_HEADS),
           ("heads", D_KV_HEADS), ("keys_aug", D_KV_HEADS), ("valsT_aug", D_KV_HEADS), ("heads", D_KV_HEADS),
           ("headsT", D_KV_HEADS), ("f32", 0))
CD_PLAN = (_chunks(0, 4, True, lambda j: (("tok", 0, j),))
           + _chunks(4, 1, True, lambda j: (("keys_aug", 1, C_BLOCK), ("kmean", 2, 0)))
           + _chunks(5, 1, False, lambda j: (("valsT_aug", 3, 0),))
           + _chunks(6, 4, True, lambda j: (("tok", 0, 4 + j),))
           + _chunks(10, 1, True, lambda j: (("heads", 4, 0),))
           + _chunks(11, 1, False, lambda j: (("heads", 5, 0),))
           + _chunks(12, 1, True, lambda j: (("keys_aug", 6, D_SEL_LEN),))
           + _chunks(13, 1, False, lambda j: (("valsT_aug", 7, 0),))
           + _chunks(14, 1, True, lambda j: (("heads", 8, 0),))
           + _chunks(15, 1, False, lambda j: (("headsT", 9, 0),))
           + _chunks(16, 1, False, lambda j: (("f32", 10, 0),)))
QD_COL = 4 * LANES


def _router_weights(router_group, router_expert):
    pad = jnp.zeros((D_MODEL, LANES - N_GROUPS - N_EXPERTS), F32)
    return jnp.concatenate([router_group.astype(F32), router_expert.astype(F32), pad], axis=1)


def _pad_cols(w, n):
    return jnp.concatenate([w, jnp.zeros((w.shape[0], n - w.shape[1]), w.dtype)], axis=1)


def _mixer_ab(parts, sinks):
    qa, ka, vaT, qb, kbT, vbT = parts
    oa = _swa_attention(qa, 0, ka, vaT, sinks)
    ob = _stick_attention(qb, kbT, vbT)
    return jnp.concatenate([oa, _tokens(ob)], axis=1)


def _mixer_cd(parts, B, S, k_pe, k_w1, k_w2, v_pe, v_w1, v_w2):
    q_cd, kc_aug, kmean, vcT_aug, kdc, vdc, kds_aug, vdsT_aug, kdw, vdwT, _ = parts
    nblk = S // C_BLOCK
    km = kmean.reshape(B, nblk, C_KV_HEADS, HEAD_DIM).transpose(0, 2, 1, 3).astype(BF16)
    oc = _moba_attention(q_cd, 0, kc_aug, vcT_aug, km)
    k_cmp = _compress(kdc, k_pe, k_w1, k_w2)
    v_cmp = _compress(vdc, v_pe, v_w1, v_w2)
    nc = (S - D_CMP_LEN) // D_CMP_STRIDE + 1
    ncp = S // D_CMP_STRIDE
    nsel = S // D_SEL_LEN
    c_start = jnp.arange(ncp) * D_CMP_STRIDE
    b_start = jnp.arange(nsel) * D_SEL_LEN
    overlap = ((c_start[:, None] <= b_start[None, :] + D_SEL_LEN - 1) & (c_start[:, None] + D_CMP_LEN - 1 >= b_start[None, :])
               & (jnp.arange(ncp)[:, None] < nc)).astype(BF16)
    o_cmp, biasT = _cmp_select(q_cd, QD_COL, k_cmp, v_cmp.transpose(0, 1, 3, 2), overlap.T, nc, S)
    o_sel = _sel_attention(q_cd, QD_COL, kds_aug, vdsT_aug, biasT)
    o_win = _win_attention(q_cd, QD_COL, kdw, vdwT)
    return oc, o_cmp, o_sel, o_win


def kernel(x, positions, ln_mix_0, w_in_0, sinks_0, w_out_0, ln_ffn_0, router_group_0, router_expert_0, expert_gate_0, expert_up_0, expert_down_0, ln_mix_1, w_in_1, cmp_k_pe_1, cmp_k_w1_1, cmp_k_w2_1, cmp_v_pe_1, cmp_v_w1_1, cmp_v_w2_1, w_out_1, ln_ffn_1, router_group_1, router_expert_1, expert_gate_1, expert_up_1, expert_down_1, ln_final):
    B, S, _ = x.shape
    T = B * S
    assert S % KEY_TILE == 0 and T % ROW_TILE == 0
    cos_t, sin_t = _rope_tables(positions)
    xf = x.reshape(T, D_MODEL)

    parts = _in_proj(xf, None, ln_mix_0, w_in_0.astype(BF16), cos_t, sin_t, AB_PLAN, AB_OUTS, B, S)
    o_ab = _mixer_ab(parts, sinks_0)
    x1, h1, route1 = _out_proj([o_ab], None, xf, w_out_0.astype(BF16), ln_ffn_0, _router_weights(router_group_0, router_expert_0))
    ys = _moe(route1, h1, expert_gate_0, expert_up_0, expert_down_0)

    x2, *parts = _in_proj(x1, ys, ln_mix_1, _pad_cols(w_in_1.astype(BF16), len(CD_PLAN) * LANES), cos_t, sin_t,
                          CD_PLAN, CD_OUTS, B, S)
    gd = parts[-1]
    parts = _mixer_cd(parts, B, S, cmp_k_pe_1, cmp_k_w1_1, cmp_k_w2_1, cmp_v_pe_1, cmp_v_w1_1, cmp_v_w2_1)
    x3, h3, route3 = _out_proj(parts, gd, x2, w_out_1.astype(BF16), ln_ffn_1, _router_weights(router_group_1, router_expert_1))
    ys = _moe(route3, h3, expert_gate_1, expert_up_1, expert_down_1)
    return _final_norm(x3, ys, ln_final).reshape(B, S, D_MODEL)
```

```python
import functools

import jax
import jax.numpy as jnp
from jax import lax
from jax.experimental import pallas as pl
from jax.experimental.pallas import tpu as pltpu

D_MODEL = 1024
HEAD_DIM = 64
HALF = HEAD_DIM // 2
ROPE_THETA = 10000.0
NORM_EPS = 1e-6
Q_BLOCK = 128
SCALE = HEAD_DIM ** -0.5

A_HEADS, A_KV_HEADS, A_WINDOW = 8, 2, 128
B_HEADS = 8
C_HEADS, C_KV_HEADS, C_BLOCK, C_TOPK = 8, 2, 256, 3
D_HEADS, D_KV_HEADS = 8, 2
D_CMP_LEN, D_CMP_STRIDE, D_CMP_HIDDEN = 32, 16, 256
D_SEL_LEN, D_SEL_TOPK, D_WINDOW = 64, 16, 512
N_GROUPS, EXPERTS_PER_GROUP, MOE_TOPK, EXPERT_HIDDEN = 4, 16, 2, 512
N_EXPERTS = N_GROUPS * EXPERTS_PER_GROUP
GROUP = 4

LANES = 128
ROW_TILE = 512
MOE_ROWS = 256
KEY_TILE = 512
FLASH_MIN_SUM = 1e-25
REF_MARGIN = 1.05
MASKED = -1e30
V_AUG_ROWS = 80
AUG_ROWS = 8
MASK_BIG = 32768.0
SB_EXIT = -104.0
STICK_HEADS = 8
VMEM_LIMIT = 56 * 1024 * 1024

F32 = jnp.float32
BF16 = jnp.bfloat16
NEG_INF = float("-inf")


def _iota(shape, dim):
    return lax.broadcasted_iota(jnp.int32, shape, dim)


def _dot(a, b):
    return jnp.dot(a, b, preferred_element_type=F32)


def _params(n_grid):
    return pltpu.CompilerParams(dimension_semantics=("arbitrary",) * n_grid, vmem_limit_bytes=VMEM_LIMIT)


GROUP_COLS = GROUP * HEAD_DIM


def _group_qT(q_ref):
    t = (q_ref[...].astype(F32) * SCALE).T
    return jnp.concatenate([t[g * HEAD_DIM:(g + 1) * HEAD_DIM] for g in range(GROUP)], axis=1).astype(BF16)


def _group_rows(oT):
    return jnp.concatenate([oT[:, g * Q_BLOCK:(g + 1) * Q_BLOCK] for g in range(GROUP)], axis=0).T


def _q_spec(S, col0):
    nq = S // Q_BLOCK
    return pl.BlockSpec((Q_BLOCK, GROUP_COLS), lambda b, h, n: (b * nq + n, col0 // GROUP_COLS + h))


def _o_spec(S):
    nq = S // Q_BLOCK
    return pl.BlockSpec((Q_BLOCK, GROUP_COLS), lambda b, h, n: (b * nq + n, h))


def _moe_combine(x, y0_ref, y1_ref, route_ref):
    return x + (y0_ref[...] * route_ref[:, MOE_TOPK:MOE_TOPK + 1] + y1_ref[...] * route_ref[:, MOE_TOPK + 1:MOE_TOPK + 2])


def _emit_chunk(ch, actions, outs):
    tm = ch.shape[0]
    chT = None
    for kind, oi, arg in actions:
        o = outs[oi]
        if kind == "tok":
            o[:, arg * LANES:(arg + 1) * LANES] = ch.astype(BF16)
        elif kind == "f32":
            o[...] = ch
        elif kind == "kmean":
            o[...] = jnp.mean(ch.reshape(tm // C_BLOCK, C_BLOCK, LANES), axis=1).reshape(tm // C_BLOCK, 1, LANES)
        elif kind == "heads":
            for hh in range(2):
                o[0, arg + hh] = ch[:, hh * HEAD_DIM:(hh + 1) * HEAD_DIM].astype(BF16)
        elif kind == "keys_aug":
            lane = _iota((tm, LANES), 1)
            block_in_tile = _iota((tm, LANES), 0) >> (arg.bit_length() - 1)
            aug = jnp.where((lane - HEAD_DIM == block_in_tile) | (lane == HEAD_DIM + AUG_ROWS), 1.0, 0.0)
            for hh in range(2):
                keys = ch if hh == 0 else pltpu.roll(ch, HEAD_DIM, 1)
                o[0, hh] = jnp.where(lane < HEAD_DIM, keys, aug).astype(BF16)
        else:
            chT = ch.astype(BF16).astype(F32).T if chT is None else chT
            for hh in range(2):
                vT = chT[hh * HEAD_DIM:(hh + 1) * HEAD_DIM]
                if kind == "headsT":
                    o[0, arg + hh] = vT.astype(BF16)
                else:
                    assert kind == "valsT_aug"
                    tail = jnp.where(_iota((V_AUG_ROWS - HEAD_DIM, tm), 0) == 0, 1.0, 0.0)
                    o[0, hh] = jnp.concatenate([vT, tail], axis=0).astype(BF16)


def _in_proj_kernel(plan, combine, *refs):
    refs = list(refs)
    x_ref = refs.pop(0)
    if combine:
        y0_ref, y1_ref, route_ref = refs.pop(0), refs.pop(0), refs.pop(0)
    g_ref, w_ref, cos_ref, sin_ref = refs[:4]
    outs = refs[4:]
    x = x_ref[...]
    if combine:
        x = _moe_combine(x, y0_ref, y1_ref, route_ref)
        xo_ref = outs.pop(0)
        xo_ref[...] = x
    ms = jnp.mean(x * x, axis=-1, keepdims=True)
    h = (x * lax.rsqrt(ms + NORM_EPS) * g_ref[...]).astype(BF16)
    tm = x.shape[0]
    cos = cos_ref[...]
    sin = sin_ref[...]
    first_half = (_iota((tm, LANES), 1) & (HEAD_DIM - 1)) < HALF
    for c, (rope, actions) in enumerate(plan):
        ch = _dot(h, w_ref[:, c * LANES:(c + 1) * LANES])
        if rope:
            partner = jnp.where(first_half, pltpu.roll(ch, LANES - HALF, 1), pltpu.roll(ch, HALF, 1))
            ch = ch * cos + partner * sin
        _emit_chunk(ch, actions, outs)


def _in_proj(x, ys, gain, w, cos_t, sin_t, plan, out_kinds, B, S):
    T = x.shape[0]
    n_cols = w.shape[1]
    tm = ROW_TILE
    assert tm == KEY_TILE and S % tm == 0 and len(plan) * LANES == n_cols
    tpb = S // tm
    combine = ys is not None
    row = lambda i: (i, 0)
    fixed = lambda i: (0, 0)
    by_seq = lambda i: (i // tpb, 0, i % tpb, 0)
    by_seq_t = lambda i: (i // tpb, 0, 0, i % tpb)
    in_specs = [pl.BlockSpec((tm, D_MODEL), row)]
    args = [x]
    if combine:
        in_specs += [pl.BlockSpec((tm, D_MODEL), row)] * 2 + [pl.BlockSpec((tm, LANES), row)]
        args += list(ys)
    in_specs += [pl.BlockSpec((1, D_MODEL), fixed), pl.BlockSpec((D_MODEL, n_cols), fixed),
                 pl.BlockSpec((tm, LANES), row), pl.BlockSpec((tm, LANES), row)]
    args += [gain.reshape(1, D_MODEL), w, cos_t, sin_t]
    out_shape, out_specs = [], []
    if combine:
        out_shape.append(jax.ShapeDtypeStruct((T, D_MODEL), F32))
        out_specs.append(pl.BlockSpec((tm, D_MODEL), row))
    for kind, size in out_kinds:
        if kind == "tok":
            shape, dtype, spec = (T, size * LANES), BF16, pl.BlockSpec((tm, size * LANES), row)
        elif kind == "f32":
            shape, dtype, spec = (T, LANES), F32, pl.BlockSpec((tm, LANES), row)
        elif kind == "kmean":
            shape, dtype, spec = (T // C_BLOCK, 1, LANES), F32, pl.BlockSpec((tm // C_BLOCK, 1, LANES), lambda i: (i, 0, 0))
        elif kind == "heads":
            shape, dtype, spec = (B, size, S, HEAD_DIM), BF16, pl.BlockSpec((1, size, tm, HEAD_DIM), by_seq)
        elif kind == "keys_aug":
            shape, dtype, spec = (B, size, S, LANES), BF16, pl.BlockSpec((1, size, tm, LANES), by_seq)
        elif kind == "headsT":
            shape, dtype, spec = (B, size, HEAD_DIM, S), BF16, pl.BlockSpec((1, size, HEAD_DIM, tm), by_seq_t)
        else:
            assert kind == "valsT_aug"
            shape, dtype, spec = (B, size, V_AUG_ROWS, S), BF16, pl.BlockSpec((1, size, V_AUG_ROWS, tm), by_seq_t)
        out_shape.append(jax.ShapeDtypeStruct(shape, dtype))
        out_specs.append(spec)
    kern = functools.partial(_in_proj_kernel, plan, combine)
    return pl.pallas_call(kern, grid=(T // tm,), in_specs=in_specs, out_specs=out_specs, out_shape=out_shape,
                          compiler_params=_params(1), name="in_proj")(*args)


def _swa_kernel(q_ref, kp_ref, ko_ref, vp_ref, vo_ref, sink_ref, o_ref):
    n = pl.program_id(2)
    qs = _group_qT(q_ref)
    R = qs.shape[1]
    k = jnp.concatenate([kp_ref[0, 0], ko_ref[0, 0]], axis=0)
    vT = jnp.concatenate([vp_ref[0, 0], vo_ref[0, 0]], axis=1)
    s = _dot(k, qs)
    qpos = _iota((2 * Q_BLOCK, R), 1) & (Q_BLOCK - 1)
    ki = _iota((2 * Q_BLOCK, R), 0)
    d = qpos + Q_BLOCK - ki
    ok = (d >= 0) & (d < A_WINDOW) & ((ki >= Q_BLOCK) | (n > 0))
    s = jnp.where(ok, s, NEG_INF)
    sink = sink_ref[0]
    m = jnp.maximum(jnp.max(s, axis=0, keepdims=True), sink)
    p = jnp.exp(s - m)
    den = jnp.sum(p, axis=0, keepdims=True) + jnp.exp(sink - m)
    o_ref[...] = _group_rows(_dot(vT, p.astype(BF16)) * (1.0 / den)).astype(o_ref.dtype)


def _swa_attention(proj, q_col, k, vT, sinks):
    B, Hkv, S, _ = k.shape
    nb = S // Q_BLOCK
    R = GROUP * Q_BLOCK
    sink_row = jnp.repeat(sinks.astype(F32).reshape(Hkv, GROUP), Q_BLOCK, axis=1).reshape(Hkv, 1, R)
    prev = lambda n: jnp.maximum(n - 1, 0)
    in_specs = [
        _q_spec(S, q_col),
        pl.BlockSpec((1, 1, Q_BLOCK, HEAD_DIM), lambda b, h, n: (b, h, prev(n), 0)),
        pl.BlockSpec((1, 1, Q_BLOCK, HEAD_DIM), lambda b, h, n: (b, h, n, 0)),
        pl.BlockSpec((1, 1, HEAD_DIM, Q_BLOCK), lambda b, h, n: (b, h, 0, prev(n))),
        pl.BlockSpec((1, 1, HEAD_DIM, Q_BLOCK), lambda b, h, n: (b, h, 0, n)),
        pl.BlockSpec((1, 1, R), lambda b, h, n: (h, 0, 0)),
    ]
    return pl.pallas_call(
        _swa_kernel, grid=(B, Hkv, nb), in_specs=in_specs,
        out_specs=_o_spec(S), out_shape=jax.ShapeDtypeStruct((B * S, Hkv * GROUP_COLS), BF16),
        compiler_params=_params(3), name="swa_attention")(proj, k, k, vT, vT, sink_row)


def _stick_kernel(q_ref, kT_ref, vT_ref, o_ref):
    n = pl.program_id(2)
    tq = tk = Q_BLOCK
    heads = q_ref.shape[1]
    row = _iota((tq, tk), 0)
    col = _iota((tk, tk), 1)
    upper = jnp.where(_iota((tk, tk), 0) > col, 1.0, 0.0).astype(BF16)
    tpos = n * tq + row

    def body(carry):
        kb, _, cs, accs = carry
        start = pl.multiple_of(kb * tk, tk)
        past = (start + col) < tpos
        hs = range(heads)
        zs = [_dot(q_ref[0, h], kT_ref[0, h, :, pl.ds(start, tk)]) * SCALE for h in hs]
        sps = [jnp.maximum(z, 0.0) + jnp.log1p(jnp.exp(-jnp.abs(z))) for z in zs]
        stays = [jnp.where(past, -sp, 0.0) for sp in sps]
        his = [st.astype(BF16) for st in stays]
        los = [(st - hi.astype(F32)).astype(BF16) for st, hi in zip(stays, his)]
        betweens = [_dot(hi, upper) + _dot(lo, upper) for hi, lo in zip(his, los)]
        ws = [jnp.where(past, jnp.exp(zs[h] - sps[h] + betweens[h] + cs[h]), 0.0).astype(BF16) for h in hs]
        new_accs = [accs[h] + lax.dot_general(ws[h], vT_ref[0, h, :, pl.ds(start, tk)], (((1,), (1,)), ((), ())),
                                              preferred_element_type=F32) for h in hs]
        new_cs = [cs[h] + jnp.sum(stays[h], axis=-1, keepdims=True) for h in hs]
        worst = functools.reduce(jnp.maximum, new_cs)
        return kb - 1, jnp.max(worst) > SB_EXIT, tuple(new_cs), tuple(new_accs)

    def cond(carry):
        kb, alive, _, _ = carry
        return (kb >= 0) & alive

    init = (n, jnp.array(True), (jnp.zeros((tq, 1), F32),) * heads, (jnp.zeros((tq, HEAD_DIM), F32),) * heads)
    _, _, _, accs = lax.while_loop(cond, body, init)
    for h in range(heads):
        o_ref[0, h] = accs[h].astype(o_ref.dtype)


def _stick_attention(q, kT, v):
    B, H, S, _ = q.shape
    nq = S // Q_BLOCK
    hb = STICK_HEADS
    resident = dict(pipeline_mode=pl.Buffered(1))
    in_specs = [
        pl.BlockSpec((1, hb, Q_BLOCK, HEAD_DIM), lambda b, h, n: (b, h, n, 0)),
        pl.BlockSpec((1, hb, HEAD_DIM, S), lambda b, h, n: (b, h, 0, 0), **resident),
        pl.BlockSpec((1, hb, HEAD_DIM, S), lambda b, h, n: (b, h, 0, 0), **resident),
    ]
    return pl.pallas_call(
        _stick_kernel, grid=(B, H // hb, nq), in_specs=in_specs,
        out_specs=pl.BlockSpec((1, hb, Q_BLOCK, HEAD_DIM), lambda b, h, n: (b, h, n, 0)),
        out_shape=jax.ShapeDtypeStruct((B, H, S, HEAD_DIM), BF16),
        compiler_params=_params(3), name="stick_attention")(q, kT, v)


def _max_key_sqnorm(ka_ref, out_ref):
    S = ka_ref.shape[2]
    tk = KEY_TILE
    is_key_lane = _iota((tk, LANES), 1) < HEAD_DIM
    ones = jnp.ones((LANES, LANES), BF16)

    def body(i, mx):
        k = jnp.where(is_key_lane, ka_ref[0, 0, pl.ds(pl.multiple_of(i * tk, tk), tk), :].astype(F32), 0.0)
        sq = _dot((k * k).astype(BF16), ones)
        return jnp.maximum(mx, jnp.max(sq, axis=0, keepdims=True))

    mx = lax.fori_loop(0, S // tk, body, jnp.zeros((1, LANES), F32))
    out_ref[...] = jnp.broadcast_to(mx, out_ref.shape)


def _flash_scratch(R):
    return [pltpu.VMEM((AUG_ROWS, LANES), F32), pltpu.VMEM((2, KEY_TILE, R), BF16), pltpu.VMEM((V_AUG_ROWS, R), F32)]


def _masked_flash_t(n, qT, bias_rows, ka_ref, vTa_ref, ksq_ref, p_scr, acc_scr):
    R = qT.shape[1]
    tk = KEY_TILE
    diag = (n * Q_BLOCK) // tk
    zpad = jnp.zeros((LANES - HEAD_DIM - 2 * AUG_ROWS, R), F32)
    causal = diag * tk + _iota((tk, R), 0) <= n * Q_BLOCK + (_iota((tk, R), 1) & (Q_BLOCK - 1))

    def scores(kt, ref_rows):
        st = pl.multiple_of(kt * tk, tk)
        low = jnp.concatenate([bias_rows(kt), ref_rows, zpad], axis=0).astype(BF16)
        return _dot(ka_ref[0, 0, pl.ds(st, tk), :], jnp.concatenate([qT, low], axis=0))

    def values(kt):
        return vTa_ref[0, 0, :, pl.ds(pl.multiple_of(kt * tk, tk), tk)]

    qsq = jnp.sum(jnp.square(qT.astype(F32)), axis=0, keepdims=True)
    ksq = jnp.concatenate([ksq_ref[0:1, :]] * (R // LANES), axis=1)
    ref = jnp.sqrt(qsq * ksq) * REF_MARGIN
    ref_rows = jnp.where(_iota((AUG_ROWS, R), 0) == 0, -ref, 0.0)

    def fast_body(kt, carry):
        slot = kt & 1
        s = jnp.where((kt < diag) | causal, scores(kt, ref_rows), MASKED)
        acc_scr[...] += _dot(values(jnp.maximum(kt - 1, 0)), p_scr[1 - slot])
        p_scr[slot] = jnp.exp(s).astype(BF16)
        return carry

    p_scr[1] = jnp.zeros((tk, R), BF16)
    acc_scr[...] = jnp.zeros(acc_scr.shape, F32)
    lax.fori_loop(0, diag + 1, fast_body, 0)
    acc = acc_scr[...] + _dot(values(diag), p_scr[diag & 1])

    def running_max_path(_):
        no_ref = jnp.zeros((AUG_ROWS, R), F32)

        def update(kt, s, m, acc):
            m_new = jnp.maximum(m, jnp.max(s, axis=0, keepdims=True))
            p = jnp.exp(s - m_new).astype(BF16)
            return m_new, jnp.exp(m - m_new) * acc + _dot(values(kt), p)

        init = (jnp.full((1, R), MASKED, F32), jnp.zeros((vTa_ref.shape[2], R), F32))
        m, acc = lax.fori_loop(0, diag, lambda kt, c: update(kt, scores(kt, no_ref), *c), init)
        return update(diag, jnp.where(causal, scores(diag, no_ref), MASKED), m, acc)[1]

    healthy = jnp.min(acc[HEAD_DIM:HEAD_DIM + 1, :]) >= FLASH_MIN_SUM
    acc = lax.cond(healthy, lambda _: acc, running_max_path, 0)
    return acc[0:HEAD_DIM] * (1.0 / acc[HEAD_DIM:HEAD_DIM + 1])


def _moba_kernel(q_ref, ka_ref, vTa_ref, km_ref, o_ref, bias_scr, ksq_scr, p_scr, acc_scr):
    n = pl.program_id(2)
    pl.when(n == 0)(lambda: _max_key_sqnorm(ka_ref, ksq_scr))
    nblk = km_ref.shape[2]
    qs = _group_qT(q_ref)
    R = qs.shape[1]
    own = (n * Q_BLOCK) // C_BLOCK
    gate = _dot(km_ref[0, 0], qs)
    blk = _iota((nblk, R), 0)
    gate = jnp.where(blk < own, gate, NEG_INF)
    bias = jnp.where(blk == own, 0.0, -MASK_BIG)
    for _ in range(C_TOPK):
        mx = jnp.max(gate, axis=0, keepdims=True)
        idx = jnp.min(jnp.where(gate == mx, blk, nblk), axis=0, keepdims=True)
        hit = blk == idx
        bias = jnp.where(hit & (mx > NEG_INF), 0.0, bias)
        gate = jnp.where(hit, NEG_INF, gate)
    per = KEY_TILE // C_BLOCK
    rows = bias_scr.shape[0]
    r = _iota((rows, nblk), 0)
    spread = jnp.where(((r & (AUG_ROWS - 1)) < per) & (_iota((rows, nblk), 1) == (r >> 3) * per + (r & (AUG_ROWS - 1))), 1.0, 0.0)
    bias_scr[...] = _dot(spread.astype(BF16), bias.astype(BF16))
    oT = _masked_flash_t(n, qs, lambda kt: bias_scr[pl.ds(pl.multiple_of(kt * AUG_ROWS, AUG_ROWS), AUG_ROWS), :],
                         ka_ref, vTa_ref, ksq_scr, p_scr, acc_scr)
    o_ref[...] = _group_rows(oT).astype(o_ref.dtype)


def _moba_attention(proj, q_col, ka, vTa, km):
    B, Hkv, S, _ = ka.shape
    nq = S // Q_BLOCK
    R = GROUP * Q_BLOCK
    nblk = km.shape[2]
    nkt = S // KEY_TILE
    in_specs = [
        _q_spec(S, q_col),
        pl.BlockSpec((1, 1, S, LANES), lambda b, h, n: (b, h, 0, 0)),
        pl.BlockSpec((1, 1, V_AUG_ROWS, S), lambda b, h, n: (b, h, 0, 0)),
        pl.BlockSpec((1, 1, nblk, HEAD_DIM), lambda b, h, n: (b, h, 0, 0)),
    ]
    return pl.pallas_call(
        _moba_kernel, grid=(B, Hkv, nq), in_specs=in_specs,
        out_specs=_o_spec(S), out_shape=jax.ShapeDtypeStruct((B * S, Hkv * GROUP_COLS), BF16),
        scratch_shapes=[pltpu.VMEM((nkt * AUG_ROWS, R), F32)] + _flash_scratch(R),
        compiler_params=_params(3), name="moba_attention")(proj, ka, vTa, km)


def _compress_kernel(u_ref, us_ref, pe_ref, w1_ref, w2_ref, o_ref):
    a = (u_ref[0, 0].astype(F32) + pe_ref[0:1, :]).astype(BF16)
    b = (us_ref[0, 0].astype(F32) + pe_ref[1:2, :]).astype(BF16)
    pre = _dot(a, w1_ref[0]) + _dot(b, w1_ref[1])
    hid = jax.nn.gelu(pre)
    o_ref[0, 0] = _dot(hid.astype(BF16), w2_ref[...]).astype(o_ref.dtype)


def _compress(t, pe, w1, w2):
    B, H, S, _ = t.shape
    nrow = S // D_CMP_STRIDE
    width = D_CMP_STRIDE * HEAD_DIM
    u = t.reshape(B, H, nrow, width)
    us = jnp.concatenate([u[:, :, 1:], jnp.zeros((B, H, 1, width), u.dtype)], axis=2)
    blk = lambda b, h: (b, h, 0, 0)
    in_specs = [
        pl.BlockSpec((1, 1, nrow, width), blk), pl.BlockSpec((1, 1, nrow, width), blk),
        pl.BlockSpec((2, width), lambda b, h: (0, 0)),
        pl.BlockSpec((2, width, D_CMP_HIDDEN), lambda b, h: (0, 0, 0)),
        pl.BlockSpec((D_CMP_HIDDEN, HEAD_DIM), lambda b, h: (0, 0)),
    ]
    return pl.pallas_call(
        _compress_kernel, grid=(B, H), in_specs=in_specs,
        out_specs=pl.BlockSpec((1, 1, nrow, HEAD_DIM), blk),
        out_shape=jax.ShapeDtypeStruct((B, H, nrow, HEAD_DIM), BF16),
        compiler_params=_params(2), name="nsa_compress")(
            u, us, pe.astype(F32).reshape(2, width), w1.astype(BF16).reshape(2, width, D_CMP_HIDDEN), w2.astype(BF16))


def _cmp_select_kernel(nc, q_ref, kc_ref, vcT_ref, ovT_ref, oc_ref, bias_ref):
    n = pl.program_id(2)
    ncp = kc_ref.shape[2]
    nsel = ovT_ref.shape[0]
    qs = _group_qT(q_ref)
    R = qs.shape[1]
    s = _dot(kc_ref[0, 0], qs)
    tpos = n * Q_BLOCK + (_iota((ncp, R), 1) & (Q_BLOCK - 1))
    c = _iota((ncp, R), 0)
    ok = (c * D_CMP_STRIDE + (D_CMP_LEN - 1) <= tpos) & (c < nc)
    s = jnp.where(ok, s, NEG_INF)
    m = jnp.max(s, axis=0, keepdims=True)
    m = jnp.where(m > NEG_INF, m, 0.0)
    e = jnp.exp(s - m)
    den = jnp.sum(e, axis=0, keepdims=True)
    p = (e * (1.0 / jnp.where(den > 0, den, 1.0))).astype(BF16)
    oc_ref[...] = _group_rows(_dot(vcT_ref[0, 0], p))
    imp_heads = _dot(ovT_ref[...], p)
    imp = imp_heads[:, 0:Q_BLOCK]
    for g in range(1, GROUP):
        imp = imp + imp_heads[:, g * Q_BLOCK:(g + 1) * Q_BLOCK]
    t = n * Q_BLOCK + _iota((nsel, Q_BLOCK), 1)
    j = _iota((nsel, Q_BLOCK), 0)
    cur = t >> 6
    allowed = j * D_SEL_LEN <= t
    forced = (j == 0) | (j == cur) | (j == cur - 1)
    imp = jnp.where(allowed, imp, NEG_INF)
    imp = jnp.where(allowed & forced, float("inf"), imp)
    bias = jnp.full((nsel, Q_BLOCK), -MASK_BIG, F32)
    for _ in range(D_SEL_TOPK):
        mx = jnp.max(imp, axis=0, keepdims=True)
        idx = jnp.min(jnp.where(imp == mx, j, nsel), axis=0, keepdims=True)
        hit = j == idx
        bias = jnp.where(hit & (mx > NEG_INF), 0.0, bias)
        imp = jnp.where(hit, NEG_INF, imp)
    bias_ref[0, 0, 0] = bias


def _cmp_select(proj, q_col, kc, vcT, overlapT, nc, S):
    B, Hkv, ncp, _ = kc.shape
    nq = S // Q_BLOCK
    nsel = overlapT.shape[0]
    in_specs = [
        _q_spec(S, q_col),
        pl.BlockSpec((1, 1, ncp, HEAD_DIM), lambda b, h, n: (b, h, 0, 0)),
        pl.BlockSpec((1, 1, HEAD_DIM, ncp), lambda b, h, n: (b, h, 0, 0)),
        pl.BlockSpec((nsel, ncp), lambda b, h, n: (0, 0)),
    ]
    out_specs = [_o_spec(S), pl.BlockSpec((1, 1, 1, nsel, Q_BLOCK), lambda b, h, n: (b, h, n, 0, 0))]
    out_shape = [jax.ShapeDtypeStruct((B * S, Hkv * GROUP_COLS), F32), jax.ShapeDtypeStruct((B, Hkv, nq, nsel, Q_BLOCK), F32)]
    return pl.pallas_call(
        functools.partial(_cmp_select_kernel, nc), grid=(B, Hkv, nq), in_specs=in_specs, out_specs=out_specs,
        out_shape=out_shape, compiler_params=_params(3), name="nsa_cmp_select")(proj, kc, vcT, overlapT)


def _sel_kernel(q_ref, ka_ref, vTa_ref, bias_ref, o_ref, ksq_scr, p_scr, acc_scr):
    n = pl.program_id(2)
    pl.when(n == 0)(lambda: _max_key_sqnorm(ka_ref, ksq_scr))
    qs = _group_qT(q_ref)

    def bias_rows(kt):
        b = bias_ref[0, 0, 0, pl.ds(pl.multiple_of(kt * AUG_ROWS, AUG_ROWS), AUG_ROWS), :]
        return jnp.concatenate([b] * GROUP, axis=1)

    o_ref[...] = _group_rows(_masked_flash_t(n, qs, bias_rows, ka_ref, vTa_ref, ksq_scr, p_scr, acc_scr))


def _sel_attention(proj, q_col, ka, vTa, biasT):
    B, Hkv, S, _ = ka.shape
    nq = S // Q_BLOCK
    nsel = biasT.shape[3]
    assert KEY_TILE // D_SEL_LEN == AUG_ROWS
    in_specs = [
        _q_spec(S, q_col),
        pl.BlockSpec((1, 1, S, LANES), lambda b, h, n: (b, h, 0, 0)),
        pl.BlockSpec((1, 1, V_AUG_ROWS, S), lambda b, h, n: (b, h, 0, 0)),
        pl.BlockSpec((1, 1, 1, nsel, Q_BLOCK), lambda b, h, n: (b, h, n, 0, 0)),
    ]
    return pl.pallas_call(
        _sel_kernel, grid=(B, Hkv, nq), in_specs=in_specs,
        out_specs=_o_spec(S), out_shape=jax.ShapeDtypeStruct((B * S, Hkv * GROUP_COLS), F32),
        scratch_shapes=_flash_scratch(GROUP * Q_BLOCK),
        compiler_params=_params(3), name="nsa_selected")(proj, ka, vTa, biasT)


def _win_kernel(span, q_ref, k_ref, vT_ref, o_ref):
    n = pl.program_id(2)
    qs = _group_qT(q_ref)
    R = qs.shape[1]
    start = pl.multiple_of(jnp.maximum(n * Q_BLOCK + Q_BLOCK - span, 0), Q_BLOCK)
    s = _dot(k_ref[0, 0, pl.ds(start, span), :], qs)
    tpos = n * Q_BLOCK + (_iota((span, R), 1) & (Q_BLOCK - 1))
    d = tpos - (start + _iota((span, R), 0))
    s = jnp.where((d >= 0) & (d < D_WINDOW), s, NEG_INF)
    m = jnp.max(s, axis=0, keepdims=True)
    p = jnp.exp(s - m)
    l = jnp.sum(p, axis=0, keepdims=True)
    o_ref[...] = _group_rows(_dot(vT_ref[0, 0, :, pl.ds(start, span)], p.astype(BF16)) * (1.0 / l))


def _win_attention(proj, q_col, k, vT):
    B, Hkv, S, _ = k.shape
    nq = S // Q_BLOCK
    span = min(D_WINDOW + Q_BLOCK, S)
    in_specs = [
        _q_spec(S, q_col),
        pl.BlockSpec((1, 1, S, HEAD_DIM), lambda b, h, n: (b, h, 0, 0)),
        pl.BlockSpec((1, 1, HEAD_DIM, S), lambda b, h, n: (b, h, 0, 0)),
    ]
    return pl.pallas_call(
        functools.partial(_win_kernel, span), grid=(B, Hkv, nq), in_specs=in_specs,
        out_specs=_o_spec(S), out_shape=jax.ShapeDtypeStruct((B * S, Hkv * GROUP_COLS), F32),
        compiler_params=_params(3), name="nsa_window")(proj, k, vT)


def _route(logits):
    tm = logits.shape[0]
    lane = _iota((tm, LANES), 1)
    gl = jnp.where(lane < N_GROUPS, logits, NEG_INF)
    gmax = jnp.max(gl, axis=-1, keepdims=True)
    gidx = jnp.min(jnp.where(gl == gmax, lane, LANES), axis=-1, keepdims=True)
    g_prob = 1.0 / jnp.sum(jnp.exp(gl - gmax), axis=-1, keepdims=True)
    elane = lane - N_GROUPS
    in_group = (elane >= 0) & (elane < N_EXPERTS) & ((elane >> 4) == gidx)
    el = jnp.where(in_group, logits, NEG_INF)
    ee = jnp.exp(el - jnp.max(el, axis=-1, keepdims=True))
    ep = jnp.where(in_group, ee / jnp.sum(ee, axis=-1, keepdims=True), -1.0)
    p1 = jnp.max(ep, axis=-1, keepdims=True)
    i1 = jnp.min(jnp.where(ep == p1, lane, LANES), axis=-1, keepdims=True)
    ep2 = jnp.where(lane == i1, -1.0, ep)
    p2 = jnp.max(ep2, axis=-1, keepdims=True)
    i2 = jnp.min(jnp.where(ep2 == p2, lane, LANES), axis=-1, keepdims=True)
    den = p1 + p2
    vals = [(i1 - N_GROUPS).astype(F32), (i2 - N_GROUPS).astype(F32), g_prob * p1 / den, g_prob * p2 / den]
    out = jnp.zeros((tm, LANES), F32)
    for k, val in enumerate(vals):
        out = jnp.where(lane == k, val, out)
    return out


def _pack_bf16_pairs(h):
    n = h.shape[1] // 2
    bits = lax.bitcast_convert_type(h.astype(BF16).astype(F32), jnp.uint32)
    return bits[:, :n] | (bits[:, n:] >> 16)


def _unpack_bf16_pairs(packed):
    hi = lax.bitcast_convert_type(packed & jnp.uint32(0xFFFF0000), F32)
    lo = lax.bitcast_convert_type(packed << 16, F32)
    return jnp.concatenate([hi, lo], axis=1).astype(BF16)


def _out_tail(x_new, gain_ref, wr_ref, xo_ref, h_ref, route_ref):
    xo_ref[...] = x_new
    ms = jnp.mean(x_new * x_new, axis=-1, keepdims=True)
    h = x_new * lax.rsqrt(ms + NORM_EPS) * gain_ref[...]
    h_ref[...] = _pack_bf16_pairs(h)
    logits =jnp.dot(h, wr_ref[...], preferred_element_type=F32, precision=lax.Precision.HIGHEST)
    route_ref[...] = _route(logits)


def _out_proj_kernel(o_ref, x_ref, w_ref, gain_ref, wr_ref, xo_ref, h_ref, route_ref):
    x_new = x_ref[...] + _dot(o_ref[...], w_ref[...])
    _out_tail(x_new, gain_ref, wr_ref, xo_ref, h_ref, route_ref)


def _out_proj_nsa_kernel(oc_ref, b0_ref, b1_ref, b2_ref, gd_ref, x_ref, w_ref, gain_ref, wr_ref, xo_ref, h_ref, route_ref):
    half = D_HEADS * HEAD_DIM
    g = jax.nn.sigmoid(gd_ref[...])
    g_hi = g.astype(BF16)
    g_lo = (g - g_hi.astype(F32)).astype(BF16)
    src = _iota((LANES, half), 0)
    head3 = (_iota((LANES, half), 1) >> 6) * 3
    od = None
    for br, b_ref in enumerate((b0_ref, b1_ref, b2_ref)):
        spread = jnp.where(src == head3 + br, 1.0, 0.0).astype(BF16)
        term = (_dot(g_hi, spread) + _dot(g_lo, spread)) * b_ref[...]
        od = term if od is None else od + term
    x_new = x_ref[...] + _dot(oc_ref[...], w_ref[0:half, :]) + _dot(od.astype(BF16), w_ref[half:, :])
    _out_tail(x_new, gain_ref, wr_ref, xo_ref, h_ref, route_ref)


def _out_proj(o_parts, gd, x, w_out, gain, w_router):
    T = x.shape[0]
    tm = ROW_TILE
    row = lambda i: (i, 0)
    fixed = lambda i: (0, 0)
    if gd is None:
        kern = _out_proj_kernel
        args = [o_parts[0]]
        in_specs = [pl.BlockSpec((tm, D_MODEL), row)]
    else:
        kern = _out_proj_nsa_kernel
        args = list(o_parts) + [gd]
        in_specs = [pl.BlockSpec((tm, D_MODEL // 2), row)] * 4 + [pl.BlockSpec((tm, LANES), row)]
    args += [x, w_out, gain.reshape(1, D_MODEL), w_router]
    in_specs += [pl.BlockSpec((tm, D_MODEL), row), pl.BlockSpec((D_MODEL, D_MODEL), fixed),
                 pl.BlockSpec((1, D_MODEL), fixed), pl.BlockSpec((D_MODEL, LANES), fixed)]
    out_shape = [jax.ShapeDtypeStruct((T, D_MODEL), F32), jax.ShapeDtypeStruct((T, D_MODEL // 2), jnp.uint32),
                 jax.ShapeDtypeStruct((T, LANES), F32)]
    out_specs = [pl.BlockSpec((tm, D_MODEL), row), pl.BlockSpec((tm, D_MODEL // 2), row), pl.BlockSpec((tm, LANES), row)]
    return pl.pallas_call(kern, grid=(T // tm,), in_specs=in_specs, out_specs=out_specs, out_shape=out_shape,
                          compiler_params=_params(1), name="out_proj_router")(*args)


def _expert_kernel(be_ref, rows_ref, wg_ref, wu_ref, wd_ref, y_ref, wg_s, wu_s, wd_s):
    i = pl.program_id(0)
    n_blk = pl.num_programs(0)

    @pl.when((i == 0) | (be_ref[i] != be_ref[jnp.maximum(i - 1, 0)]))
    def _():
        wg_s[...] = wg_ref[0].astype(BF16)
        wu_s[...] = wu_ref[0].astype(BF16)
        wd_s[...] = wd_ref[0].astype(BF16)

    @pl.when(i < be_ref[n_blk])
    def _():
        xb = _unpack_bf16_pairs(rows_ref[...])
        hid = jax.nn.silu(_dot(xb, wg_s[...])) * _dot(xb, wu_s[...])
        y_ref[...] = _dot(hid.astype(BF16), wd_s[...])

    @pl.when(i >= be_ref[n_blk])
    def _():
        y_ref[...] = jnp.zeros(y_ref.shape, F32)


def _expert_ffn(rows, blk_info, wg, wu, wd):
    n_rows = rows.shape[0]
    n_blk = n_rows // MOE_ROWS
    grid_spec = pltpu.PrefetchScalarGridSpec(
        num_scalar_prefetch=1, grid=(n_blk,),
        in_specs=[
            pl.BlockSpec((MOE_ROWS, D_MODEL // 2), lambda i, be: (i, 0)),
            pl.BlockSpec((1, D_MODEL, EXPERT_HIDDEN), lambda i, be: (be[i], 0, 0)),
            pl.BlockSpec((1, D_MODEL, EXPERT_HIDDEN), lambda i, be: (be[i], 0, 0)),
            pl.BlockSpec((1, EXPERT_HIDDEN, D_MODEL), lambda i, be: (be[i], 0, 0)),
        ],
        out_specs=pl.BlockSpec((MOE_ROWS, D_MODEL), lambda i, be: (i, 0)),
        scratch_shapes=[pltpu.VMEM((D_MODEL, EXPERT_HIDDEN), BF16), pltpu.VMEM((D_MODEL, EXPERT_HIDDEN), BF16),
                        pltpu.VMEM((EXPERT_HIDDEN, D_MODEL), BF16)])
    return pl.pallas_call(
        _expert_kernel, grid_spec=grid_spec, out_shape=jax.ShapeDtypeStruct((n_rows, D_MODEL), F32),
        compiler_params=_params(1), name="expert_ffn")(blk_info, rows, wg, wu, wd)


def _moe_dispatch(route, h):
    n_tok = h.shape[0]
    n_asg = n_tok * MOE_TOPK
    i32 = jnp.int32
    e_flat = route[:, 0:MOE_TOPK].astype(i32).reshape(n_asg)
    is_e = e_flat[:, None] == jnp.arange(N_EXPERTS, dtype=i32)[None, :]
    counts = jnp.sum(is_e, axis=0, dtype=i32)
    order = jnp.argsort(e_flat).astype(i32)
    rank = jnp.argsort(order).astype(i32)
    padded = (counts + MOE_ROWS - 1) // MOE_ROWS * MOE_ROWS
    pad_end = jnp.cumsum(padded)
    pad_start = pad_end - padded
    start = jnp.cumsum(counts) - counts
    n_rows = n_asg + N_EXPERTS * MOE_ROWS
    n_blk = n_rows // MOE_ROWS
    blk_start = jnp.arange(n_blk, dtype=i32) * MOE_ROWS
    blk_expert = jnp.minimum(jnp.sum(pad_end[None, :] <= blk_start[:, None], axis=1, dtype=i32), N_EXPERTS - 1)
    within = (blk_start - pad_start[blk_expert])[:, None] + jnp.arange(MOE_ROWS, dtype=i32)[None, :]
    valid = within < counts[blk_expert][:, None]
    src = jnp.clip(start[blk_expert][:, None] + within, 0, n_asg - 1)
    row_tok = jnp.where(valid, (order // MOE_TOPK)[src], src // MOE_TOPK).reshape(n_rows)
    shift = jnp.sum(jnp.where(is_e, (pad_start - start)[None, :], 0), axis=1, dtype=i32)
    pos = (rank + shift).reshape(n_tok, MOE_TOPK)
    blk_info = jnp.concatenate([blk_expert, (pad_end[-1:] // MOE_ROWS).astype(i32)])
    return h[row_tok], blk_info, pos


def _moe(route, h, wg, wu, wd):
    rows, blk_info, pos = _moe_dispatch(route, h)
    y = _expert_ffn(rows, blk_info, wg, wu, wd)
    return y[pos[:, 0]], y[pos[:, 1]], route


def _final_kernel(x_ref, y0_ref, y1_ref, route_ref, g_ref, o_ref):
    x = _moe_combine(x_ref[...], y0_ref, y1_ref, route_ref)
    ms = jnp.mean(x * x, axis=-1, keepdims=True)
    o_ref[...] = x * lax.rsqrt(ms + NORM_EPS) * g_ref[...]


def _final_norm(x, ys, gain):
    T = x.shape[0]
    tm = ROW_TILE
    row = lambda i: (i, 0)
    return pl.pallas_call(
        _final_kernel, grid=(T // tm,),
        in_specs=[pl.BlockSpec((tm, D_MODEL), row)] * 3 + [pl.BlockSpec((tm, LANES), row), pl.BlockSpec((1, D_MODEL), lambda i: (0, 0))],
        out_specs=pl.BlockSpec((tm, D_MODEL), row), out_shape=jax.ShapeDtypeStruct((T, D_MODEL), F32),
        compiler_params=_params(1), name="final_norm")(x, *ys, gain.reshape(1, D_MODEL))


def _rope_tables(positions):
    inv_freq = ROPE_THETA ** (-jnp.arange(0, HEAD_DIM, 2, dtype=F32) / HEAD_DIM)
    ang = positions.astype(F32).reshape(-1, 1) * inv_freq
    cos, sin = jnp.cos(ang), jnp.sin(ang)
    reps = LANES // HEAD_DIM
    return jnp.tile(jnp.concatenate([cos, cos], axis=1), (1, reps)), jnp.tile(jnp.concatenate([-sin, sin], axis=1), (1, reps))


def _tokens(o):
    B, H, S, d = o.shape
    return o.transpose(0, 2, 1, 3).reshape(B * S, H * d)


def _chunks(first, count, rope, action):
    return tuple((first + j, rope, action(j)) for j in range(count))


def _plan(*runs):
    entries = sum(runs, ())
    assert [e[0] for e in entries] == list(range(len(entries)))
    return tuple(e[1:] for e in entries)


AB_OUTS = (("tok", 4), ("heads", A_KV_HEADS), ("headsT", A_KV_HEADS), ("heads", B_HEADS), ("headsT", B_HEADS), ("headsT", B_HEADS))
AB_PLAN = _plan(_chunks(0, 4, True, lambda j: (("tok", 0, j),)),
                _chunks(4, 1, True, lambda j: (("heads", 1, 0),)),
                _chunks(5, 1, False, lambda j: (("headsT", 2, 0),)),
                _chunks(6, 4, False, lambda j: (("heads", 3, 2 * j),)),
                _chunks(10, 4, False, lambda j: (("headsT", 4, 2 * j),)),
                _chunks(14, 4, False, lambda j: (("headsT", 5, 2 * j),)))
CD_OUTS = (("tok", 8), ("keys_aug", C_KV_HEADS), ("kmean", 0), ("valsT_aug", C_KV_HEADS), ("heads", D_KV_HEADS),
           ("heads", D_KV_HEADS), ("keys_aug", D_KV_HEADS), ("valsT_aug", D_KV_HEADS), ("heads", D_KV_HEADS),
           ("headsT", D_KV_HEADS), ("f32", 0))
CD_PLAN = _plan(_chunks(0, 4, True, lambda j: (("tok", 0, j),)),
                _chunks(4, 1, True, lambda j: (("keys_aug", 1, C_BLOCK), ("kmean", 2, 0))),
                _chunks(5, 1, False, lambda j: (("valsT_aug", 3, 0),)),
                _chunks(6, 4, True, lambda j: (("tok", 0, 4 + j),)),
                _chunks(10, 1, True, lambda j: (("heads", 4, 0),)),
                _chunks(11, 1, False, lambda j: (("heads", 5, 0),)),
                _chunks(12, 1, True, lambda j: (("keys_aug", 6, D_SEL_LEN),)),
                _chunks(13, 1, False, lambda j: (("valsT_aug", 7, 0),)),
                _chunks(14, 1, True, lambda j: (("heads", 8, 0),)),
                _chunks(15, 1, False, lambda j: (("headsT", 9, 0),)),
                _chunks(16, 1, False, lambda j: (("f32", 10, 0),)))
QD_COL = 4 * LANES


def _router_weights(router_group, router_expert):
    pad = jnp.zeros((D_MODEL, LANES - N_GROUPS - N_EXPERTS), F32)
    return jnp.concatenate([router_group.astype(F32), router_expert.astype(F32), pad], axis=1)


def _pad_cols(w, n):
    return jnp.concatenate([w, jnp.zeros((w.shape[0], n - w.shape[1]), w.dtype)], axis=1)


def _mixer_ab(parts, sinks):
    qa, ka, vaT, qb, kbT, vbT = parts
    oa = _swa_attention(qa, 0, ka, vaT, sinks)
    ob = _stick_attention(qb, kbT, vbT)
    return jnp.concatenate([oa, _tokens(ob)], axis=1)


def _mixer_cd(parts, B, S, k_pe, k_w1, k_w2, v_pe, v_w1, v_w2):
    q_cd, kc_aug, kmean, vcT_aug, kdc, vdc, kds_aug, vdsT_aug, kdw, vdwT, _ = parts
    nblk = S // C_BLOCK
    km = kmean.reshape(B, nblk, C_KV_HEADS, HEAD_DIM).transpose(0, 2, 1, 3).astype(BF16)
    oc = _moba_attention(q_cd, 0, kc_aug, vcT_aug, km)
    k_cmp = _compress(kdc, k_pe, k_w1, k_w2)
    v_cmp = _compress(vdc, v_pe, v_w1, v_w2)
    nc = (S - D_CMP_LEN) // D_CMP_STRIDE + 1
    ncp = S // D_CMP_STRIDE
    nsel = S // D_SEL_LEN
    c_start = jnp.arange(ncp) * D_CMP_STRIDE
    b_start = jnp.arange(nsel) * D_SEL_LEN
    overlap = ((c_start[:, None] <= b_start[None, :] + D_SEL_LEN - 1) & (c_start[:, None] + D_CMP_LEN - 1 >= b_start[None, :])
               & (jnp.arange(ncp)[:, None] < nc)).astype(BF16)
    o_cmp, biasT = _cmp_select(q_cd, QD_COL, k_cmp, v_cmp.transpose(0, 1, 3, 2), overlap.T, nc, S)
    o_sel = _sel_attention(q_cd, QD_COL, kds_aug, vdsT_aug, biasT)
    o_win = _win_attention(q_cd, QD_COL, kdw, vdwT)
    return oc, o_cmp, o_sel, o_win


def kernel(x, positions, ln_mix_0, w_in_0, sinks_0, w_out_0, ln_ffn_0, router_group_0, router_expert_0, expert_gate_0, expert_up_0, expert_down_0, ln_mix_1, w_in_1, cmp_k_pe_1, cmp_k_w1_1, cmp_k_w2_1, cmp_v_pe_1, cmp_v_w1_1, cmp_v_w2_1, w_out_1, ln_ffn_1, router_group_1, router_expert_1, expert_gate_1, expert_up_1, expert_down_1, ln_final):
    B, S, _ = x.shape
    T = B * S
    assert S % KEY_TILE == 0 and T % ROW_TILE == 0
    cos_t, sin_t = _rope_tables(positions)
    xf = x.reshape(T, D_MODEL)

    parts = _in_proj(xf, None, ln_mix_0, w_in_0.astype(BF16), cos_t, sin_t, AB_PLAN, AB_OUTS, B, S)
    o_ab = _mixer_ab(parts, sinks_0)
    x1, h1, route1 = _out_proj([o_ab], None, xf, w_out_0.astype(BF16), ln_ffn_0, _router_weights(router_group_0, router_expert_0))
    ys = _moe(route1, h1, expert_gate_0, expert_up_0, expert_down_0)

    x2, *parts = _in_proj(x1, ys, ln_mix_1, _pad_cols(w_in_1.astype(BF16), len(CD_PLAN) * LANES), cos_t, sin_t,
                          CD_PLAN, CD_OUTS, B, S)
    gd = parts[-1]
    parts = _mixer_cd(parts, B, S, cmp_k_pe_1, cmp_k_w1_1, cmp_k_w2_1, cmp_v_pe_1, cmp_v_w1_1, cmp_v_w2_1)
    x3, h3, route3 = _out_proj(parts, gd, x2, w_out_1.astype(BF16), ln_ffn_1, _router_weights(router_group_1, router_expert_1))
    ys = _moe(route3, h3, expert_gate_1, expert_up_1, expert_down_1)
    return _final_norm(x3, ys, ln_final).reshape(B, S, D_MODEL)
```

```python
import functools

import jax
import jax.numpy as jnp
from jax import lax
from jax.experimental import pallas as pl
from jax.experimental.pallas import tpu as pltpu

D_MODEL = 1024
HEAD_DIM = 64
HALF = HEAD_DIM // 2
ROPE_THETA = 10000.0
NORM_EPS = 1e-6
Q_BLOCK = 128
SCALE = HEAD_DIM ** -0.5

A_HEADS, A_KV_HEADS, A_WINDOW = 8, 2, 128
B_HEADS = 8
C_HEADS, C_KV_HEADS, C_BLOCK, C_TOPK = 8, 2, 256, 3
D_HEADS, D_KV_HEADS = 8, 2
D_CMP_LEN, D_CMP_STRIDE, D_CMP_HIDDEN = 32, 16, 256
D_SEL_LEN, D_SEL_TOPK, D_WINDOW = 64, 16, 512
N_GROUPS, EXPERTS_PER_GROUP, MOE_TOPK, EXPERT_HIDDEN = 4, 16, 2, 512
N_EXPERTS = N_GROUPS * EXPERTS_PER_GROUP
GROUP = 4

LANES = 128
ROW_TILE = 512
MOE_ROWS = 256
KEY_TILE = 512
FLASH_MIN_SUM = 1e-25
REF_MARGIN = 1.05
MASKED = -1e30
V_AUG_ROWS = 80
AUG_ROWS = 8
MASK_BIG = 32768.0
SB_EXIT = -104.0
STICK_HEADS = 8
VMEM_LIMIT = 56 * 1024 * 1024

F32 = jnp.float32
BF16 = jnp.bfloat16
NEG_INF = float("-inf")


def _iota(shape, dim):
    return lax.broadcasted_iota(jnp.int32, shape, dim)


def _dot(a, b):
    return jnp.dot(a, b, preferred_element_type=F32)


def _params(n_grid):
    return pltpu.CompilerParams(dimension_semantics=("arbitrary",) * n_grid, vmem_limit_bytes=VMEM_LIMIT)


GROUP_COLS = GROUP * HEAD_DIM


def _group_qT(q_ref):
    t = (q_ref[...].astype(F32) * SCALE).T
    return jnp.concatenate([t[g * HEAD_DIM:(g + 1) * HEAD_DIM] for g in range(GROUP)], axis=1).astype(BF16)


def _group_rows(oT):
    return jnp.concatenate([oT[:, g * Q_BLOCK:(g + 1) * Q_BLOCK] for g in range(GROUP)], axis=0).T


def _q_spec(S, col0):
    nq = S // Q_BLOCK
    return pl.BlockSpec((Q_BLOCK, GROUP_COLS), lambda b, h, n: (b * nq + n, col0 // GROUP_COLS + h))


def _o_spec(S):
    nq = S // Q_BLOCK
    return pl.BlockSpec((Q_BLOCK, GROUP_COLS), lambda b, h, n: (b * nq + n, h))


def _moe_combine(x, y0_ref, y1_ref, route_ref):
    return x + (y0_ref[...] * route_ref[:, MOE_TOPK:MOE_TOPK + 1] + y1_ref[...] * route_ref[:, MOE_TOPK + 1:MOE_TOPK + 2])


def _emit_chunk(ch, actions, outs):
    tm = ch.shape[0]
    chT = None
    for kind, oi, arg in actions:
        o = outs[oi]
        if kind == "tok":
            o[:, arg * LANES:(arg + 1) * LANES] = ch.astype(BF16)
        elif kind == "f32":
            o[...] = ch
        elif kind == "kmean":
            o[...] = jnp.mean(ch.reshape(tm // C_BLOCK, C_BLOCK, LANES), axis=1).reshape(tm // C_BLOCK, 1, LANES)
        elif kind == "heads":
            for hh in range(2):
                o[0, arg + hh] = ch[:, hh * HEAD_DIM:(hh + 1) * HEAD_DIM].astype(BF16)
        elif kind == "keys_aug":
            lane = _iota((tm, LANES), 1)
            block_in_tile = _iota((tm, LANES), 0) >> (arg.bit_length() - 1)
            aug = jnp.where((lane - HEAD_DIM == block_in_tile) | (lane == HEAD_DIM + AUG_ROWS), 1.0, 0.0)
            for hh in range(2):
                keys = ch if hh == 0 else pltpu.roll(ch, HEAD_DIM, 1)
                o[0, hh] = jnp.where(lane < HEAD_DIM, keys, aug).astype(BF16)
        else:
            chT = ch.astype(BF16).astype(F32).T if chT is None else chT
            for hh in range(2):
                vT = chT[hh * HEAD_DIM:(hh + 1) * HEAD_DIM]
                if kind == "headsT":
                    o[0, arg + hh] = vT.astype(BF16)
                else:
                    assert kind == "valsT_aug"
                    tail = jnp.where(_iota((V_AUG_ROWS - HEAD_DIM, tm), 0) == 0, 1.0, 0.0)
                    o[0, hh] = jnp.concatenate([vT, tail], axis=0).astype(BF16)


def _in_proj_kernel(plan, combine, *refs):
    refs = list(refs)
    x_ref = refs.pop(0)
    if combine:
        y0_ref, y1_ref, route_ref = refs.pop(0), refs.pop(0), refs.pop(0)
    g_ref, w_ref, cos_ref, sin_ref = refs[:4]
    outs = refs[4:]
    x = x_ref[...]
    if combine:
        x = _moe_combine(x, y0_ref, y1_ref, route_ref)
        xo_ref = outs.pop(0)
        xo_ref[...] = x
    ms = jnp.mean(x * x, axis=-1, keepdims=True)
    h = (x * lax.rsqrt(ms + NORM_EPS) * g_ref[...]).astype(BF16)
    tm = x.shape[0]
    cos = cos_ref[...]
    sin = sin_ref[...]
    first_half = (_iota((tm, LANES), 1) & (HEAD_DIM - 1)) < HALF
    for c, (rope, actions) in enumerate(plan):
        ch = _dot(h, w_ref[:, c * LANES:(c + 1) * LANES])
        if rope:
            partner = jnp.where(first_half, pltpu.roll(ch, LANES - HALF, 1), pltpu.roll(ch, HALF, 1))
            ch = ch * cos + partner * sin
        _emit_chunk(ch, actions, outs)


def _in_proj(x, ys, gain, w, cos_t, sin_t, plan, out_kinds, B, S):
    T = x.shape[0]
    n_cols = w.shape[1]
    tm = ROW_TILE
    assert tm == KEY_TILE and S % tm == 0 and len(plan) * LANES == n_cols
    tpb = S // tm
    combine = ys is not None
    row = lambda i: (i, 0)
    fixed = lambda i: (0, 0)
    by_seq = lambda i: (i // tpb, 0, i % tpb, 0)
    by_seq_t = lambda i: (i // tpb, 0, 0, i % tpb)
    in_specs = [pl.BlockSpec((tm, D_MODEL), row)]
    args = [x]
    if combine:
        in_specs += [pl.BlockSpec((tm, D_MODEL), row)] * 2 + [pl.BlockSpec((tm, LANES), row)]
        args += list(ys)
    in_specs += [pl.BlockSpec((1, D_MODEL), fixed), pl.BlockSpec((D_MODEL, n_cols), fixed),
                 pl.BlockSpec((tm, LANES), row), pl.BlockSpec((tm, LANES), row)]
    args += [gain.reshape(1, D_MODEL), w, cos_t, sin_t]
    out_shape, out_specs = [], []
    if combine:
        out_shape.append(jax.ShapeDtypeStruct((T, D_MODEL), F32))
        out_specs.append(pl.BlockSpec((tm, D_MODEL), row))
    for kind, size in out_kinds:
        if kind == "tok":
            shape, dtype, spec = (T, size * LANES), BF16, pl.BlockSpec((tm, size * LANES), row)
        elif kind == "f32":
            shape, dtype, spec = (T, LANES), F32, pl.BlockSpec((tm, LANES), row)
        elif kind == "kmean":
            shape, dtype, spec = (T // C_BLOCK, 1, LANES), F32, pl.BlockSpec((tm // C_BLOCK, 1, LANES), lambda i: (i, 0, 0))
        elif kind == "heads":
            shape, dtype, spec = (B, size, S, HEAD_DIM), BF16, pl.BlockSpec((1, size, tm, HEAD_DIM), by_seq)
        elif kind == "keys_aug":
            shape, dtype, spec = (B, size, S, LANES), BF16, pl.BlockSpec((1, size, tm, LANES), by_seq)
        elif kind == "headsT":
            shape, dtype, spec = (B, size, HEAD_DIM, S), BF16, pl.BlockSpec((1, size, HEAD_DIM, tm), by_seq_t)
        else:
            assert kind == "valsT_aug"
            shape, dtype, spec = (B, size, V_AUG_ROWS, S), BF16, pl.BlockSpec((1, size, V_AUG_ROWS, tm), by_seq_t)
        out_shape.append(jax.ShapeDtypeStruct(shape, dtype))
        out_specs.append(spec)
    kern = functools.partial(_in_proj_kernel, plan, combine)
    return pl.pallas_call(kern, grid=(T // tm,), in_specs=in_specs, out_specs=out_specs, out_shape=out_shape,
                          compiler_params=_params(1), name="in_proj")(*args)


def _swa_kernel(q_ref, kp_ref, ko_ref, vp_ref, vo_ref, sink_ref, o_ref):
    n = pl.program_id(2)
    qs = _group_qT(q_ref)
    R = qs.shape[1]
    k = jnp.concatenate([kp_ref[0, 0], ko_ref[0, 0]], axis=0)
    vT = jnp.concatenate([vp_ref[0, 0], vo_ref[0, 0]], axis=1)
    s = _dot(k, qs)
    qpos = _iota((2 * Q_BLOCK, R), 1) & (Q_BLOCK - 1)
    ki = _iota((2 * Q_BLOCK, R), 0)
    d = qpos + Q_BLOCK - ki
    ok = (d >= 0) & (d < A_WINDOW) & ((ki >= Q_BLOCK) | (n > 0))
    s = jnp.where(ok, s, NEG_INF)
    sink = sink_ref[0]
    m = jnp.maximum(jnp.max(s, axis=0, keepdims=True), sink)
    p = jnp.exp(s - m)
    den = jnp.sum(p, axis=0, keepdims=True) + jnp.exp(sink - m)
    o_ref[...] = _group_rows(_dot(vT, p.astype(BF16)) * (1.0 / den)).astype(o_ref.dtype)


def _swa_attention(proj, q_col, k, vT, sinks):
    B, Hkv, S, _ = k.shape
    nb = S // Q_BLOCK
    R = GROUP * Q_BLOCK
    sink_row = jnp.repeat(sinks.astype(F32).reshape(Hkv, GROUP), Q_BLOCK, axis=1).reshape(Hkv, 1, R)
    prev = lambda n: jnp.maximum(n - 1, 0)
    in_specs = [
        _q_spec(S, q_col),
        pl.BlockSpec((1, 1, Q_BLOCK, HEAD_DIM), lambda b, h, n: (b, h, prev(n), 0)),
        pl.BlockSpec((1, 1, Q_BLOCK, HEAD_DIM), lambda b, h, n: (b, h, n, 0)),
        pl.BlockSpec((1, 1, HEAD_DIM, Q_BLOCK), lambda b, h, n: (b, h, 0, prev(n))),
        pl.BlockSpec((1, 1, HEAD_DIM, Q_BLOCK), lambda b, h, n: (b, h, 0, n)),
        pl.BlockSpec((1, 1, R), lambda b, h, n: (h, 0, 0)),
    ]
    return pl.pallas_call(
        _swa_kernel, grid=(B, Hkv, nb), in_specs=in_specs,
        out_specs=_o_spec(S), out_shape=jax.ShapeDtypeStruct((B * S, Hkv * GROUP_COLS), BF16),
        compiler_params=_params(3), name="swa_attention")(proj, k, k, vT, vT, sink_row)


def _stick_kernel(q_ref, kT_ref, vT_ref, o_ref):
    n = pl.program_id(2)
    tq = tk = Q_BLOCK
    heads = q_ref.shape[1]
    row = _iota((tq, tk), 0)
    col = _iota((tk, tk), 1)
    upper = jnp.where(_iota((tk, tk), 0) > col, 1.0, 0.0).astype(BF16)
    tpos = n * tq + row

    def body(carry):
        kb, _, cs, accs = carry
        start = pl.multiple_of(kb * tk, tk)
        past = (start + col) < tpos
        hs = range(heads)
        zs = [_dot(q_ref[0, h], kT_ref[0, h, :, pl.ds(start, tk)]) * SCALE for h in hs]
        sps = [jnp.maximum(z, 0.0) + jnp.log1p(jnp.exp(-jnp.abs(z))) for z in zs]
        stays = [jnp.where(past, -sp, 0.0) for sp in sps]
        his = [st.astype(BF16) for st in stays]
        los = [(st - hi.astype(F32)).astype(BF16) for st, hi in zip(stays, his)]
        betweens = [_dot(hi, upper) + _dot(lo, upper) for hi, lo in zip(his, los)]
        ws = [jnp.where(past, jnp.exp(zs[h] - sps[h] + betweens[h] + cs[h]), 0.0).astype(BF16) for h in hs]
        new_accs = [accs[h] + lax.dot_general(ws[h], vT_ref[0, h, :, pl.ds(start, tk)], (((1,), (1,)), ((), ())),
                                              preferred_element_type=F32) for h in hs]
        new_cs = [cs[h] + jnp.sum(stays[h], axis=-1, keepdims=True) for h in hs]
        worst = functools.reduce(jnp.maximum, new_cs)
        return kb - 1, jnp.max(worst) > SB_EXIT, tuple(new_cs), tuple(new_accs)

    def cond(carry):
        kb, alive, _, _ = carry
        return (kb >= 0) & alive

    init = (n, jnp.array(True), (jnp.zeros((tq, 1), F32),) * heads, (jnp.zeros((tq, HEAD_DIM), F32),) * heads)
    _, _, _, accs = lax.while_loop(cond, body, init)
    o_ref[...] = jnp.concatenate(accs, axis=1).astype(o_ref.dtype)


def _stick_attention(q, kT, v):
    B, H, S, _ = q.shape
    nq = S // Q_BLOCK
    hb = STICK_HEADS
    resident = dict(pipeline_mode=pl.Buffered(1))
    in_specs = [
        pl.BlockSpec((1, hb, Q_BLOCK, HEAD_DIM), lambda b, h, n: (b, h, n, 0)),
        pl.BlockSpec((1, hb, HEAD_DIM, S), lambda b, h, n: (b, h, 0, 0), **resident),
        pl.BlockSpec((1, hb, HEAD_DIM, S), lambda b, h, n: (b, h, 0, 0), **resident),
    ]
    return pl.pallas_call(
        _stick_kernel, grid=(B, H // hb, nq), in_specs=in_specs,
        out_specs=pl.BlockSpec((Q_BLOCK, hb * HEAD_DIM), lambda b, h, n: (b * nq + n, h)),
        out_shape=jax.ShapeDtypeStruct((B * S, H * HEAD_DIM), BF16),
        compiler_params=_params(3), name="stick_attention")(q, kT, v)


def _max_key_sqnorm(ka_ref, out_ref):
    S = ka_ref.shape[2]
    tk = KEY_TILE
    is_key_lane = _iota((tk, LANES), 1) < HEAD_DIM
    ones = jnp.ones((LANES, LANES), BF16)

    def body(i, mx):
        k = jnp.where(is_key_lane, ka_ref[0, 0, pl.ds(pl.multiple_of(i * tk, tk), tk), :].astype(F32), 0.0)
        sq = _dot((k * k).astype(BF16), ones)
        return jnp.maximum(mx, jnp.max(sq, axis=0, keepdims=True))

    mx = lax.fori_loop(0, S // tk, body, jnp.zeros((1, LANES), F32))
    out_ref[...] = jnp.broadcast_to(mx, out_ref.shape)


def _flash_scratch(R):
    return [pltpu.VMEM((AUG_ROWS, LANES), F32), pltpu.VMEM((2, KEY_TILE, R), BF16), pltpu.VMEM((V_AUG_ROWS, R), F32)]


def _masked_flash_t(n, qT, bias_rows, ka_ref, vTa_ref, ksq_ref, p_scr, acc_scr):
    R = qT.shape[1]
    tk = KEY_TILE
    diag = (n * Q_BLOCK) // tk
    zpad = jnp.zeros((LANES - HEAD_DIM - 2 * AUG_ROWS, R), F32)
    causal = diag * tk + _iota((tk, R), 0) <= n * Q_BLOCK + (_iota((tk, R), 1) & (Q_BLOCK - 1))

    def scores(kt, ref_rows):
        st = pl.multiple_of(kt * tk, tk)
        low = jnp.concatenate([bias_rows(kt), ref_rows, zpad], axis=0).astype(BF16)
        return _dot(ka_ref[0, 0, pl.ds(st, tk), :], jnp.concatenate([qT, low], axis=0))

    def values(kt):
        return vTa_ref[0, 0, :, pl.ds(pl.multiple_of(kt * tk, tk), tk)]

    qsq = jnp.sum(jnp.square(qT.astype(F32)), axis=0, keepdims=True)
    ksq = jnp.concatenate([ksq_ref[0:1, :]] * (R // LANES), axis=1)
    ref = jnp.sqrt(qsq * ksq) * REF_MARGIN
    ref_rows = jnp.where(_iota((AUG_ROWS, R), 0) == 0, -ref, 0.0)

    def fast_body(kt, carry):
        slot = kt & 1
        s = scores(kt, ref_rows)
        acc_scr[...] += _dot(values(jnp.maximum(kt - 1, 0)), p_scr[1 - slot])
        p_scr[slot] = jnp.exp(s).astype(BF16)
        return carry

    p_scr[1] = jnp.zeros((tk, R), BF16)
    acc_scr[...] = jnp.zeros(acc_scr.shape, F32)
    lax.fori_loop(0, diag, fast_body, 0)
    acc = acc_scr[...] + _dot(values(jnp.maximum(diag - 1, 0)), p_scr[1 - (diag & 1)])
    s = jnp.where(causal, scores(diag, ref_rows), MASKED)
    acc = acc + _dot(values(diag), jnp.exp(s).astype(BF16))

    def running_max_path(_):
        no_ref = jnp.zeros((AUG_ROWS, R), F32)

        def update(kt, s, m, acc):
            m_new = jnp.maximum(m, jnp.max(s, axis=0, keepdims=True))
            p = jnp.exp(s - m_new).astype(BF16)
            return m_new, jnp.exp(m - m_new) * acc + _dot(values(kt), p)

        init = (jnp.full((1, R), MASKED, F32), jnp.zeros((vTa_ref.shape[2], R), F32))
        m, acc = lax.fori_loop(0, diag, lambda kt, c: update(kt, scores(kt, no_ref), *c), init)
        return update(diag, jnp.where(causal, scores(diag, no_ref), MASKED), m, acc)[1]

    healthy = jnp.min(acc[HEAD_DIM:HEAD_DIM + 1, :]) >= FLASH_MIN_SUM
    acc = lax.cond(healthy, lambda _: acc, running_max_path, 0)
    return acc[0:HEAD_DIM] * (1.0 / acc[HEAD_DIM:HEAD_DIM + 1])


def _moba_kernel(q_ref, ka_ref, vTa_ref, km_ref, o_ref, bias_scr, ksq_scr, p_scr, acc_scr):
    n = pl.program_id(2)
    pl.when(n == 0)(lambda: _max_key_sqnorm(ka_ref, ksq_scr))
    nblk = km_ref.shape[2]
    qs = _group_qT(q_ref)
    R = qs.shape[1]
    own = (n * Q_BLOCK) // C_BLOCK
    gate = _dot(km_ref[0, 0], qs)
    blk = _iota((nblk, R), 0)
    gate = jnp.where(blk < own, gate, NEG_INF)
    bias = jnp.where(blk == own, 0.0, -MASK_BIG)
    for _ in range(C_TOPK):
        mx = jnp.max(gate, axis=0, keepdims=True)
        idx = jnp.min(jnp.where(gate == mx, blk, nblk), axis=0, keepdims=True)
        hit = blk == idx
        bias = jnp.where(hit & (mx > NEG_INF), 0.0, bias)
        gate = jnp.where(hit, NEG_INF, gate)
    per = KEY_TILE // C_BLOCK
    rows = bias_scr.shape[0]
    r = _iota((rows, nblk), 0)
    spread = jnp.where(((r & (AUG_ROWS - 1)) < per) & (_iota((rows, nblk), 1) == (r >> 3) * per + (r & (AUG_ROWS - 1))), 1.0, 0.0)
    bias_scr[...] = _dot(spread.astype(BF16), bias.astype(BF16))
    oT = _masked_flash_t(n, qs, lambda kt: bias_scr[pl.ds(pl.multiple_of(kt * AUG_ROWS, AUG_ROWS), AUG_ROWS), :],
                         ka_ref, vTa_ref, ksq_scr, p_scr, acc_scr)
    o_ref[...] = _group_rows(oT).astype(o_ref.dtype)


def _moba_attention(proj, q_col, ka, vTa, km):
    B, Hkv, S, _ = ka.shape
    nq = S // Q_BLOCK
    R = GROUP * Q_BLOCK
    nblk = km.shape[2]
    nkt = S // KEY_TILE
    in_specs = [
        _q_spec(S, q_col),
        pl.BlockSpec((1, 1, S, LANES), lambda b, h, n: (b, h, 0, 0)),
        pl.BlockSpec((1, 1, V_AUG_ROWS, S), lambda b, h, n: (b, h, 0, 0)),
        pl.BlockSpec((1, 1, nblk, HEAD_DIM), lambda b, h, n: (b, h, 0, 0)),
    ]
    return pl.pallas_call(
        _moba_kernel, grid=(B, Hkv, nq), in_specs=in_specs,
        out_specs=_o_spec(S), out_shape=jax.ShapeDtypeStruct((B * S, Hkv * GROUP_COLS), BF16),
        scratch_shapes=[pltpu.VMEM((nkt * AUG_ROWS, R), F32)] + _flash_scratch(R),
        compiler_params=_params(3), name="moba_attention")(proj, ka, vTa, km)


def _compress_kernel(u_ref, us_ref, pe_ref, w1_ref, w2_ref, o_ref):
    a = (u_ref[0, 0].astype(F32) + pe_ref[0:1, :]).astype(BF16)
    b = (us_ref[0, 0].astype(F32) + pe_ref[1:2, :]).astype(BF16)
    pre = _dot(a, w1_ref[0]) + _dot(b, w1_ref[1])
    hid = jax.nn.gelu(pre)
    o_ref[0, 0] = _dot(hid.astype(BF16), w2_ref[...]).astype(o_ref.dtype)


def _compress(t, pe, w1, w2):
    B, H, S, _ = t.shape
    nrow = S // D_CMP_STRIDE
    width = D_CMP_STRIDE * HEAD_DIM
    u = t.reshape(B, H, nrow, width)
    us = jnp.concatenate([u[:, :, 1:], jnp.zeros((B, H, 1, width), u.dtype)], axis=2)
    blk = lambda b, h: (b, h, 0, 0)
    in_specs = [
        pl.BlockSpec((1, 1, nrow, width), blk), pl.BlockSpec((1, 1, nrow, width), blk),
        pl.BlockSpec((2, width), lambda b, h: (0, 0)),
        pl.BlockSpec((2, width, D_CMP_HIDDEN), lambda b, h: (0, 0, 0)),
        pl.BlockSpec((D_CMP_HIDDEN, HEAD_DIM), lambda b, h: (0, 0)),
    ]
    return pl.pallas_call(
        _compress_kernel, grid=(B, H), in_specs=in_specs,
        out_specs=pl.BlockSpec((1, 1, nrow, HEAD_DIM), blk),
        out_shape=jax.ShapeDtypeStruct((B, H, nrow, HEAD_DIM), BF16),
        compiler_params=_params(2), name="nsa_compress")(
            u, us, pe.astype(F32).reshape(2, width), w1.astype(BF16).reshape(2, width, D_CMP_HIDDEN), w2.astype(BF16))


def _cmp_select_kernel(nc, q_ref, kc_ref, vcT_ref, ovT_ref, oc_ref, bias_ref):
    n = pl.program_id(2)
    ncp = kc_ref.shape[2]
    nsel = ovT_ref.shape[0]
    qs = _group_qT(q_ref)
    R = qs.shape[1]
    s = _dot(kc_ref[0, 0], qs)
    tpos = n * Q_BLOCK + (_iota((ncp, R), 1) & (Q_BLOCK - 1))
    c = _iota((ncp, R), 0)
    ok = (c * D_CMP_STRIDE + (D_CMP_LEN - 1) <= tpos) & (c < nc)
    s = jnp.where(ok, s, NEG_INF)
    m = jnp.max(s, axis=0, keepdims=True)
    m = jnp.where(m > NEG_INF, m, 0.0)
    e = jnp.exp(s - m)
    den = jnp.sum(e, axis=0, keepdims=True)
    p = (e * (1.0 / jnp.where(den > 0, den, 1.0))).astype(BF16)
    oc_ref[...] = _group_rows(_dot(vcT_ref[0, 0], p))
    imp_heads = _dot(ovT_ref[...], p)
    imp = imp_heads[:, 0:Q_BLOCK]
    for g in range(1, GROUP):
        imp = imp + imp_heads[:, g * Q_BLOCK:(g + 1) * Q_BLOCK]
    t = n * Q_BLOCK + _iota((nsel, Q_BLOCK), 1)
    j = _iota((nsel, Q_BLOCK), 0)
    cur = t >> 6
    allowed = j * D_SEL_LEN <= t
    forced = (j == 0) | (j == cur) | (j == cur - 1)
    imp = jnp.where(allowed, imp, NEG_INF)
    imp = jnp.where(allowed & forced, float("inf"), imp)
    bias = jnp.full((nsel, Q_BLOCK), -MASK_BIG, F32)
    for _ in range(D_SEL_TOPK):
        mx = jnp.max(imp, axis=0, keepdims=True)
        idx = jnp.min(jnp.where(imp == mx, j, nsel), axis=0, keepdims=True)
        hit = j == idx
        bias = jnp.where(hit & (mx > NEG_INF), 0.0, bias)
        imp = jnp.where(hit, NEG_INF, imp)
    bias_ref[0, 0, 0] = bias


def _cmp_select(proj, q_col, kc, vcT, overlapT, nc, S):
    B, Hkv, ncp, _ = kc.shape
    nq = S // Q_BLOCK
    nsel = overlapT.shape[0]
    in_specs = [
        _q_spec(S, q_col),
        pl.BlockSpec((1, 1, ncp, HEAD_DIM), lambda b, h, n: (b, h, 0, 0)),
        pl.BlockSpec((1, 1, HEAD_DIM, ncp), lambda b, h, n: (b, h, 0, 0)),
        pl.BlockSpec((nsel, ncp), lambda b, h, n: (0, 0)),
    ]
    out_specs = [_o_spec(S), pl.BlockSpec((1, 1, 1, nsel, Q_BLOCK), lambda b, h, n: (b, h, n, 0, 0))]
    out_shape = [jax.ShapeDtypeStruct((B * S, Hkv * GROUP_COLS), F32), jax.ShapeDtypeStruct((B, Hkv, nq, nsel, Q_BLOCK), F32)]
    return pl.pallas_call(
        functools.partial(_cmp_select_kernel, nc), grid=(B, Hkv, nq), in_specs=in_specs, out_specs=out_specs,
        out_shape=out_shape, compiler_params=_params(3), name="nsa_cmp_select")(proj, kc, vcT, overlapT)


def _sel_kernel(q_ref, ka_ref, vTa_ref, bias_ref, o_ref, ksq_scr, p_scr, acc_scr):
    n = pl.program_id(2)
    pl.when(n == 0)(lambda: _max_key_sqnorm(ka_ref, ksq_scr))
    qs = _group_qT(q_ref)

    def bias_rows(kt):
        b = bias_ref[0, 0, 0, pl.ds(pl.multiple_of(kt * AUG_ROWS, AUG_ROWS), AUG_ROWS), :]
        return jnp.concatenate([b] * GROUP, axis=1)

    o_ref[...] = _group_rows(_masked_flash_t(n, qs, bias_rows, ka_ref, vTa_ref, ksq_scr, p_scr, acc_scr))


def _sel_attention(proj, q_col, ka, vTa, biasT):
    B, Hkv, S, _ = ka.shape
    nq = S // Q_BLOCK
    nsel = biasT.shape[3]
    assert KEY_TILE // D_SEL_LEN == AUG_ROWS
    in_specs = [
        _q_spec(S, q_col),
        pl.BlockSpec((1, 1, S, LANES), lambda b, h, n: (b, h, 0, 0)),
        pl.BlockSpec((1, 1, V_AUG_ROWS, S), lambda b, h, n: (b, h, 0, 0)),
        pl.BlockSpec((1, 1, 1, nsel, Q_BLOCK), lambda b, h, n: (b, h, n, 0, 0)),
    ]
    return pl.pallas_call(
        _sel_kernel, grid=(B, Hkv, nq), in_specs=in_specs,
        out_specs=_o_spec(S), out_shape=jax.ShapeDtypeStruct((B * S, Hkv * GROUP_COLS), F32),
        scratch_shapes=_flash_scratch(GROUP * Q_BLOCK),
        compiler_params=_params(3), name="nsa_selected")(proj, ka, vTa, biasT)


def _win_kernel(span, q_ref, k_ref, vT_ref, o_ref):
    n = pl.program_id(2)
    qs = _group_qT(q_ref)
    R = qs.shape[1]
    start = pl.multiple_of(jnp.maximum(n * Q_BLOCK + Q_BLOCK - span, 0), Q_BLOCK)
    s = _dot(k_ref[0, 0, pl.ds(start, span), :], qs)
    tpos = n * Q_BLOCK + (_iota((span, R), 1) & (Q_BLOCK - 1))
    d = tpos - (start + _iota((span, R), 0))
    s = jnp.where((d >= 0) & (d < D_WINDOW), s, NEG_INF)
    m = jnp.max(s, axis=0, keepdims=True)
    p = jnp.exp(s - m)
    l = jnp.sum(p, axis=0, keepdims=True)
    o_ref[...] = _group_rows(_dot(vT_ref[0, 0, :, pl.ds(start, span)], p.astype(BF16)) * (1.0 / l))


def _win_attention(proj, q_col, k, vT):
    B, Hkv, S, _ = k.shape
    nq = S // Q_BLOCK
    span = min(D_WINDOW + Q_BLOCK, S)
    in_specs = [
        _q_spec(S, q_col),
        pl.BlockSpec((1, 1, S, HEAD_DIM), lambda b, h, n: (b, h, 0, 0)),
        pl.BlockSpec((1, 1, HEAD_DIM, S), lambda b, h, n: (b, h, 0, 0)),
    ]
    return pl.pallas_call(
        functools.partial(_win_kernel, span), grid=(B, Hkv, nq), in_specs=in_specs,
        out_specs=_o_spec(S), out_shape=jax.ShapeDtypeStruct((B * S, Hkv * GROUP_COLS), F32),
        compiler_params=_params(3), name="nsa_window")(proj, k, vT)


def _route(logits):
    tm = logits.shape[0]
    lane = _iota((tm, LANES), 1)
    gl = jnp.where(lane < N_GROUPS, logits, NEG_INF)
    gmax = jnp.max(gl, axis=-1, keepdims=True)
    gidx = jnp.min(jnp.where(gl == gmax, lane, LANES), axis=-1, keepdims=True)
    g_prob = 1.0 / jnp.sum(jnp.exp(gl - gmax), axis=-1, keepdims=True)
    elane = lane - N_GROUPS
    in_group = (elane >= 0) & (elane < N_EXPERTS) & ((elane >> 4) == gidx)
    el = jnp.where(in_group, logits, NEG_INF)
    ee = jnp.exp(el - jnp.max(el, axis=-1, keepdims=True))
    ep = jnp.where(in_group, ee / jnp.sum(ee, axis=-1, keepdims=True), -1.0)
    p1 = jnp.max(ep, axis=-1, keepdims=True)
    i1 = jnp.min(jnp.where(ep == p1, lane, LANES), axis=-1, keepdims=True)
    ep2 = jnp.where(lane == i1, -1.0, ep)
    p2 = jnp.max(ep2, axis=-1, keepdims=True)
    i2 = jnp.min(jnp.where(ep2 == p2, lane, LANES), axis=-1, keepdims=True)
    den = p1 + p2
    vals = [(i1 - N_GROUPS).astype(F32), (i2 - N_GROUPS).astype(F32), g_prob * p1 / den, g_prob * p2 / den]
    out = jnp.zeros((tm, LANES), F32)
    for k, val in enumerate(vals):
        out = jnp.where(lane == k, val, out)
    return out


def _pack_bf16_pairs(h):
    n = h.shape[1] // 2
    bits = lax.bitcast_convert_type(h.astype(BF16).astype(F32), jnp.uint32)
    return bits[:, :n] | (bits[:, n:] >> 16)


def _unpack_bf16_pairs(packed):
    hi = lax.bitcast_convert_type(packed & jnp.uint32(0xFFFF0000), F32)
    lo = lax.bitcast_convert_type(packed << 16, F32)
    return jnp.concatenate([hi, lo], axis=1).astype(BF16)


def _out_tail(x_new, gain_ref, wr_ref, xo_ref, h_ref, route_ref):
    xo_ref[...] = x_new
    ms = jnp.mean(x_new * x_new, axis=-1, keepdims=True)
    h = x_new * lax.rsqrt(ms + NORM_EPS) * gain_ref[...]
    h_ref[...] = _pack_bf16_pairs(h)
    h_hi = h.astype(BF16)
    h_lo = (h - h_hi.astype(F32)).astype(BF16)
    logits = _dot(h_hi, wr_ref[0]) + (_dot(h_lo, wr_ref[0]) + _dot(h_hi, wr_ref[1]))
    route_ref[...] = _route(logits)


def _out_proj_kernel(oa_ref, ob_ref, x_ref, w_ref, gain_ref, wr_ref, xo_ref, h_ref, route_ref):
    half = w_ref.shape[0] // 2
    x_new = x_ref[...] + _dot(oa_ref[...], w_ref[0:half, :]) + _dot(ob_ref[...], w_ref[half:, :])
    _out_tail(x_new, gain_ref, wr_ref, xo_ref, h_ref, route_ref)


def _out_proj_nsa_kernel(oc_ref, b0_ref, b1_ref, b2_ref, gd_ref, x_ref, w_ref, gain_ref, wr_ref, xo_ref, h_ref, route_ref):
    half = D_HEADS * HEAD_DIM
    g = jax.nn.sigmoid(gd_ref[...])
    g_hi = g.astype(BF16)
    g_lo = (g - g_hi.astype(F32)).astype(BF16)
    src = _iota((LANES, half), 0)
    head3 = (_iota((LANES, half), 1) >> 6) * 3
    od = None
    for br, b_ref in enumerate((b0_ref, b1_ref, b2_ref)):
        spread = jnp.where(src == head3 + br, 1.0, 0.0).astype(BF16)
        term = (_dot(g_hi, spread) + _dot(g_lo, spread)) * b_ref[...]
        od = term if od is None else od + term
    x_new = x_ref[...] + _dot(oc_ref[...], w_ref[0:half, :]) + _dot(od.astype(BF16), w_ref[half:, :])
    _out_tail(x_new, gain_ref, wr_ref, xo_ref, h_ref, route_ref)


def _out_proj(o_parts, gd, x, w_out, gain, w_router):
    T = x.shape[0]
    tm = ROW_TILE
    row = lambda i: (i, 0)
    fixed = lambda i: (0, 0)
    if gd is None:
        kern = _out_proj_kernel
        args = list(o_parts)
        in_specs = [pl.BlockSpec((tm, D_MODEL // 2), row)] * 2
    else:
        kern = _out_proj_nsa_kernel
        args = list(o_parts) + [gd]
        in_specs = [pl.BlockSpec((tm, D_MODEL // 2), row)] * 4 + [pl.BlockSpec((tm, LANES), row)]
    args += [x, w_out, gain.reshape(1, D_MODEL), w_router]
    in_specs += [pl.BlockSpec((tm, D_MODEL), row), pl.BlockSpec((D_MODEL, D_MODEL), fixed),
                 pl.BlockSpec((1, D_MODEL), fixed), pl.BlockSpec((2, D_MODEL, LANES), lambda i: (0, 0, 0))]
    out_shape = [jax.ShapeDtypeStruct((T, D_MODEL), F32), jax.ShapeDtypeStruct((T, D_MODEL // 2), jnp.uint32),
                 jax.ShapeDtypeStruct((T, LANES), F32)]
    out_specs = [pl.BlockSpec((tm, D_MODEL), row), pl.BlockSpec((tm, D_MODEL // 2), row), pl.BlockSpec((tm, LANES), row)]
    return pl.pallas_call(kern, grid=(T // tm,), in_specs=in_specs, out_specs=out_specs, out_shape=out_shape,
                          compiler_params=_params(1), name="out_proj_router")(*args)


def _expert_kernel(be_ref, rows_ref, wg_ref, wu_ref, wd_ref, y_ref, wg_s, wu_s, wd_s):
    i = pl.program_id(0)
    n_blk = pl.num_programs(0)

    @pl.when((i == 0) | (be_ref[i] != be_ref[jnp.maximum(i - 1, 0)]))
    def _():
        wg_s[...] = wg_ref[0].astype(BF16)
        wu_s[...] = wu_ref[0].astype(BF16)
        wd_s[...] = wd_ref[0].astype(BF16)

    @pl.when(i < be_ref[n_blk])
    def _():
        xb = _unpack_bf16_pairs(rows_ref[...])
        hid = jax.nn.silu(_dot(xb, wg_s[...])) * _dot(xb, wu_s[...])
        y_ref[...] = _dot(hid.astype(BF16), wd_s[...])

    @pl.when(i >= be_ref[n_blk])
    def _():
        y_ref[...] = jnp.zeros(y_ref.shape, F32)


def _expert_ffn(rows, blk_info, wg, wu, wd):
    n_rows = rows.shape[0]
    n_blk = n_rows // MOE_ROWS
    grid_spec = pltpu.PrefetchScalarGridSpec(
        num_scalar_prefetch=1, grid=(n_blk,),
        in_specs=[
            pl.BlockSpec((MOE_ROWS, D_MODEL // 2), lambda i, be: (i, 0)),
            pl.BlockSpec((1, D_MODEL, EXPERT_HIDDEN), lambda i, be: (be[i], 0, 0)),
            pl.BlockSpec((1, D_MODEL, EXPERT_HIDDEN), lambda i, be: (be[i], 0, 0)),
            pl.BlockSpec((1, EXPERT_HIDDEN, D_MODEL), lambda i, be: (be[i], 0, 0)),
        ],
        out_specs=pl.BlockSpec((MOE_ROWS, D_MODEL), lambda i, be: (i, 0)),
        scratch_shapes=[pltpu.VMEM((D_MODEL, EXPERT_HIDDEN), BF16), pltpu.VMEM((D_MODEL, EXPERT_HIDDEN), BF16),
                        pltpu.VMEM((EXPERT_HIDDEN, D_MODEL), BF16)])
    return pl.pallas_call(
        _expert_kernel, grid_spec=grid_spec, out_shape=jax.ShapeDtypeStruct((n_rows, D_MODEL), F32),
        compiler_params=_params(1), name="expert_ffn")(blk_info, rows, wg, wu, wd)


def _moe_dispatch(route, h):
    n_tok = h.shape[0]
    n_asg = n_tok * MOE_TOPK
    i32 = jnp.int32
    e_flat = route[:, 0:MOE_TOPK].astype(i32).reshape(n_asg)
    is_e = e_flat[:, None] == jnp.arange(N_EXPERTS, dtype=i32)[None, :]
    counts = jnp.sum(is_e, axis=0, dtype=i32)
    order = jnp.argsort(e_flat).astype(i32)
    rank = jnp.argsort(order).astype(i32)
    padded = (counts + MOE_ROWS - 1) // MOE_ROWS * MOE_ROWS
    pad_end = jnp.cumsum(padded)
    pad_start = pad_end - padded
    start = jnp.cumsum(counts) - counts
    n_rows = n_asg + N_EXPERTS * MOE_ROWS
    n_blk = n_rows // MOE_ROWS
    blk_start = jnp.arange(n_blk, dtype=i32) * MOE_ROWS
    blk_expert = jnp.minimum(jnp.sum(pad_end[None, :] <= blk_start[:, None], axis=1, dtype=i32), N_EXPERTS - 1)
    within = (blk_start - pad_start[blk_expert])[:, None] + jnp.arange(MOE_ROWS, dtype=i32)[None, :]
    valid = within < counts[blk_expert][:, None]
    src = jnp.clip(start[blk_expert][:, None] + within, 0, n_asg - 1)
    row_tok = jnp.where(valid, (order // MOE_TOPK)[src], src // MOE_TOPK).reshape(n_rows)
    shift = jnp.sum(jnp.where(is_e, (pad_start - start)[None, :], 0), axis=1, dtype=i32)
    pos = (rank + shift).reshape(n_tok, MOE_TOPK)
    blk_info = jnp.concatenate([blk_expert, (pad_end[-1:] // MOE_ROWS).astype(i32)])
    return h[row_tok], blk_info, pos


def _moe(route, h, wg, wu, wd):
    rows, blk_info, pos = _moe_dispatch(route, h)
    y = _expert_ffn(rows, blk_info, wg, wu, wd)
    return y[pos[:, 0]], y[pos[:, 1]], route


def _final_kernel(x_ref, y0_ref, y1_ref, route_ref, g_ref, o_ref):
    x = _moe_combine(x_ref[...], y0_ref, y1_ref, route_ref)
    ms = jnp.mean(x * x, axis=-1, keepdims=True)
    o_ref[...] = x * lax.rsqrt(ms + NORM_EPS) * g_ref[...]


def _final_norm(x, ys, gain):
    T = x.shape[0]
    tm = ROW_TILE
    row = lambda i: (i, 0)
    return pl.pallas_call(
        _final_kernel, grid=(T // tm,),
        in_specs=[pl.BlockSpec((tm, D_MODEL), row)] * 3 + [pl.BlockSpec((tm, LANES), row), pl.BlockSpec((1, D_MODEL), lambda i: (0, 0))],
        out_specs=pl.BlockSpec((tm, D_MODEL), row), out_shape=jax.ShapeDtypeStruct((T, D_MODEL), F32),
        compiler_params=_params(1), name="final_norm")(x, *ys, gain.reshape(1, D_MODEL))


def _rope_tables(positions):
    inv_freq = ROPE_THETA ** (-jnp.arange(0, HEAD_DIM, 2, dtype=F32) / HEAD_DIM)
    ang = positions.astype(F32).reshape(-1, 1) * inv_freq
    cos, sin = jnp.cos(ang), jnp.sin(ang)
    reps = LANES // HEAD_DIM
    return jnp.tile(jnp.concatenate([cos, cos], axis=1), (1, reps)), jnp.tile(jnp.concatenate([-sin, sin], axis=1), (1, reps))


def _chunks(first, count, rope, action):
    return tuple((first + j, rope, action(j)) for j in range(count))


def _plan(*runs):
    entries = sum(runs, ())
    assert [e[0] for e in entries] == list(range(len(entries)))
    return tuple(e[1:] for e in entries)


AB_OUTS = (("tok", 4), ("heads", A_KV_HEADS), ("headsT", A_KV_HEADS), ("heads", B_HEADS), ("headsT", B_HEADS), ("headsT", B_HEADS))
AB_PLAN = _plan(_chunks(0, 4, True, lambda j: (("tok", 0, j),)),
                _chunks(4, 1, True, lambda j: (("heads", 1, 0),)),
                _chunks(5, 1, False, lambda j: (("headsT", 2, 0),)),
                _chunks(6, 4, False, lambda j: (("heads", 3, 2 * j),)),
                _chunks(10, 4, False, lambda j: (("headsT", 4, 2 * j),)),
                _chunks(14, 4, False, lambda j: (("headsT", 5, 2 * j),)))
CD_OUTS = (("tok", 8), ("keys_aug", C_KV_HEADS), ("kmean", 0), ("valsT_aug", C_KV_HEADS), ("heads", D_KV_HEADS),
           ("heads", D_KV_HEADS), ("keys_aug", D_KV_HEADS), ("valsT_aug", D_KV_HEADS), ("heads", D_KV_HEADS),
           ("headsT", D_KV_HEADS), ("f32", 0))
CD_PLAN = _plan(_chunks(0, 4, True, lambda j: (("tok", 0, j),)),
                _chunks(4, 1, True, lambda j: (("keys_aug", 1, C_BLOCK), ("kmean", 2, 0))),
                _chunks(5, 1, False, lambda j: (("valsT_aug", 3, 0),)),
                _chunks(6, 4, True, lambda j: (("tok", 0, 4 + j),)),
                _chunks(10, 1, True, lambda j: (("heads", 4, 0),)),
                _chunks(11, 1, False, lambda j: (("heads", 5, 0),)),
                _chunks(12, 1, True, lambda j: (("keys_aug", 6, D_SEL_LEN),)),
                _chunks(13, 1, False, lambda j: (("valsT_aug", 7, 0),)),
                _chunks(14, 1, True, lambda j: (("heads", 8, 0),)),
                _chunks(15, 1, False, lambda j: (("headsT", 9, 0),)),
                _chunks(16, 1, False, lambda j: (("f32", 10, 0),)))
QD_COL = 4 * LANES


def _router_weights(router_group, router_expert):
    pad = jnp.zeros((D_MODEL, LANES - N_GROUPS - N_EXPERTS), F32)
    w = jnp.concatenate([router_group.astype(F32), router_expert.astype(F32), pad], axis=1)
    hi = w.astype(BF16)
    return jnp.stack([hi, (w - hi.astype(F32)).astype(BF16)])


def _pad_cols(w, n):
    return jnp.concatenate([w, jnp.zeros((w.shape[0], n - w.shape[1]), w.dtype)], axis=1)


def _mixer_ab(parts, sinks):
    qa, ka, vaT, qb, kbT, vbT = parts
    oa = _swa_attention(qa, 0, ka, vaT, sinks)
    ob = _stick_attention(qb, kbT, vbT)
    return oa, ob


def _mixer_cd(parts, B, S, k_pe, k_w1, k_w2, v_pe, v_w1, v_w2):
    q_cd, kc_aug, kmean, vcT_aug, kdc, vdc, kds_aug, vdsT_aug, kdw, vdwT, _ = parts
    nblk = S // C_BLOCK
    km = kmean.reshape(B, nblk, C_KV_HEADS, HEAD_DIM).transpose(0, 2, 1, 3).astype(BF16)
    oc = _moba_attention(q_cd, 0, kc_aug, vcT_aug, km)
    k_cmp = _compress(kdc, k_pe, k_w1, k_w2)
    v_cmp = _compress(vdc, v_pe, v_w1, v_w2)
    nc = (S - D_CMP_LEN) // D_CMP_STRIDE + 1
    ncp = S // D_CMP_STRIDE
    nsel = S // D_SEL_LEN
    c_start = jnp.arange(ncp) * D_CMP_STRIDE
    b_start = jnp.arange(nsel) * D_SEL_LEN
    overlap = ((c_start[:, None] <= b_start[None, :] + D_SEL_LEN - 1) & (c_start[:, None] + D_CMP_LEN - 1 >= b_start[None, :])
               & (jnp.arange(ncp)[:, None] < nc)).astype(BF16)
    o_cmp, biasT = _cmp_select(q_cd, QD_COL, k_cmp, v_cmp.transpose(0, 1, 3, 2), overlap.T, nc, S)
    o_sel = _sel_attention(q_cd, QD_COL, kds_aug, vdsT_aug, biasT)
    o_win = _win_attention(q_cd, QD_COL, kdw, vdwT)
    return oc, o_cmp, o_sel, o_win


def kernel(x, positions, ln_mix_0, w_in_0, sinks_0, w_out_0, ln_ffn_0, router_group_0, router_expert_0, expert_gate_0, expert_up_0, expert_down_0, ln_mix_1, w_in_1, cmp_k_pe_1, cmp_k_w1_1, cmp_k_w2_1, cmp_v_pe_1, cmp_v_w1_1, cmp_v_w2_1, w_out_1, ln_ffn_1, router_group_1, router_expert_1, expert_gate_1, expert_up_1, expert_down_1, ln_final):
    B, S, _ = x.shape
    T = B * S
    assert S % KEY_TILE == 0 and T % ROW_TILE == 0
    cos_t, sin_t = _rope_tables(positions)
    xf = x.reshape(T, D_MODEL)

    parts = _in_proj(xf, None, ln_mix_0, w_in_0.astype(BF16), cos_t, sin_t, AB_PLAN, AB_OUTS, B, S)
    o_ab = _mixer_ab(parts, sinks_0)
    x1, h1, route1 = _out_proj(o_ab, None, xf, w_out_0.astype(BF16), ln_ffn_0, _router_weights(router_group_0, router_expert_0))
    ys = _moe(route1, h1, expert_gate_0, expert_up_0, expert_down_0)

    x2, *parts = _in_proj(x1, ys, ln_mix_1, _pad_cols(w_in_1.astype(BF16), len(CD_PLAN) * LANES), cos_t, sin_t,
                          CD_PLAN, CD_OUTS, B, S)
    gd = parts[-1]
    parts = _mixer_cd(parts, B, S, cmp_k_pe_1, cmp_k_w1_1, cmp_k_w2_1, cmp_v_pe_1, cmp_v_w1_1, cmp_v_w2_1)
    x3, h3, route3 = _out_proj(parts, gd, x2, w_out_1.astype(BF16), ln_ffn_1, _router_weights(router_group_1, router_expert_1))
    ys = _moe(route3, h3, expert_gate_1, expert_up_1, expert_down_1)
    return _final_norm(x3, ys, ln_final).reshape(B, S, D_MODEL)
```

```python
import functools

import jax
import jax.numpy as jnp
from jax import lax
from jax.experimental import pallas as pl
from jax.experimental.pallas import tpu as pltpu

D_MODEL = 1024
HEAD_DIM = 64
HALF = HEAD_DIM // 2
ROPE_THETA = 10000.0
NORM_EPS = 1e-6
Q_BLOCK = 128
SCALE = HEAD_DIM ** -0.5

A_HEADS, A_KV_HEADS, A_WINDOW = 8, 2, 128
B_HEADS = 8
C_HEADS, C_KV_HEADS, C_BLOCK, C_TOPK = 8, 2, 256, 3
D_HEADS, D_KV_HEADS = 8, 2
D_CMP_LEN, D_CMP_STRIDE, D_CMP_HIDDEN = 32, 16, 256
D_SEL_LEN, D_SEL_TOPK, D_WINDOW = 64, 16, 512
N_GROUPS, EXPERTS_PER_GROUP, MOE_TOPK, EXPERT_HIDDEN = 4, 16, 2, 512
N_EXPERTS = N_GROUPS * EXPERTS_PER_GROUP
GROUP = 4

LANES = 128
ROW_TILE = 512
MOE_ROWS = 256
KEY_TILE = 512
FLASH_MIN_SUM = 1e-25
REF_MARGIN = 1.05
MASKED = -1e30
V_AUG_ROWS = 80
AUG_ROWS = 8
MASK_BIG = 2.0 ** 100
SB_EXIT = -104.0
STICK_HEADS = 8
VMEM_LIMIT = 56 * 1024 * 1024

F32 = jnp.float32
BF16 = jnp.bfloat16
NEG_INF = float("-inf")


def _iota(shape, dim):
    return lax.broadcasted_iota(jnp.int32, shape, dim)


def _dot(a, b):
    return jnp.dot(a, b, preferred_element_type=F32)


def _params(n_grid):
    return pltpu.CompilerParams(dimension_semantics=("arbitrary",) * n_grid, vmem_limit_bytes=VMEM_LIMIT)


GROUP_COLS = GROUP * HEAD_DIM


def _group_qT(q_ref):
    t = (q_ref[...].astype(F32) * SCALE).T
    return jnp.concatenate([t[g * HEAD_DIM:(g + 1) * HEAD_DIM] for g in range(GROUP)], axis=1).astype(BF16)


def _group_rows(oT):
    return jnp.concatenate([oT[:, g * Q_BLOCK:(g + 1) * Q_BLOCK] for g in range(GROUP)], axis=0).T


def _q_spec(S, col0):
    nq = S // Q_BLOCK
    return pl.BlockSpec((Q_BLOCK, GROUP_COLS), lambda b, h, n: (b * nq + n, col0 // GROUP_COLS + h))


def _o_spec(S):
    nq = S // Q_BLOCK
    return pl.BlockSpec((Q_BLOCK, GROUP_COLS), lambda b, h, n: (b * nq + n, h))


def _moe_combine(x, y0_ref, y1_ref, route_ref):
    return x + (y0_ref[...] * route_ref[:, MOE_TOPK:MOE_TOPK + 1] + y1_ref[...] * route_ref[:, MOE_TOPK + 1:MOE_TOPK + 2])


def _emit_chunk(ch, actions, outs):
    tm = ch.shape[0]
    chT = None
    for kind, oi, arg in actions:
        o = outs[oi]
        if kind == "tok":
            o[:, arg * LANES:(arg + 1) * LANES] = ch.astype(BF16)
        elif kind == "f32":
            o[...] = ch
        elif kind == "kmean":
            o[...] = jnp.mean(ch.reshape(tm // C_BLOCK, C_BLOCK, LANES), axis=1).reshape(tm // C_BLOCK, 1, LANES)
        elif kind == "heads":
            for hh in range(2):
                o[0, arg + hh] = ch[:, hh * HEAD_DIM:(hh + 1) * HEAD_DIM].astype(BF16)
        elif kind == "keys_aug":
            lane = _iota((tm, LANES), 1)
            block_in_tile = _iota((tm, LANES), 0) >> (arg.bit_length() - 1)
            aug = jnp.where((lane - HEAD_DIM == block_in_tile) | (lane == HEAD_DIM + AUG_ROWS), 1.0, 0.0)
            for hh in range(2):
                keys = ch if hh == 0 else pltpu.roll(ch, HEAD_DIM, 1)
                o[0, hh] = jnp.where(lane < HEAD_DIM, keys, aug).astype(BF16)
        else:
            chT = ch.astype(BF16).astype(F32).T if chT is None else chT
            for hh in range(2):
                vT = chT[hh * HEAD_DIM:(hh + 1) * HEAD_DIM]
                if kind == "headsT":
                    o[0, arg + hh] = vT.astype(BF16)
                else:
                    assert kind == "valsT_aug"
                    tail = jnp.where(_iota((V_AUG_ROWS - HEAD_DIM, tm), 0) == 0, 1.0, 0.0)
                    o[0, hh] = jnp.concatenate([vT, tail], axis=0).astype(BF16)


def _in_proj_kernel(plan, combine, *refs):
    refs = list(refs)
    x_ref = refs.pop(0)
    if combine:
        y0_ref, y1_ref, route_ref = refs.pop(0), refs.pop(0), refs.pop(0)
    g_ref, w_ref, cos_ref, sin_ref = refs[:4]
    outs = refs[4:]
    x = x_ref[...]
    if combine:
        x = _moe_combine(x, y0_ref, y1_ref, route_ref)
        xo_ref = outs.pop(0)
        xo_ref[...] = x
    ms = jnp.mean(x * x, axis=-1, keepdims=True)
    h = (x * lax.rsqrt(ms + NORM_EPS) * g_ref[...]).astype(BF16)
    tm = x.shape[0]
    cos = cos_ref[...]
    sin = sin_ref[...]
    first_half = (_iota((tm, LANES), 1) & (HEAD_DIM - 1)) < HALF
    for c, (rope, actions) in enumerate(plan):
        ch = _dot(h, w_ref[:, c * LANES:(c + 1) * LANES])
        if rope:
            partner = jnp.where(first_half, pltpu.roll(ch, LANES - HALF, 1), pltpu.roll(ch, HALF, 1))
            ch = ch * cos + partner * sin
        _emit_chunk(ch, actions, outs)


def _in_proj(x, ys, gain, w, cos_t, sin_t, plan, out_kinds, B, S):
    T = x.shape[0]
    n_cols = w.shape[1]
    tm = ROW_TILE
    assert tm == KEY_TILE and S % tm == 0 and len(plan) * LANES == n_cols
    tpb = S // tm
    combine = ys is not None
    row = lambda i: (i, 0)
    fixed = lambda i: (0, 0)
    by_seq = lambda i: (i // tpb, 0, i % tpb, 0)
    by_seq_t = lambda i: (i // tpb, 0, 0, i % tpb)
    in_specs = [pl.BlockSpec((tm, D_MODEL), row)]
    args = [x]
    if combine:
        in_specs += [pl.BlockSpec((tm, D_MODEL), row)] * 2 + [pl.BlockSpec((tm, LANES), row)]
        args += list(ys)
    in_specs += [pl.BlockSpec((1, D_MODEL), fixed), pl.BlockSpec((D_MODEL, n_cols), fixed),
                 pl.BlockSpec((tm, LANES), row), pl.BlockSpec((tm, LANES), row)]
    args += [gain.reshape(1, D_MODEL), w, cos_t, sin_t]
    out_shape, out_specs = [], []
    if combine:
        out_shape.append(jax.ShapeDtypeStruct((T, D_MODEL), F32))
        out_specs.append(pl.BlockSpec((tm, D_MODEL), row))
    for kind, size in out_kinds:
        if kind == "tok":
            shape, dtype, spec = (T, size * LANES), BF16, pl.BlockSpec((tm, size * LANES), row)
        elif kind == "f32":
            shape, dtype, spec = (T, LANES), F32, pl.BlockSpec((tm, LANES), row)
        elif kind == "kmean":
            shape, dtype, spec = (T // C_BLOCK, 1, LANES), F32, pl.BlockSpec((tm // C_BLOCK, 1, LANES), lambda i: (i, 0, 0))
        elif kind == "heads":
            shape, dtype, spec = (B, size, S, HEAD_DIM), BF16, pl.BlockSpec((1, size, tm, HEAD_DIM), by_seq)
        elif kind == "keys_aug":
            shape, dtype, spec = (B, size, S, LANES), BF16, pl.BlockSpec((1, size, tm, LANES), by_seq)
        elif kind == "headsT":
            shape, dtype, spec = (B, size, HEAD_DIM, S), BF16, pl.BlockSpec((1, size, HEAD_DIM, tm), by_seq_t)
        else:
            assert kind == "valsT_aug"
            shape, dtype, spec = (B, size, V_AUG_ROWS, S), BF16, pl.BlockSpec((1, size, V_AUG_ROWS, tm), by_seq_t)
        out_shape.append(jax.ShapeDtypeStruct(shape, dtype))
        out_specs.append(spec)
    kern = functools.partial(_in_proj_kernel, plan, combine)
    return pl.pallas_call(kern, grid=(T // tm,), in_specs=in_specs, out_specs=out_specs, out_shape=out_shape,
                          compiler_params=_params(1), name="in_proj")(*args)


def _swa_kernel(q_ref, kp_ref, ko_ref, vp_ref, vo_ref, sink_ref, o_ref):
    n = pl.program_id(2)
    qs = _group_qT(q_ref)
    R = qs.shape[1]
    k = jnp.concatenate([kp_ref[0, 0], ko_ref[0, 0]], axis=0)
    vT = jnp.concatenate([vp_ref[0, 0], vo_ref[0, 0]], axis=1)
    s = _dot(k, qs)
    qpos = _iota((2 * Q_BLOCK, R), 1) & (Q_BLOCK - 1)
    ki = _iota((2 * Q_BLOCK, R), 0)
    d = qpos + Q_BLOCK - ki
    ok = (d >= 0) & (d < A_WINDOW) & ((ki >= Q_BLOCK) | (n > 0))
    s = jnp.where(ok, s, NEG_INF)
    sink = sink_ref[0]
    m = jnp.maximum(jnp.max(s, axis=0, keepdims=True), sink)
    p = jnp.exp(s - m)
    den = jnp.sum(p, axis=0, keepdims=True) + jnp.exp(sink - m)
    o_ref[...] = _group_rows(_dot(vT, p.astype(BF16)) * (1.0 / den)).astype(o_ref.dtype)


def _swa_attention(proj, q_col, k, vT, sinks):
    B, Hkv, S, _ = k.shape
    nb = S // Q_BLOCK
    R = GROUP * Q_BLOCK
    sink_row = jnp.repeat(sinks.astype(F32).reshape(Hkv, GROUP), Q_BLOCK, axis=1).reshape(Hkv, 1, R)
    prev = lambda n: jnp.maximum(n - 1, 0)
    in_specs = [
        _q_spec(S, q_col),
        pl.BlockSpec((1, 1, Q_BLOCK, HEAD_DIM), lambda b, h, n: (b, h, prev(n), 0)),
        pl.BlockSpec((1, 1, Q_BLOCK, HEAD_DIM), lambda b, h, n: (b, h, n, 0)),
        pl.BlockSpec((1, 1, HEAD_DIM, Q_BLOCK), lambda b, h, n: (b, h, 0, prev(n))),
        pl.BlockSpec((1, 1, HEAD_DIM, Q_BLOCK), lambda b, h, n: (b, h, 0, n)),
        pl.BlockSpec((1, 1, R), lambda b, h, n: (h, 0, 0)),
    ]
    return pl.pallas_call(
        _swa_kernel, grid=(B, Hkv, nb), in_specs=in_specs,
        out_specs=_o_spec(S), out_shape=jax.ShapeDtypeStruct((B * S, Hkv * GROUP_COLS), BF16),
        compiler_params=_params(3), name="swa_attention")(proj, k, k, vT, vT, sink_row)


def _stick_kernel(q_ref, kT_ref, vT_ref, o_ref):
    n = pl.program_id(2)
    tq = tk = Q_BLOCK
    heads = q_ref.shape[1]
    row = _iota((tq, tk), 0)
    col = _iota((tk, tk), 1)
    upper = jnp.where(_iota((tk, tk), 0) > col, 1.0, 0.0).astype(BF16)
    tpos = n * tq + row

    def body(carry):
        kb, _, cs, accs = carry
        start = pl.multiple_of(kb * tk, tk)
        past = (start + col) < tpos
        hs = range(heads)
        zs = [_dot(q_ref[0, h], kT_ref[0, h, :, pl.ds(start, tk)]) * SCALE for h in hs]
        sps = [jnp.maximum(z, 0.0) + jnp.log1p(jnp.exp(-jnp.abs(z))) for z in zs]
        stays = [jnp.where(past, -sp, 0.0) for sp in sps]
        his = [st.astype(BF16) for st in stays]
        los = [(st - hi.astype(F32)).astype(BF16) for st, hi in zip(stays, his)]
        betweens = [_dot(hi, upper) + _dot(lo, upper) for hi, lo in zip(his, los)]
        ws = [jnp.where(past, jnp.exp(zs[h] - sps[h] + betweens[h] + cs[h]), 0.0).astype(BF16) for h in hs]
        new_accs = [accs[h] + lax.dot_general(ws[h], vT_ref[0, h, :, pl.ds(start, tk)], (((1,), (1,)), ((), ())),
                                              preferred_element_type=F32) for h in hs]
        new_cs = [cs[h] + jnp.sum(stays[h], axis=-1, keepdims=True) for h in hs]
        worst = functools.reduce(jnp.maximum, new_cs)
        return kb - 1, jnp.max(worst) > SB_EXIT, tuple(new_cs), tuple(new_accs)

    def cond(carry):
        kb, alive, _, _ = carry
        return (kb >= 0) & alive

    init = (n, jnp.array(True), (jnp.zeros((tq, 1), F32),) * heads, (jnp.zeros((tq, HEAD_DIM), F32),) * heads)
    _, _, _, accs = lax.while_loop(cond, body, init)
    o_ref[...] = jnp.concatenate(accs, axis=1).astype(o_ref.dtype)


def _stick_attention(q, kT, v):
    B, H, S, _ = q.shape
    nq = S // Q_BLOCK
    hb = STICK_HEADS
    resident = dict(pipeline_mode=pl.Buffered(1))
    in_specs = [
        pl.BlockSpec((1, hb, Q_BLOCK, HEAD_DIM), lambda b, h, n: (b, h, n, 0)),
        pl.BlockSpec((1, hb, HEAD_DIM, S), lambda b, h, n: (b, h, 0, 0), **resident),
        pl.BlockSpec((1, hb, HEAD_DIM, S), lambda b, h, n: (b, h, 0, 0), **resident),
    ]
    return pl.pallas_call(
        _stick_kernel, grid=(B, H // hb, nq), in_specs=in_specs,
        out_specs=pl.BlockSpec((Q_BLOCK, hb * HEAD_DIM), lambda b, h, n: (b * nq + n, h)),
        out_shape=jax.ShapeDtypeStruct((B * S, H * HEAD_DIM), BF16),
        compiler_params=_params(3), name="stick_attention")(q, kT, v)


def _max_key_sqnorm(ka_ref, out_ref):
    S = ka_ref.shape[2]
    tk = KEY_TILE
    is_key_lane = _iota((tk, LANES), 1) < HEAD_DIM
    ones = jnp.ones((LANES, LANES), BF16)

    def body(i, mx):
        k = jnp.where(is_key_lane, ka_ref[0, 0, pl.ds(pl.multiple_of(i * tk, tk), tk), :].astype(F32), 0.0)
        sq = _dot((k * k).astype(BF16), ones)
        return jnp.maximum(mx, jnp.max(sq, axis=0, keepdims=True))

    mx = lax.fori_loop(0, S // tk, body, jnp.zeros((1, LANES), F32))
    out_ref[...] = jnp.broadcast_to(mx, out_ref.shape)


def _flash_scratch(R):
    return [pltpu.VMEM((AUG_ROWS, LANES), F32), pltpu.VMEM((2, KEY_TILE, R), BF16), pltpu.VMEM((V_AUG_ROWS, R), F32)]


def _masked_flash_t(n, qT, bias_rows, ka_ref, vTa_ref, ksq_ref, p_scr, acc_scr):
    R = qT.shape[1]
    tk = KEY_TILE
    diag = (n * Q_BLOCK) // tk
    zpad = jnp.zeros((LANES - HEAD_DIM - 2 * AUG_ROWS, R), F32)
    causal = diag * tk + _iota((tk, R), 0) <= n * Q_BLOCK + (_iota((tk, R), 1) & (Q_BLOCK - 1))

    def scores(kt, ref_rows):
        st = pl.multiple_of(kt * tk, tk)
        low = jnp.concatenate([bias_rows(kt), ref_rows, zpad], axis=0).astype(BF16)
        return _dot(ka_ref[0, 0, pl.ds(st, tk), :], jnp.concatenate([qT, low], axis=0))

    def values(kt):
        return vTa_ref[0, 0, :, pl.ds(pl.multiple_of(kt * tk, tk), tk)]

    qsq = jnp.sum(jnp.square(qT.astype(F32)), axis=0, keepdims=True)
    ksq = jnp.concatenate([ksq_ref[0:1, :]] * (R // LANES), axis=1)
    ref = jnp.sqrt(qsq * ksq) * REF_MARGIN
    ref_rows = jnp.where(_iota((AUG_ROWS, R), 0) == 0, -ref, 0.0)

    def fast_body(kt, carry):
        slot = kt & 1
        s = scores(kt, ref_rows)
        acc_scr[...] += _dot(values(jnp.maximum(kt - 1, 0)), p_scr[1 - slot])
        p_scr[slot] = jnp.exp(s).astype(BF16)
        return carry

    p_scr[1] = jnp.zeros((tk, R), BF16)
    acc_scr[...] = jnp.zeros(acc_scr.shape, F32)
    lax.fori_loop(0, diag, fast_body, 0)
    acc = acc_scr[...] + _dot(values(jnp.maximum(diag - 1, 0)), p_scr[1 - (diag & 1)])
    s = jnp.where(causal, scores(diag, ref_rows), MASKED)
    acc = acc + _dot(values(diag), jnp.exp(s).astype(BF16))

    def running_max_path(_):
        no_ref = jnp.zeros((AUG_ROWS, R), F32)

        def update(kt, s, m, acc):
            m_new = jnp.maximum(m, jnp.max(s, axis=0, keepdims=True))
            p = jnp.exp(s - m_new).astype(BF16)
            return m_new, jnp.exp(m - m_new) * acc + _dot(values(kt), p)

        init = (jnp.full((1, R), MASKED, F32), jnp.zeros((vTa_ref.shape[2], R), F32))
        m, acc = lax.fori_loop(0, diag, lambda kt, c: update(kt, scores(kt, no_ref), *c), init)
        return update(diag, jnp.where(causal, scores(diag, no_ref), MASKED), m, acc)[1]

    healthy = jnp.min(acc[HEAD_DIM:HEAD_DIM + 1, :]) >= FLASH_MIN_SUM
    acc = lax.cond(healthy, lambda _: acc, running_max_path, 0)
    return acc[0:HEAD_DIM] * (1.0 / acc[HEAD_DIM:HEAD_DIM + 1])


def _moba_kernel(q_ref, ka_ref, vTa_ref, km_ref, o_ref, bias_scr, ksq_scr, p_scr, acc_scr):
    n = pl.program_id(2)
    pl.when(n == 0)(lambda: _max_key_sqnorm(ka_ref, ksq_scr))
    nblk = km_ref.shape[2]
    qs = _group_qT(q_ref)
    R = qs.shape[1]
    own = (n * Q_BLOCK) // C_BLOCK
    gate = _dot(km_ref[0, 0], qs)
    blk = _iota((nblk, R), 0)
    gate = jnp.where(blk < own, gate, NEG_INF)
    bias = jnp.where(blk == own, 0.0, -MASK_BIG)
    for _ in range(C_TOPK):
        mx = jnp.max(gate, axis=0, keepdims=True)
        idx = jnp.min(jnp.where(gate == mx, blk, nblk), axis=0, keepdims=True)
        hit = blk == idx
        bias = jnp.where(hit & (mx > NEG_INF), 0.0, bias)
        gate = jnp.where(hit, NEG_INF, gate)
    per = KEY_TILE // C_BLOCK
    rows = bias_scr.shape[0]
    r = _iota((rows, nblk), 0)
    spread = jnp.where(((r & (AUG_ROWS - 1)) < per) & (_iota((rows, nblk), 1) == (r >> 3) * per + (r & (AUG_ROWS - 1))), 1.0, 0.0)
    bias_scr[...] = _dot(spread.astype(BF16), bias.astype(BF16))
    oT = _masked_flash_t(n, qs, lambda kt: bias_scr[pl.ds(pl.multiple_of(kt * AUG_ROWS, AUG_ROWS), AUG_ROWS), :],
                         ka_ref, vTa_ref, ksq_scr, p_scr, acc_scr)
    o_ref[...] = _group_rows(oT).astype(o_ref.dtype)


def _moba_attention(proj, q_col, ka, vTa, km):
    B, Hkv, S, _ = ka.shape
    nq = S // Q_BLOCK
    R = GROUP * Q_BLOCK
    nblk = km.shape[2]
    nkt = S // KEY_TILE
    in_specs = [
        _q_spec(S, q_col),
        pl.BlockSpec((1, 1, S, LANES), lambda b, h, n: (b, h, 0, 0)),
        pl.BlockSpec((1, 1, V_AUG_ROWS, S), lambda b, h, n: (b, h, 0, 0)),
        pl.BlockSpec((1, 1, nblk, HEAD_DIM), lambda b, h, n: (b, h, 0, 0)),
    ]
    return pl.pallas_call(
        _moba_kernel, grid=(B, Hkv, nq), in_specs=in_specs,
        out_specs=_o_spec(S), out_shape=jax.ShapeDtypeStruct((B * S, Hkv * GROUP_COLS), BF16),
        scratch_shapes=[pltpu.VMEM((nkt * AUG_ROWS, R), F32)] + _flash_scratch(R),
        compiler_params=_params(3), name="moba_attention")(proj, ka, vTa, km)


def _compress_kernel(u_ref, us_ref, pe_ref, w1_ref, w2_ref, o_ref):
    a = (u_ref[0, 0].astype(F32) + pe_ref[0:1, :]).astype(BF16)
    b = (us_ref[0, 0].astype(F32) + pe_ref[1:2, :]).astype(BF16)
    pre = _dot(a, w1_ref[0]) + _dot(b, w1_ref[1])
    hid = jax.nn.gelu(pre)
    o_ref[0, 0] = _dot(hid.astype(BF16), w2_ref[...]).astype(o_ref.dtype)


def _compress(t, pe, w1, w2):
    B, H, S, _ = t.shape
    nrow = S // D_CMP_STRIDE
    width = D_CMP_STRIDE * HEAD_DIM
    u = t.reshape(B, H, nrow, width)
    us = jnp.concatenate([u[:, :, 1:], jnp.zeros((B, H, 1, width), u.dtype)], axis=2)
    blk = lambda b, h: (b, h, 0, 0)
    in_specs = [
        pl.BlockSpec((1, 1, nrow, width), blk), pl.BlockSpec((1, 1, nrow, width), blk),
        pl.BlockSpec((2, width), lambda b, h: (0, 0)),
        pl.BlockSpec((2, width, D_CMP_HIDDEN), lambda b, h: (0, 0, 0)),
        pl.BlockSpec((D_CMP_HIDDEN, HEAD_DIM), lambda b, h: (0, 0)),
    ]
    return pl.pallas_call(
        _compress_kernel, grid=(B, H), in_specs=in_specs,
        out_specs=pl.BlockSpec((1, 1, nrow, HEAD_DIM), blk),
        out_shape=jax.ShapeDtypeStruct((B, H, nrow, HEAD_DIM), BF16),
        compiler_params=_params(2), name="nsa_compress")(
            u, us, pe.astype(F32).reshape(2, width), w1.astype(BF16).reshape(2, width, D_CMP_HIDDEN), w2.astype(BF16))


def _cmp_select_kernel(nc, q_ref, kc_ref, vcT_ref, ovT_ref, oc_ref, bias_ref):
    n = pl.program_id(2)
    ncp = kc_ref.shape[2]
    nsel = ovT_ref.shape[0]
    qs = _group_qT(q_ref)
    R = qs.shape[1]
    s = _dot(kc_ref[0, 0], qs)
    tpos = n * Q_BLOCK + (_iota((ncp, R), 1) & (Q_BLOCK - 1))
    c = _iota((ncp, R), 0)
    ok = (c * D_CMP_STRIDE + (D_CMP_LEN - 1) <= tpos) & (c < nc)
    s = jnp.where(ok, s, NEG_INF)
    m = jnp.max(s, axis=0, keepdims=True)
    m = jnp.where(m > NEG_INF, m, 0.0)
    e = jnp.exp(s - m)
    den = jnp.sum(e, axis=0, keepdims=True)
    p = (e * (1.0 / jnp.where(den > 0, den, 1.0))).astype(BF16)
    oc_ref[...] = _group_rows(_dot(vcT_ref[0, 0], p))
    imp_heads = _dot(ovT_ref[...], p)
    imp = imp_heads[:, 0:Q_BLOCK]
    for g in range(1, GROUP):
        imp = imp + imp_heads[:, g * Q_BLOCK:(g + 1) * Q_BLOCK]
    t = n * Q_BLOCK + _iota((nsel, Q_BLOCK), 1)
    j = _iota((nsel, Q_BLOCK), 0)
    cur = t >> 6
    allowed = j * D_SEL_LEN <= t
    forced = (j == 0) | (j == cur) | (j == cur - 1)
    imp = jnp.where(allowed, imp, NEG_INF)
    imp = jnp.where(allowed & forced, float("inf"), imp)
    bias = jnp.full((nsel, Q_BLOCK), -MASK_BIG, F32)
    for _ in range(D_SEL_TOPK):
        mx = jnp.max(imp, axis=0, keepdims=True)
        idx = jnp.min(jnp.where(imp == mx, j, nsel), axis=0, keepdims=True)
        hit = j == idx
        bias = jnp.where(hit & (mx > NEG_INF), 0.0, bias)
        imp = jnp.where(hit, NEG_INF, imp)
    bias_ref[0, 0, 0] = bias


def _cmp_select(proj, q_col, kc, vcT, overlapT, nc, S):
    B, Hkv, ncp, _ = kc.shape
    nq = S // Q_BLOCK
    nsel = overlapT.shape[0]
    in_specs = [
        _q_spec(S, q_col),
        pl.BlockSpec((1, 1, ncp, HEAD_DIM), lambda b, h, n: (b, h, 0, 0)),
        pl.BlockSpec((1, 1, HEAD_DIM, ncp), lambda b, h, n: (b, h, 0, 0)),
        pl.BlockSpec((nsel, ncp), lambda b, h, n: (0, 0)),
    ]
    out_specs = [_o_spec(S), pl.BlockSpec((1, 1, 1, nsel, Q_BLOCK), lambda b, h, n: (b, h, n, 0, 0))]
    out_shape = [jax.ShapeDtypeStruct((B * S, Hkv * GROUP_COLS), F32), jax.ShapeDtypeStruct((B, Hkv, nq, nsel, Q_BLOCK), F32)]
    return pl.pallas_call(
        functools.partial(_cmp_select_kernel, nc), grid=(B, Hkv, nq), in_specs=in_specs, out_specs=out_specs,
        out_shape=out_shape, compiler_params=_params(3), name="nsa_cmp_select")(proj, kc, vcT, overlapT)


def _sel_kernel(q_ref, ka_ref, vTa_ref, bias_ref, o_ref, ksq_scr, p_scr, acc_scr):
    n = pl.program_id(2)
    pl.when(n == 0)(lambda: _max_key_sqnorm(ka_ref, ksq_scr))
    qs = _group_qT(q_ref)

    def bias_rows(kt):
        b = bias_ref[0, 0, 0, pl.ds(pl.multiple_of(kt * AUG_ROWS, AUG_ROWS), AUG_ROWS), :]
        return jnp.concatenate([b] * GROUP, axis=1)

    o_ref[...] = _group_rows(_masked_flash_t(n, qs, bias_rows, ka_ref, vTa_ref, ksq_scr, p_scr, acc_scr))


def _sel_attention(proj, q_col, ka, vTa, biasT):
    B, Hkv, S, _ = ka.shape
    nq = S // Q_BLOCK
    nsel = biasT.shape[3]
    assert KEY_TILE // D_SEL_LEN == AUG_ROWS
    in_specs = [
        _q_spec(S, q_col),
        pl.BlockSpec((1, 1, S, LANES), lambda b, h, n: (b, h, 0, 0)),
        pl.BlockSpec((1, 1, V_AUG_ROWS, S), lambda b, h, n: (b, h, 0, 0)),
        pl.BlockSpec((1, 1, 1, nsel, Q_BLOCK), lambda b, h, n: (b, h, n, 0, 0)),
    ]
    return pl.pallas_call(
        _sel_kernel, grid=(B, Hkv, nq), in_specs=in_specs,
        out_specs=_o_spec(S), out_shape=jax.ShapeDtypeStruct((B * S, Hkv * GROUP_COLS), F32),
        scratch_shapes=_flash_scratch(GROUP * Q_BLOCK),
        compiler_params=_params(3), name="nsa_selected")(proj, ka, vTa, biasT)


def _win_kernel(span, q_ref, k_ref, vT_ref, o_ref):
    n = pl.program_id(2)
    qs = _group_qT(q_ref)
    R = qs.shape[1]
    start = pl.multiple_of(jnp.maximum(n * Q_BLOCK + Q_BLOCK - span, 0), Q_BLOCK)
    s = _dot(k_ref[0, 0, pl.ds(start, span), :], qs)
    tpos = n * Q_BLOCK + (_iota((span, R), 1) & (Q_BLOCK - 1))
    d = tpos - (start + _iota((span, R), 0))
    s = jnp.where((d >= 0) & (d < D_WINDOW), s, NEG_INF)
    m = jnp.max(s, axis=0, keepdims=True)
    p = jnp.exp(s - m)
    l = jnp.sum(p, axis=0, keepdims=True)
    o_ref[...] = _group_rows(_dot(vT_ref[0, 0, :, pl.ds(start, span)], p.astype(BF16)) * (1.0 / l))


def _win_attention(proj, q_col, k, vT):
    B, Hkv, S, _ = k.shape
    nq = S // Q_BLOCK
    span = min(D_WINDOW + Q_BLOCK, S)
    in_specs = [
        _q_spec(S, q_col),
        pl.BlockSpec((1, 1, S, HEAD_DIM), lambda b, h, n: (b, h, 0, 0)),
        pl.BlockSpec((1, 1, HEAD_DIM, S), lambda b, h, n: (b, h, 0, 0)),
    ]
    return pl.pallas_call(
        functools.partial(_win_kernel, span), grid=(B, Hkv, nq), in_specs=in_specs,
        out_specs=_o_spec(S), out_shape=jax.ShapeDtypeStruct((B * S, Hkv * GROUP_COLS), F32),
        compiler_params=_params(3), name="nsa_window")(proj, k, vT)


def _route(logits):
    tm = logits.shape[0]
    lane = _iota((tm, LANES), 1)
    gl = jnp.where(lane < N_GROUPS, logits, NEG_INF)
    gmax = jnp.max(gl, axis=-1, keepdims=True)
    gidx = jnp.min(jnp.where(gl == gmax, lane, LANES), axis=-1, keepdims=True)
    g_prob = 1.0 / jnp.sum(jnp.exp(gl - gmax), axis=-1, keepdims=True)
    elane = lane - N_GROUPS
    in_group = (elane >= 0) & (elane < N_EXPERTS) & ((elane >> 4) == gidx)
    el = jnp.where(in_group, logits, NEG_INF)
    ee = jnp.exp(el - jnp.max(el, axis=-1, keepdims=True))
    ep = jnp.where(in_group, ee / jnp.sum(ee, axis=-1, keepdims=True), -1.0)
    p1 = jnp.max(ep, axis=-1, keepdims=True)
    i1 = jnp.min(jnp.where(ep == p1, lane, LANES), axis=-1, keepdims=True)
    ep2 = jnp.where(lane == i1, -1.0, ep)
    p2 = jnp.max(ep2, axis=-1, keepdims=True)
    i2 = jnp.min(jnp.where(ep2 == p2, lane, LANES), axis=-1, keepdims=True)
    den = p1 + p2
    vals = [(i1 - N_GROUPS).astype(F32), (i2 - N_GROUPS).astype(F32), g_prob * p1 / den, g_prob * p2 / den]
    out = jnp.zeros((tm, LANES), F32)
    for k, val in enumerate(vals):
        out = jnp.where(lane == k, val, out)
    return out


def _pack_bf16_pairs(h):
    n = h.shape[1] // 2
    bits = lax.bitcast_convert_type(h.astype(BF16).astype(F32), jnp.uint32)
    return bits[:, :n] | (bits[:, n:] >> 16)


def _unpack_bf16_pairs(packed):
    hi = lax.bitcast_convert_type(packed & jnp.uint32(0xFFFF0000), F32)
    lo = lax.bitcast_convert_type(packed << 16, F32)
    return jnp.concatenate([hi, lo], axis=1).astype(BF16)


def _out_tail(x_new, gain_ref, wr_ref, xo_ref, h_ref, route_ref):
    xo_ref[...] = x_new
    ms = jnp.mean(x_new * x_new, axis=-1, keepdims=True)
    h = x_new * lax.rsqrt(ms + NORM_EPS) * gain_ref[...]
    h_ref[...] = _pack_bf16_pairs(h)
    h_hi = h.astype(BF16)
    h_lo = (h - h_hi.astype(F32)).astype(BF16)
    logits = _dot(h_hi, wr_ref[0]) + (_dot(h_lo, wr_ref[0]) + _dot(h_hi, wr_ref[1]))
    route_ref[...] = _route(logits)


def _out_proj_kernel(oa_ref, ob_ref, x_ref, w_ref, gain_ref, wr_ref, xo_ref, h_ref, route_ref):
    half = w_ref.shape[0] // 2
    x_new = x_ref[...] + _dot(oa_ref[...], w_ref[0:half, :]) + _dot(ob_ref[...], w_ref[half:, :])
    _out_tail(x_new, gain_ref, wr_ref, xo_ref, h_ref, route_ref)


def _out_proj_nsa_kernel(oc_ref, b0_ref, b1_ref, b2_ref, gd_ref, x_ref, w_ref, gain_ref, wr_ref, xo_ref, h_ref, route_ref):
    half = D_HEADS * HEAD_DIM
    g = jax.nn.sigmoid(gd_ref[...])
    g_hi = g.astype(BF16)
    g_lo = (g - g_hi.astype(F32)).astype(BF16)
    src = _iota((LANES, half), 0)
    head3 = (_iota((LANES, half), 1) >> 6) * 3
    od = None
    for br, b_ref in enumerate((b0_ref, b1_ref, b2_ref)):
        spread = jnp.where(src == head3 + br, 1.0, 0.0).astype(BF16)
        term = (_dot(g_hi, spread) + _dot(g_lo, spread)) * b_ref[...]
        od = term if od is None else od + term
    x_new = x_ref[...] + _dot(oc_ref[...], w_ref[0:half, :]) + _dot(od.astype(BF16), w_ref[half:, :])
    _out_tail(x_new, gain_ref, wr_ref, xo_ref, h_ref, route_ref)


def _out_proj(o_parts, gd, x, w_out, gain, w_router):
    T = x.shape[0]
    tm = ROW_TILE
    row = lambda i: (i, 0)
    fixed = lambda i: (0, 0)
    if gd is None:
        kern = _out_proj_kernel
        args = list(o_parts)
        in_specs = [pl.BlockSpec((tm, D_MODEL // 2), row)] * 2
    else:
        kern = _out_proj_nsa_kernel
        args = list(o_parts) + [gd]
        in_specs = [pl.BlockSpec((tm, D_MODEL // 2), row)] * 4 + [pl.BlockSpec((tm, LANES), row)]
    args += [x, w_out, gain.reshape(1, D_MODEL), w_router]
    in_specs += [pl.BlockSpec((tm, D_MODEL), row), pl.BlockSpec((D_MODEL, D_MODEL), fixed),
                 pl.BlockSpec((1, D_MODEL), fixed), pl.BlockSpec((2, D_MODEL, LANES), lambda i: (0, 0, 0))]
    out_shape = [jax.ShapeDtypeStruct((T, D_MODEL), F32), jax.ShapeDtypeStruct((T, D_MODEL // 2), jnp.uint32),
                 jax.ShapeDtypeStruct((T, LANES), F32)]
    out_specs = [pl.BlockSpec((tm, D_MODEL), row), pl.BlockSpec((tm, D_MODEL // 2), row), pl.BlockSpec((tm, LANES), row)]
    return pl.pallas_call(kern, grid=(T // tm,), in_specs=in_specs, out_specs=out_specs, out_shape=out_shape,
                          compiler_params=_params(1), name="out_proj_router")(*args)


def _expert_kernel(be_ref, rows_ref, wg_ref, wu_ref, wd_ref, y_ref, wg_s, wu_s, wd_s):
    i = pl.program_id(0)
    n_blk = pl.num_programs(0)

    @pl.when((i == 0) | (be_ref[i] != be_ref[jnp.maximum(i - 1, 0)]))
    def _():
        wg_s[...] = wg_ref[0].astype(BF16)
        wu_s[...] = wu_ref[0].astype(BF16)
        wd_s[...] = wd_ref[0].astype(BF16)

    @pl.when(i < be_ref[n_blk])
    def _():
        xb = _unpack_bf16_pairs(rows_ref[...])
        hid = jax.nn.silu(_dot(xb, wg_s[...])) * _dot(xb, wu_s[...])
        y_ref[...] = _dot(hid.astype(BF16), wd_s[...])

    @pl.when(i >= be_ref[n_blk])
    def _():
        y_ref[...] = jnp.zeros(y_ref.shape, F32)


def _expert_ffn(rows, blk_info, wg, wu, wd):
    n_rows = rows.shape[0]
    n_blk = n_rows // MOE_ROWS
    grid_spec = pltpu.PrefetchScalarGridSpec(
        num_scalar_prefetch=1, grid=(n_blk,),
        in_specs=[
            pl.BlockSpec((MOE_ROWS, D_MODEL // 2), lambda i, be: (i, 0)),
            pl.BlockSpec((1, D_MODEL, EXPERT_HIDDEN), lambda i, be: (be[i], 0, 0)),
            pl.BlockSpec((1, D_MODEL, EXPERT_HIDDEN), lambda i, be: (be[i], 0, 0)),
            pl.BlockSpec((1, EXPERT_HIDDEN, D_MODEL), lambda i, be: (be[i], 0, 0)),
        ],
        out_specs=pl.BlockSpec((MOE_ROWS, D_MODEL), lambda i, be: (i, 0)),
        scratch_shapes=[pltpu.VMEM((D_MODEL, EXPERT_HIDDEN), BF16), pltpu.VMEM((D_MODEL, EXPERT_HIDDEN), BF16),
                        pltpu.VMEM((EXPERT_HIDDEN, D_MODEL), BF16)])
    return pl.pallas_call(
        _expert_kernel, grid_spec=grid_spec, out_shape=jax.ShapeDtypeStruct((n_rows, D_MODEL), F32),
        compiler_params=_params(1), name="expert_ffn")(blk_info, rows, wg, wu, wd)


def _moe_dispatch(route, h):
    n_tok = h.shape[0]
    n_asg = n_tok * MOE_TOPK
    i32 = jnp.int32
    e_flat = route[:, 0:MOE_TOPK].astype(i32).reshape(n_asg)
    is_e = e_flat[:, None] == jnp.arange(N_EXPERTS, dtype=i32)[None, :]
    counts = jnp.sum(is_e, axis=0, dtype=i32)
    order = jnp.argsort(e_flat).astype(i32)
    rank = jnp.argsort(order).astype(i32)
    padded = (counts + MOE_ROWS - 1) // MOE_ROWS * MOE_ROWS
    pad_end = jnp.cumsum(padded)
    pad_start = pad_end - padded
    start = jnp.cumsum(counts) - counts
    n_rows = n_asg + N_EXPERTS * MOE_ROWS
    n_blk = n_rows // MOE_ROWS
    blk_start = jnp.arange(n_blk, dtype=i32) * MOE_ROWS
    blk_expert = jnp.minimum(jnp.sum(pad_end[None, :] <= blk_start[:, None], axis=1, dtype=i32), N_EXPERTS - 1)
    within = (blk_start - pad_start[blk_expert])[:, None] + jnp.arange(MOE_ROWS, dtype=i32)[None, :]
    valid = within < counts[blk_expert][:, None]
    src = jnp.clip(start[blk_expert][:, None] + within, 0, n_asg - 1)
    row_tok = jnp.where(valid, (order // MOE_TOPK)[src], src // MOE_TOPK).reshape(n_rows)
    shift = jnp.sum(jnp.where(is_e, (pad_start - start)[None, :], 0), axis=1, dtype=i32)
    pos = (rank + shift).reshape(n_tok, MOE_TOPK)
    blk_info = jnp.concatenate([blk_expert, (pad_end[-1:] // MOE_ROWS).astype(i32)])
    return h[row_tok], blk_info, pos


def _moe(route, h, wg, wu, wd):
    rows, blk_info, pos = _moe_dispatch(route, h)
    y = _expert_ffn(rows, blk_info, wg, wu, wd)
    return y[pos[:, 0]], y[pos[:, 1]], route


def _final_kernel(x_ref, y0_ref, y1_ref, route_ref, g_ref, o_ref):
    x = _moe_combine(x_ref[...], y0_ref, y1_ref, route_ref)
    ms = jnp.mean(x * x, axis=-1, keepdims=True)
    o_ref[...] = x * lax.rsqrt(ms + NORM_EPS) * g_ref[...]


def _final_norm(x, ys, gain):
    T = x.shape[0]
    tm = ROW_TILE
    row = lambda i: (i, 0)
    return pl.pallas_call(
        _final_kernel, grid=(T // tm,),
        in_specs=[pl.BlockSpec((tm, D_MODEL), row)] * 3 + [pl.BlockSpec((tm, LANES), row), pl.BlockSpec((1, D_MODEL), lambda i: (0, 0))],
        out_specs=pl.BlockSpec((tm, D_MODEL), row), out_shape=jax.ShapeDtypeStruct((T, D_MODEL), F32),
        compiler_params=_params(1), name="final_norm")(x, *ys, gain.reshape(1, D_MODEL))


def _rope_tables(positions):
    inv_freq = ROPE_THETA ** (-jnp.arange(0, HEAD_DIM, 2, dtype=F32) / HEAD_DIM)
    ang = positions.astype(F32).reshape(-1, 1) * inv_freq
    cos, sin = jnp.cos(ang), jnp.sin(ang)
    reps = LANES // HEAD_DIM
    return jnp.tile(jnp.concatenate([cos, cos], axis=1), (1, reps)), jnp.tile(jnp.concatenate([-sin, sin], axis=1), (1, reps))


def _chunks(first, count, rope, action):
    return tuple((first + j, rope, action(j)) for j in range(count))


def _plan(*runs):
    entries = sum(runs, ())
    assert [e[0] for e in entries] == list(range(len(entries)))
    return tuple(e[1:] for e in entries)


AB_OUTS = (("tok", 4), ("heads", A_KV_HEADS), ("headsT", A_KV_HEADS), ("heads", B_HEADS), ("headsT", B_HEADS), ("headsT", B_HEADS))
AB_PLAN = _plan(_chunks(0, 4, True, lambda j: (("tok", 0, j),)),
                _chunks(4, 1, True, lambda j: (("heads", 1, 0),)),
                _chunks(5, 1, False, lambda j: (("headsT", 2, 0),)),
                _chunks(6, 4, False, lambda j: (("heads", 3, 2 * j),)),
                _chunks(10, 4, False, lambda j: (("headsT", 4, 2 * j),)),
                _chunks(14, 4, False, lambda j: (("headsT", 5, 2 * j),)))
CD_OUTS = (("tok", 8), ("keys_aug", C_KV_HEADS), ("kmean", 0), ("valsT_aug", C_KV_HEADS), ("heads", D_KV_HEADS),
           ("heads", D_KV_HEADS), ("keys_aug", D_KV_HEADS), ("valsT_aug", D_KV_HEADS), ("heads", D_KV_HEADS),
           ("headsT", D_KV_HEADS), ("f32", 0))
CD_PLAN = _plan(_chunks(0, 4, True, lambda j: (("tok", 0, j),)),
                _chunks(4, 1, True, lambda j: (("keys_aug", 1, C_BLOCK), ("kmean", 2, 0))),
                _chunks(5, 1, False, lambda j: (("valsT_aug", 3, 0),)),
                _chunks(6, 4, True, lambda j: (("tok", 0, 4 + j),)),
                _chunks(10, 1, True, lambda j: (("heads", 4, 0),)),
                _chunks(11, 1, False, lambda j: (("heads", 5, 0),)),
                _chunks(12, 1, True, lambda j: (("keys_aug", 6, D_SEL_LEN),)),
                _chunks(13, 1, False, lambda j: (("valsT_aug", 7, 0),)),
                _chunks(14, 1, True, lambda j: (("heads", 8, 0),)),
                _chunks(15, 1, False, lambda j: (("headsT", 9, 0),)),
                _chunks(16, 1, False, lambda j: (("f32", 10, 0),)))
QD_COL = 4 * LANES


def _router_weights(router_group, router_expert):
    pad = jnp.zeros((D_MODEL, LANES - N_GROUPS - N_EXPERTS), F32)
    w = jnp.concatenate([router_group.astype(F32), router_expert.astype(F32), pad], axis=1)
    hi = w.astype(BF16)
    return jnp.stack([hi, (w - hi.astype(F32)).astype(BF16)])


def _pad_cols(w, n):
    return jnp.concatenate([w, jnp.zeros((w.shape[0], n - w.shape[1]), w.dtype)], axis=1)


def _mixer_ab(parts, sinks):
    qa, ka, vaT, qb, kbT, vbT = parts
    oa = _swa_attention(qa, 0, ka, vaT, sinks)
    ob = _stick_attention(qb, kbT, vbT)
    return oa, ob


def _mixer_cd(parts, B, S, k_pe, k_w1, k_w2, v_pe, v_w1, v_w2):
    q_cd, kc_aug, kmean, vcT_aug, kdc, vdc, kds_aug, vdsT_aug, kdw, vdwT, _ = parts
    nblk = S // C_BLOCK
    km = kmean.reshape(B, nblk, C_KV_HEADS, HEAD_DIM).transpose(0, 2, 1, 3).astype(BF16)
    oc = _moba_attention(q_cd, 0, kc_aug, vcT_aug, km)
    k_cmp = _compress(kdc, k_pe, k_w1, k_w2)
    v_cmp = _compress(vdc, v_pe, v_w1, v_w2)
    nc = (S - D_CMP_LEN) // D_CMP_STRIDE + 1
    ncp = S // D_CMP_STRIDE
    nsel = S // D_SEL_LEN
    c_start = jnp.arange(ncp) * D_CMP_STRIDE
    b_start = jnp.arange(nsel) * D_SEL_LEN
    overlap = ((c_start[:, None] <= b_start[None, :] + D_SEL_LEN - 1) & (c_start[:, None] + D_CMP_LEN - 1 >= b_start[None, :])
               & (jnp.arange(ncp)[:, None] < nc)).astype(BF16)
    o_cmp, biasT = _cmp_select(q_cd, QD_COL, k_cmp, v_cmp.transpose(0, 1, 3, 2), overlap.T, nc, S)
    o_sel = _sel_attention(q_cd, QD_COL, kds_aug, vdsT_aug, biasT)
    o_win = _win_attention(q_cd, QD_COL, kdw, vdwT)
    return oc, o_cmp, o_sel, o_win


def kernel(x, positions, ln_mix_0, w_in_0, sinks_0, w_out_0, ln_ffn_0, router_group_0, router_expert_0, expert_gate_0, expert_up_0, expert_down_0, ln_mix_1, w_in_1, cmp_k_pe_1, cmp_k_w1_1, cmp_k_w2_1, cmp_v_pe_1, cmp_v_w1_1, cmp_v_w2_1, w_out_1, ln_ffn_1, router_group_1, router_expert_1, expert_gate_1, expert_up_1, expert_down_1, ln_final):
    B, S, _ = x.shape
    T = B * S
    assert S % KEY_TILE == 0 and T % ROW_TILE == 0
    cos_t, sin_t = _rope_tables(positions)
    xf = x.reshape(T, D_MODEL)

    parts = _in_proj(xf, None, ln_mix_0, w_in_0.astype(BF16), cos_t, sin_t, AB_PLAN, AB_OUTS, B, S)
    o_ab = _mixer_ab(parts, sinks_0)
    x1, h1, route1 = _out_proj(o_ab, None, xf, w_out_0.astype(BF16), ln_ffn_0, _router_weights(router_group_0, router_expert_0))
    ys = _moe(route1, h1, expert_gate_0, expert_up_0, expert_down_0)

    x2, *parts = _in_proj(x1, ys, ln_mix_1, _pad_cols(w_in_1.astype(BF16), len(CD_PLAN) * LANES), cos_t, sin_t,
                          CD_PLAN, CD_OUTS, B, S)
    gd = parts[-1]
    parts = _mixer_cd(parts, B, S, cmp_k_pe_1, cmp_k_w1_1, cmp_k_w2_1, cmp_v_pe_1, cmp_v_w1_1, cmp_v_w2_1)
    x3, h3, route3 = _out_proj(parts, gd, x2, w_out_1.astype(BF16), ln_ffn_1, _router_weights(router_group_1, router_expert_1))
    ys = _moe(route3, h3, expert_gate_1, expert_up_1, expert_down_1)
    return _final_norm(x3, ys, ln_final).reshape(B, S, D_MODEL)
```

```python
import functools

import jax
import jax.numpy as jnp
from jax import lax
from jax.experimental import pallas as pl
from jax.experimental.pallas import tpu as pltpu

D_MODEL = 1024
HEAD_DIM = 64
HALF = HEAD_DIM // 2
ROPE_THETA = 10000.0
NORM_EPS = 1e-6
Q_BLOCK = 128
SCALE = HEAD_DIM ** -0.5

A_HEADS, A_KV_HEADS, A_WINDOW = 8, 2, 128
B_HEADS = 8
C_HEADS, C_KV_HEADS, C_BLOCK, C_TOPK = 8, 2, 256, 3
D_HEADS, D_KV_HEADS = 8, 2
D_CMP_LEN, D_CMP_STRIDE, D_CMP_HIDDEN = 32, 16, 256
D_SEL_LEN, D_SEL_TOPK, D_WINDOW = 64, 16, 512
N_GROUPS, EXPERTS_PER_GROUP, MOE_TOPK, EXPERT_HIDDEN = 4, 16, 2, 512
N_EXPERTS = N_GROUPS * EXPERTS_PER_GROUP
GROUP = 4

LANES = 128
ROW_TILE = 512
MOE_ROWS = 256
KEY_TILE = 512
FLASH_MIN_SUM = 1e-25
REF_MARGIN = 1.05
MASKED = -1e30
V_AUG_ROWS = 80
AUG_ROWS = 8
MASK_BIG = 2.0 ** 100
PICKED = -1.0
SB_EXIT = -104.0
STICK_HEADS = 8
VMEM_LIMIT = 56 * 1024 * 1024

F32 = jnp.float32
BF16 = jnp.bfloat16
NEG_INF = float("-inf")


def _iota(shape, dim):
    return lax.broadcasted_iota(jnp.int32, shape, dim)


def _dot(a, b):
    return jnp.dot(a, b, preferred_element_type=F32)


def _in_range(d, width):
    return lax.bitcast_convert_type(d, jnp.uint32) < lax.bitcast_convert_type(jnp.asarray(width, jnp.int32), jnp.uint32)


def _params(n_grid):
    return pltpu.CompilerParams(dimension_semantics=("arbitrary",) * n_grid, vmem_limit_bytes=VMEM_LIMIT)


GROUP_COLS = GROUP * HEAD_DIM


def _group_qT(q_ref):
    t = (q_ref[...].astype(F32) * SCALE).T
    return jnp.concatenate([t[g * HEAD_DIM:(g + 1) * HEAD_DIM] for g in range(GROUP)], axis=1).astype(BF16)


def _group_rows(oT):
    return jnp.concatenate([oT[:, g * Q_BLOCK:(g + 1) * Q_BLOCK] for g in range(GROUP)], axis=0).T


def _q_spec(S, col0):
    nq = S // Q_BLOCK
    return pl.BlockSpec((Q_BLOCK, GROUP_COLS), lambda b, h, n: (b * nq + n, col0 // GROUP_COLS + h))


def _o_spec(S):
    nq = S // Q_BLOCK
    return pl.BlockSpec((Q_BLOCK, GROUP_COLS), lambda b, h, n: (b * nq + n, h))


def _moe_combine(x, y0_ref, y1_ref, route_ref):
    return x + (y0_ref[...] * route_ref[:, MOE_TOPK:MOE_TOPK + 1] + y1_ref[...] * route_ref[:, MOE_TOPK + 1:MOE_TOPK + 2])


def _emit_chunk(ch, actions, outs):
    tm = ch.shape[0]
    chT = None
    for kind, oi, arg in actions:
        o = outs[oi]
        if kind == "tok":
            o[:, arg * LANES:(arg + 1) * LANES] = ch.astype(BF16)
        elif kind == "f32":
            o[...] = ch
        elif kind == "kmean":
            o[...] = jnp.mean(ch.reshape(tm // C_BLOCK, C_BLOCK, LANES), axis=1).reshape(tm // C_BLOCK, 1, LANES)
        elif kind == "heads":
            for hh in range(2):
                o[0, arg + hh] = ch[:, hh * HEAD_DIM:(hh + 1) * HEAD_DIM].astype(BF16)
        elif kind == "keys_aug":
            lane = _iota((tm, LANES), 1)
            block_in_tile = _iota((tm, LANES), 0) >> (arg.bit_length() - 1)
            aug = jnp.where((lane - HEAD_DIM == block_in_tile) | (lane == HEAD_DIM + AUG_ROWS), 1.0, 0.0)
            for hh in range(2):
                keys = ch if hh == 0 else pltpu.roll(ch, HEAD_DIM, 1)
                o[0, hh] = jnp.where(lane < HEAD_DIM, keys, aug).astype(BF16)
        else:
            chT = ch.astype(BF16).astype(F32).T if chT is None else chT
            for hh in range(2):
                vT = chT[hh * HEAD_DIM:(hh + 1) * HEAD_DIM]
                if kind == "headsT":
                    o[0, arg + hh] = vT.astype(BF16)
                else:
                    assert kind == "valsT_aug"
                    tail = jnp.where(_iota((V_AUG_ROWS - HEAD_DIM, tm), 0) == 0, 1.0, 0.0)
                    o[0, hh] = jnp.concatenate([vT, tail], axis=0).astype(BF16)


def _in_proj_kernel(plan, combine, *refs):
    refs = list(refs)
    x_ref = refs.pop(0)
    if combine:
        y0_ref, y1_ref, route_ref = refs.pop(0), refs.pop(0), refs.pop(0)
    g_ref, w_ref, cos_ref, sin_ref = refs[:4]
    outs = refs[4:]
    x = x_ref[...]
    if combine:
        x = _moe_combine(x, y0_ref, y1_ref, route_ref)
        xo_ref = outs.pop(0)
        xo_ref[...] = x
    ms = jnp.mean(x * x, axis=-1, keepdims=True)
    h = (x * lax.rsqrt(ms + NORM_EPS) * g_ref[...]).astype(BF16)
    tm = x.shape[0]
    cos = cos_ref[...]
    sin = sin_ref[...]
    first_half = (_iota((tm, LANES), 1) & (HEAD_DIM - 1)) < HALF
    for c, (rope, actions) in enumerate(plan):
        ch = _dot(h, w_ref[:, c * LANES:(c + 1) * LANES])
        if rope:
            partner = jnp.where(first_half, pltpu.roll(ch, LANES - HALF, 1), pltpu.roll(ch, HALF, 1))
            ch = ch * cos + partner * sin
        _emit_chunk(ch, actions, outs)


def _in_proj(x, ys, gain, w, cos_t, sin_t, plan, out_kinds, B, S):
    T = x.shape[0]
    n_cols = w.shape[1]
    tm = ROW_TILE
    assert tm == KEY_TILE and S % tm == 0 and len(plan) * LANES == n_cols
    tpb = S // tm
    combine = ys is not None
    row = lambda i: (i, 0)
    fixed = lambda i: (0, 0)
    by_seq = lambda i: (i // tpb, 0, i % tpb, 0)
    by_seq_t = lambda i: (i // tpb, 0, 0, i % tpb)
    in_specs = [pl.BlockSpec((tm, D_MODEL), row)]
    args = [x]
    if combine:
        in_specs += [pl.BlockSpec((tm, D_MODEL), row)] * 2 + [pl.BlockSpec((tm, LANES), row)]
        args += list(ys)
    in_specs += [pl.BlockSpec((1, D_MODEL), fixed), pl.BlockSpec((D_MODEL, n_cols), fixed),
                 pl.BlockSpec((tm, LANES), row), pl.BlockSpec((tm, LANES), row)]
    args += [gain.reshape(1, D_MODEL), w, cos_t, sin_t]
    out_shape, out_specs = [], []
    if combine:
        out_shape.append(jax.ShapeDtypeStruct((T, D_MODEL), F32))
        out_specs.append(pl.BlockSpec((tm, D_MODEL), row))
    for kind, size in out_kinds:
        if kind == "tok":
            shape, dtype, spec = (T, size * LANES), BF16, pl.BlockSpec((tm, size * LANES), row)
        elif kind == "f32":
            shape, dtype, spec = (T, LANES), F32, pl.BlockSpec((tm, LANES), row)
        elif kind == "kmean":
            shape, dtype, spec = (T // C_BLOCK, 1, LANES), F32, pl.BlockSpec((tm // C_BLOCK, 1, LANES), lambda i: (i, 0, 0))
        elif kind == "heads":
            shape, dtype, spec = (B, size, S, HEAD_DIM), BF16, pl.BlockSpec((1, size, tm, HEAD_DIM), by_seq)
        elif kind == "keys_aug":
            shape, dtype, spec = (B, size, S, LANES), BF16, pl.BlockSpec((1, size, tm, LANES), by_seq)
        elif kind == "headsT":
            shape, dtype, spec = (B, size, HEAD_DIM, S), BF16, pl.BlockSpec((1, size, HEAD_DIM, tm), by_seq_t)
        else:
            assert kind == "valsT_aug"
            shape, dtype, spec = (B, size, V_AUG_ROWS, S), BF16, pl.BlockSpec((1, size, V_AUG_ROWS, tm), by_seq_t)
        out_shape.append(jax.ShapeDtypeStruct(shape, dtype))
        out_specs.append(spec)
    kern = functools.partial(_in_proj_kernel, plan, combine)
    return pl.pallas_call(kern, grid=(T // tm,), in_specs=in_specs, out_specs=out_specs, out_shape=out_shape,
                          compiler_params=_params(1), name="in_proj")(*args)


def _swa_kernel(q_ref, kp_ref, ko_ref, vp_ref, vo_ref, sink_ref, o_ref):
    n = pl.program_id(2)
    qs = _group_qT(q_ref)
    R = qs.shape[1]
    k = jnp.concatenate([kp_ref[0, 0], ko_ref[0, 0]], axis=0)
    vT = jnp.concatenate([vp_ref[0, 0], vo_ref[0, 0]], axis=1)
    s = _dot(k, qs)
    qpos = _iota((1, R), 1) & (Q_BLOCK - 1)
    d = qpos + Q_BLOCK - _iota((2 * Q_BLOCK, R), 0)
    width = jnp.minimum(A_WINDOW, qpos + 1 + jnp.where(n > 0, Q_BLOCK, 0))
    s = jnp.where(_in_range(d, width), s, NEG_INF)
    sink = sink_ref[0]
    m = jnp.maximum(jnp.max(s, axis=0, keepdims=True), sink)
    p = jnp.exp(s - m)
    den = jnp.sum(p, axis=0, keepdims=True) + jnp.exp(sink - m)
    o_ref[...] = _group_rows(_dot(vT, p.astype(BF16)) * (1.0 / den)).astype(o_ref.dtype)


def _swa_attention(proj, q_col, k, vT, sinks):
    B, Hkv, S, _ = k.shape
    nb = S // Q_BLOCK
    R = GROUP * Q_BLOCK
    sink_row = jnp.repeat(sinks.astype(F32).reshape(Hkv, GROUP), Q_BLOCK, axis=1).reshape(Hkv, 1, R)
    prev = lambda n: jnp.maximum(n - 1, 0)
    in_specs = [
        _q_spec(S, q_col),
        pl.BlockSpec((1, 1, Q_BLOCK, HEAD_DIM), lambda b, h, n: (b, h, prev(n), 0)),
        pl.BlockSpec((1, 1, Q_BLOCK, HEAD_DIM), lambda b, h, n: (b, h, n, 0)),
        pl.BlockSpec((1, 1, HEAD_DIM, Q_BLOCK), lambda b, h, n: (b, h, 0, prev(n))),
        pl.BlockSpec((1, 1, HEAD_DIM, Q_BLOCK), lambda b, h, n: (b, h, 0, n)),
        pl.BlockSpec((1, 1, R), lambda b, h, n: (h, 0, 0)),
    ]
    return pl.pallas_call(
        _swa_kernel, grid=(B, Hkv, nb), in_specs=in_specs,
        out_specs=_o_spec(S), out_shape=jax.ShapeDtypeStruct((B * S, Hkv * GROUP_COLS), BF16),
        compiler_params=_params(3), name="swa_attention")(proj, k, k, vT, vT, sink_row)


def _stick_kernel(q_ref, kT_ref, vT_ref, o_ref):
    n = pl.program_id(2)
    tq = tk = Q_BLOCK
    heads = q_ref.shape[1]
    row = _iota((tq, tk), 0)
    col = _iota((tk, tk), 1)
    upper = jnp.where(_iota((tk, tk), 0) > col, 1.0, 0.0).astype(BF16)
    tpos = n * tq + row

    def body(carry):
        kb, _, cs, accs = carry
        start = pl.multiple_of(kb * tk, tk)
        past = (start + col) < tpos
        hs = range(heads)
        zs = [_dot(q_ref[0, h], kT_ref[0, h, :, pl.ds(start, tk)]) * SCALE for h in hs]
        sps = [jnp.maximum(z, 0.0) + jnp.log1p(jnp.exp(-jnp.abs(z))) for z in zs]
        stays = [jnp.where(past, -sp, 0.0) for sp in sps]
        his = [st.astype(BF16) for st in stays]
        los = [(st - hi.astype(F32)).astype(BF16) for st, hi in zip(stays, his)]
        betweens = [_dot(hi, upper) + _dot(lo, upper) for hi, lo in zip(his, los)]
        ws = [jnp.where(past, jnp.exp(zs[h] - sps[h] + betweens[h] + cs[h]), 0.0).astype(BF16) for h in hs]
        new_accs = [accs[h] + lax.dot_general(ws[h], vT_ref[0, h, :, pl.ds(start, tk)], (((1,), (1,)), ((), ())),
                                              preferred_element_type=F32) for h in hs]
        new_cs = [cs[h] + jnp.sum(stays[h], axis=-1, keepdims=True) for h in hs]
        worst = functools.reduce(jnp.maximum, new_cs)
        return kb - 1, jnp.max(worst) > SB_EXIT, tuple(new_cs), tuple(new_accs)

    def cond(carry):
        kb, alive, _, _ = carry
        return (kb >= 0) & alive

    init = (n, jnp.array(True), (jnp.zeros((tq, 1), F32),) * heads, (jnp.zeros((tq, HEAD_DIM), F32),) * heads)
    _, _, _, accs = lax.while_loop(cond, body, init)
    o_ref[...] = jnp.concatenate(accs, axis=1).astype(o_ref.dtype)


def _stick_attention(q, kT, v):
    B, H, S, _ = q.shape
    nq = S // Q_BLOCK
    hb = STICK_HEADS
    resident = dict(pipeline_mode=pl.Buffered(1))
    in_specs = [
        pl.BlockSpec((1, hb, Q_BLOCK, HEAD_DIM), lambda b, h, n: (b, h, n, 0)),
        pl.BlockSpec((1, hb, HEAD_DIM, S), lambda b, h, n: (b, h, 0, 0), **resident),
        pl.BlockSpec((1, hb, HEAD_DIM, S), lambda b, h, n: (b, h, 0, 0), **resident),
    ]
    return pl.pallas_call(
        _stick_kernel, grid=(B, H // hb, nq), in_specs=in_specs,
        out_specs=pl.BlockSpec((Q_BLOCK, hb * HEAD_DIM), lambda b, h, n: (b * nq + n, h)),
        out_shape=jax.ShapeDtypeStruct((B * S, H * HEAD_DIM), BF16),
        compiler_params=_params(3), name="stick_attention")(q, kT, v)


def _max_key_sqnorm(ka_ref, out_ref):
    S = ka_ref.shape[2]
    tk = KEY_TILE
    is_key_lane = _iota((tk, LANES), 1) < HEAD_DIM
    ones = jnp.ones((LANES, LANES), BF16)

    def body(i, mx):
        k = jnp.where(is_key_lane, ka_ref[0, 0, pl.ds(pl.multiple_of(i * tk, tk), tk), :].astype(F32), 0.0)
        sq = _dot((k * k).astype(BF16), ones)
        return jnp.maximum(mx, jnp.max(sq, axis=0, keepdims=True))

    mx = lax.fori_loop(0, S // tk, body, jnp.zeros((1, LANES), F32))
    out_ref[...] = jnp.broadcast_to(mx, out_ref.shape)


def _flash_scratch(R):
    return [pltpu.VMEM((AUG_ROWS, LANES), F32), pltpu.VMEM((2, KEY_TILE, R), BF16), pltpu.VMEM((V_AUG_ROWS, R), F32)]


def _masked_flash_t(n, qT, bias_rows, ka_ref, vTa_ref, ksq_ref, p_scr, acc_scr):
    R = qT.shape[1]
    tk = KEY_TILE
    diag = (n * Q_BLOCK) // tk
    zpad = jnp.zeros((LANES - HEAD_DIM - 2 * AUG_ROWS, R), F32)
    causal = diag * tk + _iota((tk, R), 0) <= n * Q_BLOCK + (_iota((tk, R), 1) & (Q_BLOCK - 1))

    def scores(kt, ref_rows):
        st = pl.multiple_of(kt * tk, tk)
        low = jnp.concatenate([bias_rows(kt), ref_rows, zpad], axis=0).astype(BF16)
        return _dot(ka_ref[0, 0, pl.ds(st, tk), :], jnp.concatenate([qT, low], axis=0))

    def values(kt):
        return vTa_ref[0, 0, :, pl.ds(pl.multiple_of(kt * tk, tk), tk)]

    qsq = jnp.sum(jnp.square(qT.astype(F32)), axis=0, keepdims=True)
    ksq = jnp.concatenate([ksq_ref[0:1, :]] * (R // LANES), axis=1)
    ref = jnp.sqrt(qsq * ksq) * REF_MARGIN
    ref_rows = jnp.where(_iota((AUG_ROWS, R), 0) == 0, -ref, 0.0)

    def fast_body(kt, carry):
        slot = kt & 1
        s = scores(kt, ref_rows)
        acc_scr[...] += _dot(values(jnp.maximum(kt - 1, 0)), p_scr[1 - slot])
        p_scr[slot] = jnp.exp(s).astype(BF16)
        return carry

    p_scr[1] = jnp.zeros((tk, R), BF16)
    acc_scr[...] = jnp.zeros(acc_scr.shape, F32)
    lax.fori_loop(0, diag, fast_body, 0)
    acc = acc_scr[...] + _dot(values(jnp.maximum(diag - 1, 0)), p_scr[1 - (diag & 1)])
    s = jnp.where(causal, scores(diag, ref_rows), MASKED)
    acc = acc + _dot(values(diag), jnp.exp(s).astype(BF16))

    def running_max_path(_):
        no_ref = jnp.zeros((AUG_ROWS, R), F32)

        def update(kt, s, m, acc):
            m_new = jnp.maximum(m, jnp.max(s, axis=0, keepdims=True))
            p = jnp.exp(s - m_new).astype(BF16)
            return m_new, jnp.exp(m - m_new) * acc + _dot(values(kt), p)

        init = (jnp.full((1, R), MASKED, F32), jnp.zeros((vTa_ref.shape[2], R), F32))
        m, acc = lax.fori_loop(0, diag, lambda kt, c: update(kt, scores(kt, no_ref), *c), init)
        return update(diag, jnp.where(causal, scores(diag, no_ref), MASKED), m, acc)[1]

    healthy = jnp.min(acc[HEAD_DIM:HEAD_DIM + 1, :]) >= FLASH_MIN_SUM
    acc = lax.cond(healthy, lambda _: acc, running_max_path, 0)
    return acc[0:HEAD_DIM] * (1.0 / acc[HEAD_DIM:HEAD_DIM + 1])


def _moba_kernel(q_ref, ka_ref, vTa_ref, km_ref, o_ref, bias_scr, ksq_scr, p_scr, acc_scr):
    n = pl.program_id(2)
    pl.when(n == 0)(lambda: _max_key_sqnorm(ka_ref, ksq_scr))
    nblk = km_ref.shape[2]
    qs = _group_qT(q_ref)
    R = qs.shape[1]
    own = (n * Q_BLOCK) // C_BLOCK
    gate = _dot(km_ref[0, 0], qs)
    blk = _iota((nblk, R), 0)
    gate = jnp.where(blk < own, gate, NEG_INF)
    bias = jnp.where(blk == own, 0.0, -MASK_BIG)
    for _ in range(C_TOPK):
        mx = jnp.max(gate, axis=0, keepdims=True)
        idx = jnp.min(jnp.where(gate == mx, blk, nblk), axis=0, keepdims=True)
        hit = blk == idx
        bias = jnp.where(hit, jnp.where(mx > NEG_INF, 0.0, bias), bias)
        gate = jnp.where(hit, NEG_INF, gate)
    per = KEY_TILE // C_BLOCK
    rows = bias_scr.shape[0]
    r = _iota((rows, nblk), 0)
    spread = jnp.where(((r & (AUG_ROWS - 1)) < per) & (_iota((rows, nblk), 1) == (r >> 3) * per + (r & (AUG_ROWS - 1))), 1.0, 0.0)
    bias_scr[...] = _dot(spread.astype(BF16), bias.astype(BF16))
    oT = _masked_flash_t(n, qs, lambda kt: bias_scr[pl.ds(pl.multiple_of(kt * AUG_ROWS, AUG_ROWS), AUG_ROWS), :],
                         ka_ref, vTa_ref, ksq_scr, p_scr, acc_scr)
    o_ref[...] = _group_rows(oT).astype(o_ref.dtype)


def _moba_attention(proj, q_col, ka, vTa, km):
    B, Hkv, S, _ = ka.shape
    nq = S // Q_BLOCK
    R = GROUP * Q_BLOCK
    nblk = km.shape[2]
    nkt = S // KEY_TILE
    in_specs = [
        _q_spec(S, q_col),
        pl.BlockSpec((1, 1, S, LANES), lambda b, h, n: (b, h, 0, 0)),
        pl.BlockSpec((1, 1, V_AUG_ROWS, S), lambda b, h, n: (b, h, 0, 0)),
        pl.BlockSpec((1, 1, nblk, HEAD_DIM), lambda b, h, n: (b, h, 0, 0)),
    ]
    return pl.pallas_call(
        _moba_kernel, grid=(B, Hkv, nq), in_specs=in_specs,
        out_specs=_o_spec(S), out_shape=jax.ShapeDtypeStruct((B * S, Hkv * GROUP_COLS), BF16),
        scratch_shapes=[pltpu.VMEM((nkt * AUG_ROWS, R), F32)] + _flash_scratch(R),
        compiler_params=_params(3), name="moba_attention")(proj, ka, vTa, km)


def _compress_kernel(u_ref, us_ref, pe_ref, w1_ref, w2_ref, o_ref):
    a = (u_ref[0, 0].astype(F32) + pe_ref[0:1, :]).astype(BF16)
    b = (us_ref[0, 0].astype(F32) + pe_ref[1:2, :]).astype(BF16)
    pre = _dot(a, w1_ref[0]) + _dot(b, w1_ref[1])
    hid = jax.nn.gelu(pre)
    o_ref[0, 0] = _dot(hid.astype(BF16), w2_ref[...]).astype(o_ref.dtype)


def _compress(t, pe, w1, w2):
    B, H, S, _ = t.shape
    nrow = S // D_CMP_STRIDE
    width = D_CMP_STRIDE * HEAD_DIM
    u = t.reshape(B, H, nrow, width)
    us = jnp.concatenate([u[:, :, 1:], jnp.zeros((B, H, 1, width), u.dtype)], axis=2)
    blk = lambda b, h: (b, h, 0, 0)
    in_specs = [
        pl.BlockSpec((1, 1, nrow, width), blk), pl.BlockSpec((1, 1, nrow, width), blk),
        pl.BlockSpec((2, width), lambda b, h: (0, 0)),
        pl.BlockSpec((2, width, D_CMP_HIDDEN), lambda b, h: (0, 0, 0)),
        pl.BlockSpec((D_CMP_HIDDEN, HEAD_DIM), lambda b, h: (0, 0)),
    ]
    return pl.pallas_call(
        _compress_kernel, grid=(B, H), in_specs=in_specs,
        out_specs=pl.BlockSpec((1, 1, nrow, HEAD_DIM), blk),
        out_shape=jax.ShapeDtypeStruct((B, H, nrow, HEAD_DIM), BF16),
        compiler_params=_params(2), name="nsa_compress")(
            u, us, pe.astype(F32).reshape(2, width), w1.astype(BF16).reshape(2, width, D_CMP_HIDDEN), w2.astype(BF16))


def _cmp_select_kernel(nc, q_ref, kc_ref, vcT_ref, ovT_ref, oc_ref, bias_ref):
    n = pl.program_id(2)
    ncp = kc_ref.shape[2]
    nsel = ovT_ref.shape[0]
    qs = _group_qT(q_ref)
    R = qs.shape[1]
    s = _dot(kc_ref[0, 0], qs)
    tpos = n * Q_BLOCK + (_iota((1, R), 1) & (Q_BLOCK - 1))
    c_last = jnp.minimum((tpos - (D_CMP_LEN - 1)) >> 4, nc - 1)
    s = jnp.where(_iota((ncp, R), 0) <= c_last, s, NEG_INF)
    m = jnp.max(s, axis=0, keepdims=True)
    m = jnp.where(m > NEG_INF, m, 0.0)
    e = jnp.exp(s - m)
    den = jnp.sum(e, axis=0, keepdims=True)
    p = (e * (1.0 / jnp.where(den > 0, den, 1.0))).astype(BF16)
    oc_ref[...] = _group_rows(_dot(vcT_ref[0, 0], p))
    imp_heads = _dot(ovT_ref[...], p)
    imp = imp_heads[:, 0:Q_BLOCK]
    for g in range(1, GROUP):
        imp = imp + imp_heads[:, g * Q_BLOCK:(g + 1) * Q_BLOCK]
    t = n * Q_BLOCK + _iota((nsel, Q_BLOCK), 1)
    j = _iota((nsel, Q_BLOCK), 0)
    cur = t >> 6
    imp = jnp.where(j <= cur, imp, NEG_INF)
    imp = jnp.where(_in_range(cur - j, 2), float("inf"), imp)
    imp = jnp.where(j == 0, float("inf"), imp)
    for _ in range(D_SEL_TOPK):
        mx = jnp.max(imp, axis=0, keepdims=True)
        idx = jnp.min(jnp.where(imp == mx, j, nsel), axis=0, keepdims=True)
        imp = jnp.where(j == idx, PICKED, imp)
    bias_ref[0, 0, 0] = jnp.where(imp == PICKED, 0.0, -MASK_BIG)


def _cmp_select(proj, q_col, kc, vcT, overlapT, nc, S):
    B, Hkv, ncp, _ = kc.shape
    nq = S // Q_BLOCK
    nsel = overlapT.shape[0]
    in_specs = [
        _q_spec(S, q_col),
        pl.BlockSpec((1, 1, ncp, HEAD_DIM), lambda b, h, n: (b, h, 0, 0)),
        pl.BlockSpec((1, 1, HEAD_DIM, ncp), lambda b, h, n: (b, h, 0, 0)),
        pl.BlockSpec((nsel, ncp), lambda b, h, n: (0, 0)),
    ]
    out_specs = [_o_spec(S), pl.BlockSpec((1, 1, 1, nsel, Q_BLOCK), lambda b, h, n: (b, h, n, 0, 0))]
    out_shape = [jax.ShapeDtypeStruct((B * S, Hkv * GROUP_COLS), F32), jax.ShapeDtypeStruct((B, Hkv, nq, nsel, Q_BLOCK), F32)]
    return pl.pallas_call(
        functools.partial(_cmp_select_kernel, nc), grid=(B, Hkv, nq), in_specs=in_specs, out_specs=out_specs,
        out_shape=out_shape, compiler_params=_params(3), name="nsa_cmp_select")(proj, kc, vcT, overlapT)


def _sel_kernel(q_ref, ka_ref, vTa_ref, bias_ref, o_ref, ksq_scr, p_scr, acc_scr):
    n = pl.program_id(2)
    pl.when(n == 0)(lambda: _max_key_sqnorm(ka_ref, ksq_scr))
    qs = _group_qT(q_ref)

    def bias_rows(kt):
        b = bias_ref[0, 0, 0, pl.ds(pl.multiple_of(kt * AUG_ROWS, AUG_ROWS), AUG_ROWS), :]
        return jnp.concatenate([b] * GROUP, axis=1)

    o_ref[...] = _group_rows(_masked_flash_t(n, qs, bias_rows, ka_ref, vTa_ref, ksq_scr, p_scr, acc_scr))


def _sel_attention(proj, q_col, ka, vTa, biasT):
    B, Hkv, S, _ = ka.shape
    nq = S // Q_BLOCK
    nsel = biasT.shape[3]
    assert KEY_TILE // D_SEL_LEN == AUG_ROWS
    in_specs = [
        _q_spec(S, q_col),
        pl.BlockSpec((1, 1, S, LANES), lambda b, h, n: (b, h, 0, 0)),
        pl.BlockSpec((1, 1, V_AUG_ROWS, S), lambda b, h, n: (b, h, 0, 0)),
        pl.BlockSpec((1, 1, 1, nsel, Q_BLOCK), lambda b, h, n: (b, h, n, 0, 0)),
    ]
    return pl.pallas_call(
        _sel_kernel, grid=(B, Hkv, nq), in_specs=in_specs,
        out_specs=_o_spec(S), out_shape=jax.ShapeDtypeStruct((B * S, Hkv * GROUP_COLS), F32),
        scratch_shapes=_flash_scratch(GROUP * Q_BLOCK),
        compiler_params=_params(3), name="nsa_selected")(proj, ka, vTa, biasT)


def _win_kernel(span, q_ref, k_ref, vT_ref, o_ref):
    n = pl.program_id(2)
    qs = _group_qT(q_ref)
    R = qs.shape[1]
    start = pl.multiple_of(jnp.maximum(n * Q_BLOCK + Q_BLOCK - span, 0), Q_BLOCK)
    s = _dot(k_ref[0, 0, pl.ds(start, span), :], qs)
    tpos = n * Q_BLOCK + (_iota((1, R), 1) & (Q_BLOCK - 1))
    d = tpos - (start + _iota((span, R), 0))
    s = jnp.where(_in_range(d, jnp.full((1, R), D_WINDOW, jnp.int32)), s, NEG_INF)
    m = jnp.max(s, axis=0, keepdims=True)
    p = jnp.exp(s - m)
    l = jnp.sum(p, axis=0, keepdims=True)
    o_ref[...] = _group_rows(_dot(vT_ref[0, 0, :, pl.ds(start, span)], p.astype(BF16)) * (1.0 / l))


def _win_attention(proj, q_col, k, vT):
    B, Hkv, S, _ = k.shape
    nq = S // Q_BLOCK
    span = min(D_WINDOW + Q_BLOCK, S)
    in_specs = [
        _q_spec(S, q_col),
        pl.BlockSpec((1, 1, S, HEAD_DIM), lambda b, h, n: (b, h, 0, 0)),
        pl.BlockSpec((1, 1, HEAD_DIM, S), lambda b, h, n: (b, h, 0, 0)),
    ]
    return pl.pallas_call(
        functools.partial(_win_kernel, span), grid=(B, Hkv, nq), in_specs=in_specs,
        out_specs=_o_spec(S), out_shape=jax.ShapeDtypeStruct((B * S, Hkv * GROUP_COLS), F32),
        compiler_params=_params(3), name="nsa_window")(proj, k, vT)


def _route(logits):
    tm = logits.shape[0]
    lane = _iota((tm, LANES), 1)
    gl = jnp.where(lane < N_GROUPS, logits, NEG_INF)
    gmax = jnp.max(gl, axis=-1, keepdims=True)
    gidx = jnp.min(jnp.where(gl == gmax, lane, LANES), axis=-1, keepdims=True)
    g_prob = 1.0 / jnp.sum(jnp.exp(gl - gmax), axis=-1, keepdims=True)
    elane = lane - N_GROUPS
    in_group = (elane >= 0) & (elane < N_EXPERTS) & ((elane >> 4) == gidx)
    el = jnp.where(in_group, logits, NEG_INF)
    ee = jnp.exp(el - jnp.max(el, axis=-1, keepdims=True))
    ep = jnp.where(in_group, ee / jnp.sum(ee, axis=-1, keepdims=True), -1.0)
    p1 = jnp.max(ep, axis=-1, keepdims=True)
    i1 = jnp.min(jnp.where(ep == p1, lane, LANES), axis=-1, keepdims=True)
    ep2 = jnp.where(lane == i1, -1.0, ep)
    p2 = jnp.max(ep2, axis=-1, keepdims=True)
    i2 = jnp.min(jnp.where(ep2 == p2, lane, LANES), axis=-1, keepdims=True)
    den = p1 + p2
    vals = [(i1 - N_GROUPS).astype(F32), (i2 - N_GROUPS).astype(F32), g_prob * p1 / den, g_prob * p2 / den]
    out = jnp.zeros((tm, LANES), F32)
    for k, val in enumerate(vals):
        out = jnp.where(lane == k, val, out)
    return out


def _pack_bf16_pairs(h):
    n = h.shape[1] // 2
    bits = lax.bitcast_convert_type(h.astype(BF16).astype(F32), jnp.uint32)
    return bits[:, :n] | (bits[:, n:] >> 16)


def _unpack_bf16_pairs(packed):
    hi = lax.bitcast_convert_type(packed & jnp.uint32(0xFFFF0000), F32)
    lo = lax.bitcast_convert_type(packed << 16, F32)
    return jnp.concatenate([hi, lo], axis=1).astype(BF16)


def _out_tail(x_new, gain_ref, wr_ref, xo_ref, h_ref, route_ref):
    xo_ref[...] = x_new
    ms = jnp.mean(x_new * x_new, axis=-1, keepdims=True)
    h = x_new * lax.rsqrt(ms + NORM_EPS) * gain_ref[...]
    h_ref[...] = _pack_bf16_pairs(h)
    h_hi = h.astype(BF16)
    h_lo = (h - h_hi.astype(F32)).astype(BF16)
    logits = _dot(h_hi, wr_ref[0]) + (_dot(h_lo, wr_ref[0]) + _dot(h_hi, wr_ref[1]))
    route_ref[...] = _route(logits)


def _out_proj_kernel(oa_ref, ob_ref, x_ref, w_ref, gain_ref, wr_ref, xo_ref, h_ref, route_ref):
    half = w_ref.shape[0] // 2
    x_new = x_ref[...] + _dot(oa_ref[...], w_ref[0:half, :]) + _dot(ob_ref[...], w_ref[half:, :])
    _out_tail(x_new, gain_ref, wr_ref, xo_ref, h_ref, route_ref)


def _out_proj_nsa_kernel(oc_ref, b0_ref, b1_ref, b2_ref, gd_ref, x_ref, w_ref, gain_ref, wr_ref, xo_ref, h_ref, route_ref):
    half = D_HEADS * HEAD_DIM
    g = jax.nn.sigmoid(gd_ref[...])
    g_hi = g.astype(BF16)
    g_lo = (g - g_hi.astype(F32)).astype(BF16)
    src = _iota((LANES, half), 0)
    head3 = (_iota((LANES, half), 1) >> 6) * 3
    od = None
    for br, b_ref in enumerate((b0_ref, b1_ref, b2_ref)):
        spread = jnp.where(src == head3 + br, 1.0, 0.0).astype(BF16)
        term = (_dot(g_hi, spread) + _dot(g_lo, spread)) * b_ref[...]
        od = term if od is None else od + term
    x_new = x_ref[...] + _dot(oc_ref[...], w_ref[0:half, :]) + _dot(od.astype(BF16), w_ref[half:, :])
    _out_tail(x_new, gain_ref, wr_ref, xo_ref, h_ref, route_ref)


def _out_proj(o_parts, gd, x, w_out, gain, w_router):
    T = x.shape[0]
    tm = ROW_TILE
    row = lambda i: (i, 0)
    fixed = lambda i: (0, 0)
    if gd is None:
        kern = _out_proj_kernel
        args = list(o_parts)
        in_specs = [pl.BlockSpec((tm, D_MODEL // 2), row)] * 2
    else:
        kern = _out_proj_nsa_kernel
        args = list(o_parts) + [gd]
        in_specs = [pl.BlockSpec((tm, D_MODEL // 2), row)] * 4 + [pl.BlockSpec((tm, LANES), row)]
    args += [x, w_out, gain.reshape(1, D_MODEL), w_router]
    in_specs += [pl.BlockSpec((tm, D_MODEL), row), pl.BlockSpec((D_MODEL, D_MODEL), fixed),
                 pl.BlockSpec((1, D_MODEL), fixed), pl.BlockSpec((2, D_MODEL, LANES), lambda i: (0, 0, 0))]
    out_shape = [jax.ShapeDtypeStruct((T, D_MODEL), F32), jax.ShapeDtypeStruct((T, D_MODEL // 2), jnp.uint32),
                 jax.ShapeDtypeStruct((T, LANES), F32)]
    out_specs = [pl.BlockSpec((tm, D_MODEL), row), pl.BlockSpec((tm, D_MODEL // 2), row), pl.BlockSpec((tm, LANES), row)]
    return pl.pallas_call(kern, grid=(T // tm,), in_specs=in_specs, out_specs=out_specs, out_shape=out_shape,
                          compiler_params=_params(1), name="out_proj_router")(*args)


def _expert_kernel(be_ref, rows_ref, wg_ref, wu_ref, wd_ref, y_ref, wg_s, wu_s, wd_s):
    i = pl.program_id(0)
    n_blk = pl.num_programs(0)

    @pl.when((i == 0) | (be_ref[i] != be_ref[jnp.maximum(i - 1, 0)]))
    def _():
        wg_s[...] = wg_ref[0].astype(BF16)
        wu_s[...] = wu_ref[0].astype(BF16)
        wd_s[...] = wd_ref[0].astype(BF16)

    @pl.when(i < be_ref[n_blk])
    def _():
        xb = _unpack_bf16_pairs(rows_ref[...])
        hid = jax.nn.silu(_dot(xb, wg_s[...])) * _dot(xb, wu_s[...])
        y_ref[...] = _dot(hid.astype(BF16), wd_s[...])

    @pl.when(i >= be_ref[n_blk])
    def _():
        y_ref[...] = jnp.zeros(y_ref.shape, F32)


def _expert_ffn(rows, blk_info, wg, wu, wd):
    n_rows = rows.shape[0]
    n_blk = n_rows // MOE_ROWS
    grid_spec = pltpu.PrefetchScalarGridSpec(
        num_scalar_prefetch=1, grid=(n_blk,),
        in_specs=[
            pl.BlockSpec((MOE_ROWS, D_MODEL // 2), lambda i, be: (i, 0)),
            pl.BlockSpec((1, D_MODEL, EXPERT_HIDDEN), lambda i, be: (be[i], 0, 0)),
            pl.BlockSpec((1, D_MODEL, EXPERT_HIDDEN), lambda i, be: (be[i], 0, 0)),
            pl.BlockSpec((1, EXPERT_HIDDEN, D_MODEL), lambda i, be: (be[i], 0, 0)),
        ],
        out_specs=pl.BlockSpec((MOE_ROWS, D_MODEL), lambda i, be: (i, 0)),
        scratch_shapes=[pltpu.VMEM((D_MODEL, EXPERT_HIDDEN), BF16), pltpu.VMEM((D_MODEL, EXPERT_HIDDEN), BF16),
                        pltpu.VMEM((EXPERT_HIDDEN, D_MODEL), BF16)])
    return pl.pallas_call(
        _expert_kernel, grid_spec=grid_spec, out_shape=jax.ShapeDtypeStruct((n_rows, D_MODEL), F32),
        compiler_params=_params(1), name="expert_ffn")(blk_info, rows, wg, wu, wd)


def _moe_dispatch(route, h):
    n_tok = h.shape[0]
    n_asg = n_tok * MOE_TOPK
    i32 = jnp.int32
    e_flat = route[:, 0:MOE_TOPK].astype(i32).reshape(n_asg)
    is_e = e_flat[:, None] == jnp.arange(N_EXPERTS, dtype=i32)[None, :]
    counts = jnp.sum(is_e, axis=0, dtype=i32)
    order = jnp.argsort(e_flat).astype(i32)
    rank = jnp.argsort(order).astype(i32)
    padded = (counts + MOE_ROWS - 1) // MOE_ROWS * MOE_ROWS
    pad_end = jnp.cumsum(padded)
    pad_start = pad_end - padded
    start = jnp.cumsum(counts) - counts
    n_rows = n_asg + N_EXPERTS * MOE_ROWS
    n_blk = n_rows // MOE_ROWS
    blk_start = jnp.arange(n_blk, dtype=i32) * MOE_ROWS
    blk_expert = jnp.minimum(jnp.sum(pad_end[None, :] <= blk_start[:, None], axis=1, dtype=i32), N_EXPERTS - 1)
    within = (blk_start - pad_start[blk_expert])[:, None] + jnp.arange(MOE_ROWS, dtype=i32)[None, :]
    valid = within < counts[blk_expert][:, None]
    src = jnp.clip(start[blk_expert][:, None] + within, 0, n_asg - 1)
    row_tok = jnp.where(valid, (order // MOE_TOPK)[src], src // MOE_TOPK).reshape(n_rows)
    shift = jnp.sum(jnp.where(is_e, (pad_start - start)[None, :], 0), axis=1, dtype=i32)
    pos = (rank + shift).reshape(n_tok, MOE_TOPK)
    blk_info = jnp.concatenate([blk_expert, (pad_end[-1:] // MOE_ROWS).astype(i32)])
    return h[row_tok], blk_info, pos


def _moe(route, h, wg, wu, wd):
    rows, blk_info, pos = _moe_dispatch(route, h)
    y = _expert_ffn(rows, blk_info, wg, wu, wd)
    return y[pos[:, 0]], y[pos[:, 1]], route


def _final_kernel(x_ref, y0_ref, y1_ref, route_ref, g_ref, o_ref):
    x = _moe_combine(x_ref[...], y0_ref, y1_ref, route_ref)
    ms = jnp.mean(x * x, axis=-1, keepdims=True)
    o_ref[...] = x * lax.rsqrt(ms + NORM_EPS) * g_ref[...]


def _final_norm(x, ys, gain):
    T = x.shape[0]
    tm = ROW_TILE
    row = lambda i: (i, 0)
    return pl.pallas_call(
        _final_kernel, grid=(T // tm,),
        in_specs=[pl.BlockSpec((tm, D_MODEL), row)] * 3 + [pl.BlockSpec((tm, LANES), row), pl.BlockSpec((1, D_MODEL), lambda i: (0, 0))],
        out_specs=pl.BlockSpec((tm, D_MODEL), row), out_shape=jax.ShapeDtypeStruct((T, D_MODEL), F32),
        compiler_params=_params(1), name="final_norm")(x, *ys, gain.reshape(1, D_MODEL))


def _rope_tables(positions):
    inv_freq = ROPE_THETA ** (-jnp.arange(0, HEAD_DIM, 2, dtype=F32) / HEAD_DIM)
    ang = positions.astype(F32).reshape(-1, 1) * inv_freq
    cos, sin = jnp.cos(ang), jnp.sin(ang)
    reps = LANES // HEAD_DIM
    return jnp.tile(jnp.concatenate([cos, cos], axis=1), (1, reps)), jnp.tile(jnp.concatenate([-sin, sin], axis=1), (1, reps))


def _chunks(first, count, rope, action):
    return tuple((first + j, rope, action(j)) for j in range(count))


def _plan(*runs):
    entries = sum(runs, ())
    assert [e[0] for e in entries] == list(range(len(entries)))
    return tuple(e[1:] for e in entries)


AB_OUTS = (("tok", 4), ("heads", A_KV_HEADS), ("headsT", A_KV_HEADS), ("heads", B_HEADS), ("headsT", B_HEADS), ("headsT", B_HEADS))
AB_PLAN = _plan(_chunks(0, 4, True, lambda j: (("tok", 0, j),)),
                _chunks(4, 1, True, lambda j: (("heads", 1, 0),)),
                _chunks(5, 1, False, lambda j: (("headsT", 2, 0),)),
                _chunks(6, 4, False, lambda j: (("heads", 3, 2 * j),)),
                _chunks(10, 4, False, lambda j: (("headsT", 4, 2 * j),)),
                _chunks(14, 4, False, lambda j: (("headsT", 5, 2 * j),)))
CD_OUTS = (("tok", 8), ("keys_aug", C_KV_HEADS), ("kmean", 0), ("valsT_aug", C_KV_HEADS), ("heads", D_KV_HEADS),
           ("heads", D_KV_HEADS), ("keys_aug", D_KV_HEADS), ("valsT_aug", D_KV_HEADS), ("heads", D_KV_HEADS),
           ("headsT", D_KV_HEADS), ("f32", 0))
CD_PLAN = _plan(_chunks(0, 4, True, lambda j: (("tok", 0, j),)),
                _chunks(4, 1, True, lambda j: (("keys_aug", 1, C_BLOCK), ("kmean", 2, 0))),
                _chunks(5, 1, False, lambda j: (("valsT_aug", 3, 0),)),
                _chunks(6, 4, True, lambda j: (("tok", 0, 4 + j),)),
                _chunks(10, 1, True, lambda j: (("heads", 4, 0),)),
                _chunks(11, 1, False, lambda j: (("heads", 5, 0),)),
                _chunks(12, 1, True, lambda j: (("keys_aug", 6, D_SEL_LEN),)),
                _chunks(13, 1, False, lambda j: (("valsT_aug", 7, 0),)),
                _chunks(14, 1, True, lambda j: (("heads", 8, 0),)),
                _chunks(15, 1, False, lambda j: (("headsT", 9, 0),)),
                _chunks(16, 1, False, lambda j: (("f32", 10, 0),)))
QD_COL = 4 * LANES


def _router_weights(router_group, router_expert):
    pad = jnp.zeros((D_MODEL, LANES - N_GROUPS - N_EXPERTS), F32)
    w = jnp.concatenate([router_group.astype(F32), router_expert.astype(F32), pad], axis=1)
    hi = w.astype(BF16)
    return jnp.stack([hi, (w - hi.astype(F32)).astype(BF16)])


def _pad_cols(w, n):
    return jnp.concatenate([w, jnp.zeros((w.shape[0], n - w.shape[1]), w.dtype)], axis=1)


def _mixer_ab(parts, sinks):
    qa, ka, vaT, qb, kbT, vbT = parts
    oa = _swa_attention(qa, 0, ka, vaT, sinks)
    ob = _stick_attention(qb, kbT, vbT)
    return oa, ob


def _mixer_cd(parts, B, S, k_pe, k_w1, k_w2, v_pe, v_w1, v_w2):
    q_cd, kc_aug, kmean, vcT_aug, kdc, vdc, kds_aug, vdsT_aug, kdw, vdwT, _ = parts
    nblk = S // C_BLOCK
    km = kmean.reshape(B, nblk, C_KV_HEADS, HEAD_DIM).transpose(0, 2, 1, 3).astype(BF16)
    oc = _moba_attention(q_cd, 0, kc_aug, vcT_aug, km)
    k_cmp = _compress(kdc, k_pe, k_w1, k_w2)
    v_cmp = _compress(vdc, v_pe, v_w1, v_w2)
    nc = (S - D_CMP_LEN) // D_CMP_STRIDE + 1
    ncp = S // D_CMP_STRIDE
    nsel = S // D_SEL_LEN
    c_start = jnp.arange(ncp) * D_CMP_STRIDE
    b_start = jnp.arange(nsel) * D_SEL_LEN
    overlap = ((c_start[:, None] <= b_start[None, :] + D_SEL_LEN - 1) & (c_start[:, None] + D_CMP_LEN - 1 >= b_start[None, :])
               & (jnp.arange(ncp)[:, None] < nc)).astype(BF16)
    o_cmp, biasT = _cmp_select(q_cd, QD_COL, k_cmp, v_cmp.transpose(0, 1, 3, 2), overlap.T, nc, S)
    o_sel = _sel_attention(q_cd, QD_COL, kds_aug, vdsT_aug, biasT)
    o_win = _win_attention(q_cd, QD_COL, kdw, vdwT)
    return oc, o_cmp, o_sel, o_win


def kernel(x, positions, ln_mix_0, w_in_0, sinks_0, w_out_0, ln_ffn_0, router_group_0, router_expert_0, expert_gate_0, expert_up_0, expert_down_0, ln_mix_1, w_in_1, cmp_k_pe_1, cmp_k_w1_1, cmp_k_w2_1, cmp_v_pe_1, cmp_v_w1_1, cmp_v_w2_1, w_out_1, ln_ffn_1, router_group_1, router_expert_1, expert_gate_1, expert_up_1, expert_down_1, ln_final):
    B, S, _ = x.shape
    T = B * S
    assert S % KEY_TILE == 0 and T % ROW_TILE == 0
    cos_t, sin_t = _rope_tables(positions)
    xf = x.reshape(T, D_MODEL)

    parts = _in_proj(xf, None, ln_mix_0, w_in_0.astype(BF16), cos_t, sin_t, AB_PLAN, AB_OUTS, B, S)
    o_ab = _mixer_ab(parts, sinks_0)
    x1, h1, route1 = _out_proj(o_ab, None, xf, w_out_0.astype(BF16), ln_ffn_0, _router_weights(router_group_0, router_expert_0))
    ys = _moe(route1, h1, expert_gate_0, expert_up_0, expert_down_0)

    x2, *parts = _in_proj(x1, ys, ln_mix_1, _pad_cols(w_in_1.astype(BF16), len(CD_PLAN) * LANES), cos_t, sin_t,
                          CD_PLAN, CD_OUTS, B, S)
    gd = parts[-1]
    parts = _mixer_cd(parts, B, S, cmp_k_pe_1, cmp_k_w1_1, cmp_k_w2_1, cmp_v_pe_1, cmp_v_w1_1, cmp_v_w2_1)
    x3, h3, route3 = _out_proj(parts, gd, x2, w_out_1.astype(BF16), ln_ffn_1, _router_weights(router_group_1, router_expert_1))
    ys = _moe(route3, h3, expert_gate_1, expert_up_1, expert_down_1)
    return _final_norm(x3, ys, ln_final).reshape(B, S, D_MODEL)
```

```python
import functools

import jax
import jax.numpy as jnp
from jax import lax
from jax.experimental import pallas as pl
from jax.experimental.pallas import tpu as pltpu

D_MODEL = 1024
HEAD_DIM = 64
HALF = HEAD_DIM // 2
ROPE_THETA = 10000.0
NORM_EPS = 1e-6
Q_BLOCK = 128
SCALE = HEAD_DIM ** -0.5

A_HEADS, A_KV_HEADS, A_WINDOW = 8, 2, 128
B_HEADS = 8
C_HEADS, C_KV_HEADS, C_BLOCK, C_TOPK = 8, 2, 256, 3
D_HEADS, D_KV_HEADS = 8, 2
D_CMP_LEN, D_CMP_STRIDE, D_CMP_HIDDEN = 32, 16, 256
D_SEL_LEN, D_SEL_TOPK, D_WINDOW = 64, 16, 512
N_GROUPS, EXPERTS_PER_GROUP, MOE_TOPK, EXPERT_HIDDEN = 4, 16, 2, 512
N_EXPERTS = N_GROUPS * EXPERTS_PER_GROUP
GROUP = 4

LANES = 128
ROW_TILE = 512
MOE_ROWS = 256
KEY_TILE = 512
FLASH_MIN_SUM = 1e-25
REF_MARGIN = 1.05
MASKED = -1e30
V_AUG_ROWS = 80
AUG_ROWS = 8
MASK_BIG = 2.0 ** 100
PICKED = -1.0
SB_EXIT = -104.0
STICK_HEADS = 8
VMEM_LIMIT = 56 * 1024 * 1024

F32 = jnp.float32
BF16 = jnp.bfloat16
NEG_INF = float("-inf")


def _iota(shape, dim):
    return lax.broadcasted_iota(jnp.int32, shape, dim)


def _dot(a, b):
    return jnp.dot(a, b, preferred_element_type=F32)


def _in_range(d, width):
    return lax.bitcast_convert_type(d, jnp.uint32) < lax.bitcast_convert_type(jnp.asarray(width, jnp.int32), jnp.uint32)


def _params(n_grid):
    return pltpu.CompilerParams(dimension_semantics=("arbitrary",) * n_grid, vmem_limit_bytes=VMEM_LIMIT)


GROUP_COLS = GROUP * HEAD_DIM


def _group_qT(q_ref):
    t = (q_ref[...].astype(F32) * SCALE).T
    return jnp.concatenate([t[g * HEAD_DIM:(g + 1) * HEAD_DIM] for g in range(GROUP)], axis=1).astype(BF16)


def _group_rows(oT):
    return jnp.concatenate([oT[:, g * Q_BLOCK:(g + 1) * Q_BLOCK] for g in range(GROUP)], axis=0).T


def _q_spec(S, col0):
    nq = S // Q_BLOCK
    return pl.BlockSpec((Q_BLOCK, GROUP_COLS), lambda b, h, n: (b * nq + n, col0 // GROUP_COLS + h))


def _o_spec(S):
    nq = S // Q_BLOCK
    return pl.BlockSpec((Q_BLOCK, GROUP_COLS), lambda b, h, n: (b * nq + n, h))


def _moe_combine(x, y0_ref, y1_ref, route_ref):
    return x + (y0_ref[...] * route_ref[:, MOE_TOPK:MOE_TOPK + 1] + y1_ref[...] * route_ref[:, MOE_TOPK + 1:MOE_TOPK + 2])


def _emit_chunk(ch, actions, outs):
    tm = ch.shape[0]
    chT = None
    for kind, oi, arg in actions:
        o = outs[oi]
        if kind == "tok":
            o[:, arg * LANES:(arg + 1) * LANES] = ch.astype(BF16)
        elif kind == "f32":
            o[...] = ch
        elif kind == "kmean":
            o[...] = jnp.mean(ch.reshape(tm // C_BLOCK, C_BLOCK, LANES), axis=1).reshape(tm // C_BLOCK, 1, LANES)
        elif kind == "heads":
            for hh in range(2):
                o[0, arg + hh] = ch[:, hh * HEAD_DIM:(hh + 1) * HEAD_DIM].astype(BF16)
        elif kind == "keys_aug":
            lane = _iota((tm, LANES), 1)
            block_in_tile = _iota((tm, LANES), 0) >> (arg.bit_length() - 1)
            aug = jnp.where((lane - HEAD_DIM == block_in_tile) | (lane == HEAD_DIM + AUG_ROWS), 1.0, 0.0)
            for hh in range(2):
                keys = ch if hh == 0 else pltpu.roll(ch, HEAD_DIM, 1)
                o[0, hh] = jnp.where(lane < HEAD_DIM, keys, aug).astype(BF16)
        else:
            chT = ch.astype(BF16).astype(F32).T if chT is None else chT
            for hh in range(2):
                vT = chT[hh * HEAD_DIM:(hh + 1) * HEAD_DIM]
                if kind == "headsT":
                    o[0, arg + hh] = vT.astype(BF16)
                else:
                    assert kind == "valsT_aug"
                    tail = jnp.where(_iota((V_AUG_ROWS - HEAD_DIM, tm), 0) == 0, 1.0, 0.0)
                    o[0, hh] = jnp.concatenate([vT, tail], axis=0).astype(BF16)


def _in_proj_kernel(plan, combine, *refs):
    refs = list(refs)
    x_ref = refs.pop(0)
    if combine:
        y0_ref, y1_ref, route_ref = refs.pop(0), refs.pop(0), refs.pop(0)
    g_ref, w_ref, cos_ref, sin_ref = refs[:4]
    outs = refs[4:]
    x = x_ref[...]
    if combine:
        x = _moe_combine(x, y0_ref, y1_ref, route_ref)
        xo_ref = outs.pop(0)
        xo_ref[...] = x
    ms = jnp.mean(x * x, axis=-1, keepdims=True)
    h = (x * lax.rsqrt(ms + NORM_EPS) * g_ref[...]).astype(BF16)
    tm = x.shape[0]
    cos = cos_ref[...]
    sin = sin_ref[...]
    first_half = (_iota((tm, LANES), 1) & (HEAD_DIM - 1)) < HALF
    for c, (rope, actions) in enumerate(plan):
        ch = _dot(h, w_ref[:, c * LANES:(c + 1) * LANES])
        if rope:
            partner = jnp.where(first_half, pltpu.roll(ch, LANES - HALF, 1), pltpu.roll(ch, HALF, 1))
            ch = ch * cos + partner * sin
        _emit_chunk(ch, actions, outs)


def _in_proj(x, ys, gain, w, cos_t, sin_t, plan, out_kinds, B, S):
    T = x.shape[0]
    n_cols = w.shape[1]
    tm = ROW_TILE
    assert tm == KEY_TILE and S % tm == 0 and len(plan) * LANES == n_cols
    tpb = S // tm
    combine = ys is not None
    row = lambda i: (i, 0)
    fixed = lambda i: (0, 0)
    by_seq = lambda i: (i // tpb, 0, i % tpb, 0)
    by_seq_t = lambda i: (i // tpb, 0, 0, i % tpb)
    in_specs = [pl.BlockSpec((tm, D_MODEL), row)]
    args = [x]
    if combine:
        in_specs += [pl.BlockSpec((tm, D_MODEL), row)] * 2 + [pl.BlockSpec((tm, LANES), row)]
        args += list(ys)
    in_specs += [pl.BlockSpec((1, D_MODEL), fixed), pl.BlockSpec((D_MODEL, n_cols), fixed),
                 pl.BlockSpec((tm, LANES), row), pl.BlockSpec((tm, LANES), row)]
    args += [gain.reshape(1, D_MODEL), w, cos_t, sin_t]
    out_shape, out_specs = [], []
    if combine:
        out_shape.append(jax.ShapeDtypeStruct((T, D_MODEL), F32))
        out_specs.append(pl.BlockSpec((tm, D_MODEL), row))
    for kind, size in out_kinds:
        if kind == "tok":
            shape, dtype, spec = (T, size * LANES), BF16, pl.BlockSpec((tm, size * LANES), row)
        elif kind == "f32":
            shape, dtype, spec = (T, LANES), F32, pl.BlockSpec((tm, LANES), row)
        elif kind == "kmean":
            shape, dtype, spec = (T // C_BLOCK, 1, LANES), F32, pl.BlockSpec((tm // C_BLOCK, 1, LANES), lambda i: (i, 0, 0))
        elif kind == "heads":
            shape, dtype, spec = (B, size, S, HEAD_DIM), BF16, pl.BlockSpec((1, size, tm, HEAD_DIM), by_seq)
        elif kind == "keys_aug":
            shape, dtype, spec = (B, size, S, LANES), BF16, pl.BlockSpec((1, size, tm, LANES), by_seq)
        elif kind == "headsT":
            shape, dtype, spec = (B, size, HEAD_DIM, S), BF16, pl.BlockSpec((1, size, HEAD_DIM, tm), by_seq_t)
        else:
            assert kind == "valsT_aug"
            shape, dtype, spec = (B, size, V_AUG_ROWS, S), BF16, pl.BlockSpec((1, size, V_AUG_ROWS, tm), by_seq_t)
        out_shape.append(jax.ShapeDtypeStruct(shape, dtype))
        out_specs.append(spec)
    kern = functools.partial(_in_proj_kernel, plan, combine)
    return pl.pallas_call(kern, grid=(T // tm,), in_specs=in_specs, out_specs=out_specs, out_shape=out_shape,
                          compiler_params=_params(1), name="in_proj")(*args)


def _swa_kernel(q_ref, kp_ref, ko_ref, vp_ref, vo_ref, sink_ref, o_ref):
    n = pl.program_id(2)
    qs = _group_qT(q_ref)
    R = qs.shape[1]
    k = jnp.concatenate([kp_ref[0, 0], ko_ref[0, 0]], axis=0)
    vT = jnp.concatenate([vp_ref[0, 0], vo_ref[0, 0]], axis=1)
    s = _dot(k, qs)
    qpos = _iota((1, R), 1) & (Q_BLOCK - 1)
    d = qpos + Q_BLOCK - _iota((2 * Q_BLOCK, R), 0)
    width = jnp.minimum(A_WINDOW, qpos + 1 + jnp.where(n > 0, Q_BLOCK, 0))
    s = jnp.where(_in_range(d, width), s, NEG_INF)
    sink = sink_ref[0]
    m = jnp.maximum(jnp.max(s, axis=0, keepdims=True), sink)
    p = jnp.exp(s - m)
    den = jnp.sum(p, axis=0, keepdims=True) + jnp.exp(sink - m)
    o_ref[...] = _group_rows(_dot(vT, p.astype(BF16)) * (1.0 / den)).astype(o_ref.dtype)


def _swa_attention(proj, q_col, k, vT, sinks):
    B, Hkv, S, _ = k.shape
    nb = S // Q_BLOCK
    R = GROUP * Q_BLOCK
    sink_row = jnp.repeat(sinks.astype(F32).reshape(Hkv, GROUP), Q_BLOCK, axis=1).reshape(Hkv, 1, R)
    prev = lambda n: jnp.maximum(n - 1, 0)
    in_specs = [
        _q_spec(S, q_col),
        pl.BlockSpec((1, 1, Q_BLOCK, HEAD_DIM), lambda b, h, n: (b, h, prev(n), 0)),
        pl.BlockSpec((1, 1, Q_BLOCK, HEAD_DIM), lambda b, h, n: (b, h, n, 0)),
        pl.BlockSpec((1, 1, HEAD_DIM, Q_BLOCK), lambda b, h, n: (b, h, 0, prev(n))),
        pl.BlockSpec((1, 1, HEAD_DIM, Q_BLOCK), lambda b, h, n: (b, h, 0, n)),
        pl.BlockSpec((1, 1, R), lambda b, h, n: (h, 0, 0)),
    ]
    return pl.pallas_call(
        _swa_kernel, grid=(B, Hkv, nb), in_specs=in_specs,
        out_specs=_o_spec(S), out_shape=jax.ShapeDtypeStruct((B * S, Hkv * GROUP_COLS), BF16),
        compiler_params=_params(3), name="swa_attention")(proj, k, k, vT, vT, sink_row)


def _stick_kernel(q_ref, kT_ref, vT_ref, o_ref):
    n = pl.program_id(2)
    tq = tk = Q_BLOCK
    heads = q_ref.shape[1]
    row = _iota((tq, tk), 0)
    col = _iota((tk, tk), 1)
    upper = jnp.where(_iota((tk, tk), 0) > col, 1.0, 0.0).astype(BF16)
    tpos = n * tq + row

    def body(carry):
        kb, _, cs, accs = carry
        start = pl.multiple_of(kb * tk, tk)
        past = (start + col) < tpos
        hs = range(heads)
        zs = [_dot(q_ref[0, h], kT_ref[0, h, :, pl.ds(start, tk)]) * SCALE for h in hs]
        sps = [jnp.maximum(z, 0.0) + jnp.log(1.0 + jnp.exp(-jnp.abs(z))) for z in zs]
        stays = [jnp.where(past, -sp, 0.0) for sp in sps]
        his = [st.astype(BF16) for st in stays]
        los = [(st - hi.astype(F32)).astype(BF16) for st, hi in zip(stays, his)]
        betweens = [_dot(hi, upper) + _dot(lo, upper) for hi, lo in zip(his, los)]
        ws = [jnp.where(past, jnp.exp(zs[h] - sps[h] + betweens[h] + cs[h]), 0.0).astype(BF16) for h in hs]
        new_accs = [accs[h] + lax.dot_general(ws[h], vT_ref[0, h, :, pl.ds(start, tk)], (((1,), (1,)), ((), ())),
                                              preferred_element_type=F32) for h in hs]
        new_cs = [cs[h] + jnp.sum(stays[h], axis=-1, keepdims=True) for h in hs]
        worst = functools.reduce(jnp.maximum, new_cs)
        return kb - 1, jnp.max(worst) > SB_EXIT, tuple(new_cs), tuple(new_accs)

    def cond(carry):
        kb, alive, _, _ = carry
        return (kb >= 0) & alive

    init = (n, jnp.array(True), (jnp.zeros((tq, 1), F32),) * heads, (jnp.zeros((tq, HEAD_DIM), F32),) * heads)
    _, _, _, accs = lax.while_loop(cond, body, init)
    o_ref[...] = jnp.concatenate(accs, axis=1).astype(o_ref.dtype)


def _stick_attention(q, kT, v):
    B, H, S, _ = q.shape
    nq = S // Q_BLOCK
    hb = STICK_HEADS
    resident = dict(pipeline_mode=pl.Buffered(1))
    in_specs = [
        pl.BlockSpec((1, hb, Q_BLOCK, HEAD_DIM), lambda b, h, n: (b, h, n, 0)),
        pl.BlockSpec((1, hb, HEAD_DIM, S), lambda b, h, n: (b, h, 0, 0), **resident),
        pl.BlockSpec((1, hb, HEAD_DIM, S), lambda b, h, n: (b, h, 0, 0), **resident),
    ]
    return pl.pallas_call(
        _stick_kernel, grid=(B, H // hb, nq), in_specs=in_specs,
        out_specs=pl.BlockSpec((Q_BLOCK, hb * HEAD_DIM), lambda b, h, n: (b * nq + n, h)),
        out_shape=jax.ShapeDtypeStruct((B * S, H * HEAD_DIM), BF16),
        compiler_params=_params(3), name="stick_attention")(q, kT, v)


def _max_key_sqnorm(ka_ref, out_ref):
    S = ka_ref.shape[2]
    tk = KEY_TILE
    is_key_lane = _iota((tk, LANES), 1) < HEAD_DIM
    ones = jnp.ones((LANES, LANES), BF16)

    def body(i, mx):
        k = jnp.where(is_key_lane, ka_ref[0, 0, pl.ds(pl.multiple_of(i * tk, tk), tk), :].astype(F32), 0.0)
        sq = _dot((k * k).astype(BF16), ones)
        return jnp.maximum(mx, jnp.max(sq, axis=0, keepdims=True))

    mx = lax.fori_loop(0, S // tk, body, jnp.zeros((1, LANES), F32))
    out_ref[...] = jnp.broadcast_to(mx, out_ref.shape)


def _flash_scratch(R):
    return [pltpu.VMEM((AUG_ROWS, LANES), F32), pltpu.VMEM((2, KEY_TILE, R), BF16), pltpu.VMEM((V_AUG_ROWS, R), F32)]


def _masked_flash_t(n, qT, bias_rows, ka_ref, vTa_ref, ksq_ref, p_scr, acc_scr):
    R = qT.shape[1]
    tk = KEY_TILE
    diag = (n * Q_BLOCK) // tk
    zpad = jnp.zeros((LANES - HEAD_DIM - 2 * AUG_ROWS, R), F32)
    causal = diag * tk + _iota((tk, R), 0) <= n * Q_BLOCK + (_iota((tk, R), 1) & (Q_BLOCK - 1))

    def scores(kt, ref_rows):
        st = pl.multiple_of(kt * tk, tk)
        low = jnp.concatenate([bias_rows(kt), ref_rows, zpad], axis=0).astype(BF16)
        return _dot(ka_ref[0, 0, pl.ds(st, tk), :], jnp.concatenate([qT, low], axis=0))

    def values(kt):
        return vTa_ref[0, 0, :, pl.ds(pl.multiple_of(kt * tk, tk), tk)]

    qsq = jnp.sum(jnp.square(qT.astype(F32)), axis=0, keepdims=True)
    ksq = jnp.concatenate([ksq_ref[0:1, :]] * (R // LANES), axis=1)
    ref = jnp.sqrt(qsq * ksq) * REF_MARGIN
    ref_rows = jnp.where(_iota((AUG_ROWS, R), 0) == 0, -ref, 0.0)

    def fast_body(kt, carry):
        slot = kt & 1
        s = scores(kt, ref_rows)
        acc_scr[...] += _dot(values(jnp.maximum(kt - 1, 0)), p_scr[1 - slot])
        p_scr[slot] = jnp.exp(s).astype(BF16)
        return carry

    p_scr[1] = jnp.zeros((tk, R), BF16)
    acc_scr[...] = jnp.zeros(acc_scr.shape, F32)
    lax.fori_loop(0, diag, fast_body, 0)
    acc = acc_scr[...] + _dot(values(jnp.maximum(diag - 1, 0)), p_scr[1 - (diag & 1)])
    s = jnp.where(causal, scores(diag, ref_rows), MASKED)
    acc = acc + _dot(values(diag), jnp.exp(s).astype(BF16))

    def running_max_path(_):
        no_ref = jnp.zeros((AUG_ROWS, R), F32)

        def update(kt, s, m, acc):
            m_new = jnp.maximum(m, jnp.max(s, axis=0, keepdims=True))
            p = jnp.exp(s - m_new).astype(BF16)
            return m_new, jnp.exp(m - m_new) * acc + _dot(values(kt), p)

        init = (jnp.full((1, R), MASKED, F32), jnp.zeros((vTa_ref.shape[2], R), F32))
        m, acc = lax.fori_loop(0, diag, lambda kt, c: update(kt, scores(kt, no_ref), *c), init)
        return update(diag, jnp.where(causal, scores(diag, no_ref), MASKED), m, acc)[1]

    healthy = jnp.min(acc[HEAD_DIM:HEAD_DIM + 1, :]) >= FLASH_MIN_SUM
    acc = lax.cond(healthy, lambda _: acc, running_max_path, 0)
    return acc[0:HEAD_DIM] * (1.0 / acc[HEAD_DIM:HEAD_DIM + 1])


def _moba_kernel(q_ref, ka_ref, vTa_ref, km_ref, o_ref, bias_scr, ksq_scr, p_scr, acc_scr):
    n = pl.program_id(2)
    pl.when(n == 0)(lambda: _max_key_sqnorm(ka_ref, ksq_scr))
    nblk = km_ref.shape[2]
    qs = _group_qT(q_ref)
    R = qs.shape[1]
    own = (n * Q_BLOCK) // C_BLOCK
    gate = _dot(km_ref[0, 0], qs)
    blk = _iota((nblk, R), 0)
    gate = jnp.where(blk < own, gate, NEG_INF)
    bias = jnp.where(blk == own, 0.0, -MASK_BIG)
    for _ in range(C_TOPK):
        mx = jnp.max(gate, axis=0, keepdims=True)
        idx = jnp.min(jnp.where(gate == mx, blk, nblk), axis=0, keepdims=True)
        hit = blk == idx
        bias = jnp.where(hit, jnp.where(mx > NEG_INF, 0.0, bias), bias)
        gate = jnp.where(hit, NEG_INF, gate)
    per = KEY_TILE // C_BLOCK
    rows = bias_scr.shape[0]
    r = _iota((rows, nblk), 0)
    spread = jnp.where(((r & (AUG_ROWS - 1)) < per) & (_iota((rows, nblk), 1) == (r >> 3) * per + (r & (AUG_ROWS - 1))), 1.0, 0.0)
    bias_scr[...] = _dot(spread.astype(BF16), bias.astype(BF16))
    oT = _masked_flash_t(n, qs, lambda kt: bias_scr[pl.ds(pl.multiple_of(kt * AUG_ROWS, AUG_ROWS), AUG_ROWS), :],
                         ka_ref, vTa_ref, ksq_scr, p_scr, acc_scr)
    o_ref[...] = _group_rows(oT).astype(o_ref.dtype)


def _moba_attention(proj, q_col, ka, vTa, km):
    B, Hkv, S, _ = ka.shape
    nq = S // Q_BLOCK
    R = GROUP * Q_BLOCK
    nblk = km.shape[2]
    nkt = S // KEY_TILE
    in_specs = [
        _q_spec(S, q_col),
        pl.BlockSpec((1, 1, S, LANES), lambda b, h, n: (b, h, 0, 0)),
        pl.BlockSpec((1, 1, V_AUG_ROWS, S), lambda b, h, n: (b, h, 0, 0)),
        pl.BlockSpec((1, 1, nblk, HEAD_DIM), lambda b, h, n: (b, h, 0, 0)),
    ]
    return pl.pallas_call(
        _moba_kernel, grid=(B, Hkv, nq), in_specs=in_specs,
        out_specs=_o_spec(S), out_shape=jax.ShapeDtypeStruct((B * S, Hkv * GROUP_COLS), BF16),
        scratch_shapes=[pltpu.VMEM((nkt * AUG_ROWS, R), F32)] + _flash_scratch(R),
        compiler_params=_params(3), name="moba_attention")(proj, ka, vTa, km)


def _compress_kernel(u_ref, us_ref, pe_ref, w1_ref, w2_ref, o_ref):
    a = (u_ref[0, 0].astype(F32) + pe_ref[0:1, :]).astype(BF16)
    b = (us_ref[0, 0].astype(F32) + pe_ref[1:2, :]).astype(BF16)
    pre = _dot(a, w1_ref[0]) + _dot(b, w1_ref[1])
    hid = jax.nn.gelu(pre)
    o_ref[0, 0] = _dot(hid.astype(BF16), w2_ref[...]).astype(o_ref.dtype)


def _compress(t, pe, w1, w2):
    B, H, S, _ = t.shape
    nrow = S // D_CMP_STRIDE
    width = D_CMP_STRIDE * HEAD_DIM
    u = t.reshape(B, H, nrow, width)
    us = jnp.concatenate([u[:, :, 1:], jnp.zeros((B, H, 1, width), u.dtype)], axis=2)
    blk = lambda b, h: (b, h, 0, 0)
    in_specs = [
        pl.BlockSpec((1, 1, nrow, width), blk), pl.BlockSpec((1, 1, nrow, width), blk),
        pl.BlockSpec((2, width), lambda b, h: (0, 0)),
        pl.BlockSpec((2, width, D_CMP_HIDDEN), lambda b, h: (0, 0, 0)),
        pl.BlockSpec((D_CMP_HIDDEN, HEAD_DIM), lambda b, h: (0, 0)),
    ]
    return pl.pallas_call(
        _compress_kernel, grid=(B, H), in_specs=in_specs,
        out_specs=pl.BlockSpec((1, 1, nrow, HEAD_DIM), blk),
        out_shape=jax.ShapeDtypeStruct((B, H, nrow, HEAD_DIM), BF16),
        compiler_params=_params(2), name="nsa_compress")(
            u, us, pe.astype(F32).reshape(2, width), w1.astype(BF16).reshape(2, width, D_CMP_HIDDEN), w2.astype(BF16))


def _cmp_select_kernel(nc, q_ref, kc_ref, vcT_ref, ovT_ref, oc_ref, bias_ref):
    n = pl.program_id(2)
    ncp = kc_ref.shape[2]
    nsel = ovT_ref.shape[0]
    qs = _group_qT(q_ref)
    R = qs.shape[1]
    s = _dot(kc_ref[0, 0], qs)
    tpos = n * Q_BLOCK + (_iota((1, R), 1) & (Q_BLOCK - 1))
    c_last = jnp.minimum((tpos - (D_CMP_LEN - 1)) >> 4, nc - 1)
    s = jnp.where(_iota((ncp, R), 0) <= c_last, s, NEG_INF)
    m = jnp.max(s, axis=0, keepdims=True)
    m = jnp.where(m > NEG_INF, m, 0.0)
    e = jnp.exp(s - m)
    den = jnp.sum(e, axis=0, keepdims=True)
    p = (e * (1.0 / jnp.where(den > 0, den, 1.0))).astype(BF16)
    oc_ref[...] = _group_rows(_dot(vcT_ref[0, 0], p))
    imp_heads = _dot(ovT_ref[...], p)
    imp = imp_heads[:, 0:Q_BLOCK]
    for g in range(1, GROUP):
        imp = imp + imp_heads[:, g * Q_BLOCK:(g + 1) * Q_BLOCK]
    t = n * Q_BLOCK + _iota((nsel, Q_BLOCK), 1)
    j = _iota((nsel, Q_BLOCK), 0)
    cur = t >> 6
    imp = jnp.where(j <= cur, imp, NEG_INF)
    imp = jnp.where(_in_range(cur - j, 2), float("inf"), imp)
    imp = jnp.where(j == 0, float("inf"), imp)
    for _ in range(D_SEL_TOPK):
        mx = jnp.max(imp, axis=0, keepdims=True)
        idx = jnp.min(jnp.where(imp == mx, j, nsel), axis=0, keepdims=True)
        imp = jnp.where(j == idx, PICKED, imp)
    bias_ref[0, 0, 0] = jnp.where(imp == PICKED, 0.0, -MASK_BIG)


def _cmp_select(proj, q_col, kc, vcT, overlapT, nc, S):
    B, Hkv, ncp, _ = kc.shape
    nq = S // Q_BLOCK
    nsel = overlapT.shape[0]
    in_specs = [
        _q_spec(S, q_col),
        pl.BlockSpec((1, 1, ncp, HEAD_DIM), lambda b, h, n: (b, h, 0, 0)),
        pl.BlockSpec((1, 1, HEAD_DIM, ncp), lambda b, h, n: (b, h, 0, 0)),
        pl.BlockSpec((nsel, ncp), lambda b, h, n: (0, 0)),
    ]
    out_specs = [_o_spec(S), pl.BlockSpec((1, 1, 1, nsel, Q_BLOCK), lambda b, h, n: (b, h, n, 0, 0))]
    out_shape = [jax.ShapeDtypeStruct((B * S, Hkv * GROUP_COLS), F32), jax.ShapeDtypeStruct((B, Hkv, nq, nsel, Q_BLOCK), F32)]
    return pl.pallas_call(
        functools.partial(_cmp_select_kernel, nc), grid=(B, Hkv, nq), in_specs=in_specs, out_specs=out_specs,
        out_shape=out_shape, compiler_params=_params(3), name="nsa_cmp_select")(proj, kc, vcT, overlapT)


def _sel_kernel(q_ref, ka_ref, vTa_ref, bias_ref, o_ref, ksq_scr, p_scr, acc_scr):
    n = pl.program_id(2)
    pl.when(n == 0)(lambda: _max_key_sqnorm(ka_ref, ksq_scr))
    qs = _group_qT(q_ref)

    def bias_rows(kt):
        b = bias_ref[0, 0, 0, pl.ds(pl.multiple_of(kt * AUG_ROWS, AUG_ROWS), AUG_ROWS), :]
        return jnp.concatenate([b] * GROUP, axis=1)

    o_ref[...] = _group_rows(_masked_flash_t(n, qs, bias_rows, ka_ref, vTa_ref, ksq_scr, p_scr, acc_scr))


def _sel_attention(proj, q_col, ka, vTa, biasT):
    B, Hkv, S, _ = ka.shape
    nq = S // Q_BLOCK
    nsel = biasT.shape[3]
    assert KEY_TILE // D_SEL_LEN == AUG_ROWS
    in_specs = [
        _q_spec(S, q_col),
        pl.BlockSpec((1, 1, S, LANES), lambda b, h, n: (b, h, 0, 0)),
        pl.BlockSpec((1, 1, V_AUG_ROWS, S), lambda b, h, n: (b, h, 0, 0)),
        pl.BlockSpec((1, 1, 1, nsel, Q_BLOCK), lambda b, h, n: (b, h, n, 0, 0)),
    ]
    return pl.pallas_call(
        _sel_kernel, grid=(B, Hkv, nq), in_specs=in_specs,
        out_specs=_o_spec(S), out_shape=jax.ShapeDtypeStruct((B * S, Hkv * GROUP_COLS), F32),
        scratch_shapes=_flash_scratch(GROUP * Q_BLOCK),
        compiler_params=_params(3), name="nsa_selected")(proj, ka, vTa, biasT)


def _win_kernel(span, q_ref, k_ref, vT_ref, o_ref):
    n = pl.program_id(2)
    qs = _group_qT(q_ref)
    R = qs.shape[1]
    start = pl.multiple_of(jnp.maximum(n * Q_BLOCK + Q_BLOCK - span, 0), Q_BLOCK)
    s = _dot(k_ref[0, 0, pl.ds(start, span), :], qs)
    tpos = n * Q_BLOCK + (_iota((1, R), 1) & (Q_BLOCK - 1))
    d = tpos - (start + _iota((span, R), 0))
    s = jnp.where(_in_range(d, jnp.full((1, R), D_WINDOW, jnp.int32)), s, NEG_INF)
    m = jnp.max(s, axis=0, keepdims=True)
    p = jnp.exp(s - m)
    l = jnp.sum(p, axis=0, keepdims=True)
    o_ref[...] = _group_rows(_dot(vT_ref[0, 0, :, pl.ds(start, span)], p.astype(BF16)) * (1.0 / l))


def _win_attention(proj, q_col, k, vT):
    B, Hkv, S, _ = k.shape
    nq = S // Q_BLOCK
    span = min(D_WINDOW + Q_BLOCK, S)
    in_specs = [
        _q_spec(S, q_col),
        pl.BlockSpec((1, 1, S, HEAD_DIM), lambda b, h, n: (b, h, 0, 0)),
        pl.BlockSpec((1, 1, HEAD_DIM, S), lambda b, h, n: (b, h, 0, 0)),
    ]
    return pl.pallas_call(
        functools.partial(_win_kernel, span), grid=(B, Hkv, nq), in_specs=in_specs,
        out_specs=_o_spec(S), out_shape=jax.ShapeDtypeStruct((B * S, Hkv * GROUP_COLS), F32),
        compiler_params=_params(3), name="nsa_window")(proj, k, vT)


def _route(logits):
    tm = logits.shape[0]
    lane = _iota((tm, LANES), 1)
    gl = jnp.where(lane < N_GROUPS, logits, NEG_INF)
    gmax = jnp.max(gl, axis=-1, keepdims=True)
    gidx = jnp.min(jnp.where(gl == gmax, lane, LANES), axis=-1, keepdims=True)
    g_prob = 1.0 / jnp.sum(jnp.exp(gl - gmax), axis=-1, keepdims=True)
    elane = lane - N_GROUPS
    in_group = (elane >= 0) & (elane < N_EXPERTS) & ((elane >> 4) == gidx)
    el = jnp.where(in_group, logits, NEG_INF)
    ee = jnp.exp(el - jnp.max(el, axis=-1, keepdims=True))
    ep = jnp.where(in_group, ee / jnp.sum(ee, axis=-1, keepdims=True), -1.0)
    p1 = jnp.max(ep, axis=-1, keepdims=True)
    i1 = jnp.min(jnp.where(ep == p1, lane, LANES), axis=-1, keepdims=True)
    ep2 = jnp.where(lane == i1, -1.0, ep)
    p2 = jnp.max(ep2, axis=-1, keepdims=True)
    i2 = jnp.min(jnp.where(ep2 == p2, lane, LANES), axis=-1, keepdims=True)
    den = p1 + p2
    vals = [(i1 - N_GROUPS).astype(F32), (i2 - N_GROUPS).astype(F32), g_prob * p1 / den, g_prob * p2 / den]
    out = jnp.zeros((tm, LANES), F32)
    for k, val in enumerate(vals):
        out = jnp.where(lane == k, val, out)
    return out


def _pack_bf16_pairs(h):
    n = h.shape[1] // 2
    bits = lax.bitcast_convert_type(h.astype(BF16).astype(F32), jnp.uint32)
    return bits[:, :n] | (bits[:, n:] >> 16)


def _unpack_bf16_pairs(packed):
    hi = lax.bitcast_convert_type(packed & jnp.uint32(0xFFFF0000), F32)
    lo = lax.bitcast_convert_type(packed << 16, F32)
    return jnp.concatenate([hi, lo], axis=1).astype(BF16)


def _out_tail(x_new, gain_ref, wr_ref, xo_ref, h_ref, route_ref):
    xo_ref[...] = x_new
    ms = jnp.mean(x_new * x_new, axis=-1, keepdims=True)
    h = x_new * lax.rsqrt(ms + NORM_EPS) * gain_ref[...]
    h_ref[...] = _pack_bf16_pairs(h)
    h_hi = h.astype(BF16)
    h_lo = (h - h_hi.astype(F32)).astype(BF16)
    logits = _dot(h_hi, wr_ref[0]) + (_dot(h_lo, wr_ref[0]) + _dot(h_hi, wr_ref[1]))
    route_ref[...] = _route(logits)


def _out_proj_kernel(oa_ref, ob_ref, x_ref, w_ref, gain_ref, wr_ref, xo_ref, h_ref, route_ref):
    half = w_ref.shape[0] // 2
    x_new = x_ref[...] + _dot(oa_ref[...], w_ref[0:half, :]) + _dot(ob_ref[...], w_ref[half:, :])
    _out_tail(x_new, gain_ref, wr_ref, xo_ref, h_ref, route_ref)


def _out_proj_nsa_kernel(oc_ref, b0_ref, b1_ref, b2_ref, gd_ref, x_ref, w_ref, gain_ref, wr_ref, xo_ref, h_ref, route_ref):
    half = D_HEADS * HEAD_DIM
    g = jax.nn.sigmoid(gd_ref[...])
    g_hi = g.astype(BF16)
    g_lo = (g - g_hi.astype(F32)).astype(BF16)
    src = _iota((LANES, half), 0)
    head3 = (_iota((LANES, half), 1) >> 6) * 3
    od = None
    for br, b_ref in enumerate((b0_ref, b1_ref, b2_ref)):
        spread = jnp.where(src == head3 + br, 1.0, 0.0).astype(BF16)
        term = (_dot(g_hi, spread) + _dot(g_lo, spread)) * b_ref[...]
        od = term if od is None else od + term
    x_new = x_ref[...] + _dot(oc_ref[...], w_ref[0:half, :]) + _dot(od.astype(BF16), w_ref[half:, :])
    _out_tail(x_new, gain_ref, wr_ref, xo_ref, h_ref, route_ref)


def _out_proj(o_parts, gd, x, w_out, gain, w_router):
    T = x.shape[0]
    tm = ROW_TILE
    row = lambda i: (i, 0)
    fixed = lambda i: (0, 0)
    if gd is None:
        kern = _out_proj_kernel
        args = list(o_parts)
        in_specs = [pl.BlockSpec((tm, D_MODEL // 2), row)] * 2
    else:
        kern = _out_proj_nsa_kernel
        args = list(o_parts) + [gd]
        in_specs = [pl.BlockSpec((tm, D_MODEL // 2), row)] * 4 + [pl.BlockSpec((tm, LANES), row)]
    args += [x, w_out, gain.reshape(1, D_MODEL), w_router]
    in_specs += [pl.BlockSpec((tm, D_MODEL), row), pl.BlockSpec((D_MODEL, D_MODEL), fixed),
                 pl.BlockSpec((1, D_MODEL), fixed), pl.BlockSpec((2, D_MODEL, LANES), lambda i: (0, 0, 0))]
    out_shape = [jax.ShapeDtypeStruct((T, D_MODEL), F32), jax.ShapeDtypeStruct((T, D_MODEL // 2), jnp.uint32),
                 jax.ShapeDtypeStruct((T, LANES), F32)]
    out_specs = [pl.BlockSpec((tm, D_MODEL), row), pl.BlockSpec((tm, D_MODEL // 2), row), pl.BlockSpec((tm, LANES), row)]
    return pl.pallas_call(kern, grid=(T // tm,), in_specs=in_specs, out_specs=out_specs, out_shape=out_shape,
                          compiler_params=_params(1), name="out_proj_router")(*args)


def _expert_kernel(be_ref, rows_ref, wg_ref, wu_ref, wd_ref, y_ref, wg_s, wu_s, wd_s):
    i = pl.program_id(0)
    n_blk = pl.num_programs(0)

    @pl.when((i == 0) | (be_ref[i] != be_ref[jnp.maximum(i - 1, 0)]))
    def _():
        wg_s[...] = wg_ref[0].astype(BF16)
        wu_s[...] = wu_ref[0].astype(BF16)
        wd_s[...] = wd_ref[0].astype(BF16)

    @pl.when(i < be_ref[n_blk])
    def _():
        xb = _unpack_bf16_pairs(rows_ref[...])
        hid = jax.nn.silu(_dot(xb, wg_s[...])) * _dot(xb, wu_s[...])
        y_ref[...] = _dot(hid.astype(BF16), wd_s[...])

    @pl.when(i >= be_ref[n_blk])
    def _():
        y_ref[...] = jnp.zeros(y_ref.shape, F32)


def _expert_ffn(rows, blk_info, wg, wu, wd):
    n_rows = rows.shape[0]
    n_blk = n_rows // MOE_ROWS
    grid_spec = pltpu.PrefetchScalarGridSpec(
        num_scalar_prefetch=1, grid=(n_blk,),
        in_specs=[
            pl.BlockSpec((MOE_ROWS, D_MODEL // 2), lambda i, be: (i, 0)),
            pl.BlockSpec((1, D_MODEL, EXPERT_HIDDEN), lambda i, be: (be[i], 0, 0)),
            pl.BlockSpec((1, D_MODEL, EXPERT_HIDDEN), lambda i, be: (be[i], 0, 0)),
            pl.BlockSpec((1, EXPERT_HIDDEN, D_MODEL), lambda i, be: (be[i], 0, 0)),
        ],
        out_specs=pl.BlockSpec((MOE_ROWS, D_MODEL), lambda i, be: (i, 0)),
        scratch_shapes=[pltpu.VMEM((D_MODEL, EXPERT_HIDDEN), BF16), pltpu.VMEM((D_MODEL, EXPERT_HIDDEN), BF16),
                        pltpu.VMEM((EXPERT_HIDDEN, D_MODEL), BF16)])
    return pl.pallas_call(
        _expert_kernel, grid_spec=grid_spec, out_shape=jax.ShapeDtypeStruct((n_rows, D_MODEL), F32),
        compiler_params=_params(1), name="expert_ffn")(blk_info, rows, wg, wu, wd)


def _moe_dispatch(route, h):
    n_tok = h.shape[0]
    n_asg = n_tok * MOE_TOPK
    i32 = jnp.int32
    e_flat = route[:, 0:MOE_TOPK].astype(i32).reshape(n_asg)
    is_e = e_flat[:, None] == jnp.arange(N_EXPERTS, dtype=i32)[None, :]
    counts = jnp.sum(is_e, axis=0, dtype=i32)
    assert n_asg & (n_asg - 1) == 0 and n_asg <= 1 << 16
    idx = jnp.arange(n_asg, dtype=jnp.uint32)
    low = jnp.uint32(n_asg - 1)
    order = (jnp.sort(e_flat.astype(jnp.uint32) * n_asg + idx) & low).astype(i32)
    rank = (jnp.sort(order.astype(jnp.uint32) * n_asg + idx) & low).astype(i32)
    padded = (counts + MOE_ROWS - 1) // MOE_ROWS * MOE_ROWS
    pad_end = jnp.cumsum(padded)
    pad_start = pad_end - padded
    start = jnp.cumsum(counts) - counts
    n_rows = n_asg + N_EXPERTS * MOE_ROWS
    n_blk = n_rows // MOE_ROWS
    blk_start = jnp.arange(n_blk, dtype=i32) * MOE_ROWS
    blk_expert = jnp.minimum(jnp.sum(pad_end[None, :] <= blk_start[:, None], axis=1, dtype=i32), N_EXPERTS - 1)
    within = (blk_start - pad_start[blk_expert])[:, None] + jnp.arange(MOE_ROWS, dtype=i32)[None, :]
    valid = within < counts[blk_expert][:, None]
    src = jnp.clip(start[blk_expert][:, None] + within, 0, n_asg - 1)
    row_tok = jnp.where(valid, (order // MOE_TOPK)[src], src // MOE_TOPK).reshape(n_rows)
    shift = jnp.sum(jnp.where(is_e, (pad_start - start)[None, :], 0), axis=1, dtype=i32)
    pos = (rank + shift).reshape(n_tok, MOE_TOPK)
    blk_info = jnp.concatenate([blk_expert, (pad_end[-1:] // MOE_ROWS).astype(i32)])
    return h[row_tok], blk_info, pos


def _moe(route, h, wg, wu, wd):
    rows, blk_info, pos = _moe_dispatch(route, h)
    y = _expert_ffn(rows, blk_info, wg, wu, wd)
    return y[pos[:, 0]], y[pos[:, 1]], route


def _final_kernel(x_ref, y0_ref, y1_ref, route_ref, g_ref, o_ref):
    x = _moe_combine(x_ref[...], y0_ref, y1_ref, route_ref)
    ms = jnp.mean(x * x, axis=-1, keepdims=True)
    o_ref[...] = x * lax.rsqrt(ms + NORM_EPS) * g_ref[...]


def _final_norm(x, ys, gain):
    T = x.shape[0]
    tm = ROW_TILE
    row = lambda i: (i, 0)
    return pl.pallas_call(
        _final_kernel, grid=(T // tm,),
        in_specs=[pl.BlockSpec((tm, D_MODEL), row)] * 3 + [pl.BlockSpec((tm, LANES), row), pl.BlockSpec((1, D_MODEL), lambda i: (0, 0))],
        out_specs=pl.BlockSpec((tm, D_MODEL), row), out_shape=jax.ShapeDtypeStruct((T, D_MODEL), F32),
        compiler_params=_params(1), name="final_norm")(x, *ys, gain.reshape(1, D_MODEL))


def _rope_tables(positions):
    inv_freq = ROPE_THETA ** (-jnp.arange(0, HEAD_DIM, 2, dtype=F32) / HEAD_DIM)
    ang = positions.astype(F32).reshape(-1, 1) * inv_freq
    cos, sin = jnp.cos(ang), jnp.sin(ang)
    reps = LANES // HEAD_DIM
    return jnp.tile(jnp.concatenate([cos, cos], axis=1), (1, reps)), jnp.tile(jnp.concatenate([-sin, sin], axis=1), (1, reps))


def _chunks(first, count, rope, action):
    return tuple((first + j, rope, action(j)) for j in range(count))


def _plan(*runs):
    entries = sum(runs, ())
    assert [e[0] for e in entries] == list(range(len(entries)))
    return tuple(e[1:] for e in entries)


AB_OUTS = (("tok", 4), ("heads", A_KV_HEADS), ("headsT", A_KV_HEADS), ("heads", B_HEADS), ("headsT", B_HEADS), ("headsT", B_HEADS))
AB_PLAN = _plan(_chunks(0, 4, True, lambda j: (("tok", 0, j),)),
                _chunks(4, 1, True, lambda j: (("heads", 1, 0),)),
                _chunks(5, 1, False, lambda j: (("headsT", 2, 0),)),
                _chunks(6, 4, False, lambda j: (("heads", 3, 2 * j),)),
                _chunks(10, 4, False, lambda j: (("headsT", 4, 2 * j),)),
                _chunks(14, 4, False, lambda j: (("headsT", 5, 2 * j),)))
CD_OUTS = (("tok", 8), ("keys_aug", C_KV_HEADS), ("kmean", 0), ("valsT_aug", C_KV_HEADS), ("heads", D_KV_HEADS),
           ("heads", D_KV_HEADS), ("keys_aug", D_KV_HEADS), ("valsT_aug", D_KV_HEADS), ("heads", D_KV_HEADS),
           ("headsT", D_KV_HEADS), ("f32", 0))
CD_PLAN = _plan(_chunks(0, 4, True, lambda j: (("tok", 0, j),)),
                _chunks(4, 1, True, lambda j: (("keys_aug", 1, C_BLOCK), ("kmean", 2, 0))),
                _chunks(5, 1, False, lambda j: (("valsT_aug", 3, 0),)),
                _chunks(6, 4, True, lambda j: (("tok", 0, 4 + j),)),
                _chunks(10, 1, True, lambda j: (("heads", 4, 0),)),
                _chunks(11, 1, False, lambda j: (("heads", 5, 0),)),
                _chunks(12, 1, True, lambda j: (("keys_aug", 6, D_SEL_LEN),)),
                _chunks(13, 1, False, lambda j: (("valsT_aug", 7, 0),)),
                _chunks(14, 1, True, lambda j: (("heads", 8, 0),)),
                _chunks(15, 1, False, lambda j: (("headsT", 9, 0),)),
                _chunks(16, 1, False, lambda j: (("f32", 10, 0),)))
QD_COL = 4 * LANES


def _router_weights(router_group, router_expert):
    pad = jnp.zeros((D_MODEL, LANES - N_GROUPS - N_EXPERTS), F32)
    w = jnp.concatenate([router_group.astype(F32), router_expert.astype(F32), pad], axis=1)
    hi = w.astype(BF16)
    return jnp.stack([hi, (w - hi.astype(F32)).astype(BF16)])


def _pad_cols(w, n):
    return jnp.concatenate([w, jnp.zeros((w.shape[0], n - w.shape[1]), w.dtype)], axis=1)


def _mixer_ab(parts, sinks):
    qa, ka, vaT, qb, kbT, vbT = parts
    oa = _swa_attention(qa, 0, ka, vaT, sinks)
    ob = _stick_attention(qb, kbT, vbT)
    return oa, ob


def _mixer_cd(parts, B, S, k_pe, k_w1, k_w2, v_pe, v_w1, v_w2):
    q_cd, kc_aug, kmean, vcT_aug, kdc, vdc, kds_aug, vdsT_aug, kdw, vdwT, _ = parts
    nblk = S // C_BLOCK
    km = kmean.reshape(B, nblk, C_KV_HEADS, HEAD_DIM).transpose(0, 2, 1, 3).astype(BF16)
    oc = _moba_attention(q_cd, 0, kc_aug, vcT_aug, km)
    k_cmp = _compress(kdc, k_pe, k_w1, k_w2)
    v_cmp = _compress(vdc, v_pe, v_w1, v_w2)
    nc = (S - D_CMP_LEN) // D_CMP_STRIDE + 1
    ncp = S // D_CMP_STRIDE
    nsel = S // D_SEL_LEN
    c_start = jnp.arange(ncp) * D_CMP_STRIDE
    b_start = jnp.arange(nsel) * D_SEL_LEN
    overlap = ((c_start[:, None] <= b_start[None, :] + D_SEL_LEN - 1) & (c_start[:, None] + D_CMP_LEN - 1 >= b_start[None, :])
               & (jnp.arange(ncp)[:, None] < nc)).astype(BF16)
    o_cmp, biasT = _cmp_select(q_cd, QD_COL, k_cmp, v_cmp.transpose(0, 1, 3, 2), overlap.T, nc, S)
    o_sel = _sel_attention(q_cd, QD_COL, kds_aug, vdsT_aug, biasT)
    o_win = _win_attention(q_cd, QD_COL, kdw, vdwT)
    return oc, o_cmp, o_sel, o_win


def kernel(x, positions, ln_mix_0, w_in_0, sinks_0, w_out_0, ln_ffn_0, router_group_0, router_expert_0, expert_gate_0, expert_up_0, expert_down_0, ln_mix_1, w_in_1, cmp_k_pe_1, cmp_k_w1_1, cmp_k_w2_1, cmp_v_pe_1, cmp_v_w1_1, cmp_v_w2_1, w_out_1, ln_ffn_1, router_group_1, router_expert_1, expert_gate_1, expert_up_1, expert_down_1, ln_final):
    B, S, _ = x.shape
    T = B * S
    assert S % KEY_TILE == 0 and T % ROW_TILE == 0
    cos_t, sin_t = _rope_tables(positions)
    xf = x.reshape(T, D_MODEL)

    parts = _in_proj(xf, None, ln_mix_0, w_in_0.astype(BF16), cos_t, sin_t, AB_PLAN, AB_OUTS, B, S)
    o_ab = _mixer_ab(parts, sinks_0)
    x1, h1, route1 = _out_proj(o_ab, None, xf, w_out_0.astype(BF16), ln_ffn_0, _router_weights(router_group_0, router_expert_0))
    ys = _moe(route1, h1, expert_gate_0, expert_up_0, expert_down_0)

    x2, *parts = _in_proj(x1, ys, ln_mix_1, _pad_cols(w_in_1.astype(BF16), len(CD_PLAN) * LANES), cos_t, sin_t,
                          CD_PLAN, CD_OUTS, B, S)
    gd = parts[-1]
    parts = _mixer_cd(parts, B, S, cmp_k_pe_1, cmp_k_w1_1, cmp_k_w2_1, cmp_v_pe_1, cmp_v_w1_1, cmp_v_w2_1)
    x3, h3, route3 = _out_proj(parts, gd, x2, w_out_1.astype(BF16), ln_ffn_1, _router_weights(router_group_1, router_expert_1))
    ys = _moe(route3, h3, expert_gate_1, expert_up_1, expert_down_1)
    return _final_norm(x3, ys, ln_final).reshape(B, S, D_MODEL)
```

```python
import functools

import jax
import jax.numpy as jnp
from jax import lax
from jax.experimental import pallas as pl
from jax.experimental.pallas import tpu as pltpu

D_MODEL = 1024
HEAD_DIM = 64
HALF = HEAD_DIM // 2
ROPE_THETA = 10000.0
NORM_EPS = 1e-6
Q_BLOCK = 128
SCALE = HEAD_DIM ** -0.5

A_HEADS, A_KV_HEADS, A_WINDOW = 8, 2, 128
B_HEADS = 8
C_HEADS, C_KV_HEADS, C_BLOCK, C_TOPK = 8, 2, 256, 3
D_HEADS, D_KV_HEADS = 8, 2
D_CMP_LEN, D_CMP_STRIDE, D_CMP_HIDDEN = 32, 16, 256
D_SEL_LEN, D_SEL_TOPK, D_WINDOW = 64, 16, 512
N_GROUPS, EXPERTS_PER_GROUP, MOE_TOPK, EXPERT_HIDDEN = 4, 16, 2, 512
N_EXPERTS = N_GROUPS * EXPERTS_PER_GROUP
GROUP = 4

LANES = 128
ROW_TILE = 512
MOE_ROWS = 256
KEY_TILE = 512
FLASH_MIN_SUM = 1e-25
REF_MARGIN = 1.05
MASKED = -1e30
V_AUG_ROWS = 80
AUG_ROWS = 8
MASK_BIG = 2.0 ** 100
CMP_PARTS = 4
PICKED = -1.0
SB_EXIT = -104.0
STICK_HEADS = 8
VMEM_LIMIT = 56 * 1024 * 1024

F32 = jnp.float32
BF16 = jnp.bfloat16
NEG_INF = float("-inf")


def _iota(shape, dim):
    return lax.broadcasted_iota(jnp.int32, shape, dim)


def _dot(a, b):
    return jnp.dot(a, b, preferred_element_type=F32)


def _in_range(d, width):
    return lax.bitcast_convert_type(d, jnp.uint32) < lax.bitcast_convert_type(jnp.asarray(width, jnp.int32), jnp.uint32)


def _params(n_grid):
    return pltpu.CompilerParams(dimension_semantics=("arbitrary",) * n_grid, vmem_limit_bytes=VMEM_LIMIT)


GROUP_COLS = GROUP * HEAD_DIM


def _group_qT(q_ref):
    t = (q_ref[...].astype(F32) * SCALE).T
    return jnp.concatenate([t[g * HEAD_DIM:(g + 1) * HEAD_DIM] for g in range(GROUP)], axis=1).astype(BF16)


def _group_rows(oT):
    return jnp.concatenate([oT[:, g * Q_BLOCK:(g + 1) * Q_BLOCK] for g in range(GROUP)], axis=0).T


def _q_spec(S, col0):
    nq = S // Q_BLOCK
    return pl.BlockSpec((Q_BLOCK, GROUP_COLS), lambda b, h, n: (b * nq + n, col0 // GROUP_COLS + h))


def _o_spec(S):
    nq = S // Q_BLOCK
    return pl.BlockSpec((Q_BLOCK, GROUP_COLS), lambda b, h, n: (b * nq + n, h))


def _moe_combine(x, y0_ref, y1_ref, route_ref):
    return x + (y0_ref[...] * route_ref[:, MOE_TOPK:MOE_TOPK + 1] + y1_ref[...] * route_ref[:, MOE_TOPK + 1:MOE_TOPK + 2])


def _emit_chunk(ch, actions, outs):
    tm = ch.shape[0]
    chT = None
    for kind, oi, arg in actions:
        o = outs[oi]
        if kind == "tok":
            o[:, arg * LANES:(arg + 1) * LANES] = ch.astype(BF16)
        elif kind == "f32":
            o[...] = ch
        elif kind == "kmean":
            o[...] = jnp.mean(ch.reshape(tm // C_BLOCK, C_BLOCK, LANES), axis=1).reshape(tm // C_BLOCK, 1, LANES)
        elif kind == "heads":
            for hh in range(2):
                o[0, arg + hh] = ch[:, hh * HEAD_DIM:(hh + 1) * HEAD_DIM].astype(BF16)
        elif kind == "keys_aug":
            lane = _iota((tm, LANES), 1)
            block_in_tile = _iota((tm, LANES), 0) >> (arg.bit_length() - 1)
            aug = jnp.where((lane - HEAD_DIM == block_in_tile) | (lane == HEAD_DIM + AUG_ROWS), 1.0, 0.0)
            for hh in range(2):
                keys = ch if hh == 0 else pltpu.roll(ch, HEAD_DIM, 1)
                o[0, hh] = jnp.where(lane < HEAD_DIM, keys, aug).astype(BF16)
        else:
            chT = ch.astype(BF16).astype(F32).T if chT is None else chT
            for hh in range(2):
                vT = chT[hh * HEAD_DIM:(hh + 1) * HEAD_DIM]
                if kind == "headsT":
                    o[0, arg + hh] = vT.astype(BF16)
                else:
                    assert kind == "valsT_aug"
                    tail = jnp.where(_iota((V_AUG_ROWS - HEAD_DIM, tm), 0) == 0, 1.0, 0.0)
                    o[0, hh] = jnp.concatenate([vT, tail], axis=0).astype(BF16)


def _in_proj_kernel(plan, combine, *refs):
    refs = list(refs)
    x_ref = refs.pop(0)
    if combine:
        y0_ref, y1_ref, route_ref = refs.pop(0), refs.pop(0), refs.pop(0)
    g_ref, w_ref, cos_ref, sin_ref = refs[:4]
    outs = refs[4:]
    x = x_ref[...]
    if combine:
        x = _moe_combine(x, y0_ref, y1_ref, route_ref)
        xo_ref = outs.pop(0)
        xo_ref[...] = x
    ms = jnp.mean(x * x, axis=-1, keepdims=True)
    h = (x * lax.rsqrt(ms + NORM_EPS) * g_ref[...]).astype(BF16)
    tm = x.shape[0]
    cos = cos_ref[...]
    sin = sin_ref[...]
    first_half = (_iota((tm, LANES), 1) & (HEAD_DIM - 1)) < HALF
    for c, (rope, actions) in enumerate(plan):
        ch = _dot(h, w_ref[:, c * LANES:(c + 1) * LANES])
        if rope:
            partner = jnp.where(first_half, pltpu.roll(ch, LANES - HALF, 1), pltpu.roll(ch, HALF, 1))
            ch = ch * cos + partner * sin
        _emit_chunk(ch, actions, outs)


def _in_proj(x, ys, gain, w, cos_t, sin_t, plan, out_kinds, B, S):
    T = x.shape[0]
    n_cols = w.shape[1]
    tm = ROW_TILE
    assert tm == KEY_TILE and S % tm == 0 and len(plan) * LANES == n_cols
    tpb = S // tm
    combine = ys is not None
    row = lambda i: (i, 0)
    fixed = lambda i: (0, 0)
    by_seq = lambda i: (i // tpb, 0, i % tpb, 0)
    by_seq_t = lambda i: (i // tpb, 0, 0, i % tpb)
    in_specs = [pl.BlockSpec((tm, D_MODEL), row)]
    args = [x]
    if combine:
        in_specs += [pl.BlockSpec((tm, D_MODEL), row)] * 2 + [pl.BlockSpec((tm, LANES), row)]
        args += list(ys)
    in_specs += [pl.BlockSpec((1, D_MODEL), fixed), pl.BlockSpec((D_MODEL, n_cols), fixed),
                 pl.BlockSpec((tm, LANES), row), pl.BlockSpec((tm, LANES), row)]
    args += [gain.reshape(1, D_MODEL), w, cos_t, sin_t]
    out_shape, out_specs = [], []
    if combine:
        out_shape.append(jax.ShapeDtypeStruct((T, D_MODEL), F32))
        out_specs.append(pl.BlockSpec((tm, D_MODEL), row))
    for kind, size in out_kinds:
        if kind == "tok":
            shape, dtype, spec = (T, size * LANES), BF16, pl.BlockSpec((tm, size * LANES), row)
        elif kind == "f32":
            shape, dtype, spec = (T, LANES), F32, pl.BlockSpec((tm, LANES), row)
        elif kind == "kmean":
            shape, dtype, spec = (T // C_BLOCK, 1, LANES), F32, pl.BlockSpec((tm // C_BLOCK, 1, LANES), lambda i: (i, 0, 0))
        elif kind == "heads":
            shape, dtype, spec = (B, size, S, HEAD_DIM), BF16, pl.BlockSpec((1, size, tm, HEAD_DIM), by_seq)
        elif kind == "keys_aug":
            shape, dtype, spec = (B, size, S, LANES), BF16, pl.BlockSpec((1, size, tm, LANES), by_seq)
        elif kind == "headsT":
            shape, dtype, spec = (B, size, HEAD_DIM, S), BF16, pl.BlockSpec((1, size, HEAD_DIM, tm), by_seq_t)
        else:
            assert kind == "valsT_aug"
            shape, dtype, spec = (B, size, V_AUG_ROWS, S), BF16, pl.BlockSpec((1, size, V_AUG_ROWS, tm), by_seq_t)
        out_shape.append(jax.ShapeDtypeStruct(shape, dtype))
        out_specs.append(spec)
    kern = functools.partial(_in_proj_kernel, plan, combine)
    return pl.pallas_call(kern, grid=(T // tm,), in_specs=in_specs, out_specs=out_specs, out_shape=out_shape,
                          compiler_params=_params(1), name="in_proj")(*args)


def _swa_kernel(q_ref, kp_ref, ko_ref, vp_ref, vo_ref, sink_ref, o_ref):
    n = pl.program_id(2)
    qs = _group_qT(q_ref)
    R = qs.shape[1]
    k = jnp.concatenate([kp_ref[0, 0], ko_ref[0, 0]], axis=0)
    vT = jnp.concatenate([vp_ref[0, 0], vo_ref[0, 0]], axis=1)
    s = _dot(k, qs)
    qpos = _iota((1, R), 1) & (Q_BLOCK - 1)
    d = qpos + Q_BLOCK - _iota((2 * Q_BLOCK, R), 0)
    width = jnp.minimum(A_WINDOW, qpos + 1 + jnp.where(n > 0, Q_BLOCK, 0))
    s = jnp.where(_in_range(d, width), s, NEG_INF)
    sink = sink_ref[0]
    m = jnp.maximum(jnp.max(s, axis=0, keepdims=True), sink)
    p = jnp.exp(s - m)
    den = jnp.sum(p, axis=0, keepdims=True) + jnp.exp(sink - m)
    o_ref[...] = _group_rows(_dot(vT, p.astype(BF16)) * (1.0 / den)).astype(o_ref.dtype)


def _swa_attention(proj, q_col, k, vT, sinks):
    B, Hkv, S, _ = k.shape
    nb = S // Q_BLOCK
    R = GROUP * Q_BLOCK
    sink_row = jnp.repeat(sinks.astype(F32).reshape(Hkv, GROUP), Q_BLOCK, axis=1).reshape(Hkv, 1, R)
    prev = lambda n: jnp.maximum(n - 1, 0)
    in_specs = [
        _q_spec(S, q_col),
        pl.BlockSpec((1, 1, Q_BLOCK, HEAD_DIM), lambda b, h, n: (b, h, prev(n), 0)),
        pl.BlockSpec((1, 1, Q_BLOCK, HEAD_DIM), lambda b, h, n: (b, h, n, 0)),
        pl.BlockSpec((1, 1, HEAD_DIM, Q_BLOCK), lambda b, h, n: (b, h, 0, prev(n))),
        pl.BlockSpec((1, 1, HEAD_DIM, Q_BLOCK), lambda b, h, n: (b, h, 0, n)),
        pl.BlockSpec((1, 1, R), lambda b, h, n: (h, 0, 0)),
    ]
    return pl.pallas_call(
        _swa_kernel, grid=(B, Hkv, nb), in_specs=in_specs,
        out_specs=_o_spec(S), out_shape=jax.ShapeDtypeStruct((B * S, Hkv * GROUP_COLS), BF16),
        compiler_params=_params(3), name="swa_attention")(proj, k, k, vT, vT, sink_row)


def _stick_kernel(q_ref, kT_ref, vT_ref, o_ref):
    n = pl.program_id(2)
    tq = tk = Q_BLOCK
    heads = q_ref.shape[1]
    row = _iota((tq, tk), 0)
    col = _iota((tk, tk), 1)
    upper = jnp.where(_iota((tk, tk), 0) > col, 1.0, 0.0).astype(BF16)
    tpos = n * tq + row

    def body(carry):
        kb, _, cs, accs = carry
        start = pl.multiple_of(kb * tk, tk)
        past = (start + col) < tpos
        hs = range(heads)
        zs = [_dot(q_ref[0, h], kT_ref[0, h, :, pl.ds(start, tk)]) * SCALE for h in hs]
        sps = [jnp.maximum(z, 0.0) + jnp.log(1.0 + jnp.exp(-jnp.abs(z))) for z in zs]
        stays = [jnp.where(past, -sp, 0.0) for sp in sps]
        his = [st.astype(BF16) for st in stays]
        los = [(st - hi.astype(F32)).astype(BF16) for st, hi in zip(stays, his)]
        betweens = [_dot(hi, upper) + _dot(lo, upper) for hi, lo in zip(his, los)]
        ws = [jnp.where(past, jnp.exp(zs[h] - sps[h] + betweens[h] + cs[h]), 0.0).astype(BF16) for h in hs]
        new_accs = [accs[h] + lax.dot_general(ws[h], vT_ref[0, h, :, pl.ds(start, tk)], (((1,), (1,)), ((), ())),
                                              preferred_element_type=F32) for h in hs]
        new_cs = [cs[h] + jnp.sum(stays[h], axis=-1, keepdims=True) for h in hs]
        worst = functools.reduce(jnp.maximum, new_cs)
        return kb - 1, jnp.max(worst) > SB_EXIT, tuple(new_cs), tuple(new_accs)

    def cond(carry):
        kb, alive, _, _ = carry
        return (kb >= 0) & alive

    init = (n, jnp.array(True), (jnp.zeros((tq, 1), F32),) * heads, (jnp.zeros((tq, HEAD_DIM), F32),) * heads)
    _, _, _, accs = lax.while_loop(cond, body, init)
    o_ref[...] = jnp.concatenate(accs, axis=1).astype(o_ref.dtype)


def _stick_attention(q, kT, v):
    B, H, S, _ = q.shape
    nq = S // Q_BLOCK
    hb = STICK_HEADS
    resident = dict(pipeline_mode=pl.Buffered(1))
    in_specs = [
        pl.BlockSpec((1, hb, Q_BLOCK, HEAD_DIM), lambda b, h, n: (b, h, n, 0)),
        pl.BlockSpec((1, hb, HEAD_DIM, S), lambda b, h, n: (b, h, 0, 0), **resident),
        pl.BlockSpec((1, hb, HEAD_DIM, S), lambda b, h, n: (b, h, 0, 0), **resident),
    ]
    return pl.pallas_call(
        _stick_kernel, grid=(B, H // hb, nq), in_specs=in_specs,
        out_specs=pl.BlockSpec((Q_BLOCK, hb * HEAD_DIM), lambda b, h, n: (b * nq + n, h)),
        out_shape=jax.ShapeDtypeStruct((B * S, H * HEAD_DIM), BF16),
        compiler_params=_params(3), name="stick_attention")(q, kT, v)


def _max_key_sqnorm(ka_ref, out_ref):
    S = ka_ref.shape[2]
    tk = KEY_TILE
    is_key_lane = _iota((tk, LANES), 1) < HEAD_DIM
    ones = jnp.ones((LANES, LANES), BF16)

    def body(i, mx):
        k = jnp.where(is_key_lane, ka_ref[0, 0, pl.ds(pl.multiple_of(i * tk, tk), tk), :].astype(F32), 0.0)
        sq = _dot((k * k).astype(BF16), ones)
        return jnp.maximum(mx, jnp.max(sq, axis=0, keepdims=True))

    mx = lax.fori_loop(0, S // tk, body, jnp.zeros((1, LANES), F32))
    out_ref[...] = jnp.broadcast_to(mx, out_ref.shape)


def _flash_scratch(R):
    return [pltpu.VMEM((AUG_ROWS, LANES), F32), pltpu.VMEM((2, KEY_TILE, R), BF16), pltpu.VMEM((V_AUG_ROWS, R), F32)]


def _masked_flash_t(n, qT, bias_rows, ka_ref, vTa_ref, ksq_ref, p_scr, acc_scr):
    R = qT.shape[1]
    tk = KEY_TILE
    diag = (n * Q_BLOCK) // tk
    zpad = jnp.zeros((LANES - HEAD_DIM - 2 * AUG_ROWS, R), F32)
    causal = diag * tk + _iota((tk, R), 0) <= n * Q_BLOCK + (_iota((tk, R), 1) & (Q_BLOCK - 1))

    def scores(kt, ref_rows):
        st = pl.multiple_of(kt * tk, tk)
        low = jnp.concatenate([bias_rows(kt), ref_rows, zpad], axis=0).astype(BF16)
        return _dot(ka_ref[0, 0, pl.ds(st, tk), :], jnp.concatenate([qT, low], axis=0))

    def values(kt):
        return vTa_ref[0, 0, :, pl.ds(pl.multiple_of(kt * tk, tk), tk)]

    qsq = jnp.sum(jnp.square(qT.astype(F32)), axis=0, keepdims=True)
    ksq = jnp.concatenate([ksq_ref[0:1, :]] * (R // LANES), axis=1)
    ref = jnp.sqrt(qsq * ksq) * REF_MARGIN
    ref_rows = jnp.where(_iota((AUG_ROWS, R), 0) == 0, -ref, 0.0)

    def fast_body(kt, carry):
        slot = kt & 1
        s = scores(kt, ref_rows)
        acc_scr[...] += _dot(values(jnp.maximum(kt - 1, 0)), p_scr[1 - slot])
        p_scr[slot] = jnp.exp(s).astype(BF16)
        return carry

    p_scr[1] = jnp.zeros((tk, R), BF16)
    acc_scr[...] = jnp.zeros(acc_scr.shape, F32)
    lax.fori_loop(0, diag, fast_body, 0)
    acc = acc_scr[...] + _dot(values(jnp.maximum(diag - 1, 0)), p_scr[1 - (diag & 1)])
    s = jnp.where(causal, scores(diag, ref_rows), MASKED)
    acc = acc + _dot(values(diag), jnp.exp(s).astype(BF16))

    def running_max_path(_):
        no_ref = jnp.zeros((AUG_ROWS, R), F32)

        def update(kt, s, m, acc):
            m_new = jnp.maximum(m, jnp.max(s, axis=0, keepdims=True))
            p = jnp.exp(s - m_new).astype(BF16)
            return m_new, jnp.exp(m - m_new) * acc + _dot(values(kt), p)

        init = (jnp.full((1, R), MASKED, F32), jnp.zeros((vTa_ref.shape[2], R), F32))
        m, acc = lax.fori_loop(0, diag, lambda kt, c: update(kt, scores(kt, no_ref), *c), init)
        return update(diag, jnp.where(causal, scores(diag, no_ref), MASKED), m, acc)[1]

    healthy = jnp.min(acc[HEAD_DIM:HEAD_DIM + 1, :]) >= FLASH_MIN_SUM
    acc = lax.cond(healthy, lambda _: acc, running_max_path, 0)
    return acc[0:HEAD_DIM] * (1.0 / acc[HEAD_DIM:HEAD_DIM + 1])


def _moba_kernel(q_ref, ka_ref, vTa_ref, km_ref, o_ref, bias_scr, ksq_scr, p_scr, acc_scr):
    n = pl.program_id(2)
    pl.when(n == 0)(lambda: _max_key_sqnorm(ka_ref, ksq_scr))
    nblk = km_ref.shape[2]
    qs = _group_qT(q_ref)
    R = qs.shape[1]
    own = (n * Q_BLOCK) // C_BLOCK
    gate = _dot(km_ref[0, 0], qs)
    blk = _iota((nblk, R), 0)
    gate = jnp.where(blk < own, gate, NEG_INF)
    bias = jnp.where(blk == own, 0.0, -MASK_BIG)
    for _ in range(C_TOPK):
        mx = jnp.max(gate, axis=0, keepdims=True)
        idx = jnp.min(jnp.where(gate == mx, blk, nblk), axis=0, keepdims=True)
        hit = blk == idx
        bias = jnp.where(hit, jnp.where(mx > NEG_INF, 0.0, bias), bias)
        gate = jnp.where(hit, NEG_INF, gate)
    per = KEY_TILE // C_BLOCK
    rows = bias_scr.shape[0]
    r = _iota((rows, nblk), 0)
    spread = jnp.where(((r & (AUG_ROWS - 1)) < per) & (_iota((rows, nblk), 1) == (r >> 3) * per + (r & (AUG_ROWS - 1))), 1.0, 0.0)
    bias_scr[...] = _dot(spread.astype(BF16), bias.astype(BF16))
    oT = _masked_flash_t(n, qs, lambda kt: bias_scr[pl.ds(pl.multiple_of(kt * AUG_ROWS, AUG_ROWS), AUG_ROWS), :],
                         ka_ref, vTa_ref, ksq_scr, p_scr, acc_scr)
    o_ref[...] = _group_rows(oT).astype(o_ref.dtype)


def _moba_attention(proj, q_col, ka, vTa, km):
    B, Hkv, S, _ = ka.shape
    nq = S // Q_BLOCK
    R = GROUP * Q_BLOCK
    nblk = km.shape[2]
    nkt = S // KEY_TILE
    in_specs = [
        _q_spec(S, q_col),
        pl.BlockSpec((1, 1, S, LANES), lambda b, h, n: (b, h, 0, 0)),
        pl.BlockSpec((1, 1, V_AUG_ROWS, S), lambda b, h, n: (b, h, 0, 0)),
        pl.BlockSpec((1, 1, nblk, HEAD_DIM), lambda b, h, n: (b, h, 0, 0)),
    ]
    return pl.pallas_call(
        _moba_kernel, grid=(B, Hkv, nq), in_specs=in_specs,
        out_specs=_o_spec(S), out_shape=jax.ShapeDtypeStruct((B * S, Hkv * GROUP_COLS), BF16),
        scratch_shapes=[pltpu.VMEM((nkt * AUG_ROWS, R), F32)] + _flash_scratch(R),
        compiler_params=_params(3), name="moba_attention")(proj, ka, vTa, km)


def _compress_kernel(u_ref, us_ref, pe_ref, w1_ref, w2_ref, o_ref):
    a = (u_ref[0, 0].astype(F32) + pe_ref[0:1, :]).astype(BF16)
    b = (us_ref[0, 0].astype(F32) + pe_ref[1:2, :]).astype(BF16)
    pre = _dot(a, w1_ref[0]) + _dot(b, w1_ref[1])
    hid = jax.nn.gelu(pre)
    o_ref[0, 0] = _dot(hid.astype(BF16), w2_ref[...]).astype(o_ref.dtype)


def _compress(t, pe, w1, w2):
    B, H, S, _ = t.shape
    nrow = S // D_CMP_STRIDE
    width = D_CMP_STRIDE * HEAD_DIM
    u = t.reshape(B, H, nrow, width)
    us = jnp.concatenate([u[:, :, 1:], jnp.zeros((B, H, 1, width), u.dtype)], axis=2)
    blk = lambda b, h: (b, h, 0, 0)
    in_specs = [
        pl.BlockSpec((1, 1, nrow, width), blk), pl.BlockSpec((1, 1, nrow, width), blk),
        pl.BlockSpec((2, width), lambda b, h: (0, 0)),
        pl.BlockSpec((2, width, D_CMP_HIDDEN), lambda b, h: (0, 0, 0)),
        pl.BlockSpec((D_CMP_HIDDEN, HEAD_DIM), lambda b, h: (0, 0)),
    ]
    return pl.pallas_call(
        _compress_kernel, grid=(B, H), in_specs=in_specs,
        out_specs=pl.BlockSpec((1, 1, nrow, HEAD_DIM), blk),
        out_shape=jax.ShapeDtypeStruct((B, H, nrow, HEAD_DIM), BF16),
        compiler_params=_params(2), name="nsa_compress")(
            u, us, pe.astype(F32).reshape(2, width), w1.astype(BF16).reshape(2, width, D_CMP_HIDDEN), w2.astype(BF16))


def _cmp_select_kernel(nc, q_ref, kc_ref, vcT_ref, ovT_ref, oc_ref, bias_ref):
    n = pl.program_id(2)
    ncp_all = kc_ref.shape[2]
    nsel_all = ovT_ref.shape[0]
    last_visible = (n * Q_BLOCK + Q_BLOCK - D_CMP_LEN) // D_CMP_STRIDE
    quarter = last_visible // (ncp_all // CMP_PARTS)
    for part in range(CMP_PARTS):
        visible = ((part + 1) * ncp_all // CMP_PARTS, (part + 1) * nsel_all // CMP_PARTS)
        pl.when(quarter == part)(functools.partial(_cmp_select_body, nc, visible, q_ref, kc_ref, vcT_ref, ovT_ref, oc_ref, bias_ref))


def _cmp_select_body(nc, visible, q_ref, kc_ref, vcT_ref, ovT_ref, oc_ref, bias_ref):
    n = pl.program_id(2)
    ncp, nsel = visible
    nsel_all = ovT_ref.shape[0]
    qs = _group_qT(q_ref)
    R = qs.shape[1]
    s = _dot(kc_ref[0, 0, 0:ncp, :], qs)
    tpos = n * Q_BLOCK + (_iota((1, R), 1) & (Q_BLOCK - 1))
    c_last = jnp.minimum((tpos - (D_CMP_LEN - 1)) >> 4, nc - 1)
    s = jnp.where(_iota((ncp, R), 0) <= c_last, s, NEG_INF)
    m = jnp.max(s, axis=0, keepdims=True)
    m = jnp.where(m > NEG_INF, m, 0.0)
    e = jnp.exp(s - m)
    den = jnp.sum(e, axis=0, keepdims=True)
    p = (e * (1.0 / jnp.where(den > 0, den, 1.0))).astype(BF16)
    oc_ref[...] = _group_rows(_dot(vcT_ref[0, 0, :, 0:ncp], p))
    imp_heads = _dot(ovT_ref[0:nsel, 0:ncp], p)
    imp = imp_heads[:, 0:Q_BLOCK]
    for g in range(1, GROUP):
        imp = imp + imp_heads[:, g * Q_BLOCK:(g + 1) * Q_BLOCK]
    t = n * Q_BLOCK + _iota((nsel, Q_BLOCK), 1)
    j = _iota((nsel, Q_BLOCK), 0)
    cur = t >> 6
    imp = jnp.where(j <= cur, imp, NEG_INF)
    imp = jnp.where(_in_range(cur - j, 2), float("inf"), imp)
    imp = jnp.where(j == 0, float("inf"), imp)
    for _ in range(D_SEL_TOPK):
        mx = jnp.max(imp, axis=0, keepdims=True)
        idx = jnp.min(jnp.where(imp == mx, j, nsel), axis=0, keepdims=True)
        imp = jnp.where(j == idx, PICKED, imp)
    bias_ref[0, 0, 0, 0:nsel, :] = jnp.where(imp == PICKED, 0.0, -MASK_BIG)
    if nsel < nsel_all:
        bias_ref[0, 0, 0, nsel:, :] = jnp.full((nsel_all - nsel, Q_BLOCK), -MASK_BIG, F32)


def _cmp_select(proj, q_col, kc, vcT, overlapT, nc, S):
    B, Hkv, ncp, _ = kc.shape
    nq = S // Q_BLOCK
    nsel = overlapT.shape[0]
    in_specs = [
        _q_spec(S, q_col),
        pl.BlockSpec((1, 1, ncp, HEAD_DIM), lambda b, h, n: (b, h, 0, 0)),
        pl.BlockSpec((1, 1, HEAD_DIM, ncp), lambda b, h, n: (b, h, 0, 0)),
        pl.BlockSpec((nsel, ncp), lambda b, h, n: (0, 0)),
    ]
    out_specs = [_o_spec(S), pl.BlockSpec((1, 1, 1, nsel, Q_BLOCK), lambda b, h, n: (b, h, n, 0, 0))]
    out_shape = [jax.ShapeDtypeStruct((B * S, Hkv * GROUP_COLS), F32), jax.ShapeDtypeStruct((B, Hkv, nq, nsel, Q_BLOCK), F32)]
    return pl.pallas_call(
        functools.partial(_cmp_select_kernel, nc), grid=(B, Hkv, nq), in_specs=in_specs, out_specs=out_specs,
        out_shape=out_shape, compiler_params=_params(3), name="nsa_cmp_select")(proj, kc, vcT, overlapT)


def _sel_kernel(q_ref, ka_ref, vTa_ref, bias_ref, o_ref, ksq_scr, p_scr, acc_scr):
    n = pl.program_id(2)
    pl.when(n == 0)(lambda: _max_key_sqnorm(ka_ref, ksq_scr))
    qs = _group_qT(q_ref)

    def bias_rows(kt):
        b = bias_ref[0, 0, 0, pl.ds(pl.multiple_of(kt * AUG_ROWS, AUG_ROWS), AUG_ROWS), :]
        return jnp.concatenate([b] * GROUP, axis=1)

    o_ref[...] = _group_rows(_masked_flash_t(n, qs, bias_rows, ka_ref, vTa_ref, ksq_scr, p_scr, acc_scr))


def _sel_attention(proj, q_col, ka, vTa, biasT):
    B, Hkv, S, _ = ka.shape
    nq = S // Q_BLOCK
    nsel = biasT.shape[3]
    assert KEY_TILE // D_SEL_LEN == AUG_ROWS
    in_specs = [
        _q_spec(S, q_col),
        pl.BlockSpec((1, 1, S, LANES), lambda b, h, n: (b, h, 0, 0)),
        pl.BlockSpec((1, 1, V_AUG_ROWS, S), lambda b, h, n: (b, h, 0, 0)),
        pl.BlockSpec((1, 1, 1, nsel, Q_BLOCK), lambda b, h, n: (b, h, n, 0, 0)),
    ]
    return pl.pallas_call(
        _sel_kernel, grid=(B, Hkv, nq), in_specs=in_specs,
        out_specs=_o_spec(S), out_shape=jax.ShapeDtypeStruct((B * S, Hkv * GROUP_COLS), F32),
        scratch_shapes=_flash_scratch(GROUP * Q_BLOCK),
        compiler_params=_params(3), name="nsa_selected")(proj, ka, vTa, biasT)


def _win_kernel(span, q_ref, k_ref, vT_ref, o_ref):
    n = pl.program_id(2)
    qs = _group_qT(q_ref)
    R = qs.shape[1]
    start = pl.multiple_of(jnp.maximum(n * Q_BLOCK + Q_BLOCK - span, 0), Q_BLOCK)
    s = _dot(k_ref[0, 0, pl.ds(start, span), :], qs)
    tpos = n * Q_BLOCK + (_iota((1, R), 1) & (Q_BLOCK - 1))
    d = tpos - (start + _iota((span, R), 0))
    s = jnp.where(_in_range(d, jnp.full((1, R), D_WINDOW, jnp.int32)), s, NEG_INF)
    m = jnp.max(s, axis=0, keepdims=True)
    p = jnp.exp(s - m)
    l = jnp.sum(p, axis=0, keepdims=True)
    o_ref[...] = _group_rows(_dot(vT_ref[0, 0, :, pl.ds(start, span)], p.astype(BF16)) * (1.0 / l))


def _win_attention(proj, q_col, k, vT):
    B, Hkv, S, _ = k.shape
    nq = S // Q_BLOCK
    span = min(D_WINDOW + Q_BLOCK, S)
    in_specs = [
        _q_spec(S, q_col),
        pl.BlockSpec((1, 1, S, HEAD_DIM), lambda b, h, n: (b, h, 0, 0)),
        pl.BlockSpec((1, 1, HEAD_DIM, S), lambda b, h, n: (b, h, 0, 0)),
    ]
    return pl.pallas_call(
        functools.partial(_win_kernel, span), grid=(B, Hkv, nq), in_specs=in_specs,
        out_specs=_o_spec(S), out_shape=jax.ShapeDtypeStruct((B * S, Hkv * GROUP_COLS), F32),
        compiler_params=_params(3), name="nsa_window")(proj, k, vT)


def _route(logits):
    tm = logits.shape[0]
    lane = _iota((tm, LANES), 1)
    gl = jnp.where(lane < N_GROUPS, logits, NEG_INF)
    gmax = jnp.max(gl, axis=-1, keepdims=True)
    gidx = jnp.min(jnp.where(gl == gmax, lane, LANES), axis=-1, keepdims=True)
    g_prob = 1.0 / jnp.sum(jnp.exp(gl - gmax), axis=-1, keepdims=True)
    elane = lane - N_GROUPS
    in_group = (elane >= 0) & (elane < N_EXPERTS) & ((elane >> 4) == gidx)
    el = jnp.where(in_group, logits, NEG_INF)
    ee = jnp.exp(el - jnp.max(el, axis=-1, keepdims=True))
    ep = jnp.where(in_group, ee / jnp.sum(ee, axis=-1, keepdims=True), -1.0)
    p1 = jnp.max(ep, axis=-1, keepdims=True)
    i1 = jnp.min(jnp.where(ep == p1, lane, LANES), axis=-1, keepdims=True)
    ep2 = jnp.where(lane == i1, -1.0, ep)
    p2 = jnp.max(ep2, axis=-1, keepdims=True)
    i2 = jnp.min(jnp.where(ep2 == p2, lane, LANES), axis=-1, keepdims=True)
    den = p1 + p2
    vals = [(i1 - N_GROUPS).astype(F32), (i2 - N_GROUPS).astype(F32), g_prob * p1 / den, g_prob * p2 / den]
    out = jnp.zeros((tm, LANES), F32)
    for k, val in enumerate(vals):
        out = jnp.where(lane == k, val, out)
    return out


def _pack_bf16_pairs(h):
    n = h.shape[1] // 2
    bits = lax.bitcast_convert_type(h.astype(BF16).astype(F32), jnp.uint32)
    return bits[:, :n] | (bits[:, n:] >> 16)


def _unpack_bf16_pairs(packed):
    hi = lax.bitcast_convert_type(packed & jnp.uint32(0xFFFF0000), F32)
    lo = lax.bitcast_convert_type(packed << 16, F32)
    return jnp.concatenate([hi, lo], axis=1).astype(BF16)


def _out_tail(x_new, gain_ref, wr_ref, xo_ref, h_ref, route_ref):
    xo_ref[...] = x_new
    ms = jnp.mean(x_new * x_new, axis=-1, keepdims=True)
    h = x_new * lax.rsqrt(ms + NORM_EPS) * gain_ref[...]
    h_ref[...] = _pack_bf16_pairs(h)
    h_hi = h.astype(BF16)
    h_lo = (h - h_hi.astype(F32)).astype(BF16)
    logits = _dot(h_hi, wr_ref[0]) + (_dot(h_lo, wr_ref[0]) + _dot(h_hi, wr_ref[1]))
    route_ref[...] = _route(logits)


def _out_proj_kernel(oa_ref, ob_ref, x_ref, w_ref, gain_ref, wr_ref, xo_ref, h_ref, route_ref):
    half = w_ref.shape[0] // 2
    x_new = x_ref[...] + _dot(oa_ref[...], w_ref[0:half, :]) + _dot(ob_ref[...], w_ref[half:, :])
    _out_tail(x_new, gain_ref, wr_ref, xo_ref, h_ref, route_ref)


def _out_proj_nsa_kernel(oc_ref, b0_ref, b1_ref, b2_ref, gd_ref, x_ref, w_ref, gain_ref, wr_ref, xo_ref, h_ref, route_ref):
    half = D_HEADS * HEAD_DIM
    g = jax.nn.sigmoid(gd_ref[...])
    g_hi = g.astype(BF16)
    g_lo = (g - g_hi.astype(F32)).astype(BF16)
    src = _iota((LANES, half), 0)
    head3 = (_iota((LANES, half), 1) >> 6) * 3
    od = None
    for br, b_ref in enumerate((b0_ref, b1_ref, b2_ref)):
        spread = jnp.where(src == head3 + br, 1.0, 0.0).astype(BF16)
        term = (_dot(g_hi, spread) + _dot(g_lo, spread)) * b_ref[...]
        od = term if od is None else od + term
    x_new = x_ref[...] + _dot(oc_ref[...], w_ref[0:half, :]) + _dot(od.astype(BF16), w_ref[half:, :])
    _out_tail(x_new, gain_ref, wr_ref, xo_ref, h_ref, route_ref)


def _out_proj(o_parts, gd, x, w_out, gain, w_router):
    T = x.shape[0]
    tm = ROW_TILE
    row = lambda i: (i, 0)
    fixed = lambda i: (0, 0)
    if gd is None:
        kern = _out_proj_kernel
        args = list(o_parts)
        in_specs = [pl.BlockSpec((tm, D_MODEL // 2), row)] * 2
    else:
        kern = _out_proj_nsa_kernel
        args = list(o_parts) + [gd]
        in_specs = [pl.BlockSpec((tm, D_MODEL // 2), row)] * 4 + [pl.BlockSpec((tm, LANES), row)]
    args += [x, w_out, gain.reshape(1, D_MODEL), w_router]
    in_specs += [pl.BlockSpec((tm, D_MODEL), row), pl.BlockSpec((D_MODEL, D_MODEL), fixed),
                 pl.BlockSpec((1, D_MODEL), fixed), pl.BlockSpec((2, D_MODEL, LANES), lambda i: (0, 0, 0))]
    out_shape = [jax.ShapeDtypeStruct((T, D_MODEL), F32), jax.ShapeDtypeStruct((T, D_MODEL // 2), jnp.uint32),
                 jax.ShapeDtypeStruct((T, LANES), F32)]
    out_specs = [pl.BlockSpec((tm, D_MODEL), row), pl.BlockSpec((tm, D_MODEL // 2), row), pl.BlockSpec((tm, LANES), row)]
    return pl.pallas_call(kern, grid=(T // tm,), in_specs=in_specs, out_specs=out_specs, out_shape=out_shape,
                          compiler_params=_params(1), name="out_proj_router")(*args)


def _expert_kernel(be_ref, rows_ref, wg_ref, wu_ref, wd_ref, y_ref, wg_s, wu_s, wd_s):
    i = pl.program_id(0)
    n_blk = pl.num_programs(0)

    @pl.when((i == 0) | (be_ref[i] != be_ref[jnp.maximum(i - 1, 0)]))
    def _():
        wg_s[...] = wg_ref[0].astype(BF16)
        wu_s[...] = wu_ref[0].astype(BF16)
        wd_s[...] = wd_ref[0].astype(BF16)

    @pl.when(i < be_ref[n_blk])
    def _():
        xb = _unpack_bf16_pairs(rows_ref[...])
        hid = jax.nn.silu(_dot(xb, wg_s[...])) * _dot(xb, wu_s[...])
        y_ref[...] = _dot(hid.astype(BF16), wd_s[...])

    @pl.when(i >= be_ref[n_blk])
    def _():
        y_ref[...] = jnp.zeros(y_ref.shape, F32)


def _expert_ffn(rows, blk_info, wg, wu, wd):
    n_rows = rows.shape[0]
    n_blk = n_rows // MOE_ROWS
    grid_spec = pltpu.PrefetchScalarGridSpec(
        num_scalar_prefetch=1, grid=(n_blk,),
        in_specs=[
            pl.BlockSpec((MOE_ROWS, D_MODEL // 2), lambda i, be: (i, 0)),
            pl.BlockSpec((1, D_MODEL, EXPERT_HIDDEN), lambda i, be: (be[i], 0, 0)),
            pl.BlockSpec((1, D_MODEL, EXPERT_HIDDEN), lambda i, be: (be[i], 0, 0)),
            pl.BlockSpec((1, EXPERT_HIDDEN, D_MODEL), lambda i, be: (be[i], 0, 0)),
        ],
        out_specs=pl.BlockSpec((MOE_ROWS, D_MODEL), lambda i, be: (i, 0)),
        scratch_shapes=[pltpu.VMEM((D_MODEL, EXPERT_HIDDEN), BF16), pltpu.VMEM((D_MODEL, EXPERT_HIDDEN), BF16),
                        pltpu.VMEM((EXPERT_HIDDEN, D_MODEL), BF16)])
    return pl.pallas_call(
        _expert_kernel, grid_spec=grid_spec, out_shape=jax.ShapeDtypeStruct((n_rows, D_MODEL), F32),
        compiler_params=_params(1), name="expert_ffn")(blk_info, rows, wg, wu, wd)


def _moe_dispatch(route, h):
    n_tok = h.shape[0]
    n_asg = n_tok * MOE_TOPK
    i32 = jnp.int32
    e_flat = route[:, 0:MOE_TOPK].astype(i32).reshape(n_asg)
    is_e = e_flat[:, None] == jnp.arange(N_EXPERTS, dtype=i32)[None, :]
    counts = jnp.sum(is_e, axis=0, dtype=i32)
    assert n_asg & (n_asg - 1) == 0 and n_asg <= 1 << 16
    idx = jnp.arange(n_asg, dtype=jnp.uint32)
    low = jnp.uint32(n_asg - 1)
    order = (jnp.sort(e_flat.astype(jnp.uint32) * n_asg + idx) & low).astype(i32)
    rank = (jnp.sort(order.astype(jnp.uint32) * n_asg + idx) & low).astype(i32)
    padded = (counts + MOE_ROWS - 1) // MOE_ROWS * MOE_ROWS
    pad_end = jnp.cumsum(padded)
    pad_start = pad_end - padded
    start = jnp.cumsum(counts) - counts
    n_rows = n_asg + N_EXPERTS * MOE_ROWS
    n_blk = n_rows // MOE_ROWS
    blk_start = jnp.arange(n_blk, dtype=i32) * MOE_ROWS
    blk_expert = jnp.minimum(jnp.sum(pad_end[None, :] <= blk_start[:, None], axis=1, dtype=i32), N_EXPERTS - 1)
    within = (blk_start - pad_start[blk_expert])[:, None] + jnp.arange(MOE_ROWS, dtype=i32)[None, :]
    valid = within < counts[blk_expert][:, None]
    src = jnp.clip(start[blk_expert][:, None] + within, 0, n_asg - 1)
    row_tok = jnp.where(valid, (order // MOE_TOPK)[src], src // MOE_TOPK).reshape(n_rows)
    shift = jnp.sum(jnp.where(is_e, (pad_start - start)[None, :], 0), axis=1, dtype=i32)
    pos = (rank + shift).reshape(n_tok, MOE_TOPK)
    blk_info = jnp.concatenate([blk_expert, (pad_end[-1:] // MOE_ROWS).astype(i32)])
    return h[row_tok], blk_info, pos


def _moe(route, h, wg, wu, wd):
    rows, blk_info, pos = _moe_dispatch(route, h)
    y = _expert_ffn(rows, blk_info, wg, wu, wd)
    return y[pos[:, 0]], y[pos[:, 1]], route


def _final_kernel(x_ref, y0_ref, y1_ref, route_ref, g_ref, o_ref):
    x = _moe_combine(x_ref[...], y0_ref, y1_ref, route_ref)
    ms = jnp.mean(x * x, axis=-1, keepdims=True)
    o_ref[...] = x * lax.rsqrt(ms + NORM_EPS) * g_ref[...]


def _final_norm(x, ys, gain):
    T = x.shape[0]
    tm = ROW_TILE
    row = lambda i: (i, 0)
    return pl.pallas_call(
        _final_kernel, grid=(T // tm,),
        in_specs=[pl.BlockSpec((tm, D_MODEL), row)] * 3 + [pl.BlockSpec((tm, LANES), row), pl.BlockSpec((1, D_MODEL), lambda i: (0, 0))],
        out_specs=pl.BlockSpec((tm, D_MODEL), row), out_shape=jax.ShapeDtypeStruct((T, D_MODEL), F32),
        compiler_params=_params(1), name="final_norm")(x, *ys, gain.reshape(1, D_MODEL))


def _rope_tables(positions):
    inv_freq = ROPE_THETA ** (-jnp.arange(0, HEAD_DIM, 2, dtype=F32) / HEAD_DIM)
    ang = positions.astype(F32).reshape(-1, 1) * inv_freq
    cos, sin = jnp.cos(ang), jnp.sin(ang)
    reps = LANES // HEAD_DIM
    return jnp.tile(jnp.concatenate([cos, cos], axis=1), (1, reps)), jnp.tile(jnp.concatenate([-sin, sin], axis=1), (1, reps))


def _chunks(first, count, rope, action):
    return tuple((first + j, rope, action(j)) for j in range(count))


def _plan(*runs):
    entries = sum(runs, ())
    assert [e[0] for e in entries] == list(range(len(entries)))
    return tuple(e[1:] for e in entries)


AB_OUTS = (("tok", 4), ("heads", A_KV_HEADS), ("headsT", A_KV_HEADS), ("heads", B_HEADS), ("headsT", B_HEADS), ("headsT", B_HEADS))
AB_PLAN = _plan(_chunks(0, 4, True, lambda j: (("tok", 0, j),)),
                _chunks(4, 1, True, lambda j: (("heads", 1, 0),)),
                _chunks(5, 1, False, lambda j: (("headsT", 2, 0),)),
                _chunks(6, 4, False, lambda j: (("heads", 3, 2 * j),)),
                _chunks(10, 4, False, lambda j: (("headsT", 4, 2 * j),)),
                _chunks(14, 4, False, lambda j: (("headsT", 5, 2 * j),)))
CD_OUTS = (("tok", 8), ("keys_aug", C_KV_HEADS), ("kmean", 0), ("valsT_aug", C_KV_HEADS), ("heads", D_KV_HEADS),
           ("heads", D_KV_HEADS), ("keys_aug", D_KV_HEADS), ("valsT_aug", D_KV_HEADS), ("heads", D_KV_HEADS),
           ("headsT", D_KV_HEADS), ("f32", 0))
CD_PLAN = _plan(_chunks(0, 4, True, lambda j: (("tok", 0, j),)),
                _chunks(4, 1, True, lambda j: (("keys_aug", 1, C_BLOCK), ("kmean", 2, 0))),
                _chunks(5, 1, False, lambda j: (("valsT_aug", 3, 0),)),
                _chunks(6, 4, True, lambda j: (("tok", 0, 4 + j),)),
                _chunks(10, 1, True, lambda j: (("heads", 4, 0),)),
                _chunks(11, 1, False, lambda j: (("heads", 5, 0),)),
                _chunks(12, 1, True, lambda j: (("keys_aug", 6, D_SEL_LEN),)),
                _chunks(13, 1, False, lambda j: (("valsT_aug", 7, 0),)),
                _chunks(14, 1, True, lambda j: (("heads", 8, 0),)),
                _chunks(15, 1, False, lambda j: (("headsT", 9, 0),)),
                _chunks(16, 1, False, lambda j: (("f32", 10, 0),)))
QD_COL = 4 * LANES


def _router_weights(router_group, router_expert):
    pad = jnp.zeros((D_MODEL, LANES - N_GROUPS - N_EXPERTS), F32)
    w = jnp.concatenate([router_group.astype(F32), router_expert.astype(F32), pad], axis=1)
    hi = w.astype(BF16)
    return jnp.stack([hi, (w - hi.astype(F32)).astype(BF16)])


def _pad_cols(w, n):
    return jnp.concatenate([w, jnp.zeros((w.shape[0], n - w.shape[1]), w.dtype)], axis=1)


def _mixer_ab(parts, sinks):
    qa, ka, vaT, qb, kbT, vbT = parts
    oa = _swa_attention(qa, 0, ka, vaT, sinks)
    ob = _stick_attention(qb, kbT, vbT)
    return oa, ob


def _mixer_cd(parts, B, S, k_pe, k_w1, k_w2, v_pe, v_w1, v_w2):
    q_cd, kc_aug, kmean, vcT_aug, kdc, vdc, kds_aug, vdsT_aug, kdw, vdwT, _ = parts
    nblk = S // C_BLOCK
    km = kmean.reshape(B, nblk, C_KV_HEADS, HEAD_DIM).transpose(0, 2, 1, 3).astype(BF16)
    oc = _moba_attention(q_cd, 0, kc_aug, vcT_aug, km)
    k_cmp = _compress(kdc, k_pe, k_w1, k_w2)
    v_cmp = _compress(vdc, v_pe, v_w1, v_w2)
    nc = (S - D_CMP_LEN) // D_CMP_STRIDE + 1
    ncp = S // D_CMP_STRIDE
    nsel = S // D_SEL_LEN
    c_start = jnp.arange(ncp) * D_CMP_STRIDE
    b_start = jnp.arange(nsel) * D_SEL_LEN
    overlap = ((c_start[:, None] <= b_start[None, :] + D_SEL_LEN - 1) & (c_start[:, None] + D_CMP_LEN - 1 >= b_start[None, :])
               & (jnp.arange(ncp)[:, None] < nc)).astype(BF16)
    o_cmp, biasT = _cmp_select(q_cd, QD_COL, k_cmp, v_cmp.transpose(0, 1, 3, 2), overlap.T, nc, S)
    o_sel = _sel_attention(q_cd, QD_COL, kds_aug, vdsT_aug, biasT)
    o_win = _win_attention(q_cd, QD_COL, kdw, vdwT)
    return oc, o_cmp, o_sel, o_win


def kernel(x, positions, ln_mix_0, w_in_0, sinks_0, w_out_0, ln_ffn_0, router_group_0, router_expert_0, expert_gate_0, expert_up_0, expert_down_0, ln_mix_1, w_in_1, cmp_k_pe_1, cmp_k_w1_1, cmp_k_w2_1, cmp_v_pe_1, cmp_v_w1_1, cmp_v_w2_1, w_out_1, ln_ffn_1, router_group_1, router_expert_1, expert_gate_1, expert_up_1, expert_down_1, ln_final):
    B, S, _ = x.shape
    T = B * S
    assert S % KEY_TILE == 0 and T % ROW_TILE == 0
    cos_t, sin_t = _rope_tables(positions)
    xf = x.reshape(T, D_MODEL)

    parts = _in_proj(xf, None, ln_mix_0, w_in_0.astype(BF16), cos_t, sin_t, AB_PLAN, AB_OUTS, B, S)
    o_ab = _mixer_ab(parts, sinks_0)
    x1, h1, route1 = _out_proj(o_ab, None, xf, w_out_0.astype(BF16), ln_ffn_0, _router_weights(router_group_0, router_expert_0))
    ys = _moe(route1, h1, expert_gate_0, expert_up_0, expert_down_0)

    x2, *parts = _in_proj(x1, ys, ln_mix_1, _pad_cols(w_in_1.astype(BF16), len(CD_PLAN) * LANES), cos_t, sin_t,
                          CD_PLAN, CD_OUTS, B, S)
    gd = parts[-1]
    parts = _mixer_cd(parts, B, S, cmp_k_pe_1, cmp_k_w1_1, cmp_k_w2_1, cmp_v_pe_1, cmp_v_w1_1, cmp_v_w2_1)
    x3, h3, route3 = _out_proj(parts, gd, x2, w_out_1.astype(BF16), ln_ffn_1, _router_weights(router_group_1, router_expert_1))
    ys = _moe(route3, h3, expert_gate_1, expert_up_1, expert_down_1)
    return _final_norm(x3, ys, ln_final).reshape(B, S, D_MODEL)
```

```python
import functools

import jax
import jax.numpy as jnp
from jax import lax
from jax.experimental import pallas as pl
from jax.experimental.pallas import tpu as pltpu

D_MODEL = 1024
HEAD_DIM = 64
HALF = HEAD_DIM // 2
ROPE_THETA = 10000.0
NORM_EPS = 1e-6
Q_BLOCK = 128
SCALE = HEAD_DIM ** -0.5

A_HEADS, A_KV_HEADS, A_WINDOW = 8, 2, 128
B_HEADS = 8
C_HEADS, C_KV_HEADS, C_BLOCK, C_TOPK = 8, 2, 256, 3
D_HEADS, D_KV_HEADS = 8, 2
D_CMP_LEN, D_CMP_STRIDE, D_CMP_HIDDEN = 32, 16, 256
D_SEL_LEN, D_SEL_TOPK, D_WINDOW = 64, 16, 512
N_GROUPS, EXPERTS_PER_GROUP, MOE_TOPK, EXPERT_HIDDEN = 4, 16, 2, 512
N_EXPERTS = N_GROUPS * EXPERTS_PER_GROUP
GROUP = 4

LANES = 128
MXU_COLS = 256
ROW_TILE = 512
MOE_ROWS = 256
KEY_TILE = 512
FLASH_MIN_SUM = 1e-25
REF_MARGIN = 1.05
MASKED = -1e30
V_AUG_ROWS = 80
AUG_ROWS = 8
MASK_BIG = 2.0 ** 100
CMP_PARTS = 4
PICKED = -1.0
SB_EXIT = -104.0
STICK_HEADS = 8
VMEM_LIMIT = 56 * 1024 * 1024

F32 = jnp.float32
BF16 = jnp.bfloat16
NEG_INF = float("-inf")


def _iota(shape, dim):
    return lax.broadcasted_iota(jnp.int32, shape, dim)


def _dot(a, b):
    return jnp.dot(a, b, preferred_element_type=F32)


def _in_range(d, width):
    return lax.bitcast_convert_type(d, jnp.uint32) < lax.bitcast_convert_type(jnp.asarray(width, jnp.int32), jnp.uint32)


def _params(n_grid):
    return pltpu.CompilerParams(dimension_semantics=("arbitrary",) * n_grid, vmem_limit_bytes=VMEM_LIMIT)


GROUP_COLS = GROUP * HEAD_DIM


def _group_qT(q_ref):
    t = (q_ref[...].astype(F32) * SCALE).T
    return jnp.concatenate([t[g * HEAD_DIM:(g + 1) * HEAD_DIM] for g in range(GROUP)], axis=1).astype(BF16)


def _group_rows(oT):
    return jnp.concatenate([oT[:, g * Q_BLOCK:(g + 1) * Q_BLOCK] for g in range(GROUP)], axis=0).T


def _q_spec(S, col0):
    nq = S // Q_BLOCK
    return pl.BlockSpec((Q_BLOCK, GROUP_COLS), lambda b, h, n: (b * nq + n, col0 // GROUP_COLS + h))


def _o_spec(S):
    nq = S // Q_BLOCK
    return pl.BlockSpec((Q_BLOCK, GROUP_COLS), lambda b, h, n: (b * nq + n, h))


def _moe_combine(x, y0_ref, y1_ref, route_ref):
    return x + (y0_ref[...] * route_ref[:, MOE_TOPK:MOE_TOPK + 1] + y1_ref[...] * route_ref[:, MOE_TOPK + 1:MOE_TOPK + 2])


def _emit_chunk(ch, actions, outs):
    tm = ch.shape[0]
    chT = None
    for kind, oi, arg in actions:
        o = outs[oi]
        if kind == "tok":
            o[:, arg * LANES:(arg + 1) * LANES] = ch.astype(BF16)
        elif kind == "f32":
            o[...] = ch
        elif kind == "kmean":
            o[...] = jnp.mean(ch.reshape(tm // C_BLOCK, C_BLOCK, LANES), axis=1).reshape(tm // C_BLOCK, 1, LANES)
        elif kind == "heads":
            for hh in range(2):
                o[0, arg + hh] = ch[:, hh * HEAD_DIM:(hh + 1) * HEAD_DIM].astype(BF16)
        elif kind == "keys_aug":
            lane = _iota((tm, LANES), 1)
            block_in_tile = _iota((tm, LANES), 0) >> (arg.bit_length() - 1)
            aug = jnp.where((lane - HEAD_DIM == block_in_tile) | (lane == HEAD_DIM + AUG_ROWS), 1.0, 0.0)
            for hh in range(2):
                keys = ch if hh == 0 else pltpu.roll(ch, HEAD_DIM, 1)
                o[0, hh] = jnp.where(lane < HEAD_DIM, keys, aug).astype(BF16)
        else:
            chT = ch.astype(BF16).astype(F32).T if chT is None else chT
            for hh in range(2):
                vT = chT[hh * HEAD_DIM:(hh + 1) * HEAD_DIM]
                if kind == "headsT":
                    o[0, arg + hh] = vT.astype(BF16)
                else:
                    assert kind == "valsT_aug"
                    tail = jnp.where(_iota((V_AUG_ROWS - HEAD_DIM, tm), 0) == 0, 1.0, 0.0)
                    o[0, hh] = jnp.concatenate([vT, tail], axis=0).astype(BF16)


def _in_proj_kernel(plan, combine, *refs):
    refs = list(refs)
    x_ref = refs.pop(0)
    if combine:
        y0_ref, y1_ref, route_ref = refs.pop(0), refs.pop(0), refs.pop(0)
    g_ref, w_ref, cos_ref, sin_ref = refs[:4]
    outs = refs[4:]
    x = x_ref[...]
    if combine:
        x = _moe_combine(x, y0_ref, y1_ref, route_ref)
        xo_ref = outs.pop(0)
        xo_ref[...] = x
    ms = jnp.mean(x * x, axis=-1, keepdims=True)
    h = (x * lax.rsqrt(ms + NORM_EPS) * g_ref[...]).astype(BF16)
    tm = x.shape[0]
    cos = cos_ref[...]
    sin = sin_ref[...]
    first_half = (_iota((tm, LANES), 1) & (HEAD_DIM - 1)) < HALF
    for c0 in range(0, len(plan), MXU_COLS // LANES):
        group = plan[c0:c0 + MXU_COLS // LANES]
        wide = _dot(h, w_ref[:, c0 * LANES:(c0 + len(group)) * LANES])
        for k, (rope, actions) in enumerate(group):
            ch = wide[:, k * LANES:(k + 1) * LANES]
            if rope:
                partner = jnp.where(first_half, pltpu.roll(ch, LANES - HALF, 1), pltpu.roll(ch, HALF, 1))
                ch = ch * cos + partner * sin
            _emit_chunk(ch, actions, outs)


def _in_proj(x, ys, gain, w, cos_t, sin_t, plan, out_kinds, B, S):
    T = x.shape[0]
    n_cols = w.shape[1]
    tm = ROW_TILE
    assert tm == KEY_TILE and S % tm == 0 and len(plan) * LANES == n_cols
    tpb = S // tm
    combine = ys is not None
    row = lambda i: (i, 0)
    fixed = lambda i: (0, 0)
    by_seq = lambda i: (i // tpb, 0, i % tpb, 0)
    by_seq_t = lambda i: (i // tpb, 0, 0, i % tpb)
    in_specs = [pl.BlockSpec((tm, D_MODEL), row)]
    args = [x]
    if combine:
        in_specs += [pl.BlockSpec((tm, D_MODEL), row)] * 2 + [pl.BlockSpec((tm, LANES), row)]
        args += list(ys)
    in_specs += [pl.BlockSpec((1, D_MODEL), fixed), pl.BlockSpec((D_MODEL, n_cols), fixed),
                 pl.BlockSpec((tm, LANES), row), pl.BlockSpec((tm, LANES), row)]
    args += [gain.reshape(1, D_MODEL), w, cos_t, sin_t]
    out_shape, out_specs = [], []
    if combine:
        out_shape.append(jax.ShapeDtypeStruct((T, D_MODEL), F32))
        out_specs.append(pl.BlockSpec((tm, D_MODEL), row))
    for kind, size in out_kinds:
        if kind == "tok":
            shape, dtype, spec = (T, size * LANES), BF16, pl.BlockSpec((tm, size * LANES), row)
        elif kind == "f32":
            shape, dtype, spec = (T, LANES), F32, pl.BlockSpec((tm, LANES), row)
        elif kind == "kmean":
            shape, dtype, spec = (T // C_BLOCK, 1, LANES), F32, pl.BlockSpec((tm // C_BLOCK, 1, LANES), lambda i: (i, 0, 0))
        elif kind == "heads":
            shape, dtype, spec = (B, size, S, HEAD_DIM), BF16, pl.BlockSpec((1, size, tm, HEAD_DIM), by_seq)
        elif kind == "keys_aug":
            shape, dtype, spec = (B, size, S, LANES), BF16, pl.BlockSpec((1, size, tm, LANES), by_seq)
        elif kind == "headsT":
            shape, dtype, spec = (B, size, HEAD_DIM, S), BF16, pl.BlockSpec((1, size, HEAD_DIM, tm), by_seq_t)
        else:
            assert kind == "valsT_aug"
            shape, dtype, spec = (B, size, V_AUG_ROWS, S), BF16, pl.BlockSpec((1, size, V_AUG_ROWS, tm), by_seq_t)
        out_shape.append(jax.ShapeDtypeStruct(shape, dtype))
        out_specs.append(spec)
    kern = functools.partial(_in_proj_kernel, plan, combine)
    return pl.pallas_call(kern, grid=(T // tm,), in_specs=in_specs, out_specs=out_specs, out_shape=out_shape,
                          compiler_params=_params(1), name="in_proj")(*args)


def _swa_kernel(q_ref, kp_ref, ko_ref, vp_ref, vo_ref, sink_ref, o_ref):
    n = pl.program_id(2)
    qs = _group_qT(q_ref)
    R = qs.shape[1]
    k = jnp.concatenate([kp_ref[0, 0], ko_ref[0, 0]], axis=0)
    vT = jnp.concatenate([vp_ref[0, 0], vo_ref[0, 0]], axis=1)
    s = _dot(k, qs)
    qpos = _iota((1, R), 1) & (Q_BLOCK - 1)
    d = qpos + Q_BLOCK - _iota((2 * Q_BLOCK, R), 0)
    width = jnp.minimum(A_WINDOW, qpos + 1 + jnp.where(n > 0, Q_BLOCK, 0))
    s = jnp.where(_in_range(d, width), s, NEG_INF)
    sink = sink_ref[0]
    m = jnp.maximum(jnp.max(s, axis=0, keepdims=True), sink)
    p = jnp.exp(s - m)
    den = jnp.sum(p, axis=0, keepdims=True) + jnp.exp(sink - m)
    o_ref[...] = _group_rows(_dot(vT, p.astype(BF16)) * (1.0 / den)).astype(o_ref.dtype)


def _swa_attention(proj, q_col, k, vT, sinks):
    B, Hkv, S, _ = k.shape
    nb = S // Q_BLOCK
    R = GROUP * Q_BLOCK
    sink_row = jnp.repeat(sinks.astype(F32).reshape(Hkv, GROUP), Q_BLOCK, axis=1).reshape(Hkv, 1, R)
    prev = lambda n: jnp.maximum(n - 1, 0)
    in_specs = [
        _q_spec(S, q_col),
        pl.BlockSpec((1, 1, Q_BLOCK, HEAD_DIM), lambda b, h, n: (b, h, prev(n), 0)),
        pl.BlockSpec((1, 1, Q_BLOCK, HEAD_DIM), lambda b, h, n: (b, h, n, 0)),
        pl.BlockSpec((1, 1, HEAD_DIM, Q_BLOCK), lambda b, h, n: (b, h, 0, prev(n))),
        pl.BlockSpec((1, 1, HEAD_DIM, Q_BLOCK), lambda b, h, n: (b, h, 0, n)),
        pl.BlockSpec((1, 1, R), lambda b, h, n: (h, 0, 0)),
    ]
    return pl.pallas_call(
        _swa_kernel, grid=(B, Hkv, nb), in_specs=in_specs,
        out_specs=_o_spec(S), out_shape=jax.ShapeDtypeStruct((B * S, Hkv * GROUP_COLS), BF16),
        compiler_params=_params(3), name="swa_attention")(proj, k, k, vT, vT, sink_row)


def _stick_kernel(q_ref, kT_ref, vT_ref, o_ref):
    n = pl.program_id(2)
    tq = tk = Q_BLOCK
    heads = q_ref.shape[1]
    row = _iota((tq, tk), 0)
    col = _iota((tk, tk), 1)
    upper = jnp.where(_iota((tk, tk), 0) > col, 1.0, 0.0).astype(BF16)
    tpos = n * tq + row

    def body(carry):
        kb, _, cs, accs = carry
        start = pl.multiple_of(kb * tk, tk)
        past = (start + col) < tpos
        hs = range(heads)
        zs = [_dot(q_ref[0, h], kT_ref[0, h, :, pl.ds(start, tk)]) * SCALE for h in hs]
        sps = [jnp.maximum(z, 0.0) + jnp.log(1.0 + jnp.exp(-jnp.abs(z))) for z in zs]
        stays = [jnp.where(past, -sp, 0.0) for sp in sps]
        his = [st.astype(BF16) for st in stays]
        los = [(st - hi.astype(F32)).astype(BF16) for st, hi in zip(stays, his)]
        betweens = [_dot(hi, upper) + _dot(lo, upper) for hi, lo in zip(his, los)]
        ws = [jnp.where(past, jnp.exp(zs[h] - sps[h] + betweens[h] + cs[h]), 0.0).astype(BF16) for h in hs]
        new_accs = [accs[h] + lax.dot_general(ws[h], vT_ref[0, h, :, pl.ds(start, tk)], (((1,), (1,)), ((), ())),
                                              preferred_element_type=F32) for h in hs]
        new_cs = [cs[h] + jnp.sum(stays[h], axis=-1, keepdims=True) for h in hs]
        worst = functools.reduce(jnp.maximum, new_cs)
        return kb - 1, jnp.max(worst) > SB_EXIT, tuple(new_cs), tuple(new_accs)

    def cond(carry):
        kb, alive, _, _ = carry
        return (kb >= 0) & alive

    init = (n, jnp.array(True), (jnp.zeros((tq, 1), F32),) * heads, (jnp.zeros((tq, HEAD_DIM), F32),) * heads)
    _, _, _, accs = lax.while_loop(cond, body, init)
    o_ref[...] = jnp.concatenate(accs, axis=1).astype(o_ref.dtype)


def _stick_attention(q, kT, v):
    B, H, S, _ = q.shape
    nq = S // Q_BLOCK
    hb = STICK_HEADS
    resident = dict(pipeline_mode=pl.Buffered(1))
    in_specs = [
        pl.BlockSpec((1, hb, Q_BLOCK, HEAD_DIM), lambda b, h, n: (b, h, n, 0)),
        pl.BlockSpec((1, hb, HEAD_DIM, S), lambda b, h, n: (b, h, 0, 0), **resident),
        pl.BlockSpec((1, hb, HEAD_DIM, S), lambda b, h, n: (b, h, 0, 0), **resident),
    ]
    return pl.pallas_call(
        _stick_kernel, grid=(B, H // hb, nq), in_specs=in_specs,
        out_specs=pl.BlockSpec((Q_BLOCK, hb * HEAD_DIM), lambda b, h, n: (b * nq + n, h)),
        out_shape=jax.ShapeDtypeStruct((B * S, H * HEAD_DIM), BF16),
        compiler_params=_params(3), name="stick_attention")(q, kT, v)


def _max_key_sqnorm(ka_ref, out_ref):
    S = ka_ref.shape[2]
    tk = KEY_TILE
    is_key_lane = _iota((tk, LANES), 1) < HEAD_DIM
    ones = jnp.ones((LANES, LANES), BF16)

    def body(i, mx):
        k = jnp.where(is_key_lane, ka_ref[0, 0, pl.ds(pl.multiple_of(i * tk, tk), tk), :].astype(F32), 0.0)
        sq = _dot((k * k).astype(BF16), ones)
        return jnp.maximum(mx, jnp.max(sq, axis=0, keepdims=True))

    mx = lax.fori_loop(0, S // tk, body, jnp.zeros((1, LANES), F32))
    out_ref[...] = jnp.broadcast_to(mx, out_ref.shape)


def _flash_scratch(R):
    return [pltpu.VMEM((AUG_ROWS, LANES), F32), pltpu.VMEM((2, KEY_TILE, R), BF16), pltpu.VMEM((V_AUG_ROWS, R), F32)]


def _masked_flash_t(n, qT, bias_rows, ka_ref, vTa_ref, ksq_ref, p_scr, acc_scr):
    R = qT.shape[1]
    tk = KEY_TILE
    diag = (n * Q_BLOCK) // tk
    zpad = jnp.zeros((LANES - HEAD_DIM - 2 * AUG_ROWS, R), F32)
    causal = diag * tk + _iota((tk, R), 0) <= n * Q_BLOCK + (_iota((tk, R), 1) & (Q_BLOCK - 1))

    def scores(kt, ref_rows):
        st = pl.multiple_of(kt * tk, tk)
        low = jnp.concatenate([bias_rows(kt), ref_rows, zpad], axis=0).astype(BF16)
        return _dot(ka_ref[0, 0, pl.ds(st, tk), :], jnp.concatenate([qT, low], axis=0))

    def values(kt):
        return vTa_ref[0, 0, :, pl.ds(pl.multiple_of(kt * tk, tk), tk)]

    qsq = jnp.sum(jnp.square(qT.astype(F32)), axis=0, keepdims=True)
    ksq = jnp.concatenate([ksq_ref[0:1, :]] * (R // LANES), axis=1)
    ref = jnp.sqrt(qsq * ksq) * REF_MARGIN
    ref_rows = jnp.where(_iota((AUG_ROWS, R), 0) == 0, -ref, 0.0)

    def fast_body(kt, carry):
        slot = kt & 1
        s = scores(kt, ref_rows)
        acc_scr[...] += _dot(values(jnp.maximum(kt - 1, 0)), p_scr[1 - slot])
        p_scr[slot] = jnp.exp(s).astype(BF16)
        return carry

    p_scr[1] = jnp.zeros((tk, R), BF16)
    acc_scr[...] = jnp.zeros(acc_scr.shape, F32)
    lax.fori_loop(0, diag, fast_body, 0)
    acc = acc_scr[...] + _dot(values(jnp.maximum(diag - 1, 0)), p_scr[1 - (diag & 1)])
    s = jnp.where(causal, scores(diag, ref_rows), MASKED)
    acc = acc + _dot(values(diag), jnp.exp(s).astype(BF16))

    def running_max_path(_):
        no_ref = jnp.zeros((AUG_ROWS, R), F32)

        def update(kt, s, m, acc):
            m_new = jnp.maximum(m, jnp.max(s, axis=0, keepdims=True))
            p = jnp.exp(s - m_new).astype(BF16)
            return m_new, jnp.exp(m - m_new) * acc + _dot(values(kt), p)

        init = (jnp.full((1, R), MASKED, F32), jnp.zeros((vTa_ref.shape[2], R), F32))
        m, acc = lax.fori_loop(0, diag, lambda kt, c: update(kt, scores(kt, no_ref), *c), init)
        return update(diag, jnp.where(causal, scores(diag, no_ref), MASKED), m, acc)[1]

    healthy = jnp.min(acc[HEAD_DIM:HEAD_DIM + 1, :]) >= FLASH_MIN_SUM
    acc = lax.cond(healthy, lambda _: acc, running_max_path, 0)
    return acc[0:HEAD_DIM] * (1.0 / acc[HEAD_DIM:HEAD_DIM + 1])


def _moba_kernel(q_ref, ka_ref, vTa_ref, km_ref, o_ref, bias_scr, ksq_scr, p_scr, acc_scr):
    n = pl.program_id(2)
    pl.when(n == 0)(lambda: _max_key_sqnorm(ka_ref, ksq_scr))
    nblk = km_ref.shape[2]
    qs = _group_qT(q_ref)
    R = qs.shape[1]
    own = (n * Q_BLOCK) // C_BLOCK
    gate = _dot(km_ref[0, 0], qs)
    blk = _iota((nblk, R), 0)
    gate = jnp.where(blk < own, gate, NEG_INF)
    bias = jnp.where(blk == own, 0.0, -MASK_BIG)
    for _ in range(C_TOPK):
        mx = jnp.max(gate, axis=0, keepdims=True)
        idx = jnp.min(jnp.where(gate == mx, blk, nblk), axis=0, keepdims=True)
        hit = blk == idx
        bias = jnp.where(hit, jnp.where(mx > NEG_INF, 0.0, bias), bias)
        gate = jnp.where(hit, NEG_INF, gate)
    per = KEY_TILE // C_BLOCK
    rows = bias_scr.shape[0]
    r = _iota((rows, nblk), 0)
    spread = jnp.where(((r & (AUG_ROWS - 1)) < per) & (_iota((rows, nblk), 1) == (r >> 3) * per + (r & (AUG_ROWS - 1))), 1.0, 0.0)
    bias_scr[...] = _dot(spread.astype(BF16), bias.astype(BF16))
    oT = _masked_flash_t(n, qs, lambda kt: bias_scr[pl.ds(pl.multiple_of(kt * AUG_ROWS, AUG_ROWS), AUG_ROWS), :],
                         ka_ref, vTa_ref, ksq_scr, p_scr, acc_scr)
    o_ref[...] = _group_rows(oT).astype(o_ref.dtype)


def _moba_attention(proj, q_col, ka, vTa, km):
    B, Hkv, S, _ = ka.shape
    nq = S // Q_BLOCK
    R = GROUP * Q_BLOCK
    nblk = km.shape[2]
    nkt = S // KEY_TILE
    in_specs = [
        _q_spec(S, q_col),
        pl.BlockSpec((1, 1, S, LANES), lambda b, h, n: (b, h, 0, 0)),
        pl.BlockSpec((1, 1, V_AUG_ROWS, S), lambda b, h, n: (b, h, 0, 0)),
        pl.BlockSpec((1, 1, nblk, HEAD_DIM), lambda b, h, n: (b, h, 0, 0)),
    ]
    return pl.pallas_call(
        _moba_kernel, grid=(B, Hkv, nq), in_specs=in_specs,
        out_specs=_o_spec(S), out_shape=jax.ShapeDtypeStruct((B * S, Hkv * GROUP_COLS), BF16),
        scratch_shapes=[pltpu.VMEM((nkt * AUG_ROWS, R), F32)] + _flash_scratch(R),
        compiler_params=_params(3), name="moba_attention")(proj, ka, vTa, km)


def _compress_kernel(u_ref, us_ref, pe_ref, w1_ref, w2_ref, o_ref):
    a = (u_ref[0, 0].astype(F32) + pe_ref[0:1, :]).astype(BF16)
    b = (us_ref[0, 0].astype(F32) + pe_ref[1:2, :]).astype(BF16)
    pre = _dot(a, w1_ref[0]) + _dot(b, w1_ref[1])
    hid = jax.nn.gelu(pre)
    o_ref[0, 0] = _dot(hid.astype(BF16), w2_ref[...]).astype(o_ref.dtype)


def _compress(t, pe, w1, w2):
    B, H, S, _ = t.shape
    nrow = S // D_CMP_STRIDE
    width = D_CMP_STRIDE * HEAD_DIM
    u = t.reshape(B, H, nrow, width)
    us = jnp.concatenate([u[:, :, 1:], jnp.zeros((B, H, 1, width), u.dtype)], axis=2)
    blk = lambda b, h: (b, h, 0, 0)
    in_specs = [
        pl.BlockSpec((1, 1, nrow, width), blk), pl.BlockSpec((1, 1, nrow, width), blk),
        pl.BlockSpec((2, width), lambda b, h: (0, 0)),
        pl.BlockSpec((2, width, D_CMP_HIDDEN), lambda b, h: (0, 0, 0)),
        pl.BlockSpec((D_CMP_HIDDEN, HEAD_DIM), lambda b, h: (0, 0)),
    ]
    return pl.pallas_call(
        _compress_kernel, grid=(B, H), in_specs=in_specs,
        out_specs=pl.BlockSpec((1, 1, nrow, HEAD_DIM), blk),
        out_shape=jax.ShapeDtypeStruct((B, H, nrow, HEAD_DIM), BF16),
        compiler_params=_params(2), name="nsa_compress")(
            u, us, pe.astype(F32).reshape(2, width), w1.astype(BF16).reshape(2, width, D_CMP_HIDDEN), w2.astype(BF16))


def _cmp_select_kernel(nc, q_ref, kc_ref, vcT_ref, ovT_ref, oc_ref, bias_ref):
    n = pl.program_id(2)
    ncp_all = kc_ref.shape[2]
    nsel_all = ovT_ref.shape[0]
    last_visible = (n * Q_BLOCK + Q_BLOCK - D_CMP_LEN) // D_CMP_STRIDE
    quarter = last_visible // (ncp_all // CMP_PARTS)
    for part in range(CMP_PARTS):
        visible = ((part + 1) * ncp_all // CMP_PARTS, (part + 1) * nsel_all // CMP_PARTS)
        pl.when(quarter == part)(functools.partial(_cmp_select_body, nc, visible, q_ref, kc_ref, vcT_ref, ovT_ref, oc_ref, bias_ref))


def _cmp_select_body(nc, visible, q_ref, kc_ref, vcT_ref, ovT_ref, oc_ref, bias_ref):
    n = pl.program_id(2)
    ncp, nsel = visible
    nsel_all = ovT_ref.shape[0]
    qs = _group_qT(q_ref)
    R = qs.shape[1]
    s = _dot(kc_ref[0, 0, 0:ncp, :], qs)
    tpos = n * Q_BLOCK + (_iota((1, R), 1) & (Q_BLOCK - 1))
    c_last = jnp.minimum((tpos - (D_CMP_LEN - 1)) >> 4, nc - 1)
    s = jnp.where(_iota((ncp, R), 0) <= c_last, s, NEG_INF)
    m = jnp.max(s, axis=0, keepdims=True)
    m = jnp.where(m > NEG_INF, m, 0.0)
    e = jnp.exp(s - m)
    den = jnp.sum(e, axis=0, keepdims=True)
    p = (e * (1.0 / jnp.where(den > 0, den, 1.0))).astype(BF16)
    oc_ref[...] = _group_rows(_dot(vcT_ref[0, 0, :, 0:ncp], p))
    imp_heads = _dot(ovT_ref[0:nsel, 0:ncp], p)
    imp = imp_heads[:, 0:Q_BLOCK]
    for g in range(1, GROUP):
        imp = imp + imp_heads[:, g * Q_BLOCK:(g + 1) * Q_BLOCK]
    t = n * Q_BLOCK + _iota((nsel, Q_BLOCK), 1)
    j = _iota((nsel, Q_BLOCK), 0)
    cur = t >> 6
    imp = jnp.where(j <= cur, imp, NEG_INF)
    imp = jnp.where(_in_range(cur - j, 2), float("inf"), imp)
    imp = jnp.where(j == 0, float("inf"), imp)
    for _ in range(D_SEL_TOPK):
        mx = jnp.max(imp, axis=0, keepdims=True)
        idx = jnp.min(jnp.where(imp == mx, j, nsel), axis=0, keepdims=True)
        imp = jnp.where(j == idx, PICKED, imp)
    bias_ref[0, 0, 0, 0:nsel, :] = jnp.where(imp == PICKED, 0.0, -MASK_BIG)
    if nsel < nsel_all:
        bias_ref[0, 0, 0, nsel:, :] = jnp.full((nsel_all - nsel, Q_BLOCK), -MASK_BIG, F32)


def _cmp_select(proj, q_col, kc, vcT, overlapT, nc, S):
    B, Hkv, ncp, _ = kc.shape
    nq = S // Q_BLOCK
    nsel = overlapT.shape[0]
    in_specs = [
        _q_spec(S, q_col),
        pl.BlockSpec((1, 1, ncp, HEAD_DIM), lambda b, h, n: (b, h, 0, 0)),
        pl.BlockSpec((1, 1, HEAD_DIM, ncp), lambda b, h, n: (b, h, 0, 0)),
        pl.BlockSpec((nsel, ncp), lambda b, h, n: (0, 0)),
    ]
    out_specs = [_o_spec(S), pl.BlockSpec((1, 1, 1, nsel, Q_BLOCK), lambda b, h, n: (b, h, n, 0, 0))]
    out_shape = [jax.ShapeDtypeStruct((B * S, Hkv * GROUP_COLS), F32), jax.ShapeDtypeStruct((B, Hkv, nq, nsel, Q_BLOCK), F32)]
    return pl.pallas_call(
        functools.partial(_cmp_select_kernel, nc), grid=(B, Hkv, nq), in_specs=in_specs, out_specs=out_specs,
        out_shape=out_shape, compiler_params=_params(3), name="nsa_cmp_select")(proj, kc, vcT, overlapT)


def _sel_kernel(q_ref, ka_ref, vTa_ref, bias_ref, o_ref, ksq_scr, p_scr, acc_scr):
    n = pl.program_id(2)
    pl.when(n == 0)(lambda: _max_key_sqnorm(ka_ref, ksq_scr))
    qs = _group_qT(q_ref)

    def bias_rows(kt):
        b = bias_ref[0, 0, 0, pl.ds(pl.multiple_of(kt * AUG_ROWS, AUG_ROWS), AUG_ROWS), :]
        return jnp.concatenate([b] * GROUP, axis=1)

    o_ref[...] = _group_rows(_masked_flash_t(n, qs, bias_rows, ka_ref, vTa_ref, ksq_scr, p_scr, acc_scr))


def _sel_attention(proj, q_col, ka, vTa, biasT):
    B, Hkv, S, _ = ka.shape
    nq = S // Q_BLOCK
    nsel = biasT.shape[3]
    assert KEY_TILE // D_SEL_LEN == AUG_ROWS
    in_specs = [
        _q_spec(S, q_col),
        pl.BlockSpec((1, 1, S, LANES), lambda b, h, n: (b, h, 0, 0)),
        pl.BlockSpec((1, 1, V_AUG_ROWS, S), lambda b, h, n: (b, h, 0, 0)),
        pl.BlockSpec((1, 1, 1, nsel, Q_BLOCK), lambda b, h, n: (b, h, n, 0, 0)),
    ]
    return pl.pallas_call(
        _sel_kernel, grid=(B, Hkv, nq), in_specs=in_specs,
        out_specs=_o_spec(S), out_shape=jax.ShapeDtypeStruct((B * S, Hkv * GROUP_COLS), F32),
        scratch_shapes=_flash_scratch(GROUP * Q_BLOCK),
        compiler_params=_params(3), name="nsa_selected")(proj, ka, vTa, biasT)


def _win_kernel(span, q_ref, k_ref, vT_ref, o_ref):
    n = pl.program_id(2)
    qs = _group_qT(q_ref)
    R = qs.shape[1]
    start = pl.multiple_of(jnp.maximum(n * Q_BLOCK + Q_BLOCK - span, 0), Q_BLOCK)
    s = _dot(k_ref[0, 0, pl.ds(start, span), :], qs)
    tpos = n * Q_BLOCK + (_iota((1, R), 1) & (Q_BLOCK - 1))
    d = tpos - (start + _iota((span, R), 0))
    s = jnp.where(_in_range(d, jnp.full((1, R), D_WINDOW, jnp.int32)), s, NEG_INF)
    m = jnp.max(s, axis=0, keepdims=True)
    p = jnp.exp(s - m)
    l = jnp.sum(p, axis=0, keepdims=True)
    o_ref[...] = _group_rows(_dot(vT_ref[0, 0, :, pl.ds(start, span)], p.astype(BF16)) * (1.0 / l))


def _win_attention(proj, q_col, k, vT):
    B, Hkv, S, _ = k.shape
    nq = S // Q_BLOCK
    span = min(D_WINDOW + Q_BLOCK, S)
    in_specs = [
        _q_spec(S, q_col),
        pl.BlockSpec((1, 1, S, HEAD_DIM), lambda b, h, n: (b, h, 0, 0)),
        pl.BlockSpec((1, 1, HEAD_DIM, S), lambda b, h, n: (b, h, 0, 0)),
    ]
    return pl.pallas_call(
        functools.partial(_win_kernel, span), grid=(B, Hkv, nq), in_specs=in_specs,
        out_specs=_o_spec(S), out_shape=jax.ShapeDtypeStruct((B * S, Hkv * GROUP_COLS), F32),
        compiler_params=_params(3), name="nsa_window")(proj, k, vT)


def _route(logits):
    tm = logits.shape[0]
    lane = _iota((tm, LANES), 1)
    gl = jnp.where(lane < N_GROUPS, logits, NEG_INF)
    gmax = jnp.max(gl, axis=-1, keepdims=True)
    gidx = jnp.min(jnp.where(gl == gmax, lane, LANES), axis=-1, keepdims=True)
    g_prob = 1.0 / jnp.sum(jnp.exp(gl - gmax), axis=-1, keepdims=True)
    elane = lane - N_GROUPS
    in_group = (elane >= 0) & (elane < N_EXPERTS) & ((elane >> 4) == gidx)
    el = jnp.where(in_group, logits, NEG_INF)
    ee = jnp.exp(el - jnp.max(el, axis=-1, keepdims=True))
    ep = jnp.where(in_group, ee / jnp.sum(ee, axis=-1, keepdims=True), -1.0)
    p1 = jnp.max(ep, axis=-1, keepdims=True)
    i1 = jnp.min(jnp.where(ep == p1, lane, LANES), axis=-1, keepdims=True)
    ep2 = jnp.where(lane == i1, -1.0, ep)
    p2 = jnp.max(ep2, axis=-1, keepdims=True)
    i2 = jnp.min(jnp.where(ep2 == p2, lane, LANES), axis=-1, keepdims=True)
    den = p1 + p2
    vals = [(i1 - N_GROUPS).astype(F32), (i2 - N_GROUPS).astype(F32), g_prob * p1 / den, g_prob * p2 / den]
    out = jnp.zeros((tm, LANES), F32)
    for k, val in enumerate(vals):
        out = jnp.where(lane == k, val, out)
    return out


def _pack_bf16_pairs(h):
    n = h.shape[1] // 2
    bits = lax.bitcast_convert_type(h.astype(BF16).astype(F32), jnp.uint32)
    return bits[:, :n] | (bits[:, n:] >> 16)


def _unpack_bf16_pairs(packed):
    hi = lax.bitcast_convert_type(packed & jnp.uint32(0xFFFF0000), F32)
    lo = lax.bitcast_convert_type(packed << 16, F32)
    return jnp.concatenate([hi, lo], axis=1).astype(BF16)


def _out_tail(x_new, gain_ref, wr_ref, xo_ref, h_ref, route_ref):
    xo_ref[...] = x_new
    ms = jnp.mean(x_new * x_new, axis=-1, keepdims=True)
    h = x_new * lax.rsqrt(ms + NORM_EPS) * gain_ref[...]
    h_ref[...] = _pack_bf16_pairs(h)
    h_hi = h.astype(BF16)
    h_lo = (h - h_hi.astype(F32)).astype(BF16)
    logits = _dot(h_hi, wr_ref[0]) + (_dot(h_lo, wr_ref[0]) + _dot(h_hi, wr_ref[1]))
    route_ref[...] = _route(logits)


def _out_proj_kernel(oa_ref, ob_ref, x_ref, w_ref, gain_ref, wr_ref, xo_ref, h_ref, route_ref):
    half = w_ref.shape[0] // 2
    x_new = x_ref[...] + _dot(oa_ref[...], w_ref[0:half, :]) + _dot(ob_ref[...], w_ref[half:, :])
    _out_tail(x_new, gain_ref, wr_ref, xo_ref, h_ref, route_ref)


def _out_proj_nsa_kernel(oc_ref, b0_ref, b1_ref, b2_ref, gd_ref, x_ref, w_ref, gain_ref, wr_ref, xo_ref, h_ref, route_ref):
    half = D_HEADS * HEAD_DIM
    g = jax.nn.sigmoid(gd_ref[...])
    g_hi = g.astype(BF16)
    g_lo = (g - g_hi.astype(F32)).astype(BF16)
    src = _iota((LANES, half), 0)
    head3 = (_iota((LANES, half), 1) >> 6) * 3
    od = None
    for br, b_ref in enumerate((b0_ref, b1_ref, b2_ref)):
        spread = jnp.where(src == head3 + br, 1.0, 0.0).astype(BF16)
        term = (_dot(g_hi, spread) + _dot(g_lo, spread)) * b_ref[...]
        od = term if od is None else od + term
    x_new = x_ref[...] + _dot(oc_ref[...], w_ref[0:half, :]) + _dot(od.astype(BF16), w_ref[half:, :])
    _out_tail(x_new, gain_ref, wr_ref, xo_ref, h_ref, route_ref)


def _out_proj(o_parts, gd, x, w_out, gain, w_router):
    T = x.shape[0]
    tm = ROW_TILE
    row = lambda i: (i, 0)
    fixed = lambda i: (0, 0)
    if gd is None:
        kern = _out_proj_kernel
        args = list(o_parts)
        in_specs = [pl.BlockSpec((tm, D_MODEL // 2), row)] * 2
    else:
        kern = _out_proj_nsa_kernel
        args = list(o_parts) + [gd]
        in_specs = [pl.BlockSpec((tm, D_MODEL // 2), row)] * 4 + [pl.BlockSpec((tm, LANES), row)]
    args += [x, w_out, gain.reshape(1, D_MODEL), w_router]
    in_specs += [pl.BlockSpec((tm, D_MODEL), row), pl.BlockSpec((D_MODEL, D_MODEL), fixed),
                 pl.BlockSpec((1, D_MODEL), fixed), pl.BlockSpec((2, D_MODEL, LANES), lambda i: (0, 0, 0))]
    out_shape = [jax.ShapeDtypeStruct((T, D_MODEL), F32), jax.ShapeDtypeStruct((T, D_MODEL // 2), jnp.uint32),
                 jax.ShapeDtypeStruct((T, LANES), F32)]
    out_specs = [pl.BlockSpec((tm, D_MODEL), row), pl.BlockSpec((tm, D_MODEL // 2), row), pl.BlockSpec((tm, LANES), row)]
    return pl.pallas_call(kern, grid=(T // tm,), in_specs=in_specs, out_specs=out_specs, out_shape=out_shape,
                          compiler_params=_params(1), name="out_proj_router")(*args)


def _expert_kernel(be_ref, rows_ref, wg_ref, wu_ref, wd_ref, y_ref, wg_s, wu_s, wd_s):
    i = pl.program_id(0)
    n_blk = pl.num_programs(0)

    @pl.when((i == 0) | (be_ref[i] != be_ref[jnp.maximum(i - 1, 0)]))
    def _():
        wg_s[...] = wg_ref[0].astype(BF16)
        wu_s[...] = wu_ref[0].astype(BF16)
        wd_s[...] = wd_ref[0].astype(BF16)

    @pl.when(i < be_ref[n_blk])
    def _():
        xb = _unpack_bf16_pairs(rows_ref[...])
        hid = jax.nn.silu(_dot(xb, wg_s[...])) * _dot(xb, wu_s[...])
        y_ref[...] = _dot(hid.astype(BF16), wd_s[...])

    @pl.when(i >= be_ref[n_blk])
    def _():
        y_ref[...] = jnp.zeros(y_ref.shape, F32)


def _expert_ffn(rows, blk_info, wg, wu, wd):
    n_rows = rows.shape[0]
    n_blk = n_rows // MOE_ROWS
    grid_spec = pltpu.PrefetchScalarGridSpec(
        num_scalar_prefetch=1, grid=(n_blk,),
        in_specs=[
            pl.BlockSpec((MOE_ROWS, D_MODEL // 2), lambda i, be: (i, 0)),
            pl.BlockSpec((1, D_MODEL, EXPERT_HIDDEN), lambda i, be: (be[i], 0, 0)),
            pl.BlockSpec((1, D_MODEL, EXPERT_HIDDEN), lambda i, be: (be[i], 0, 0)),
            pl.BlockSpec((1, EXPERT_HIDDEN, D_MODEL), lambda i, be: (be[i], 0, 0)),
        ],
        out_specs=pl.BlockSpec((MOE_ROWS, D_MODEL), lambda i, be: (i, 0)),
        scratch_shapes=[pltpu.VMEM((D_MODEL, EXPERT_HIDDEN), BF16), pltpu.VMEM((D_MODEL, EXPERT_HIDDEN), BF16),
                        pltpu.VMEM((EXPERT_HIDDEN, D_MODEL), BF16)])
    return pl.pallas_call(
        _expert_kernel, grid_spec=grid_spec, out_shape=jax.ShapeDtypeStruct((n_rows, D_MODEL), F32),
        compiler_params=_params(1), name="expert_ffn")(blk_info, rows, wg, wu, wd)


def _moe_dispatch(route, h):
    n_tok = h.shape[0]
    n_asg = n_tok * MOE_TOPK
    i32 = jnp.int32
    e_flat = route[:, 0:MOE_TOPK].astype(i32).reshape(n_asg)
    is_e = e_flat[:, None] == jnp.arange(N_EXPERTS, dtype=i32)[None, :]
    counts = jnp.sum(is_e, axis=0, dtype=i32)
    assert n_asg & (n_asg - 1) == 0 and n_asg <= 1 << 16
    idx = jnp.arange(n_asg, dtype=jnp.uint32)
    low = jnp.uint32(n_asg - 1)
    order = (jnp.sort(e_flat.astype(jnp.uint32) * n_asg + idx) & low).astype(i32)
    rank = (jnp.sort(order.astype(jnp.uint32) * n_asg + idx) & low).astype(i32)
    padded = (counts + MOE_ROWS - 1) // MOE_ROWS * MOE_ROWS
    pad_end = jnp.cumsum(padded)
    pad_start = pad_end - padded
    start = jnp.cumsum(counts) - counts
    n_rows = n_asg + N_EXPERTS * MOE_ROWS
    n_blk = n_rows // MOE_ROWS
    blk_start = jnp.arange(n_blk, dtype=i32) * MOE_ROWS
    blk_expert = jnp.minimum(jnp.sum(pad_end[None, :] <= blk_start[:, None], axis=1, dtype=i32), N_EXPERTS - 1)
    within = (blk_start - pad_start[blk_expert])[:, None] + jnp.arange(MOE_ROWS, dtype=i32)[None, :]
    valid = within < counts[blk_expert][:, None]
    src = jnp.clip(start[blk_expert][:, None] + within, 0, n_asg - 1)
    row_tok = jnp.where(valid, (order // MOE_TOPK)[src], src // MOE_TOPK).reshape(n_rows)
    shift = jnp.sum(jnp.where(is_e, (pad_start - start)[None, :], 0), axis=1, dtype=i32)
    pos = (rank + shift).reshape(n_tok, MOE_TOPK)
    blk_info = jnp.concatenate([blk_expert, (pad_end[-1:] // MOE_ROWS).astype(i32)])
    return h[row_tok], blk_info, pos


def _moe(route, h, wg, wu, wd):
    rows, blk_info, pos = _moe_dispatch(route, h)
    y = _expert_ffn(rows, blk_info, wg, wu, wd)
    return y[pos[:, 0]], y[pos[:, 1]], route


def _final_kernel(x_ref, y0_ref, y1_ref, route_ref, g_ref, o_ref):
    x = _moe_combine(x_ref[...], y0_ref, y1_ref, route_ref)
    ms = jnp.mean(x * x, axis=-1, keepdims=True)
    o_ref[...] = x * lax.rsqrt(ms + NORM_EPS) * g_ref[...]


def _final_norm(x, ys, gain):
    T = x.shape[0]
    tm = ROW_TILE
    row = lambda i: (i, 0)
    return pl.pallas_call(
        _final_kernel, grid=(T // tm,),
        in_specs=[pl.BlockSpec((tm, D_MODEL), row)] * 3 + [pl.BlockSpec((tm, LANES), row), pl.BlockSpec((1, D_MODEL), lambda i: (0, 0))],
        out_specs=pl.BlockSpec((tm, D_MODEL), row), out_shape=jax.ShapeDtypeStruct((T, D_MODEL), F32),
        compiler_params=_params(1), name="final_norm")(x, *ys, gain.reshape(1, D_MODEL))


def _rope_tables(positions):
    inv_freq = ROPE_THETA ** (-jnp.arange(0, HEAD_DIM, 2, dtype=F32) / HEAD_DIM)
    ang = positions.astype(F32).reshape(-1, 1) * inv_freq
    cos, sin = jnp.cos(ang), jnp.sin(ang)
    reps = LANES // HEAD_DIM
    return jnp.tile(jnp.concatenate([cos, cos], axis=1), (1, reps)), jnp.tile(jnp.concatenate([-sin, sin], axis=1), (1, reps))


def _chunks(first, count, rope, action):
    return tuple((first + j, rope, action(j)) for j in range(count))


def _plan(*runs):
    entries = sum(runs, ())
    assert [e[0] for e in entries] == list(range(len(entries)))
    return tuple(e[1:] for e in entries)


AB_OUTS = (("tok", 4), ("heads", A_KV_HEADS), ("headsT", A_KV_HEADS), ("heads", B_HEADS), ("headsT", B_HEADS), ("headsT", B_HEADS))
AB_PLAN = _plan(_chunks(0, 4, True, lambda j: (("tok", 0, j),)),
                _chunks(4, 1, True, lambda j: (("heads", 1, 0),)),
                _chunks(5, 1, False, lambda j: (("headsT", 2, 0),)),
                _chunks(6, 4, False, lambda j: (("heads", 3, 2 * j),)),
                _chunks(10, 4, False, lambda j: (("headsT", 4, 2 * j),)),
                _chunks(14, 4, False, lambda j: (("headsT", 5, 2 * j),)))
CD_OUTS = (("tok", 8), ("keys_aug", C_KV_HEADS), ("kmean", 0), ("valsT_aug", C_KV_HEADS), ("heads", D_KV_HEADS),
           ("heads", D_KV_HEADS), ("keys_aug", D_KV_HEADS), ("valsT_aug", D_KV_HEADS), ("heads", D_KV_HEADS),
           ("headsT", D_KV_HEADS), ("f32", 0))
CD_PLAN = _plan(_chunks(0, 4, True, lambda j: (("tok", 0, j),)),
                _chunks(4, 1, True, lambda j: (("keys_aug", 1, C_BLOCK), ("kmean", 2, 0))),
                _chunks(5, 1, False, lambda j: (("valsT_aug", 3, 0),)),
                _chunks(6, 4, True, lambda j: (("tok", 0, 4 + j),)),
                _chunks(10, 1, True, lambda j: (("heads", 4, 0),)),
                _chunks(11, 1, False, lambda j: (("heads", 5, 0),)),
                _chunks(12, 1, True, lambda j: (("keys_aug", 6, D_SEL_LEN),)),
                _chunks(13, 1, False, lambda j: (("valsT_aug", 7, 0),)),
                _chunks(14, 1, True, lambda j: (("heads", 8, 0),)),
                _chunks(15, 1, False, lambda j: (("headsT", 9, 0),)),
                _chunks(16, 1, False, lambda j: (("f32", 10, 0),)))
QD_COL = 4 * LANES


def _router_weights(router_group, router_expert):
    pad = jnp.zeros((D_MODEL, LANES - N_GROUPS - N_EXPERTS), F32)
    w = jnp.concatenate([router_group.astype(F32), router_expert.astype(F32), pad], axis=1)
    hi = w.astype(BF16)
    return jnp.stack([hi, (w - hi.astype(F32)).astype(BF16)])


def _pad_cols(w, n):
    return jnp.concatenate([w, jnp.zeros((w.shape[0], n - w.shape[1]), w.dtype)], axis=1)


def _mixer_ab(parts, sinks):
    qa, ka, vaT, qb, kbT, vbT = parts
    oa = _swa_attention(qa, 0, ka, vaT, sinks)
    ob = _stick_attention(qb, kbT, vbT)
    return oa, ob


def _mixer_cd(parts, B, S, k_pe, k_w1, k_w2, v_pe, v_w1, v_w2):
    q_cd, kc_aug, kmean, vcT_aug, kdc, vdc, kds_aug, vdsT_aug, kdw, vdwT, _ = parts
    nblk = S // C_BLOCK
    km = kmean.reshape(B, nblk, C_KV_HEADS, HEAD_DIM).transpose(0, 2, 1, 3).astype(BF16)
    oc = _moba_attention(q_cd, 0, kc_aug, vcT_aug, km)
    k_cmp = _compress(kdc, k_pe, k_w1, k_w2)
    v_cmp = _compress(vdc, v_pe, v_w1, v_w2)
    nc = (S - D_CMP_LEN) // D_CMP_STRIDE + 1
    ncp = S // D_CMP_STRIDE
    nsel = S // D_SEL_LEN
    c_start = jnp.arange(ncp) * D_CMP_STRIDE
    b_start = jnp.arange(nsel) * D_SEL_LEN
    overlap = ((c_start[:, None] <= b_start[None, :] + D_SEL_LEN - 1) & (c_start[:, None] + D_CMP_LEN - 1 >= b_start[None, :])
               & (jnp.arange(ncp)[:, None] < nc)).astype(BF16)
    o_cmp, biasT = _cmp_select(q_cd, QD_COL, k_cmp, v_cmp.transpose(0, 1, 3, 2), overlap.T, nc, S)
    o_sel = _sel_attention(q_cd, QD_COL, kds_aug, vdsT_aug, biasT)
    o_win = _win_attention(q_cd, QD_COL, kdw, vdwT)
    return oc, o_cmp, o_sel, o_win


def kernel(x, positions, ln_mix_0, w_in_0, sinks_0, w_out_0, ln_ffn_0, router_group_0, router_expert_0, expert_gate_0, expert_up_0, expert_down_0, ln_mix_1, w_in_1, cmp_k_pe_1, cmp_k_w1_1, cmp_k_w2_1, cmp_v_pe_1, cmp_v_w1_1, cmp_v_w2_1, w_out_1, ln_ffn_1, router_group_1, router_expert_1, expert_gate_1, expert_up_1, expert_down_1, ln_final):
    B, S, _ = x.shape
    T = B * S
    assert S % KEY_TILE == 0 and T % ROW_TILE == 0
    cos_t, sin_t = _rope_tables(positions)
    xf = x.reshape(T, D_MODEL)

    parts = _in_proj(xf, None, ln_mix_0, w_in_0.astype(BF16), cos_t, sin_t, AB_PLAN, AB_OUTS, B, S)
    o_ab = _mixer_ab(parts, sinks_0)
    x1, h1, route1 = _out_proj(o_ab, None, xf, w_out_0.astype(BF16), ln_ffn_0, _router_weights(router_group_0, router_expert_0))
    ys = _moe(route1, h1, expert_gate_0, expert_up_0, expert_down_0)

    x2, *parts = _in_proj(x1, ys, ln_mix_1, _pad_cols(w_in_1.astype(BF16), len(CD_PLAN) * LANES), cos_t, sin_t,
                          CD_PLAN, CD_OUTS, B, S)
    gd = parts[-1]
    parts = _mixer_cd(parts, B, S, cmp_k_pe_1, cmp_k_w1_1, cmp_k_w2_1, cmp_v_pe_1, cmp_v_w1_1, cmp_v_w2_1)
    x3, h3, route3 = _out_proj(parts, gd, x2, w_out_1.astype(BF16), ln_ffn_1, _router_weights(router_group_1, router_expert_1))
    ys = _moe(route3, h3, expert_gate_1, expert_up_1, expert_down_1)
    return _final_norm(x3, ys, ln_final).reshape(B, S, D_MODEL)
```

```python
import functools

import jax
import jax.numpy as jnp
from jax import lax
from jax.experimental import pallas as pl
from jax.experimental.pallas import tpu as pltpu

D_MODEL = 1024
HEAD_DIM = 64
HALF = HEAD_DIM // 2
ROPE_THETA = 10000.0
NORM_EPS = 1e-6
Q_BLOCK = 128
SCALE = HEAD_DIM ** -0.5

A_HEADS, A_KV_HEADS, A_WINDOW = 8, 2, 128
B_HEADS = 8
C_HEADS, C_KV_HEADS, C_BLOCK, C_TOPK = 8, 2, 256, 3
D_HEADS, D_KV_HEADS = 8, 2
D_CMP_LEN, D_CMP_STRIDE, D_CMP_HIDDEN = 32, 16, 256
D_SEL_LEN, D_SEL_TOPK, D_WINDOW = 64, 16, 512
N_GROUPS, EXPERTS_PER_GROUP, MOE_TOPK, EXPERT_HIDDEN = 4, 16, 2, 512
N_EXPERTS = N_GROUPS * EXPERTS_PER_GROUP
GROUP = 4

LANES = 128
MXU_COLS = 256
ROW_TILE = 512
MOE_ROWS = 256
KEY_TILE = 512
FLASH_MIN_SUM = 1e-25
REF_MARGIN = 1.05
MASKED = -1e30
V_AUG_ROWS = 80
AUG_ROWS = 8
MASK_BIG = 2.0 ** 100
CMP_PARTS = 4
PICKED = -1.0
SB_EXIT = -104.0
STICK_HEADS = 8
VMEM_LIMIT = 56 * 1024 * 1024

F32 = jnp.float32
BF16 = jnp.bfloat16
NEG_INF = float("-inf")


def _iota(shape, dim):
    return lax.broadcasted_iota(jnp.int32, shape, dim)


def _dot(a, b):
    return jnp.dot(a, b, preferred_element_type=F32)


def _in_range(d, width):
    return lax.bitcast_convert_type(d, jnp.uint32) < lax.bitcast_convert_type(jnp.asarray(width, jnp.int32), jnp.uint32)


def _params(n_grid):
    return pltpu.CompilerParams(dimension_semantics=("arbitrary",) * n_grid, vmem_limit_bytes=VMEM_LIMIT)


GROUP_COLS = GROUP * HEAD_DIM


def _group_qT(q_ref):
    t = (q_ref[...].astype(F32) * SCALE).T
    return jnp.concatenate([t[g * HEAD_DIM:(g + 1) * HEAD_DIM] for g in range(GROUP)], axis=1).astype(BF16)


def _group_rows(oT):
    return jnp.concatenate([oT[:, g * Q_BLOCK:(g + 1) * Q_BLOCK] for g in range(GROUP)], axis=0).T


def _q_spec(S, col0):
    nq = S // Q_BLOCK
    return pl.BlockSpec((Q_BLOCK, GROUP_COLS), lambda b, h, n: (b * nq + n, col0 // GROUP_COLS + h))


def _o_spec(S):
    nq = S // Q_BLOCK
    return pl.BlockSpec((Q_BLOCK, GROUP_COLS), lambda b, h, n: (b * nq + n, h))


def _moe_combine(x, y0_ref, y1_ref, route_ref):
    return x + (y0_ref[...] * route_ref[:, MOE_TOPK:MOE_TOPK + 1] + y1_ref[...] * route_ref[:, MOE_TOPK + 1:MOE_TOPK + 2])


def _emit_chunk(ch, actions, outs):
    tm = ch.shape[0]
    chT = None
    for kind, oi, arg in actions:
        o = outs[oi]
        if kind == "tok":
            o[:, arg * LANES:(arg + 1) * LANES] = ch.astype(BF16)
        elif kind == "f32":
            o[...] = ch
        elif kind == "kmean":
            o[...] = jnp.mean(ch.reshape(tm // C_BLOCK, C_BLOCK, LANES), axis=1).reshape(tm // C_BLOCK, 1, LANES)
        elif kind == "heads":
            for hh in range(2):
                o[0, arg + hh] = ch[:, hh * HEAD_DIM:(hh + 1) * HEAD_DIM].astype(BF16)
        elif kind == "keys_aug":
            lane = _iota((tm, LANES), 1)
            block_in_tile = _iota((tm, LANES), 0) >> (arg.bit_length() - 1)
            aug = jnp.where((lane - HEAD_DIM == block_in_tile) | (lane == HEAD_DIM + AUG_ROWS), 1.0, 0.0)
            for hh in range(2):
                keys = ch if hh == 0 else pltpu.roll(ch, HEAD_DIM, 1)
                o[0, hh] = jnp.where(lane < HEAD_DIM, keys, aug).astype(BF16)
        else:
            chT = ch.astype(BF16).astype(F32).T if chT is None else chT
            for hh in range(2):
                vT = chT[hh * HEAD_DIM:(hh + 1) * HEAD_DIM]
                if kind == "headsT":
                    o[0, arg + hh] = vT.astype(BF16)
                else:
                    assert kind == "valsT_aug"
                    tail = jnp.where(_iota((V_AUG_ROWS - HEAD_DIM, tm), 0) == 0, 1.0, 0.0)
                    o[0, hh] = jnp.concatenate([vT, tail], axis=0).astype(BF16)


def _in_proj_kernel(plan, combine, *refs):
    refs = list(refs)
    x_ref = refs.pop(0)
    if combine:
        y0_ref, y1_ref, route_ref = refs.pop(0), refs.pop(0), refs.pop(0)
    g_ref, w_ref, cos_ref, sin_ref = refs[:4]
    outs = refs[4:]
    x = x_ref[...]
    if combine:
        x = _moe_combine(x, y0_ref, y1_ref, route_ref)
        xo_ref = outs.pop(0)
        xo_ref[...] = x
    ms = jnp.mean(x * x, axis=-1, keepdims=True)
    h = (x * lax.rsqrt(ms + NORM_EPS) * g_ref[...]).astype(BF16)
    tm = x.shape[0]
    cos = cos_ref[...]
    sin = sin_ref[...]
    first_half = (_iota((tm, LANES), 1) & (HEAD_DIM - 1)) < HALF
    for c0 in range(0, len(plan), MXU_COLS // LANES):
        group = plan[c0:c0 + MXU_COLS // LANES]
        wide = _dot(h, w_ref[:, c0 * LANES:(c0 + len(group)) * LANES])
        for k, (rope, actions) in enumerate(group):
            ch = wide[:, k * LANES:(k + 1) * LANES]
            if rope:
                partner = jnp.where(first_half, pltpu.roll(ch, LANES - HALF, 1), pltpu.roll(ch, HALF, 1))
                ch = ch * cos + partner * sin
            _emit_chunk(ch, actions, outs)


def _in_proj(x, ys, gain, w, cos_t, sin_t, plan, out_kinds, B, S):
    T = x.shape[0]
    n_cols = w.shape[1]
    tm = ROW_TILE
    assert tm == KEY_TILE and S % tm == 0 and len(plan) * LANES == n_cols
    tpb = S // tm
    combine = ys is not None
    row = lambda i: (i, 0)
    fixed = lambda i: (0, 0)
    by_seq = lambda i: (i // tpb, 0, i % tpb, 0)
    by_seq_t = lambda i: (i // tpb, 0, 0, i % tpb)
    in_specs = [pl.BlockSpec((tm, D_MODEL), row)]
    args = [x]
    if combine:
        in_specs += [pl.BlockSpec((tm, D_MODEL), row)] * 2 + [pl.BlockSpec((tm, LANES), row)]
        args += list(ys)
    in_specs += [pl.BlockSpec((1, D_MODEL), fixed), pl.BlockSpec((D_MODEL, n_cols), fixed),
                 pl.BlockSpec((tm, LANES), row), pl.BlockSpec((tm, LANES), row)]
    args += [gain.reshape(1, D_MODEL), w, cos_t, sin_t]
    out_shape, out_specs = [], []
    if combine:
        out_shape.append(jax.ShapeDtypeStruct((T, D_MODEL), F32))
        out_specs.append(pl.BlockSpec((tm, D_MODEL), row))
    for kind, size in out_kinds:
        if kind == "tok":
            shape, dtype, spec = (T, size * LANES), BF16, pl.BlockSpec((tm, size * LANES), row)
        elif kind == "f32":
            shape, dtype, spec = (T, LANES), F32, pl.BlockSpec((tm, LANES), row)
        elif kind == "kmean":
            shape, dtype, spec = (T // C_BLOCK, 1, LANES), F32, pl.BlockSpec((tm // C_BLOCK, 1, LANES), lambda i: (i, 0, 0))
        elif kind == "heads":
            shape, dtype, spec = (B, size, S, HEAD_DIM), BF16, pl.BlockSpec((1, size, tm, HEAD_DIM), by_seq)
        elif kind == "keys_aug":
            shape, dtype, spec = (B, size, S, LANES), BF16, pl.BlockSpec((1, size, tm, LANES), by_seq)
        elif kind == "headsT":
            shape, dtype, spec = (B, size, HEAD_DIM, S), BF16, pl.BlockSpec((1, size, HEAD_DIM, tm), by_seq_t)
        else:
            assert kind == "valsT_aug"
            shape, dtype, spec = (B, size, V_AUG_ROWS, S), BF16, pl.BlockSpec((1, size, V_AUG_ROWS, tm), by_seq_t)
        out_shape.append(jax.ShapeDtypeStruct(shape, dtype))
        out_specs.append(spec)
    kern = functools.partial(_in_proj_kernel, plan, combine)
    return pl.pallas_call(kern, grid=(T // tm,), in_specs=in_specs, out_specs=out_specs, out_shape=out_shape,
                          compiler_params=_params(1), name="in_proj")(*args)


def _swa_kernel(q_ref, kp_ref, ko_ref, vp_ref, vo_ref, sink_ref, o_ref):
    n = pl.program_id(2)
    qs = _group_qT(q_ref)
    R = qs.shape[1]
    k = jnp.concatenate([kp_ref[0, 0], ko_ref[0, 0]], axis=0)
    vT = jnp.concatenate([vp_ref[0, 0], vo_ref[0, 0]], axis=1)
    s = _dot(k, qs)
    qpos = _iota((1, R), 1) & (Q_BLOCK - 1)
    d = qpos + Q_BLOCK - _iota((2 * Q_BLOCK, R), 0)
    width = jnp.minimum(A_WINDOW, qpos + 1 + jnp.where(n > 0, Q_BLOCK, 0))
    s = jnp.where(_in_range(d, width), s, NEG_INF)
    sink = sink_ref[0]
    m = jnp.maximum(jnp.max(s, axis=0, keepdims=True), sink)
    p = jnp.exp(s - m)
    den = jnp.sum(p, axis=0, keepdims=True) + jnp.exp(sink - m)
    o_ref[...] = _group_rows(_dot(vT, p.astype(BF16)) * (1.0 / den)).astype(o_ref.dtype)


def _swa_attention(proj, q_col, k, vT, sinks):
    B, Hkv, S, _ = k.shape
    nb = S // Q_BLOCK
    R = GROUP * Q_BLOCK
    sink_row = jnp.repeat(sinks.astype(F32).reshape(Hkv, GROUP), Q_BLOCK, axis=1).reshape(Hkv, 1, R)
    prev = lambda n: jnp.maximum(n - 1, 0)
    in_specs = [
        _q_spec(S, q_col),
        pl.BlockSpec((1, 1, Q_BLOCK, HEAD_DIM), lambda b, h, n: (b, h, prev(n), 0)),
        pl.BlockSpec((1, 1, Q_BLOCK, HEAD_DIM), lambda b, h, n: (b, h, n, 0)),
        pl.BlockSpec((1, 1, HEAD_DIM, Q_BLOCK), lambda b, h, n: (b, h, 0, prev(n))),
        pl.BlockSpec((1, 1, HEAD_DIM, Q_BLOCK), lambda b, h, n: (b, h, 0, n)),
        pl.BlockSpec((1, 1, R), lambda b, h, n: (h, 0, 0)),
    ]
    return pl.pallas_call(
        _swa_kernel, grid=(B, Hkv, nb), in_specs=in_specs,
        out_specs=_o_spec(S), out_shape=jax.ShapeDtypeStruct((B * S, Hkv * GROUP_COLS), BF16),
        compiler_params=_params(3), name="swa_attention")(proj, k, k, vT, vT, sink_row)


def _stick_kernel(q_ref, kT_ref, vT_ref, o_ref):
    n = pl.program_id(2)
    tq = tk = Q_BLOCK
    heads = q_ref.shape[1]
    row = _iota((tq, tk), 0)
    col = _iota((tk, tk), 1)
    upper = jnp.where(_iota((tk, tk), 0) > col, 1.0, 0.0).astype(BF16)
    tpos = n * tq + row

    def body(carry):
        kb, _, cs, accs = carry
        start = pl.multiple_of(kb * tk, tk)
        past = (start + col) < tpos
        hs = range(heads)
        zs = [_dot(q_ref[0, h], kT_ref[0, h, :, pl.ds(start, tk)]) * SCALE for h in hs]
        sps = [jnp.maximum(z, 0.0) + jnp.log(1.0 + jnp.exp(-jnp.abs(z))) for z in zs]
        stays = [jnp.where(past, -sp, 0.0) for sp in sps]
        his = [st.astype(BF16) for st in stays]
        los = [(st - hi.astype(F32)).astype(BF16) for st, hi in zip(stays, his)]
        betweens = [_dot(hi, upper) + _dot(lo, upper) for hi, lo in zip(his, los)]
        ws = [jnp.where(past, jnp.exp(zs[h] - sps[h] + betweens[h] + cs[h]), 0.0).astype(BF16) for h in hs]
        new_accs = [accs[h] + lax.dot_general(ws[h], vT_ref[0, h, :, pl.ds(start, tk)], (((1,), (1,)), ((), ())),
                                              preferred_element_type=F32) for h in hs]
        new_cs = [cs[h] + jnp.sum(stays[h], axis=-1, keepdims=True) for h in hs]
        worst = functools.reduce(jnp.maximum, new_cs)
        return kb - 1, jnp.max(worst) > SB_EXIT, tuple(new_cs), tuple(new_accs)

    def cond(carry):
        kb, alive, _, _ = carry
        return (kb >= 0) & alive

    init = (n, jnp.array(True), (jnp.zeros((tq, 1), F32),) * heads, (jnp.zeros((tq, HEAD_DIM), F32),) * heads)
    _, _, _, accs = lax.while_loop(cond, body, init)
    o_ref[...] = jnp.concatenate(accs, axis=1).astype(o_ref.dtype)


def _stick_attention(q, kT, v):
    B, H, S, _ = q.shape
    nq = S // Q_BLOCK
    hb = STICK_HEADS
    resident = dict(pipeline_mode=pl.Buffered(1))
    in_specs = [
        pl.BlockSpec((1, hb, Q_BLOCK, HEAD_DIM), lambda b, h, n: (b, h, n, 0)),
        pl.BlockSpec((1, hb, HEAD_DIM, S), lambda b, h, n: (b, h, 0, 0), **resident),
        pl.BlockSpec((1, hb, HEAD_DIM, S), lambda b, h, n: (b, h, 0, 0), **resident),
    ]
    return pl.pallas_call(
        _stick_kernel, grid=(B, H // hb, nq), in_specs=in_specs,
        out_specs=pl.BlockSpec((Q_BLOCK, hb * HEAD_DIM), lambda b, h, n: (b * nq + n, h)),
        out_shape=jax.ShapeDtypeStruct((B * S, H * HEAD_DIM), BF16),
        compiler_params=_params(3), name="stick_attention")(q, kT, v)


def _max_key_sqnorm(ka_ref, out_ref):
    S = ka_ref.shape[2]
    tk = KEY_TILE
    is_key_lane = _iota((tk, LANES), 1) < HEAD_DIM
    ones = jnp.ones((LANES, LANES), BF16)

    def body(i, mx):
        k = jnp.where(is_key_lane, ka_ref[0, 0, pl.ds(pl.multiple_of(i * tk, tk), tk), :].astype(F32), 0.0)
        sq = _dot((k * k).astype(BF16), ones)
        return jnp.maximum(mx, jnp.max(sq, axis=0, keepdims=True))

    mx = lax.fori_loop(0, S // tk, body, jnp.zeros((1, LANES), F32))
    out_ref[...] = jnp.broadcast_to(mx, out_ref.shape)


def _flash_scratch(R):
    return [pltpu.VMEM((AUG_ROWS, LANES), F32), pltpu.VMEM((2, KEY_TILE, R), BF16), pltpu.VMEM((V_AUG_ROWS, R), F32)]


def _masked_flash_t(n, qT, bias_rows, ka_ref, vTa_ref, ksq_ref, p_scr, acc_scr):
    R = qT.shape[1]
    tk = KEY_TILE
    diag = (n * Q_BLOCK) // tk
    zpad = jnp.zeros((LANES - HEAD_DIM - 2 * AUG_ROWS, R), F32)
    causal = diag * tk + _iota((tk, R), 0) <= n * Q_BLOCK + (_iota((tk, R), 1) & (Q_BLOCK - 1))

    def scores(kt, ref_rows):
        st = pl.multiple_of(kt * tk, tk)
        low = jnp.concatenate([bias_rows(kt), ref_rows, zpad], axis=0).astype(BF16)
        return _dot(ka_ref[0, 0, pl.ds(st, tk), :], jnp.concatenate([qT, low], axis=0))

    def values(kt):
        return vTa_ref[0, 0, :, pl.ds(pl.multiple_of(kt * tk, tk), tk)]

    qsq = jnp.sum(jnp.square(qT.astype(F32)), axis=0, keepdims=True)
    ksq = jnp.concatenate([ksq_ref[0:1, :]] * (R // LANES), axis=1)
    ref = jnp.sqrt(qsq * ksq) * REF_MARGIN
    ref_rows = jnp.where(_iota((AUG_ROWS, R), 0) == 0, -ref, 0.0)

    def fast_body(kt, carry):
        slot = kt & 1
        s = scores(kt, ref_rows)
        acc_scr[...] += _dot(values(jnp.maximum(kt - 1, 0)), p_scr[1 - slot])
        p_scr[slot] = jnp.exp(s).astype(BF16)
        return carry

    p_scr[1] = jnp.zeros((tk, R), BF16)
    acc_scr[...] = jnp.zeros(acc_scr.shape, F32)
    lax.fori_loop(0, diag, fast_body, 0)
    acc = acc_scr[...] + _dot(values(jnp.maximum(diag - 1, 0)), p_scr[1 - (diag & 1)])
    s = jnp.where(causal, scores(diag, ref_rows), MASKED)
    acc = acc + _dot(values(diag), jnp.exp(s).astype(BF16))

    def running_max_path(_):
        no_ref = jnp.zeros((AUG_ROWS, R), F32)

        def update(kt, s, m, acc):
            m_new = jnp.maximum(m, jnp.max(s, axis=0, keepdims=True))
            p = jnp.exp(s - m_new).astype(BF16)
            return m_new, jnp.exp(m - m_new) * acc + _dot(values(kt), p)

        init = (jnp.full((1, R), MASKED, F32), jnp.zeros((vTa_ref.shape[2], R), F32))
        m, acc = lax.fori_loop(0, diag, lambda kt, c: update(kt, scores(kt, no_ref), *c), init)
        return update(diag, jnp.where(causal, scores(diag, no_ref), MASKED), m, acc)[1]

    healthy = jnp.min(acc[HEAD_DIM:HEAD_DIM + 1, :]) >= FLASH_MIN_SUM
    acc = lax.cond(healthy, lambda _: acc, running_max_path, 0)
    return acc[0:HEAD_DIM] * (1.0 / acc[HEAD_DIM:HEAD_DIM + 1])


def _moba_kernel(q_ref, ka_ref, vTa_ref, km_ref, o_ref, bias_scr, ksq_scr, p_scr, acc_scr):
    n = pl.program_id(2)
    pl.when(n == 0)(lambda: _max_key_sqnorm(ka_ref, ksq_scr))
    nblk = km_ref.shape[2]
    qs = _group_qT(q_ref)
    R = qs.shape[1]
    own = (n * Q_BLOCK) // C_BLOCK
    gate = _dot(km_ref[0, 0], qs)
    blk = _iota((nblk, R), 0)
    gate = jnp.where(blk < own, gate, NEG_INF)
    bias = jnp.where(blk == own, 0.0, -MASK_BIG)
    for _ in range(C_TOPK):
        mx = jnp.max(gate, axis=0, keepdims=True)
        idx = jnp.min(jnp.where(gate == mx, blk, nblk), axis=0, keepdims=True)
        hit = blk == idx
        bias = jnp.where(hit, jnp.where(mx > NEG_INF, 0.0, bias), bias)
        gate = jnp.where(hit, NEG_INF, gate)
    per = KEY_TILE // C_BLOCK
    rows = bias_scr.shape[0]
    r = _iota((rows, nblk), 0)
    spread = jnp.where(((r & (AUG_ROWS - 1)) < per) & (_iota((rows, nblk), 1) == (r >> 3) * per + (r & (AUG_ROWS - 1))), 1.0, 0.0)
    bias_scr[...] = _dot(spread.astype(BF16), bias.astype(BF16))
    oT = _masked_flash_t(n, qs, lambda kt: bias_scr[pl.ds(pl.multiple_of(kt * AUG_ROWS, AUG_ROWS), AUG_ROWS), :],
                         ka_ref, vTa_ref, ksq_scr, p_scr, acc_scr)
    o_ref[...] = _group_rows(oT).astype(o_ref.dtype)


def _moba_attention(proj, q_col, ka, vTa, km):
    B, Hkv, S, _ = ka.shape
    nq = S // Q_BLOCK
    R = GROUP * Q_BLOCK
    nblk = km.shape[2]
    nkt = S // KEY_TILE
    in_specs = [
        _q_spec(S, q_col),
        pl.BlockSpec((1, 1, S, LANES), lambda b, h, n: (b, h, 0, 0)),
        pl.BlockSpec((1, 1, V_AUG_ROWS, S), lambda b, h, n: (b, h, 0, 0)),
        pl.BlockSpec((1, 1, nblk, HEAD_DIM), lambda b, h, n: (b, h, 0, 0)),
    ]
    return pl.pallas_call(
        _moba_kernel, grid=(B, Hkv, nq), in_specs=in_specs,
        out_specs=_o_spec(S), out_shape=jax.ShapeDtypeStruct((B * S, Hkv * GROUP_COLS), BF16),
        scratch_shapes=[pltpu.VMEM((nkt * AUG_ROWS, R), F32)] + _flash_scratch(R),
        compiler_params=_params(3), name="moba_attention")(proj, ka, vTa, km)


def _compress_kernel(u_ref, us_ref, pe_ref, w1_ref, w2_ref, o_ref):
    a = (u_ref[0, 0].astype(F32) + pe_ref[0:1, :]).astype(BF16)
    b = (us_ref[0, 0].astype(F32) + pe_ref[1:2, :]).astype(BF16)
    pre = _dot(a, w1_ref[0]) + _dot(b, w1_ref[1])
    hid = jax.nn.gelu(pre)
    o_ref[0, 0] = _dot(hid.astype(BF16), w2_ref[...]).astype(o_ref.dtype)


def _compress(t, pe, w1, w2):
    B, H, S, _ = t.shape
    nrow = S // D_CMP_STRIDE
    width = D_CMP_STRIDE * HEAD_DIM
    u = t.reshape(B, H, nrow, width)
    us = jnp.concatenate([u[:, :, 1:], jnp.zeros((B, H, 1, width), u.dtype)], axis=2)
    blk = lambda b, h: (b, h, 0, 0)
    in_specs = [
        pl.BlockSpec((1, 1, nrow, width), blk), pl.BlockSpec((1, 1, nrow, width), blk),
        pl.BlockSpec((2, width), lambda b, h: (0, 0)),
        pl.BlockSpec((2, width, D_CMP_HIDDEN), lambda b, h: (0, 0, 0)),
        pl.BlockSpec((D_CMP_HIDDEN, HEAD_DIM), lambda b, h: (0, 0)),
    ]
    return pl.pallas_call(
        _compress_kernel, grid=(B, H), in_specs=in_specs,
        out_specs=pl.BlockSpec((1, 1, nrow, HEAD_DIM), blk),
        out_shape=jax.ShapeDtypeStruct((B, H, nrow, HEAD_DIM), BF16),
        compiler_params=_params(2), name="nsa_compress")(
            u, us, pe.astype(F32).reshape(2, width), w1.astype(BF16).reshape(2, width, D_CMP_HIDDEN), w2.astype(BF16))


def _cmp_select_kernel(nc, q_ref, kc_ref, vcT_ref, ovT_ref, oc_ref, bias_ref):
    n = pl.program_id(2)
    ncp_all = kc_ref.shape[2]
    nsel_all = ovT_ref.shape[0]
    last_visible = (n * Q_BLOCK + Q_BLOCK - D_CMP_LEN) // D_CMP_STRIDE
    quarter = last_visible // (ncp_all // CMP_PARTS)
    for part in range(CMP_PARTS):
        visible = ((part + 1) * ncp_all // CMP_PARTS, (part + 1) * nsel_all // CMP_PARTS)
        pl.when(quarter == part)(functools.partial(_cmp_select_body, nc, visible, q_ref, kc_ref, vcT_ref, ovT_ref, oc_ref, bias_ref))


def _cmp_select_body(nc, visible, q_ref, kc_ref, vcT_ref, ovT_ref, oc_ref, bias_ref):
    n = pl.program_id(2)
    ncp, nsel = visible
    nsel_all = ovT_ref.shape[0]
    qs = _group_qT(q_ref)
    R = qs.shape[1]
    s = _dot(kc_ref[0, 0, 0:ncp, :], qs)
    tpos = n * Q_BLOCK + (_iota((1, R), 1) & (Q_BLOCK - 1))
    c_last = jnp.minimum((tpos - (D_CMP_LEN - 1)) >> 4, nc - 1)
    s = jnp.where(_iota((ncp, R), 0) <= c_last, s, NEG_INF)
    m = jnp.max(s, axis=0, keepdims=True)
    m = jnp.where(m > NEG_INF, m, 0.0)
    e = jnp.exp(s - m)
    den = jnp.sum(e, axis=0, keepdims=True)
    p = (e * (1.0 / jnp.where(den > 0, den, 1.0))).astype(BF16)
    oc_ref[...] = _group_rows(_dot(vcT_ref[0, 0, :, 0:ncp], p))
    imp_heads = _dot(ovT_ref[0:nsel, 0:ncp], p)
    imp = imp_heads[:, 0:Q_BLOCK]
    for g in range(1, GROUP):
        imp = imp + imp_heads[:, g * Q_BLOCK:(g + 1) * Q_BLOCK]
    t = n * Q_BLOCK + _iota((nsel, Q_BLOCK), 1)
    j = _iota((nsel, Q_BLOCK), 0)
    cur = t >> 6
    imp = jnp.where(j <= cur, imp, NEG_INF)
    imp = jnp.where(_in_range(cur - j, 2), float("inf"), imp)
    imp = jnp.where(j == 0, float("inf"), imp)
    for _ in range(D_SEL_TOPK):
        mx = jnp.max(imp, axis=0, keepdims=True)
        idx = jnp.min(jnp.where(imp == mx, j, nsel), axis=0, keepdims=True)
        imp = jnp.where(j == idx, PICKED, imp)
    bias_ref[0, 0, 0, 0:nsel, :] = jnp.where(imp == PICKED, 0.0, -MASK_BIG)
    if nsel < nsel_all:
        bias_ref[0, 0, 0, nsel:, :] = jnp.full((nsel_all - nsel, Q_BLOCK), -MASK_BIG, F32)


def _cmp_select(proj, q_col, kc, vcT, overlapT, nc, S):
    B, Hkv, ncp, _ = kc.shape
    nq = S // Q_BLOCK
    nsel = overlapT.shape[0]
    in_specs = [
        _q_spec(S, q_col),
        pl.BlockSpec((1, 1, ncp, HEAD_DIM), lambda b, h, n: (b, h, 0, 0)),
        pl.BlockSpec((1, 1, HEAD_DIM, ncp), lambda b, h, n: (b, h, 0, 0)),
        pl.BlockSpec((nsel, ncp), lambda b, h, n: (0, 0)),
    ]
    out_specs = [_o_spec(S), pl.BlockSpec((1, 1, 1, nsel, Q_BLOCK), lambda b, h, n: (b, h, n, 0, 0))]
    out_shape = [jax.ShapeDtypeStruct((B * S, Hkv * GROUP_COLS), F32), jax.ShapeDtypeStruct((B, Hkv, nq, nsel, Q_BLOCK), F32)]
    return pl.pallas_call(
        functools.partial(_cmp_select_kernel, nc), grid=(B, Hkv, nq), in_specs=in_specs, out_specs=out_specs,
        out_shape=out_shape, compiler_params=_params(3), name="nsa_cmp_select")(proj, kc, vcT, overlapT)


def _sel_kernel(q_ref, ka_ref, vTa_ref, bias_ref, o_ref, ksq_scr, p_scr, acc_scr):
    n = pl.program_id(2)
    pl.when(n == 0)(lambda: _max_key_sqnorm(ka_ref, ksq_scr))
    qs = _group_qT(q_ref)

    def bias_rows(kt):
        b = bias_ref[0, 0, 0, pl.ds(pl.multiple_of(kt * AUG_ROWS, AUG_ROWS), AUG_ROWS), :]
        return jnp.concatenate([b] * GROUP, axis=1)

    o_ref[...] = _group_rows(_masked_flash_t(n, qs, bias_rows, ka_ref, vTa_ref, ksq_scr, p_scr, acc_scr))


def _sel_attention(proj, q_col, ka, vTa, biasT):
    B, Hkv, S, _ = ka.shape
    nq = S // Q_BLOCK
    nsel = biasT.shape[3]
    assert KEY_TILE // D_SEL_LEN == AUG_ROWS
    in_specs = [
        _q_spec(S, q_col),
        pl.BlockSpec((1, 1, S, LANES), lambda b, h, n: (b, h, 0, 0)),
        pl.BlockSpec((1, 1, V_AUG_ROWS, S), lambda b, h, n: (b, h, 0, 0)),
        pl.BlockSpec((1, 1, 1, nsel, Q_BLOCK), lambda b, h, n: (b, h, n, 0, 0)),
    ]
    return pl.pallas_call(
        _sel_kernel, grid=(B, Hkv, nq), in_specs=in_specs,
        out_specs=_o_spec(S), out_shape=jax.ShapeDtypeStruct((B * S, Hkv * GROUP_COLS), F32),
        scratch_shapes=_flash_scratch(GROUP * Q_BLOCK),
        compiler_params=_params(3), name="nsa_selected")(proj, ka, vTa, biasT)


def _win_kernel(span, q_ref, k_ref, vT_ref, o_ref):
    n = pl.program_id(2)
    qs = _group_qT(q_ref)
    R = qs.shape[1]
    start = pl.multiple_of(jnp.maximum(n * Q_BLOCK + Q_BLOCK - span, 0), Q_BLOCK)
    s = _dot(k_ref[0, 0, pl.ds(start, span), :], qs)
    tpos = n * Q_BLOCK + (_iota((1, R), 1) & (Q_BLOCK - 1))
    d = tpos - (start + _iota((span, R), 0))
    s = jnp.where(_in_range(d, jnp.full((1, R), D_WINDOW, jnp.int32)), s, NEG_INF)
    m = jnp.max(s, axis=0, keepdims=True)
    p = jnp.exp(s - m)
    l = jnp.sum(p, axis=0, keepdims=True)
    o_ref[...] = _group_rows(_dot(vT_ref[0, 0, :, pl.ds(start, span)], p.astype(BF16)) * (1.0 / l))


def _win_attention(proj, q_col, k, vT):
    B, Hkv, S, _ = k.shape
    nq = S // Q_BLOCK
    span = min(D_WINDOW + Q_BLOCK, S)
    in_specs = [
        _q_spec(S, q_col),
        pl.BlockSpec((1, 1, S, HEAD_DIM), lambda b, h, n: (b, h, 0, 0)),
        pl.BlockSpec((1, 1, HEAD_DIM, S), lambda b, h, n: (b, h, 0, 0)),
    ]
    return pl.pallas_call(
        functools.partial(_win_kernel, span), grid=(B, Hkv, nq), in_specs=in_specs,
        out_specs=_o_spec(S), out_shape=jax.ShapeDtypeStruct((B * S, Hkv * GROUP_COLS), F32),
        compiler_params=_params(3), name="nsa_window")(proj, k, vT)


def _route(logits):
    lt = logits.T
    tm = lt.shape[1]
    g_rows = 8
    grow = _iota((g_rows, tm), 0)
    gl = jnp.where(grow < N_GROUPS, lt[0:g_rows], NEG_INF)
    gmax = jnp.max(gl, axis=0, keepdims=True)
    gidx = jnp.min(jnp.where(gl == gmax, grow, N_GROUPS), axis=0, keepdims=True)
    g_prob = 1.0 / jnp.sum(jnp.exp(gl - gmax), axis=0, keepdims=True)
    e_rows = 72
    erow = _iota((e_rows, tm), 0)
    in_group = _in_range(erow - N_GROUPS - gidx * EXPERTS_PER_GROUP, EXPERTS_PER_GROUP)
    el = jnp.where(in_group, lt[0:e_rows], NEG_INF)
    ee = jnp.exp(el - jnp.max(el, axis=0, keepdims=True))
    ep = jnp.where(in_group, ee * (1.0 / jnp.sum(ee, axis=0, keepdims=True)), -1.0)
    p1 = jnp.max(ep, axis=0, keepdims=True)
    i1 = jnp.min(jnp.where(ep == p1, erow, e_rows), axis=0, keepdims=True)
    ep2 = jnp.where(erow == i1, -1.0, ep)
    p2 = jnp.max(ep2, axis=0, keepdims=True)
    i2 = jnp.min(jnp.where(ep2 == p2, erow, e_rows), axis=0, keepdims=True)
    den = p1 + p2
    vals = [(i1 - N_GROUPS).astype(F32), (i2 - N_GROUPS).astype(F32), g_prob * p1 / den, g_prob * p2 / den]
    orow = _iota((LANES, tm), 0)
    out = jnp.zeros((LANES, tm), F32)
    for k, val in enumerate(vals):
        out = jnp.where(orow == k, val, out)
    return out.T


def _pack_bf16_pairs(h):
    n = h.shape[1] // 2
    bits = lax.bitcast_convert_type(h.astype(BF16).astype(F32), jnp.uint32)
    return bits[:, :n] | (bits[:, n:] >> 16)


def _unpack_bf16_pairs(packed):
    hi = lax.bitcast_convert_type(packed & jnp.uint32(0xFFFF0000), F32)
    lo = lax.bitcast_convert_type(packed << 16, F32)
    return jnp.concatenate([hi, lo], axis=1).astype(BF16)


def _out_tail(x_new, gain_ref, wr_ref, xo_ref, h_ref, route_ref):
    xo_ref[...] = x_new
    ms = jnp.mean(x_new * x_new, axis=-1, keepdims=True)
    h = x_new * lax.rsqrt(ms + NORM_EPS) * gain_ref[...]
    h_ref[...] = _pack_bf16_pairs(h)
    h_hi = h.astype(BF16)
    h_lo = (h - h_hi.astype(F32)).astype(BF16)
    logits = _dot(h_hi, wr_ref[0]) + (_dot(h_lo, wr_ref[0]) + _dot(h_hi, wr_ref[1]))
    route_ref[...] = _route(logits)


def _out_proj_kernel(oa_ref, ob_ref, x_ref, w_ref, gain_ref, wr_ref, xo_ref, h_ref, route_ref):
    half = w_ref.shape[0] // 2
    x_new = x_ref[...] + _dot(oa_ref[...], w_ref[0:half, :]) + _dot(ob_ref[...], w_ref[half:, :])
    _out_tail(x_new, gain_ref, wr_ref, xo_ref, h_ref, route_ref)


def _out_proj_nsa_kernel(oc_ref, b0_ref, b1_ref, b2_ref, gd_ref, x_ref, w_ref, gain_ref, wr_ref, xo_ref, h_ref, route_ref):
    half = D_HEADS * HEAD_DIM
    g = jax.nn.sigmoid(gd_ref[...])
    g_hi = g.astype(BF16)
    g_lo = (g - g_hi.astype(F32)).astype(BF16)
    src = _iota((LANES, half), 0)
    head3 = (_iota((LANES, half), 1) >> 6) * 3
    od = None
    for br, b_ref in enumerate((b0_ref, b1_ref, b2_ref)):
        spread = jnp.where(src == head3 + br, 1.0, 0.0).astype(BF16)
        term = (_dot(g_hi, spread) + _dot(g_lo, spread)) * b_ref[...]
        od = term if od is None else od + term
    x_new = x_ref[...] + _dot(oc_ref[...], w_ref[0:half, :]) + _dot(od.astype(BF16), w_ref[half:, :])
    _out_tail(x_new, gain_ref, wr_ref, xo_ref, h_ref, route_ref)


def _out_proj(o_parts, gd, x, w_out, gain, w_router):
    T = x.shape[0]
    tm = ROW_TILE
    row = lambda i: (i, 0)
    fixed = lambda i: (0, 0)
    if gd is None:
        kern = _out_proj_kernel
        args = list(o_parts)
        in_specs = [pl.BlockSpec((tm, D_MODEL // 2), row)] * 2
    else:
        kern = _out_proj_nsa_kernel
        args = list(o_parts) + [gd]
        in_specs = [pl.BlockSpec((tm, D_MODEL // 2), row)] * 4 + [pl.BlockSpec((tm, LANES), row)]
    args += [x, w_out, gain.reshape(1, D_MODEL), w_router]
    in_specs += [pl.BlockSpec((tm, D_MODEL), row), pl.BlockSpec((D_MODEL, D_MODEL), fixed),
                 pl.BlockSpec((1, D_MODEL), fixed), pl.BlockSpec((2, D_MODEL, LANES), lambda i: (0, 0, 0))]
    out_shape = [jax.ShapeDtypeStruct((T, D_MODEL), F32), jax.ShapeDtypeStruct((T, D_MODEL // 2), jnp.uint32),
                 jax.ShapeDtypeStruct((T, LANES), F32)]
    out_specs = [pl.BlockSpec((tm, D_MODEL), row), pl.BlockSpec((tm, D_MODEL // 2), row), pl.BlockSpec((tm, LANES), row)]
    return pl.pallas_call(kern, grid=(T // tm,), in_specs=in_specs, out_specs=out_specs, out_shape=out_shape,
                          compiler_params=_params(1), name="out_proj_router")(*args)


def _expert_kernel(be_ref, rows_ref, wg_ref, wu_ref, wd_ref, y_ref, wg_s, wu_s, wd_s):
    i = pl.program_id(0)
    n_blk = pl.num_programs(0)

    @pl.when((i == 0) | (be_ref[i] != be_ref[jnp.maximum(i - 1, 0)]))
    def _():
        wg_s[...] = wg_ref[0].astype(BF16)
        wu_s[...] = wu_ref[0].astype(BF16)
        wd_s[...] = wd_ref[0].astype(BF16)

    @pl.when(i < be_ref[n_blk])
    def _():
        xb = _unpack_bf16_pairs(rows_ref[...])
        hid = jax.nn.silu(_dot(xb, wg_s[...])) * _dot(xb, wu_s[...])
        y_ref[...] = _dot(hid.astype(BF16), wd_s[...])

    @pl.when(i >= be_ref[n_blk])
    def _():
        y_ref[...] = jnp.zeros(y_ref.shape, F32)


def _expert_ffn(rows, blk_info, wg, wu, wd):
    n_rows = rows.shape[0]
    n_blk = n_rows // MOE_ROWS
    grid_spec = pltpu.PrefetchScalarGridSpec(
        num_scalar_prefetch=1, grid=(n_blk,),
        in_specs=[
            pl.BlockSpec((MOE_ROWS, D_MODEL // 2), lambda i, be: (i, 0)),
            pl.BlockSpec((1, D_MODEL, EXPERT_HIDDEN), lambda i, be: (be[i], 0, 0)),
            pl.BlockSpec((1, D_MODEL, EXPERT_HIDDEN), lambda i, be: (be[i], 0, 0)),
            pl.BlockSpec((1, EXPERT_HIDDEN, D_MODEL), lambda i, be: (be[i], 0, 0)),
        ],
        out_specs=pl.BlockSpec((MOE_ROWS, D_MODEL), lambda i, be: (i, 0)),
        scratch_shapes=[pltpu.VMEM((D_MODEL, EXPERT_HIDDEN), BF16), pltpu.VMEM((D_MODEL, EXPERT_HIDDEN), BF16),
                        pltpu.VMEM((EXPERT_HIDDEN, D_MODEL), BF16)])
    return pl.pallas_call(
        _expert_kernel, grid_spec=grid_spec, out_shape=jax.ShapeDtypeStruct((n_rows, D_MODEL), F32),
        compiler_params=_params(1), name="expert_ffn")(blk_info, rows, wg, wu, wd)


def _moe_dispatch(route, h):
    n_tok = h.shape[0]
    n_asg = n_tok * MOE_TOPK
    i32 = jnp.int32
    e_flat = route[:, 0:MOE_TOPK].astype(i32).reshape(n_asg)
    is_e = e_flat[:, None] == jnp.arange(N_EXPERTS, dtype=i32)[None, :]
    counts = jnp.sum(is_e, axis=0, dtype=i32)
    assert n_asg & (n_asg - 1) == 0 and n_asg <= 1 << 16
    idx = jnp.arange(n_asg, dtype=jnp.uint32)
    low = jnp.uint32(n_asg - 1)
    order = (jnp.sort(e_flat.astype(jnp.uint32) * n_asg + idx) & low).astype(i32)
    rank = (jnp.sort(order.astype(jnp.uint32) * n_asg + idx) & low).astype(i32)
    padded = (counts + MOE_ROWS - 1) // MOE_ROWS * MOE_ROWS
    pad_end = jnp.cumsum(padded)
    pad_start = pad_end - padded
    start = jnp.cumsum(counts) - counts
    n_rows = n_asg + N_EXPERTS * MOE_ROWS
    n_blk = n_rows // MOE_ROWS
    blk_start = jnp.arange(n_blk, dtype=i32) * MOE_ROWS
    blk_expert = jnp.minimum(jnp.sum(pad_end[None, :] <= blk_start[:, None], axis=1, dtype=i32), N_EXPERTS - 1)
    within = (blk_start - pad_start[blk_expert])[:, None] + jnp.arange(MOE_ROWS, dtype=i32)[None, :]
    valid = within < counts[blk_expert][:, None]
    src = jnp.clip(start[blk_expert][:, None] + within, 0, n_asg - 1)
    row_tok = jnp.where(valid, (order // MOE_TOPK)[src], src // MOE_TOPK).reshape(n_rows)
    shift = jnp.sum(jnp.where(is_e, (pad_start - start)[None, :], 0), axis=1, dtype=i32)
    pos = (rank + shift).reshape(n_tok, MOE_TOPK)
    blk_info = jnp.concatenate([blk_expert, (pad_end[-1:] // MOE_ROWS).astype(i32)])
    return h[row_tok], blk_info, pos


def _moe(route, h, wg, wu, wd):
    rows, blk_info, pos = _moe_dispatch(route, h)
    y = _expert_ffn(rows, blk_info, wg, wu, wd)
    return y[pos[:, 0]], y[pos[:, 1]], route


def _final_kernel(x_ref, y0_ref, y1_ref, route_ref, g_ref, o_ref):
    x = _moe_combine(x_ref[...], y0_ref, y1_ref, route_ref)
    ms = jnp.mean(x * x, axis=-1, keepdims=True)
    o_ref[...] = x * lax.rsqrt(ms + NORM_EPS) * g_ref[...]


def _final_norm(x, ys, gain):
    T = x.shape[0]
    tm = ROW_TILE
    row = lambda i: (i, 0)
    return pl.pallas_call(
        _final_kernel, grid=(T // tm,),
        in_specs=[pl.BlockSpec((tm, D_MODEL), row)] * 3 + [pl.BlockSpec((tm, LANES), row), pl.BlockSpec((1, D_MODEL), lambda i: (0, 0))],
        out_specs=pl.BlockSpec((tm, D_MODEL), row), out_shape=jax.ShapeDtypeStruct((T, D_MODEL), F32),
        compiler_params=_params(1), name="final_norm")(x, *ys, gain.reshape(1, D_MODEL))


def _rope_tables(positions):
    inv_freq = ROPE_THETA ** (-jnp.arange(0, HEAD_DIM, 2, dtype=F32) / HEAD_DIM)
    ang = positions.astype(F32).reshape(-1, 1) * inv_freq
    cos, sin = jnp.cos(ang), jnp.sin(ang)
    reps = LANES // HEAD_DIM
    return jnp.tile(jnp.concatenate([cos, cos], axis=1), (1, reps)), jnp.tile(jnp.concatenate([-sin, sin], axis=1), (1, reps))


def _chunks(first, count, rope, action):
    return tuple((first + j, rope, action(j)) for j in range(count))


def _plan(*runs):
    entries = sum(runs, ())
    assert [e[0] for e in entries] == list(range(len(entries)))
    return tuple(e[1:] for e in entries)


AB_OUTS = (("tok", 4), ("heads", A_KV_HEADS), ("headsT", A_KV_HEADS), ("heads", B_HEADS), ("headsT", B_HEADS), ("headsT", B_HEADS))
AB_PLAN = _plan(_chunks(0, 4, True, lambda j: (("tok", 0, j),)),
                _chunks(4, 1, True, lambda j: (("heads", 1, 0),)),
                _chunks(5, 1, False, lambda j: (("headsT", 2, 0),)),
                _chunks(6, 4, False, lambda j: (("heads", 3, 2 * j),)),
                _chunks(10, 4, False, lambda j: (("headsT", 4, 2 * j),)),
                _chunks(14, 4, False, lambda j: (("headsT", 5, 2 * j),)))
CD_OUTS = (("tok", 8), ("keys_aug", C_KV_HEADS), ("kmean", 0), ("valsT_aug", C_KV_HEADS), ("heads", D_KV_HEADS),
           ("heads", D_KV_HEADS), ("keys_aug", D_KV_HEADS), ("valsT_aug", D_KV_HEADS), ("heads", D_KV_HEADS),
           ("headsT", D_KV_HEADS), ("f32", 0))
CD_PLAN = _plan(_chunks(0, 4, True, lambda j: (("tok", 0, j),)),
                _chunks(4, 1, True, lambda j: (("keys_aug", 1, C_BLOCK), ("kmean", 2, 0))),
                _chunks(5, 1, False, lambda j: (("valsT_aug", 3, 0),)),
                _chunks(6, 4, True, lambda j: (("tok", 0, 4 + j),)),
                _chunks(10, 1, True, lambda j: (("heads", 4, 0),)),
                _chunks(11, 1, False, lambda j: (("heads", 5, 0),)),
                _chunks(12, 1, True, lambda j: (("keys_aug", 6, D_SEL_LEN),)),
                _chunks(13, 1, False, lambda j: (("valsT_aug", 7, 0),)),
                _chunks(14, 1, True, lambda j: (("heads", 8, 0),)),
                _chunks(15, 1, False, lambda j: (("headsT", 9, 0),)),
                _chunks(16, 1, False, lambda j: (("f32", 10, 0),)))
QD_COL = 4 * LANES


def _router_weights(router_group, router_expert):
    pad = jnp.zeros((D_MODEL, LANES - N_GROUPS - N_EXPERTS), F32)
    w = jnp.concatenate([router_group.astype(F32), router_expert.astype(F32), pad], axis=1)
    hi = w.astype(BF16)
    return jnp.stack([hi, (w - hi.astype(F32)).astype(BF16)])


def _pad_cols(w, n):
    return jnp.concatenate([w, jnp.zeros((w.shape[0], n - w.shape[1]), w.dtype)], axis=1)


def _mixer_ab(parts, sinks):
    qa, ka, vaT, qb, kbT, vbT = parts
    oa = _swa_attention(qa, 0, ka, vaT, sinks)
    ob = _stick_attention(qb, kbT, vbT)
    return oa, ob


def _mixer_cd(parts, B, S, k_pe, k_w1, k_w2, v_pe, v_w1, v_w2):
    q_cd, kc_aug, kmean, vcT_aug, kdc, vdc, kds_aug, vdsT_aug, kdw, vdwT, _ = parts
    nblk = S // C_BLOCK
    km = kmean.reshape(B, nblk, C_KV_HEADS, HEAD_DIM).transpose(0, 2, 1, 3).astype(BF16)
    oc = _moba_attention(q_cd, 0, kc_aug, vcT_aug, km)
    k_cmp = _compress(kdc, k_pe, k_w1, k_w2)
    v_cmp = _compress(vdc, v_pe, v_w1, v_w2)
    nc = (S - D_CMP_LEN) // D_CMP_STRIDE + 1
    ncp = S // D_CMP_STRIDE
    nsel = S // D_SEL_LEN
    c_start = jnp.arange(ncp) * D_CMP_STRIDE
    b_start = jnp.arange(nsel) * D_SEL_LEN
    overlap = ((c_start[:, None] <= b_start[None, :] + D_SEL_LEN - 1) & (c_start[:, None] + D_CMP_LEN - 1 >= b_start[None, :])
               & (jnp.arange(ncp)[:, None] < nc)).astype(BF16)
    o_cmp, biasT = _cmp_select(q_cd, QD_COL, k_cmp, v_cmp.transpose(0, 1, 3, 2), overlap.T, nc, S)
    o_sel = _sel_attention(q_cd, QD_COL, kds_aug, vdsT_aug, biasT)
    o_win = _win_attention(q_cd, QD_COL, kdw, vdwT)
    return oc, o_cmp, o_sel, o_win


def kernel(x, positions, ln_mix_0, w_in_0, sinks_0, w_out_0, ln_ffn_0, router_group_0, router_expert_0, expert_gate_0, expert_up_0, expert_down_0, ln_mix_1, w_in_1, cmp_k_pe_1, cmp_k_w1_1, cmp_k_w2_1, cmp_v_pe_1, cmp_v_w1_1, cmp_v_w2_1, w_out_1, ln_ffn_1, router_group_1, router_expert_1, expert_gate_1, expert_up_1, expert_down_1, ln_final):
    B, S, _ = x.shape
    T = B * S
    assert S % KEY_TILE == 0 and T % ROW_TILE == 0
    cos_t, sin_t = _rope_tables(positions)
    xf = x.reshape(T, D_MODEL)

    parts = _in_proj(xf, None, ln_mix_0, w_in_0.astype(BF16), cos_t, sin_t, AB_PLAN, AB_OUTS, B, S)
    o_ab = _mixer_ab(parts, sinks_0)
    x1, h1, route1 = _out_proj(o_ab, None, xf, w_out_0.astype(BF16), ln_ffn_0, _router_weights(router_group_0, router_expert_0))
    ys = _moe(route1, h1, expert_gate_0, expert_up_0, expert_down_0)

    x2, *parts = _in_proj(x1, ys, ln_mix_1, _pad_cols(w_in_1.astype(BF16), len(CD_PLAN) * LANES), cos_t, sin_t,
                          CD_PLAN, CD_OUTS, B, S)
    gd = parts[-1]
    parts = _mixer_cd(parts, B, S, cmp_k_pe_1, cmp_k_w1_1, cmp_k_w2_1, cmp_v_pe_1, cmp_v_w1_1, cmp_v_w2_1)
    x3, h3, route3 = _out_proj(parts, gd, x2, w_out_1.astype(BF16), ln_ffn_1, _router_weights(router_group_1, router_expert_1))
    ys = _moe(route3, h3, expert_gate_1, expert_up_1, expert_down_1)
    return _final_norm(x3, ys, ln_final).reshape(B, S, D_MODEL)
```

```python
import functools

import jax
import jax.numpy as jnp
from jax import lax
from jax.experimental import pallas as pl
from jax.experimental.pallas import tpu as pltpu

D_MODEL = 1024
HEAD_DIM = 64
HALF = HEAD_DIM // 2
ROPE_THETA = 10000.0
NORM_EPS = 1e-6
Q_BLOCK = 128
SCALE = HEAD_DIM ** -0.5

A_HEADS, A_KV_HEADS, A_WINDOW = 8, 2, 128
B_HEADS = 8
C_HEADS, C_KV_HEADS, C_BLOCK, C_TOPK = 8, 2, 256, 3
D_HEADS, D_KV_HEADS = 8, 2
D_CMP_LEN, D_CMP_STRIDE, D_CMP_HIDDEN = 32, 16, 256
D_SEL_LEN, D_SEL_TOPK, D_WINDOW = 64, 16, 512
N_GROUPS, EXPERTS_PER_GROUP, MOE_TOPK, EXPERT_HIDDEN = 4, 16, 2, 512
N_EXPERTS = N_GROUPS * EXPERTS_PER_GROUP
GROUP = 4

LANES = 128
MXU_COLS = 256
ROW_TILE = 512
MOE_ROWS = 256
KEY_TILE = 512
FLASH_MIN_SUM = 1e-25
REF_MARGIN = 1.05
MASKED = -1e30
V_AUG_ROWS = 80
AUG_ROWS = 8
MASK_BIG = 2.0 ** 100
CMP_PARTS = 8
PICKED = -1.0
SB_EXIT = -104.0
STICK_HEADS = 8
VMEM_LIMIT = 56 * 1024 * 1024

F32 = jnp.float32
BF16 = jnp.bfloat16
NEG_INF = float("-inf")


def _iota(shape, dim):
    return lax.broadcasted_iota(jnp.int32, shape, dim)


def _dot(a, b):
    return jnp.dot(a, b, preferred_element_type=F32)


def _in_range(d, width):
    return lax.bitcast_convert_type(d, jnp.uint32) < lax.bitcast_convert_type(jnp.asarray(width, jnp.int32), jnp.uint32)


def _params(n_grid):
    return pltpu.CompilerParams(dimension_semantics=("arbitrary",) * n_grid, vmem_limit_bytes=VMEM_LIMIT)


GROUP_COLS = GROUP * HEAD_DIM


def _group_qT(q_ref):
    t = (q_ref[...].astype(F32) * SCALE).T
    return jnp.concatenate([t[g * HEAD_DIM:(g + 1) * HEAD_DIM] for g in range(GROUP)], axis=1).astype(BF16)


def _group_rows(oT):
    return jnp.concatenate([oT[:, g * Q_BLOCK:(g + 1) * Q_BLOCK] for g in range(GROUP)], axis=0).T


def _q_spec(S, col0):
    nq = S // Q_BLOCK
    return pl.BlockSpec((Q_BLOCK, GROUP_COLS), lambda b, h, n: (b * nq + n, col0 // GROUP_COLS + h))


def _o_spec(S):
    nq = S // Q_BLOCK
    return pl.BlockSpec((Q_BLOCK, GROUP_COLS), lambda b, h, n: (b * nq + n, h))


def _moe_combine(x, y0_ref, y1_ref, route_ref):
    return x + (y0_ref[...] * route_ref[:, MOE_TOPK:MOE_TOPK + 1] + y1_ref[...] * route_ref[:, MOE_TOPK + 1:MOE_TOPK + 2])


def _emit_chunk(ch, actions, outs):
    tm = ch.shape[0]
    chT = None
    for kind, oi, arg in actions:
        o = outs[oi]
        if kind == "tok":
            o[:, arg * LANES:(arg + 1) * LANES] = ch.astype(BF16)
        elif kind == "f32":
            o[...] = ch
        elif kind == "kmean":
            o[...] = jnp.mean(ch.reshape(tm // C_BLOCK, C_BLOCK, LANES), axis=1).reshape(tm // C_BLOCK, 1, LANES)
        elif kind == "heads":
            for hh in range(2):
                o[0, arg + hh] = ch[:, hh * HEAD_DIM:(hh + 1) * HEAD_DIM].astype(BF16)
        elif kind == "keys_aug":
            lane = _iota((tm, LANES), 1)
            block_in_tile = _iota((tm, LANES), 0) >> (arg.bit_length() - 1)
            aug = jnp.where((lane - HEAD_DIM == block_in_tile) | (lane == HEAD_DIM + AUG_ROWS), 1.0, 0.0)
            for hh in range(2):
                keys = ch if hh == 0 else pltpu.roll(ch, HEAD_DIM, 1)
                o[0, hh] = jnp.where(lane < HEAD_DIM, keys, aug).astype(BF16)
        else:
            chT = ch.astype(BF16).astype(F32).T if chT is None else chT
            for hh in range(2):
                vT = chT[hh * HEAD_DIM:(hh + 1) * HEAD_DIM]
                if kind == "headsT":
                    o[0, arg + hh] = vT.astype(BF16)
                else:
                    assert kind == "valsT_aug"
                    tail = jnp.where(_iota((V_AUG_ROWS - HEAD_DIM, tm), 0) == 0, 1.0, 0.0)
                    o[0, hh] = jnp.concatenate([vT, tail], axis=0).astype(BF16)


def _in_proj_kernel(plan, combine, *refs):
    refs = list(refs)
    x_ref = refs.pop(0)
    if combine:
        y0_ref, y1_ref, route_ref = refs.pop(0), refs.pop(0), refs.pop(0)
    g_ref, w_ref, cos_ref, sin_ref = refs[:4]
    outs = refs[4:]
    x = x_ref[...]
    if combine:
        x = _moe_combine(x, y0_ref, y1_ref, route_ref)
        xo_ref = outs.pop(0)
        xo_ref[...] = x
    ms = jnp.mean(x * x, axis=-1, keepdims=True)
    h = (x * lax.rsqrt(ms + NORM_EPS) * g_ref[...]).astype(BF16)
    tm = x.shape[0]
    cos = cos_ref[...]
    sin = sin_ref[...]
    first_half = (_iota((tm, LANES), 1) & (HEAD_DIM - 1)) < HALF
    for c0 in range(0, len(plan), MXU_COLS // LANES):
        group = plan[c0:c0 + MXU_COLS // LANES]
        wide = _dot(h, w_ref[:, c0 * LANES:(c0 + len(group)) * LANES])
        for k, (rope, actions) in enumerate(group):
            ch = wide[:, k * LANES:(k + 1) * LANES]
            if rope:
                partner = jnp.where(first_half, pltpu.roll(ch, LANES - HALF, 1), pltpu.roll(ch, HALF, 1))
                ch = ch * cos + partner * sin
            _emit_chunk(ch, actions, outs)


def _in_proj(x, ys, gain, w, cos_t, sin_t, plan, out_kinds, B, S):
    T = x.shape[0]
    n_cols = w.shape[1]
    tm = ROW_TILE
    assert tm == KEY_TILE and S % tm == 0 and len(plan) * LANES == n_cols
    tpb = S // tm
    combine = ys is not None
    row = lambda i: (i, 0)
    fixed = lambda i: (0, 0)
    by_seq = lambda i: (i // tpb, 0, i % tpb, 0)
    by_seq_t = lambda i: (i // tpb, 0, 0, i % tpb)
    in_specs = [pl.BlockSpec((tm, D_MODEL), row)]
    args = [x]
    if combine:
        in_specs += [pl.BlockSpec((tm, D_MODEL), row)] * 2 + [pl.BlockSpec((tm, LANES), row)]
        args += list(ys)
    in_specs += [pl.BlockSpec((1, D_MODEL), fixed), pl.BlockSpec((D_MODEL, n_cols), fixed),
                 pl.BlockSpec((tm, LANES), row), pl.BlockSpec((tm, LANES), row)]
    args += [gain.reshape(1, D_MODEL), w, cos_t, sin_t]
    out_shape, out_specs = [], []
    if combine:
        out_shape.append(jax.ShapeDtypeStruct((T, D_MODEL), F32))
        out_specs.append(pl.BlockSpec((tm, D_MODEL), row))
    for kind, size in out_kinds:
        if kind == "tok":
            shape, dtype, spec = (T, size * LANES), BF16, pl.BlockSpec((tm, size * LANES), row)
        elif kind == "f32":
            shape, dtype, spec = (T, LANES), F32, pl.BlockSpec((tm, LANES), row)
        elif kind == "kmean":
            shape, dtype, spec = (T // C_BLOCK, 1, LANES), F32, pl.BlockSpec((tm // C_BLOCK, 1, LANES), lambda i: (i, 0, 0))
        elif kind == "heads":
            shape, dtype, spec = (B, size, S, HEAD_DIM), BF16, pl.BlockSpec((1, size, tm, HEAD_DIM), by_seq)
        elif kind == "keys_aug":
            shape, dtype, spec = (B, size, S, LANES), BF16, pl.BlockSpec((1, size, tm, LANES), by_seq)
        elif kind == "headsT":
            shape, dtype, spec = (B, size, HEAD_DIM, S), BF16, pl.BlockSpec((1, size, HEAD_DIM, tm), by_seq_t)
        else:
            assert kind == "valsT_aug"
            shape, dtype, spec = (B, size, V_AUG_ROWS, S), BF16, pl.BlockSpec((1, size, V_AUG_ROWS, tm), by_seq_t)
        out_shape.append(jax.ShapeDtypeStruct(shape, dtype))
        out_specs.append(spec)
    kern = functools.partial(_in_proj_kernel, plan, combine)
    return pl.pallas_call(kern, grid=(T // tm,), in_specs=in_specs, out_specs=out_specs, out_shape=out_shape,
                          compiler_params=_params(1), name="in_proj")(*args)


def _swa_kernel(q_ref, kp_ref, ko_ref, vp_ref, vo_ref, sink_ref, o_ref):
    n = pl.program_id(2)
    qs = _group_qT(q_ref)
    R = qs.shape[1]
    k = jnp.concatenate([kp_ref[0, 0], ko_ref[0, 0]], axis=0)
    vT = jnp.concatenate([vp_ref[0, 0], vo_ref[0, 0]], axis=1)
    s = _dot(k, qs)
    qpos = _iota((1, R), 1) & (Q_BLOCK - 1)
    d = qpos + Q_BLOCK - _iota((2 * Q_BLOCK, R), 0)
    width = jnp.minimum(A_WINDOW, qpos + 1 + jnp.where(n > 0, Q_BLOCK, 0))
    s = jnp.where(_in_range(d, width), s, NEG_INF)
    sink = sink_ref[0]
    m = jnp.maximum(jnp.max(s, axis=0, keepdims=True), sink)
    p = jnp.exp(s - m)
    den = jnp.sum(p, axis=0, keepdims=True) + jnp.exp(sink - m)
    o_ref[...] = _group_rows(_dot(vT, p.astype(BF16)) * (1.0 / den)).astype(o_ref.dtype)


def _swa_attention(proj, q_col, k, vT, sinks):
    B, Hkv, S, _ = k.shape
    nb = S // Q_BLOCK
    R = GROUP * Q_BLOCK
    sink_row = jnp.repeat(sinks.astype(F32).reshape(Hkv, GROUP), Q_BLOCK, axis=1).reshape(Hkv, 1, R)
    prev = lambda n: jnp.maximum(n - 1, 0)
    in_specs = [
        _q_spec(S, q_col),
        pl.BlockSpec((1, 1, Q_BLOCK, HEAD_DIM), lambda b, h, n: (b, h, prev(n), 0)),
        pl.BlockSpec((1, 1, Q_BLOCK, HEAD_DIM), lambda b, h, n: (b, h, n, 0)),
        pl.BlockSpec((1, 1, HEAD_DIM, Q_BLOCK), lambda b, h, n: (b, h, 0, prev(n))),
        pl.BlockSpec((1, 1, HEAD_DIM, Q_BLOCK), lambda b, h, n: (b, h, 0, n)),
        pl.BlockSpec((1, 1, R), lambda b, h, n: (h, 0, 0)),
    ]
    return pl.pallas_call(
        _swa_kernel, grid=(B, Hkv, nb), in_specs=in_specs,
        out_specs=_o_spec(S), out_shape=jax.ShapeDtypeStruct((B * S, Hkv * GROUP_COLS), BF16),
        compiler_params=_params(3), name="swa_attention")(proj, k, k, vT, vT, sink_row)


def _stick_kernel(q_ref, kT_ref, vT_ref, o_ref):
    n = pl.program_id(2)
    tq = tk = Q_BLOCK
    heads = q_ref.shape[1]
    row = _iota((tq, tk), 0)
    col = _iota((tk, tk), 1)
    upper = jnp.where(_iota((tk, tk), 0) > col, 1.0, 0.0).astype(BF16)
    tpos = n * tq + row

    def body(carry):
        kb, _, cs, accs = carry
        start = pl.multiple_of(kb * tk, tk)
        past = (start + col) < tpos
        hs = range(heads)
        zs = [_dot(q_ref[0, h], kT_ref[0, h, :, pl.ds(start, tk)]) * SCALE for h in hs]
        sps = [jnp.maximum(z, 0.0) + jnp.log(1.0 + jnp.exp(-jnp.abs(z))) for z in zs]
        stays = [jnp.where(past, -sp, 0.0) for sp in sps]
        his = [st.astype(BF16) for st in stays]
        los = [(st - hi.astype(F32)).astype(BF16) for st, hi in zip(stays, his)]
        betweens = [_dot(hi, upper) + _dot(lo, upper) for hi, lo in zip(his, los)]
        ws = [jnp.where(past, jnp.exp(zs[h] - sps[h] + betweens[h] + cs[h]), 0.0).astype(BF16) for h in hs]
        new_accs = [accs[h] + lax.dot_general(ws[h], vT_ref[0, h, :, pl.ds(start, tk)], (((1,), (1,)), ((), ())),
                                              preferred_element_type=F32) for h in hs]
        new_cs = [cs[h] + jnp.sum(stays[h], axis=-1, keepdims=True) for h in hs]
        worst = functools.reduce(jnp.maximum, new_cs)
        return kb - 1, jnp.max(worst) > SB_EXIT, tuple(new_cs), tuple(new_accs)

    def cond(carry):
        kb, alive, _, _ = carry
        return (kb >= 0) & alive

    init = (n, jnp.array(True), (jnp.zeros((tq, 1), F32),) * heads, (jnp.zeros((tq, HEAD_DIM), F32),) * heads)
    _, _, _, accs = lax.while_loop(cond, body, init)
    o_ref[...] = jnp.concatenate(accs, axis=1).astype(o_ref.dtype)


def _stick_attention(q, kT, v):
    B, H, S, _ = q.shape
    nq = S // Q_BLOCK
    hb = STICK_HEADS
    resident = dict(pipeline_mode=pl.Buffered(1))
    in_specs = [
        pl.BlockSpec((1, hb, Q_BLOCK, HEAD_DIM), lambda b, h, n: (b, h, n, 0)),
        pl.BlockSpec((1, hb, HEAD_DIM, S), lambda b, h, n: (b, h, 0, 0), **resident),
        pl.BlockSpec((1, hb, HEAD_DIM, S), lambda b, h, n: (b, h, 0, 0), **resident),
    ]
    return pl.pallas_call(
        _stick_kernel, grid=(B, H // hb, nq), in_specs=in_specs,
        out_specs=pl.BlockSpec((Q_BLOCK, hb * HEAD_DIM), lambda b, h, n: (b * nq + n, h)),
        out_shape=jax.ShapeDtypeStruct((B * S, H * HEAD_DIM), BF16),
        compiler_params=_params(3), name="stick_attention")(q, kT, v)


def _max_key_sqnorm(ka_ref, out_ref):
    S = ka_ref.shape[2]
    tk = KEY_TILE
    is_key_lane = _iota((tk, LANES), 1) < HEAD_DIM
    ones = jnp.ones((LANES, LANES), BF16)

    def body(i, mx):
        k = jnp.where(is_key_lane, ka_ref[0, 0, pl.ds(pl.multiple_of(i * tk, tk), tk), :].astype(F32), 0.0)
        sq = _dot((k * k).astype(BF16), ones)
        return jnp.maximum(mx, jnp.max(sq, axis=0, keepdims=True))

    mx = lax.fori_loop(0, S // tk, body, jnp.zeros((1, LANES), F32))
    out_ref[...] = jnp.broadcast_to(mx, out_ref.shape)


def _flash_scratch(R):
    return [pltpu.VMEM((AUG_ROWS, LANES), F32), pltpu.VMEM((2, KEY_TILE, R), BF16), pltpu.VMEM((V_AUG_ROWS, R), F32)]


def _masked_flash_t(n, qT, bias_rows, ka_ref, vTa_ref, ksq_ref, p_scr, acc_scr):
    R = qT.shape[1]
    tk = KEY_TILE
    diag = (n * Q_BLOCK) // tk
    zpad = jnp.zeros((LANES - HEAD_DIM - 2 * AUG_ROWS, R), F32)
    causal = diag * tk + _iota((tk, R), 0) <= n * Q_BLOCK + (_iota((tk, R), 1) & (Q_BLOCK - 1))

    def scores(kt, ref_rows):
        st = pl.multiple_of(kt * tk, tk)
        low = jnp.concatenate([bias_rows(kt), ref_rows, zpad], axis=0).astype(BF16)
        return _dot(ka_ref[0, 0, pl.ds(st, tk), :], jnp.concatenate([qT, low], axis=0))

    def values(kt):
        return vTa_ref[0, 0, :, pl.ds(pl.multiple_of(kt * tk, tk), tk)]

    qsq = jnp.sum(jnp.square(qT.astype(F32)), axis=0, keepdims=True)
    ksq = jnp.concatenate([ksq_ref[0:1, :]] * (R // LANES), axis=1)
    ref = jnp.sqrt(qsq * ksq) * REF_MARGIN
    ref_rows = jnp.where(_iota((AUG_ROWS, R), 0) == 0, -ref, 0.0)

    def fast_body(kt, carry):
        slot = kt & 1
        s = scores(kt, ref_rows)
        acc_scr[...] += _dot(values(jnp.maximum(kt - 1, 0)), p_scr[1 - slot])
        p_scr[slot] = jnp.exp(s).astype(BF16)
        return carry

    p_scr[1] = jnp.zeros((tk, R), BF16)
    acc_scr[...] = jnp.zeros(acc_scr.shape, F32)
    lax.fori_loop(0, diag, fast_body, 0)
    acc = acc_scr[...] + _dot(values(jnp.maximum(diag - 1, 0)), p_scr[1 - (diag & 1)])
    s = jnp.where(causal, scores(diag, ref_rows), MASKED)
    acc = acc + _dot(values(diag), jnp.exp(s).astype(BF16))

    def running_max_path(_):
        no_ref = jnp.zeros((AUG_ROWS, R), F32)

        def update(kt, s, m, acc):
            m_new = jnp.maximum(m, jnp.max(s, axis=0, keepdims=True))
            p = jnp.exp(s - m_new).astype(BF16)
            return m_new, jnp.exp(m - m_new) * acc + _dot(values(kt), p)

        init = (jnp.full((1, R), MASKED, F32), jnp.zeros((vTa_ref.shape[2], R), F32))
        m, acc = lax.fori_loop(0, diag, lambda kt, c: update(kt, scores(kt, no_ref), *c), init)
        return update(diag, jnp.where(causal, scores(diag, no_ref), MASKED), m, acc)[1]

    healthy = jnp.min(acc[HEAD_DIM:HEAD_DIM + 1, :]) >= FLASH_MIN_SUM
    acc = lax.cond(healthy, lambda _: acc, running_max_path, 0)
    return acc[0:HEAD_DIM] * (1.0 / acc[HEAD_DIM:HEAD_DIM + 1])


def _moba_kernel(q_ref, ka_ref, vTa_ref, km_ref, o_ref, bias_scr, ksq_scr, p_scr, acc_scr):
    n = pl.program_id(2)
    pl.when(n == 0)(lambda: _max_key_sqnorm(ka_ref, ksq_scr))
    nblk = km_ref.shape[2]
    qs = _group_qT(q_ref)
    R = qs.shape[1]
    own = (n * Q_BLOCK) // C_BLOCK
    gate = _dot(km_ref[0, 0], qs)
    blk = _iota((nblk, R), 0)
    gate = jnp.where(blk < own, gate, NEG_INF)
    bias = jnp.where(blk == own, 0.0, -MASK_BIG)
    for _ in range(C_TOPK):
        mx = jnp.max(gate, axis=0, keepdims=True)
        idx = jnp.min(jnp.where(gate == mx, blk, nblk), axis=0, keepdims=True)
        hit = blk == idx
        bias = jnp.where(hit, jnp.where(mx > NEG_INF, 0.0, bias), bias)
        gate = jnp.where(hit, NEG_INF, gate)
    per = KEY_TILE // C_BLOCK
    rows = bias_scr.shape[0]
    r = _iota((rows, nblk), 0)
    spread = jnp.where(((r & (AUG_ROWS - 1)) < per) & (_iota((rows, nblk), 1) == (r >> 3) * per + (r & (AUG_ROWS - 1))), 1.0, 0.0)
    bias_scr[...] = _dot(spread.astype(BF16), bias.astype(BF16))
    oT = _masked_flash_t(n, qs, lambda kt: bias_scr[pl.ds(pl.multiple_of(kt * AUG_ROWS, AUG_ROWS), AUG_ROWS), :],
                         ka_ref, vTa_ref, ksq_scr, p_scr, acc_scr)
    o_ref[...] = _group_rows(oT).astype(o_ref.dtype)


def _moba_attention(proj, q_col, ka, vTa, km):
    B, Hkv, S, _ = ka.shape
    nq = S // Q_BLOCK
    R = GROUP * Q_BLOCK
    nblk = km.shape[2]
    nkt = S // KEY_TILE
    in_specs = [
        _q_spec(S, q_col),
        pl.BlockSpec((1, 1, S, LANES), lambda b, h, n: (b, h, 0, 0)),
        pl.BlockSpec((1, 1, V_AUG_ROWS, S), lambda b, h, n: (b, h, 0, 0)),
        pl.BlockSpec((1, 1, nblk, HEAD_DIM), lambda b, h, n: (b, h, 0, 0)),
    ]
    return pl.pallas_call(
        _moba_kernel, grid=(B, Hkv, nq), in_specs=in_specs,
        out_specs=_o_spec(S), out_shape=jax.ShapeDtypeStruct((B * S, Hkv * GROUP_COLS), BF16),
        scratch_shapes=[pltpu.VMEM((nkt * AUG_ROWS, R), F32)] + _flash_scratch(R),
        compiler_params=_params(3), name="moba_attention")(proj, ka, vTa, km)


def _compress_kernel(u_ref, us_ref, pe_ref, w1_ref, w2_ref, o_ref):
    a = (u_ref[0, 0].astype(F32) + pe_ref[0:1, :]).astype(BF16)
    b = (us_ref[0, 0].astype(F32) + pe_ref[1:2, :]).astype(BF16)
    pre = _dot(a, w1_ref[0]) + _dot(b, w1_ref[1])
    hid = jax.nn.gelu(pre)
    o_ref[0, 0] = _dot(hid.astype(BF16), w2_ref[...]).astype(o_ref.dtype)


def _compress(t, pe, w1, w2):
    B, H, S, _ = t.shape
    nrow = S // D_CMP_STRIDE
    width = D_CMP_STRIDE * HEAD_DIM
    u = t.reshape(B, H, nrow, width)
    us = jnp.concatenate([u[:, :, 1:], jnp.zeros((B, H, 1, width), u.dtype)], axis=2)
    blk = lambda b, h: (b, h, 0, 0)
    in_specs = [
        pl.BlockSpec((1, 1, nrow, width), blk), pl.BlockSpec((1, 1, nrow, width), blk),
        pl.BlockSpec((2, width), lambda b, h: (0, 0)),
        pl.BlockSpec((2, width, D_CMP_HIDDEN), lambda b, h: (0, 0, 0)),
        pl.BlockSpec((D_CMP_HIDDEN, HEAD_DIM), lambda b, h: (0, 0)),
    ]
    return pl.pallas_call(
        _compress_kernel, grid=(B, H), in_specs=in_specs,
        out_specs=pl.BlockSpec((1, 1, nrow, HEAD_DIM), blk),
        out_shape=jax.ShapeDtypeStruct((B, H, nrow, HEAD_DIM), BF16),
        compiler_params=_params(2), name="nsa_compress")(
            u, us, pe.astype(F32).reshape(2, width), w1.astype(BF16).reshape(2, width, D_CMP_HIDDEN), w2.astype(BF16))


def _cmp_select_kernel(nc, q_ref, kc_ref, vcT_ref, ovT_ref, oc_ref, bias_ref):
    n = pl.program_id(2)
    ncp_all = kc_ref.shape[2]
    nsel_all = ovT_ref.shape[0]
    last_visible = (n * Q_BLOCK + Q_BLOCK - D_CMP_LEN) // D_CMP_STRIDE
    quarter = last_visible // (ncp_all // CMP_PARTS)
    for part in range(CMP_PARTS):
        visible = ((part + 1) * ncp_all // CMP_PARTS, (part + 1) * nsel_all // CMP_PARTS)
        pl.when(quarter == part)(functools.partial(_cmp_select_body, nc, visible, q_ref, kc_ref, vcT_ref, ovT_ref, oc_ref, bias_ref))


def _cmp_select_body(nc, visible, q_ref, kc_ref, vcT_ref, ovT_ref, oc_ref, bias_ref):
    n = pl.program_id(2)
    ncp, nsel = visible
    nsel_all = ovT_ref.shape[0]
    qs = _group_qT(q_ref)
    R = qs.shape[1]
    s = _dot(kc_ref[0, 0, 0:ncp, :], qs)
    tpos = n * Q_BLOCK + (_iota((1, R), 1) & (Q_BLOCK - 1))
    c_last = jnp.minimum((tpos - (D_CMP_LEN - 1)) >> 4, nc - 1)
    s = jnp.where(_iota((ncp, R), 0) <= c_last, s, NEG_INF)
    m = jnp.max(s, axis=0, keepdims=True)
    m = jnp.where(m > NEG_INF, m, 0.0)
    e = jnp.exp(s - m)
    den = jnp.sum(e, axis=0, keepdims=True)
    p = (e * (1.0 / jnp.where(den > 0, den, 1.0))).astype(BF16)
    oc_ref[...] = _group_rows(_dot(vcT_ref[0, 0, :, 0:ncp], p))
    imp_heads = _dot(ovT_ref[0:nsel, 0:ncp], p)
    imp = imp_heads[:, 0:Q_BLOCK]
    for g in range(1, GROUP):
        imp = imp + imp_heads[:, g * Q_BLOCK:(g + 1) * Q_BLOCK]
    t = n * Q_BLOCK + _iota((nsel, Q_BLOCK), 1)
    j = _iota((nsel, Q_BLOCK), 0)
    cur = t >> 6
    imp = jnp.where(j <= cur, imp, NEG_INF)
    imp = jnp.where(_in_range(cur - j, 2), float("inf"), imp)
    imp = jnp.where(j == 0, float("inf"), imp)
    for _ in range(D_SEL_TOPK):
        mx = jnp.max(imp, axis=0, keepdims=True)
        idx = jnp.min(jnp.where(imp == mx, j, nsel), axis=0, keepdims=True)
        imp = jnp.where(j == idx, PICKED, imp)
    bias_ref[0, 0, 0, 0:nsel, :] = jnp.where(imp == PICKED, 0.0, -MASK_BIG)
    if nsel < nsel_all:
        bias_ref[0, 0, 0, nsel:, :] = jnp.full((nsel_all - nsel, Q_BLOCK), -MASK_BIG, F32)


def _cmp_select(proj, q_col, kc, vcT, overlapT, nc, S):
    B, Hkv, ncp, _ = kc.shape
    nq = S // Q_BLOCK
    nsel = overlapT.shape[0]
    in_specs = [
        _q_spec(S, q_col),
        pl.BlockSpec((1, 1, ncp, HEAD_DIM), lambda b, h, n: (b, h, 0, 0)),
        pl.BlockSpec((1, 1, HEAD_DIM, ncp), lambda b, h, n: (b, h, 0, 0)),
        pl.BlockSpec((nsel, ncp), lambda b, h, n: (0, 0)),
    ]
    out_specs = [_o_spec(S), pl.BlockSpec((1, 1, 1, nsel, Q_BLOCK), lambda b, h, n: (b, h, n, 0, 0))]
    out_shape = [jax.ShapeDtypeStruct((B * S, Hkv * GROUP_COLS), F32), jax.ShapeDtypeStruct((B, Hkv, nq, nsel, Q_BLOCK), F32)]
    return pl.pallas_call(
        functools.partial(_cmp_select_kernel, nc), grid=(B, Hkv, nq), in_specs=in_specs, out_specs=out_specs,
        out_shape=out_shape, compiler_params=_params(3), name="nsa_cmp_select")(proj, kc, vcT, overlapT)


def _sel_kernel(q_ref, ka_ref, vTa_ref, bias_ref, o_ref, ksq_scr, p_scr, acc_scr):
    n = pl.program_id(2)
    pl.when(n == 0)(lambda: _max_key_sqnorm(ka_ref, ksq_scr))
    qs = _group_qT(q_ref)

    def bias_rows(kt):
        b = bias_ref[0, 0, 0, pl.ds(pl.multiple_of(kt * AUG_ROWS, AUG_ROWS), AUG_ROWS), :]
        return jnp.concatenate([b] * GROUP, axis=1)

    o_ref[...] = _group_rows(_masked_flash_t(n, qs, bias_rows, ka_ref, vTa_ref, ksq_scr, p_scr, acc_scr))


def _sel_attention(proj, q_col, ka, vTa, biasT):
    B, Hkv, S, _ = ka.shape
    nq = S // Q_BLOCK
    nsel = biasT.shape[3]
    assert KEY_TILE // D_SEL_LEN == AUG_ROWS
    in_specs = [
        _q_spec(S, q_col),
        pl.BlockSpec((1, 1, S, LANES), lambda b, h, n: (b, h, 0, 0)),
        pl.BlockSpec((1, 1, V_AUG_ROWS, S), lambda b, h, n: (b, h, 0, 0)),
        pl.BlockSpec((1, 1, 1, nsel, Q_BLOCK), lambda b, h, n: (b, h, n, 0, 0)),
    ]
    return pl.pallas_call(
        _sel_kernel, grid=(B, Hkv, nq), in_specs=in_specs,
        out_specs=_o_spec(S), out_shape=jax.ShapeDtypeStruct((B * S, Hkv * GROUP_COLS), F32),
        scratch_shapes=_flash_scratch(GROUP * Q_BLOCK),
        compiler_params=_params(3), name="nsa_selected")(proj, ka, vTa, biasT)


def _win_kernel(span, q_ref, k_ref, vT_ref, o_ref):
    n = pl.program_id(2)
    qs = _group_qT(q_ref)
    R = qs.shape[1]
    start = pl.multiple_of(jnp.maximum(n * Q_BLOCK + Q_BLOCK - span, 0), Q_BLOCK)
    s = _dot(k_ref[0, 0, pl.ds(start, span), :], qs)
    tpos = n * Q_BLOCK + (_iota((1, R), 1) & (Q_BLOCK - 1))
    d = tpos - (start + _iota((span, R), 0))
    s = jnp.where(_in_range(d, jnp.full((1, R), D_WINDOW, jnp.int32)), s, NEG_INF)
    m = jnp.max(s, axis=0, keepdims=True)
    p = jnp.exp(s - m)
    l = jnp.sum(p, axis=0, keepdims=True)
    o_ref[...] = _group_rows(_dot(vT_ref[0, 0, :, pl.ds(start, span)], p.astype(BF16)) * (1.0 / l))


def _win_attention(proj, q_col, k, vT):
    B, Hkv, S, _ = k.shape
    nq = S // Q_BLOCK
    span = min(D_WINDOW + Q_BLOCK, S)
    in_specs = [
        _q_spec(S, q_col),
        pl.BlockSpec((1, 1, S, HEAD_DIM), lambda b, h, n: (b, h, 0, 0)),
        pl.BlockSpec((1, 1, HEAD_DIM, S), lambda b, h, n: (b, h, 0, 0)),
    ]
    return pl.pallas_call(
        functools.partial(_win_kernel, span), grid=(B, Hkv, nq), in_specs=in_specs,
        out_specs=_o_spec(S), out_shape=jax.ShapeDtypeStruct((B * S, Hkv * GROUP_COLS), F32),
        compiler_params=_params(3), name="nsa_window")(proj, k, vT)


def _route(logits):
    lt = logits.T
    tm = lt.shape[1]
    g_rows = 8
    grow = _iota((g_rows, tm), 0)
    gl = jnp.where(grow < N_GROUPS, lt[0:g_rows], NEG_INF)
    gmax = jnp.max(gl, axis=0, keepdims=True)
    gidx = jnp.min(jnp.where(gl == gmax, grow, N_GROUPS), axis=0, keepdims=True)
    g_prob = 1.0 / jnp.sum(jnp.exp(gl - gmax), axis=0, keepdims=True)
    e_rows = 72
    erow = _iota((e_rows, tm), 0)
    in_group = _in_range(erow - N_GROUPS - gidx * EXPERTS_PER_GROUP, EXPERTS_PER_GROUP)
    el = jnp.where(in_group, lt[0:e_rows], NEG_INF)
    ee = jnp.exp(el - jnp.max(el, axis=0, keepdims=True))
    ep = jnp.where(in_group, ee * (1.0 / jnp.sum(ee, axis=0, keepdims=True)), -1.0)
    p1 = jnp.max(ep, axis=0, keepdims=True)
    i1 = jnp.min(jnp.where(ep == p1, erow, e_rows), axis=0, keepdims=True)
    ep2 = jnp.where(erow == i1, -1.0, ep)
    p2 = jnp.max(ep2, axis=0, keepdims=True)
    i2 = jnp.min(jnp.where(ep2 == p2, erow, e_rows), axis=0, keepdims=True)
    den = p1 + p2
    vals = [(i1 - N_GROUPS).astype(F32), (i2 - N_GROUPS).astype(F32), g_prob * p1 / den, g_prob * p2 / den]
    orow = _iota((LANES, tm), 0)
    out = jnp.zeros((LANES, tm), F32)
    for k, val in enumerate(vals):
        out = jnp.where(orow == k, val, out)
    return out.T


def _pack_bf16_pairs(h):
    n = h.shape[1] // 2
    bits = lax.bitcast_convert_type(h.astype(BF16).astype(F32), jnp.uint32)
    return bits[:, :n] | (bits[:, n:] >> 16)


def _unpack_bf16_pairs(packed):
    hi = lax.bitcast_convert_type(packed & jnp.uint32(0xFFFF0000), F32)
    lo = lax.bitcast_convert_type(packed << 16, F32)
    return jnp.concatenate([hi, lo], axis=1).astype(BF16)


def _out_tail(x_new, gain_ref, wr_ref, xo_ref, h_ref, route_ref):
    xo_ref[...] = x_new
    ms = jnp.mean(x_new * x_new, axis=-1, keepdims=True)
    h = x_new * lax.rsqrt(ms + NORM_EPS) * gain_ref[...]
    h_ref[...] = _pack_bf16_pairs(h)
    h_hi = h.astype(BF16)
    h_lo = (h - h_hi.astype(F32)).astype(BF16)
    hi_both = _dot(h_hi, wr_ref[...])
    logits = hi_both[:, 0:LANES] + (_dot(h_lo, wr_ref[:, 0:LANES]) + hi_both[:, LANES:])
    route_ref[...] = _route(logits)


def _out_proj_kernel(oa_ref, ob_ref, x_ref, w_ref, gain_ref, wr_ref, xo_ref, h_ref, route_ref):
    half = w_ref.shape[0] // 2
    x_new = x_ref[...] + _dot(oa_ref[...], w_ref[0:half, :]) + _dot(ob_ref[...], w_ref[half:, :])
    _out_tail(x_new, gain_ref, wr_ref, xo_ref, h_ref, route_ref)


def _out_proj_nsa_kernel(oc_ref, b0_ref, b1_ref, b2_ref, gd_ref, x_ref, w_ref, gain_ref, wr_ref, xo_ref, h_ref, route_ref):
    half = D_HEADS * HEAD_DIM
    g = jax.nn.sigmoid(gd_ref[...])
    g_hi = g.astype(BF16)
    g_lo = (g - g_hi.astype(F32)).astype(BF16)
    src = _iota((LANES, half), 0)
    head3 = (_iota((LANES, half), 1) >> 6) * 3
    od = None
    for br, b_ref in enumerate((b0_ref, b1_ref, b2_ref)):
        spread = jnp.where(src == head3 + br, 1.0, 0.0).astype(BF16)
        term = (_dot(g_hi, spread) + _dot(g_lo, spread)) * b_ref[...]
        od = term if od is None else od + term
    x_new = x_ref[...] + _dot(oc_ref[...], w_ref[0:half, :]) + _dot(od.astype(BF16), w_ref[half:, :])
    _out_tail(x_new, gain_ref, wr_ref, xo_ref, h_ref, route_ref)


def _out_proj(o_parts, gd, x, w_out, gain, w_router):
    T = x.shape[0]
    tm = ROW_TILE
    row = lambda i: (i, 0)
    fixed = lambda i: (0, 0)
    if gd is None:
        kern = _out_proj_kernel
        args = list(o_parts)
        in_specs = [pl.BlockSpec((tm, D_MODEL // 2), row)] * 2
    else:
        kern = _out_proj_nsa_kernel
        args = list(o_parts) + [gd]
        in_specs = [pl.BlockSpec((tm, D_MODEL // 2), row)] * 4 + [pl.BlockSpec((tm, LANES), row)]
    args += [x, w_out, gain.reshape(1, D_MODEL), w_router]
    in_specs += [pl.BlockSpec((tm, D_MODEL), row), pl.BlockSpec((D_MODEL, D_MODEL), fixed),
                 pl.BlockSpec((1, D_MODEL), fixed), pl.BlockSpec((D_MODEL, 2 * LANES), fixed)]
    out_shape = [jax.ShapeDtypeStruct((T, D_MODEL), F32), jax.ShapeDtypeStruct((T, D_MODEL // 2), jnp.uint32),
                 jax.ShapeDtypeStruct((T, LANES), F32)]
    out_specs = [pl.BlockSpec((tm, D_MODEL), row), pl.BlockSpec((tm, D_MODEL // 2), row), pl.BlockSpec((tm, LANES), row)]
    return pl.pallas_call(kern, grid=(T // tm,), in_specs=in_specs, out_specs=out_specs, out_shape=out_shape,
                          compiler_params=_params(1), name="out_proj_router")(*args)


def _expert_kernel(be_ref, rows_ref, wg_ref, wu_ref, wd_ref, y_ref, wg_s, wu_s, wd_s):
    i = pl.program_id(0)
    n_blk = pl.num_programs(0)

    @pl.when((i == 0) | (be_ref[i] != be_ref[jnp.maximum(i - 1, 0)]))
    def _():
        wg_s[...] = wg_ref[0].astype(BF16)
        wu_s[...] = wu_ref[0].astype(BF16)
        wd_s[...] = wd_ref[0].astype(BF16)

    @pl.when(i < be_ref[n_blk])
    def _():
        xb = _unpack_bf16_pairs(rows_ref[...])
        hid = jax.nn.silu(_dot(xb, wg_s[...])) * _dot(xb, wu_s[...])
        y_ref[...] = _dot(hid.astype(BF16), wd_s[...])

    @pl.when(i >= be_ref[n_blk])
    def _():
        y_ref[...] = jnp.zeros(y_ref.shape, F32)


def _expert_ffn(rows, blk_info, wg, wu, wd):
    n_rows = rows.shape[0]
    n_blk = n_rows // MOE_ROWS
    grid_spec = pltpu.PrefetchScalarGridSpec(
        num_scalar_prefetch=1, grid=(n_blk,),
        in_specs=[
            pl.BlockSpec((MOE_ROWS, D_MODEL // 2), lambda i, be: (i, 0)),
            pl.BlockSpec((1, D_MODEL, EXPERT_HIDDEN), lambda i, be: (be[i], 0, 0)),
            pl.BlockSpec((1, D_MODEL, EXPERT_HIDDEN), lambda i, be: (be[i], 0, 0)),
            pl.BlockSpec((1, EXPERT_HIDDEN, D_MODEL), lambda i, be: (be[i], 0, 0)),
        ],
        out_specs=pl.BlockSpec((MOE_ROWS, D_MODEL), lambda i, be: (i, 0)),
        scratch_shapes=[pltpu.VMEM((D_MODEL, EXPERT_HIDDEN), BF16), pltpu.VMEM((D_MODEL, EXPERT_HIDDEN), BF16),
                        pltpu.VMEM((EXPERT_HIDDEN, D_MODEL), BF16)])
    return pl.pallas_call(
        _expert_kernel, grid_spec=grid_spec, out_shape=jax.ShapeDtypeStruct((n_rows, D_MODEL), F32),
        compiler_params=_params(1), name="expert_ffn")(blk_info, rows, wg, wu, wd)


def _moe_dispatch(route, h):
    n_tok = h.shape[0]
    n_asg = n_tok * MOE_TOPK
    i32 = jnp.int32
    e_flat = route[:, 0:MOE_TOPK].astype(i32).reshape(n_asg)
    is_e = e_flat[:, None] == jnp.arange(N_EXPERTS, dtype=i32)[None, :]
    counts = jnp.sum(is_e, axis=0, dtype=i32)
    assert n_asg & (n_asg - 1) == 0 and n_asg <= 1 << 16
    idx = jnp.arange(n_asg, dtype=jnp.uint32)
    low = jnp.uint32(n_asg - 1)
    order = (jnp.sort(e_flat.astype(jnp.uint32) * n_asg + idx) & low).astype(i32)
    rank = (jnp.sort(order.astype(jnp.uint32) * n_asg + idx) & low).astype(i32)
    padded = (counts + MOE_ROWS - 1) // MOE_ROWS * MOE_ROWS
    pad_end = jnp.cumsum(padded)
    pad_start = pad_end - padded
    start = jnp.cumsum(counts) - counts
    n_rows = n_asg + N_EXPERTS * MOE_ROWS
    n_blk = n_rows // MOE_ROWS
    blk_start = jnp.arange(n_blk, dtype=i32) * MOE_ROWS
    blk_expert = jnp.minimum(jnp.sum(pad_end[None, :] <= blk_start[:, None], axis=1, dtype=i32), N_EXPERTS - 1)
    within = (blk_start - pad_start[blk_expert])[:, None] + jnp.arange(MOE_ROWS, dtype=i32)[None, :]
    valid = within < counts[blk_expert][:, None]
    src = jnp.clip(start[blk_expert][:, None] + within, 0, n_asg - 1)
    row_tok = jnp.where(valid, (order // MOE_TOPK)[src], src // MOE_TOPK).reshape(n_rows)
    shift = jnp.sum(jnp.where(is_e, (pad_start - start)[None, :], 0), axis=1, dtype=i32)
    pos = (rank + shift).reshape(n_tok, MOE_TOPK)
    blk_info = jnp.concatenate([blk_expert, (pad_end[-1:] // MOE_ROWS).astype(i32)])
    return h[row_tok], blk_info, pos


def _moe(route, h, wg, wu, wd):
    rows, blk_info, pos = _moe_dispatch(route, h)
    y = _expert_ffn(rows, blk_info, wg, wu, wd)
    return y[pos[:, 0]], y[pos[:, 1]], route


def _final_kernel(x_ref, y0_ref, y1_ref, route_ref, g_ref, o_ref):
    x = _moe_combine(x_ref[...], y0_ref, y1_ref, route_ref)
    ms = jnp.mean(x * x, axis=-1, keepdims=True)
    o_ref[...] = x * lax.rsqrt(ms + NORM_EPS) * g_ref[...]


def _final_norm(x, ys, gain):
    T = x.shape[0]
    tm = ROW_TILE
    row = lambda i: (i, 0)
    return pl.pallas_call(
        _final_kernel, grid=(T // tm,),
        in_specs=[pl.BlockSpec((tm, D_MODEL), row)] * 3 + [pl.BlockSpec((tm, LANES), row), pl.BlockSpec((1, D_MODEL), lambda i: (0, 0))],
        out_specs=pl.BlockSpec((tm, D_MODEL), row), out_shape=jax.ShapeDtypeStruct((T, D_MODEL), F32),
        compiler_params=_params(1), name="final_norm")(x, *ys, gain.reshape(1, D_MODEL))


def _rope_tables(positions):
    inv_freq = ROPE_THETA ** (-jnp.arange(0, HEAD_DIM, 2, dtype=F32) / HEAD_DIM)
    ang = positions.astype(F32).reshape(-1, 1) * inv_freq
    cos, sin = jnp.cos(ang), jnp.sin(ang)
    reps = LANES // HEAD_DIM
    return jnp.tile(jnp.concatenate([cos, cos], axis=1), (1, reps)), jnp.tile(jnp.concatenate([-sin, sin], axis=1), (1, reps))


def _chunks(first, count, rope, action):
    return tuple((first + j, rope, action(j)) for j in range(count))


def _plan(*runs):
    entries = sum(runs, ())
    assert [e[0] for e in entries] == list(range(len(entries)))
    return tuple(e[1:] for e in entries)


AB_OUTS = (("tok", 4), ("heads", A_KV_HEADS), ("headsT", A_KV_HEADS), ("heads", B_HEADS), ("headsT", B_HEADS), ("headsT", B_HEADS))
AB_PLAN = _plan(_chunks(0, 4, True, lambda j: (("tok", 0, j),)),
                _chunks(4, 1, True, lambda j: (("heads", 1, 0),)),
                _chunks(5, 1, False, lambda j: (("headsT", 2, 0),)),
                _chunks(6, 4, False, lambda j: (("heads", 3, 2 * j),)),
                _chunks(10, 4, False, lambda j: (("headsT", 4, 2 * j),)),
                _chunks(14, 4, False, lambda j: (("headsT", 5, 2 * j),)))
CD_OUTS = (("tok", 8), ("keys_aug", C_KV_HEADS), ("kmean", 0), ("valsT_aug", C_KV_HEADS), ("heads", D_KV_HEADS),
           ("heads", D_KV_HEADS), ("keys_aug", D_KV_HEADS), ("valsT_aug", D_KV_HEADS), ("heads", D_KV_HEADS),
           ("headsT", D_KV_HEADS), ("f32", 0))
CD_PLAN = _plan(_chunks(0, 4, True, lambda j: (("tok", 0, j),)),
                _chunks(4, 1, True, lambda j: (("keys_aug", 1, C_BLOCK), ("kmean", 2, 0))),
                _chunks(5, 1, False, lambda j: (("valsT_aug", 3, 0),)),
                _chunks(6, 4, True, lambda j: (("tok", 0, 4 + j),)),
                _chunks(10, 1, True, lambda j: (("heads", 4, 0),)),
                _chunks(11, 1, False, lambda j: (("heads", 5, 0),)),
                _chunks(12, 1, True, lambda j: (("keys_aug", 6, D_SEL_LEN),)),
                _chunks(13, 1, False, lambda j: (("valsT_aug", 7, 0),)),
                _chunks(14, 1, True, lambda j: (("heads", 8, 0),)),
                _chunks(15, 1, False, lambda j: (("headsT", 9, 0),)),
                _chunks(16, 1, False, lambda j: (("f32", 10, 0),)))
QD_COL = 4 * LANES


def _router_weights(router_group, router_expert):
    pad = jnp.zeros((D_MODEL, LANES - N_GROUPS - N_EXPERTS), F32)
    w = jnp.concatenate([router_group.astype(F32), router_expert.astype(F32), pad], axis=1)
    hi = w.astype(BF16)
    return jnp.concatenate([hi, (w - hi.astype(F32)).astype(BF16)], axis=1)


def _pad_cols(w, n):
    return jnp.concatenate([w, jnp.zeros((w.shape[0], n - w.shape[1]), w.dtype)], axis=1)


def _mixer_ab(parts, sinks):
    qa, ka, vaT, qb, kbT, vbT = parts
    oa = _swa_attention(qa, 0, ka, vaT, sinks)
    ob = _stick_attention(qb, kbT, vbT)
    return oa, ob


def _mixer_cd(parts, B, S, k_pe, k_w1, k_w2, v_pe, v_w1, v_w2):
    q_cd, kc_aug, kmean, vcT_aug, kdc, vdc, kds_aug, vdsT_aug, kdw, vdwT, _ = parts
    nblk = S // C_BLOCK
    km = kmean.reshape(B, nblk, C_KV_HEADS, HEAD_DIM).transpose(0, 2, 1, 3).astype(BF16)
    oc = _moba_attention(q_cd, 0, kc_aug, vcT_aug, km)
    k_cmp = _compress(kdc, k_pe, k_w1, k_w2)
    v_cmp = _compress(vdc, v_pe, v_w1, v_w2)
    nc = (S - D_CMP_LEN) // D_CMP_STRIDE + 1
    ncp = S // D_CMP_STRIDE
    nsel = S // D_SEL_LEN
    c_start = jnp.arange(ncp) * D_CMP_STRIDE
    b_start = jnp.arange(nsel) * D_SEL_LEN
    overlap = ((c_start[:, None] <= b_start[None, :] + D_SEL_LEN - 1) & (c_start[:, None] + D_CMP_LEN - 1 >= b_start[None, :])
               & (jnp.arange(ncp)[:, None] < nc)).astype(BF16)
    o_cmp, biasT = _cmp_select(q_cd, QD_COL, k_cmp, v_cmp.transpose(0, 1, 3, 2), overlap.T, nc, S)
    o_sel = _sel_attention(q_cd, QD_COL, kds_aug, vdsT_aug, biasT)
    o_win = _win_attention(q_cd, QD_COL, kdw, vdwT)
    return oc, o_cmp, o_sel, o_win


def kernel(x, positions, ln_mix_0, w_in_0, sinks_0, w_out_0, ln_ffn_0, router_group_0, router_expert_0, expert_gate_0, expert_up_0, expert_down_0, ln_mix_1, w_in_1, cmp_k_pe_1, cmp_k_w1_1, cmp_k_w2_1, cmp_v_pe_1, cmp_v_w1_1, cmp_v_w2_1, w_out_1, ln_ffn_1, router_group_1, router_expert_1, expert_gate_1, expert_up_1, expert_down_1, ln_final):
    B, S, _ = x.shape
    T = B * S
    assert S % KEY_TILE == 0 and T % ROW_TILE == 0
    cos_t, sin_t = _rope_tables(positions)
    xf = x.reshape(T, D_MODEL)

    parts = _in_proj(xf, None, ln_mix_0, w_in_0.astype(BF16), cos_t, sin_t, AB_PLAN, AB_OUTS, B, S)
    o_ab = _mixer_ab(parts, sinks_0)
    x1, h1, route1 = _out_proj(o_ab, None, xf, w_out_0.astype(BF16), ln_ffn_0, _router_weights(router_group_0, router_expert_0))
    ys = _moe(route1, h1, expert_gate_0, expert_up_0, expert_down_0)

    x2, *parts = _in_proj(x1, ys, ln_mix_1, _pad_cols(w_in_1.astype(BF16), len(CD_PLAN) * LANES), cos_t, sin_t,
                          CD_PLAN, CD_OUTS, B, S)
    gd = parts[-1]
    parts = _mixer_cd(parts, B, S, cmp_k_pe_1, cmp_k_w1_1, cmp_k_w2_1, cmp_v_pe_1, cmp_v_w1_1, cmp_v_w2_1)
    x3, h3, route3 = _out_proj(parts, gd, x2, w_out_1.astype(BF16), ln_ffn_1, _router_weights(router_group_1, router_expert_1))
    ys = _moe(route3, h3, expert_gate_1, expert_up_1, expert_down_1)
    return _final_norm(x3, ys, ln_final).reshape(B, S, D_MODEL)
```
